```python
import jax, jax.numpy as jnp
from jax import lax
import numpy as np

D_MODEL = 2048
BATCH = 16
SEQ = 256
DEPTH = 1
DEC_BATCH = 2
DEC_SEQ = 4096
PAST_LEN = 256

GRID_W = 64
CONV_WIDTH = 1024
CONV_K = 3
N_HEADS = 8
QK_NOPE = 128
QK_ROPE = 64
V_DIM = 128
QK_DIM = QK_NOPE + QK_ROPE
Q_LORA = 512
KV_LORA = 256
MLA_WIDTH = N_HEADS * V_DIM
MIX_WIDTH = CONV_WIDTH + MLA_WIDTH
IN_SPLITS = (CONV_WIDTH, 2 * CONV_WIDTH, 3 * CONV_WIDTH,
             3 * CONV_WIDTH + Q_LORA, 3 * CONV_WIDTH + Q_LORA + KV_LORA)
IN_WIDTH = 3 * CONV_WIDTH + Q_LORA + KV_LORA + QK_ROPE
ROPE_PAIRS = QK_ROPE // 4
ROPE_BASE = 10000.0
Q_BLOCK = 128
N_EXPERT_GROUPS = 4
EXPERTS_PER_GROUP = 8
N_EXPERTS = N_EXPERT_GROUPS * EXPERTS_PER_GROUP
TOP_K = 2
D_EXPERT = 512
N_MOD = 6
EPS = 1e-6

kernel_name = "hybrid_conv_mla_hmoe_diffusion_step"


def rms_norm(x, g):
    xf = x.astype(jnp.float32)
    y = xf * lax.rsqrt(jnp.mean(xf * xf, axis=-1, keepdims=True) + EPS)
    return (y * g.astype(jnp.float32)).astype(x.dtype)


def modulation(cond, ada_w, ada_b):
    m = jax.nn.silu(cond) @ ada_w + ada_b
    return jnp.split(m[..., None, :], N_MOD, axis=-1)


def short_conv(u, conv_w):
    s = u.shape[1]
    up = jnp.pad(u, ((0, 0), (1, 1), (0, 0)))
    return up[:, :s] * conv_w[0] + up[:, 1:s + 1] * conv_w[1] + up[:, 2:] * conv_w[2]


def axial_rope_tables(n_tokens):
    rows = n_tokens // GRID_W
    row = jnp.repeat(jnp.arange(rows, dtype=jnp.float32), GRID_W)
    col = jnp.tile(jnp.arange(GRID_W, dtype=jnp.float32), rows)
    inv = ROPE_BASE ** (-jnp.arange(ROPE_PAIRS, dtype=jnp.float32) / ROPE_PAIRS)
    ang = jnp.concatenate([row[:, None] * inv, col[:, None] * inv], axis=-1)
    return jnp.cos(ang), jnp.sin(ang)


def apply_axial_rope(x, cos, sin):
    xf = x.astype(jnp.float32).reshape(*x.shape[:-1], 2, 2, ROPE_PAIRS)
    x1, x2 = xf[..., 0, :], xf[..., 1, :]
    c = cos.reshape(cos.shape[0], 1, 2, ROPE_PAIRS)
    s = sin.reshape(sin.shape[0], 1, 2, ROPE_PAIRS)
    out = jnp.stack([x1 * c - x2 * s, x1 * s + x2 * c], axis=-2)
    return out.reshape(x.shape).astype(x.dtype)


def rope_slice(x, cos, sin):
    return jnp.concatenate([x[..., :QK_NOPE], apply_axial_rope(x[..., QK_NOPE:], cos, sin)], axis=-1)


def mla_queries(q_lat, q_lora_g, w_uq, q_head_g):
    q = rms_norm(q_lat, q_lora_g) @ w_uq
    q = q.reshape(*q_lat.shape[:-1], N_HEADS, QK_DIM)
    return rms_norm(q, q_head_g)


def mla_keys_values(ckv, k_rope, w_ukv, k_head_g):
    kv = (ckv @ w_ukv).reshape(*ckv.shape[:-1], N_HEADS, QK_NOPE + V_DIM)
    k_nope, v = kv[..., :QK_NOPE], kv[..., QK_NOPE:]
    k_r = jnp.broadcast_to(k_rope[..., None, :], (*k_rope.shape[:-1], N_HEADS, QK_ROPE))
    k = jnp.concatenate([k_nope, k_r], axis=-1)
    return rms_norm(k, k_head_g), v


def attend(q, k, v):
    s = jnp.einsum('bqhd,bkhd->bhqk', q, k, preferred_element_type=jnp.float32) * (QK_DIM ** -0.5)
    p = jax.nn.softmax(s, axis=-1).astype(v.dtype)
    return jnp.einsum('bhqk,bkhv->bqhv', p, v)


def attend_blocked(q, k, v):
    b, s = q.shape[:2]
    qb = q.reshape(b, s // Q_BLOCK, Q_BLOCK, N_HEADS, QK_DIM).transpose(1, 0, 2, 3, 4)
    ob = lax.map(lambda qi: attend(qi, k, v), qb)
    return ob.transpose(1, 0, 2, 3, 4).reshape(b, s, N_HEADS, V_DIM)


def hier_moe(h, router_g, router_e, w_gate, w_up, w_down):
    shp = h.shape
    t = h.reshape(-1, D_MODEL)
    g_logits = (t @ router_g).astype(jnp.float32)
    g_prob = jax.nn.softmax(g_logits, axis=-1)
    g_top = jnp.argmax(g_logits, axis=-1)
    e_logits = (t @ router_e).astype(jnp.float32).reshape(-1, N_EXPERT_GROUPS, EXPERTS_PER_GROUP)
    e_in = jnp.take_along_axis(e_logits, g_top[:, None, None], axis=1)[:, 0]
    top_v, top_i = lax.top_k(e_in, TOP_K)
    w = jax.nn.softmax(top_v, axis=-1) * jnp.take_along_axis(g_prob, g_top[:, None], axis=1)
    eid = g_top[:, None] * EXPERTS_PER_GROUP + top_i
    combine = jnp.sum(jax.nn.one_hot(eid, N_EXPERTS, dtype=jnp.float32) * w[..., None], axis=1)
    out = jnp.zeros(t.shape, jnp.float32)
    for e in range(N_EXPERTS):
        a = jax.nn.silu(t @ w_gate[e]) * (t @ w_up[e])
        out = out + combine[:, e:e + 1] * (a @ w_down[e]).astype(jnp.float32)
    return out.astype(h.dtype).reshape(shp)


def trunk_layer(x, cond, ctx_cache, rope, ada_w, ada_b, norm1_g, w_in, conv_w, q_lora_g, w_uq,
                kv_lora_g, w_ukv, q_head_g, k_head_g, w_out, norm2_g, router_g, router_e,
                w_gate, w_up, w_down):
    b, s, _ = x.shape
    sh1, sc1, g1, sh2, sc2, g2 = modulation(cond, ada_w, ada_b)
    h = rms_norm(x, norm1_g) * (1 + sc1) + sh1
    b_g, c_g, u, q_lat, kv_lat, k_rope = jnp.split(h @ w_in, IN_SPLITS, axis=-1)
    y_conv = b_g * short_conv(c_g * u, conv_w)
    ckv = rms_norm(kv_lat, kv_lora_g)
    q = mla_queries(q_lat, q_lora_g, w_uq, q_head_g)
    k, v = mla_keys_values(ckv, k_rope, w_ukv, k_head_g)
    if rope is not None:
        cos, sin = rope
        q = rope_slice(q, cos, sin)
        k = rope_slice(k, cos, sin)
    if ctx_cache is not None:
        k_ctx, v_ctx = mla_keys_values(ctx_cache[0], ctx_cache[1], w_ukv, k_head_g)
        k = jnp.concatenate([k_ctx, k], axis=1)
        v = jnp.concatenate([v_ctx, v], axis=1)
    o = attend_blocked(q, k, v).reshape(b, s, MLA_WIDTH)
    x = x + g1 * (jnp.concatenate([y_conv, o], axis=-1) @ w_out)
    h2 = rms_norm(x, norm2_g) * (1 + sc2) + sh2
    x = x + g2 * hier_moe(h2, router_g, router_e, w_gate, w_up, w_down)
    return x, ckv, k_rope


def setup_inputs(seed: int = 0) -> dict:
    key = jax.random.key(seed)
    ks = jax.random.split(key, 26)

    def nrm(k, shape, scale):
        return jax.random.normal(k, shape, jnp.float32) * scale

    def gain(k, shape):
        return 1.0 + 0.05 * jax.random.normal(k, shape, jnp.float32)

    return {
        "x_prompt": nrm(ks[0], (BATCH, SEQ, D_MODEL), 1.0),
        "x_sample": nrm(ks[1], (DEC_BATCH, DEC_SEQ, D_MODEL), 1.0),
        "cache_ckv": nrm(ks[2], (DEC_BATCH, DEPTH, PAST_LEN, KV_LORA), 1.0),
        "cache_krope": nrm(ks[3], (DEC_BATCH, DEPTH, PAST_LEN, QK_ROPE), 1.0),
        "c": nrm(ks[4], (DEC_BATCH, D_MODEL), 1.0),
        "c_ctx": nrm(ks[5], (D_MODEL,), 1.0),
        "ada_w": nrm(ks[6], (DEPTH, D_MODEL, N_MOD * D_MODEL), D_MODEL ** -0.5),
        "ada_b": nrm(ks[7], (DEPTH, N_MOD * D_MODEL), 0.02),
        "norm1_g": gain(ks[8], (DEPTH, D_MODEL)),
        "w_in": nrm(ks[9], (DEPTH, D_MODEL, IN_WIDTH), D_MODEL ** -0.5),
        "conv_w": nrm(ks[10], (DEPTH, CONV_K, CONV_WIDTH), CONV_K ** -0.5),
        "q_lora_g": gain(ks[11], (DEPTH, Q_LORA)),
        "w_uq": nrm(ks[12], (DEPTH, Q_LORA, N_HEADS * QK_DIM), Q_LORA ** -0.5),
        "kv_lora_g": gain(ks[13], (DEPTH, KV_LORA)),
        "w_ukv": nrm(ks[14], (DEPTH, KV_LORA, N_HEADS * (QK_NOPE + V_DIM)), KV_LORA ** -0.5),
        "q_head_g": gain(ks[15], (DEPTH, QK_DIM)),
        "k_head_g": gain(ks[16], (DEPTH, QK_DIM)),
        "w_out": nrm(ks[17], (DEPTH, MIX_WIDTH, D_MODEL), MIX_WIDTH ** -0.5),
        "norm2_g": gain(ks[18], (DEPTH, D_MODEL)),
        "router_g": nrm(ks[19], (DEPTH, D_MODEL, N_EXPERT_GROUPS), D_MODEL ** -0.5),
        "router_e": nrm(ks[20], (DEPTH, D_MODEL, N_EXPERTS), D_MODEL ** -0.5),
        "w_gate": nrm(ks[21], (DEPTH, N_EXPERTS, D_MODEL, D_EXPERT), D_MODEL ** -0.5),
        "w_up": nrm(ks[22], (DEPTH, N_EXPERTS, D_MODEL, D_EXPERT), D_MODEL ** -0.5),
        "w_down": nrm(ks[23], (DEPTH, N_EXPERTS, D_EXPERT, D_MODEL), D_EXPERT ** -0.5),
    }


def reference(x_prompt, x_sample, cache_ckv, cache_krope, c, c_ctx, ada_w, ada_b, norm1_g, w_in,
              conv_w, q_lora_g, w_uq, kv_lora_g, w_ukv, q_head_g, k_head_g, w_out, norm2_g,
              router_g, router_e, w_gate, w_up, w_down):
    rope = axial_rope_tables(x_sample.shape[1])
    xp = x_prompt
    xs = x_sample
    ckv_list = []
    krope_list = []
    for l in range(DEPTH):
        lw = (ada_w[l], ada_b[l], norm1_g[l], w_in[l], conv_w[l], q_lora_g[l], w_uq[l],
              kv_lora_g[l], w_ukv[l], q_head_g[l], k_head_g[l], w_out[l], norm2_g[l],
              router_g[l], router_e[l], w_gate[l], w_up[l], w_down[l])
        xp, ckv_l, krope_l = trunk_layer(xp, c_ctx, None, None, *lw)
        ckv_list.append(ckv_l)
        krope_list.append(krope_l)
        xs, _, _ = trunk_layer(xs, c, (cache_ckv[:, l], cache_krope[:, l]), rope, *lw)
    new_ckv = jnp.stack(ckv_list, axis=1)
    new_krope = jnp.stack(krope_list, axis=1)
    return (xp, xs, new_ckv, new_krope)
```

```python
import functools

import jax
import jax.numpy as jnp
from jax import lax
from jax.experimental import pallas as pl
from jax.experimental.pallas import tpu as pltpu

F32 = jnp.float32
BF16 = jnp.bfloat16
HIGHEST = lax.Precision.HIGHEST

D_MODEL = 2048
CONV_WIDTH = 1024
N_HEADS = 8
QK_NOPE = 128
QK_ROPE = 64
V_DIM = 128
QK_DIM = QK_NOPE + QK_ROPE
Q_LORA = 512
KV_LORA = 256
GRID_W = 64
ROPE_PAIRS = QK_ROPE // 4
ROPE_BASE = 10000.0
N_EXPERT_GROUPS = 4
EXPERTS_PER_GROUP = 8
N_EXPERTS = N_EXPERT_GROUPS * EXPERTS_PER_GROUP
D_EXPERT = 512
N_MOD = 6
EPS = 1e-6

HEAD_W = 2 * QK_NOPE
LANES = 128
IN_COLS = 3 * CONV_WIDTH + Q_LORA + KV_LORA + 2 * QK_ROPE
ROUTER_COLS = LANES
VMEM_LIMIT = 56 * 1024 * 1024

TM_IN = 512
TM_OUT = 512
TQ = 256
TM_MOE = 256
TM_FIN = 512
BN_MOD = 1024


def _cparams(sem):
    return pltpu.CompilerParams(dimension_semantics=sem, vmem_limit_bytes=VMEM_LIMIT)


def _const_spec(shape):
    nd = len(shape)
    return pl.BlockSpec(shape, lambda *_: (0,) * nd, pipeline_mode=pl.Buffered(1))


def _rms(x):
    return x * lax.rsqrt(jnp.mean(x * x, axis=-1, keepdims=True) + EPS)


def _rowsum(x):
    return jnp.sum(x, axis=-1, keepdims=True)


def _mod_kernel(c_ref, w_ref, b_ref, o_ref):
    c = c_ref[...]
    s = c / (1.0 + jnp.exp(-c))
    o_ref[...] = jnp.dot(s, w_ref[...], precision=HIGHEST, preferred_element_type=F32) + b_ref[...]


def _modulation(cond8, ada_w, ada_b):
    n = ada_w.shape[1]
    return pl.pallas_call(
        _mod_kernel,
        out_shape=jax.ShapeDtypeStruct((8, n), F32),
        grid=(n // BN_MOD,),
        in_specs=[pl.BlockSpec((8, D_MODEL), lambda j: (0, 0)),
                  pl.BlockSpec((D_MODEL, BN_MOD), lambda j: (0, j)),
                  pl.BlockSpec((1, BN_MOD), lambda j: (0, j))],
        out_specs=pl.BlockSpec((8, BN_MOD), lambda j: (0, j)),
        compiler_params=_cparams(("arbitrary",)),
        name="mod",
    )(cond8, ada_w, ada_b.reshape(1, n))


def _emit_kv(ckv, kraw, cs, w_ukv_ref, gk_ref, k_ref, v_ref):
    kv = jnp.dot(ckv.astype(BF16), w_ukv_ref[...], preferred_element_type=F32)
    ss_rope = 0.5 * _rowsum(kraw * kraw)
    t = kraw * (cs * gk_ref[1:2, :])
    tt = t + pltpu.roll(t, QK_ROPE, axis=1)
    g_nope = gk_ref[0:1, :]
    for h in range(N_HEADS):
        kn = kv[:, h * HEAD_W:h * HEAD_W + QK_NOPE]
        r = lax.rsqrt((_rowsum(kn * kn) + ss_rope) * (1.0 / QK_DIM) + EPS)
        k_ref[:, h * HEAD_W:h * HEAD_W + QK_NOPE] = (kn * r * g_nope).astype(BF16)
        k_ref[:, h * HEAD_W + QK_NOPE:(h + 1) * HEAD_W] = (tt * r).astype(BF16)
        v_ref[:, h * V_DIM:(h + 1) * V_DIM] = kv[:, h * HEAD_W + QK_NOPE:(h + 1) * HEAD_W].astype(BF16)


def _kvcache_kernel(ckv_ref, kraw_ref, cs_ref, w_ukv_ref, gk_ref, k_ref, v_ref):
    _emit_kv(ckv_ref[...], kraw_ref[...], cs_ref[...], w_ukv_ref, gk_ref, k_ref, v_ref)


def _kvcache(ckv, kraw, cs_id, w_ukv_b, gk):
    n = ckv.shape[0]
    tm = 256
    return pl.pallas_call(
        _kvcache_kernel,
        out_shape=(jax.ShapeDtypeStruct((n, N_HEADS * HEAD_W), BF16),
                   jax.ShapeDtypeStruct((n, N_HEADS * V_DIM), BF16)),
        grid=(n // tm,),
        in_specs=[pl.BlockSpec((tm, KV_LORA), lambda i: (i, 0)),
                  pl.BlockSpec((tm, LANES), lambda i: (i, 0)),
                  pl.BlockSpec((tm, LANES), lambda i: (0, 0)),
                  _const_spec((KV_LORA, N_HEADS * HEAD_W)),
                  _const_spec((2, LANES))],
        out_specs=(pl.BlockSpec((tm, N_HEADS * HEAD_W), lambda i: (i, 0)),
                   pl.BlockSpec((tm, N_HEADS * V_DIM), lambda i: (i, 0))),
        compiler_params=_cparams(("arbitrary",)),
        name="kvcache",
    )(ckv, kraw, cs_id, w_ukv_b, gk)


def _inproj_kernel(x_ref, mod_ref, n1g_ref, w_in_ref, qlg_ref, w_uq_ref, kvg_ref, w_ukv_ref,
                   gq_ref, gk_ref, cs_ref, bg_ref, cu_ref, q_ref, k_ref, v_ref, *cache_refs):
    x = x_ref[...]
    mod = mod_ref[0]
    h = _rms(x) * n1g_ref[...] * (1.0 + mod[1:2, :]) + mod[0:1, :]
    hb = h.astype(BF16)

    def proj(a, b):
        return jnp.dot(hb, w_in_ref[:, a:b], preferred_element_type=F32)

    c1, c2, c3 = CONV_WIDTH, 2 * CONV_WIDTH, 3 * CONV_WIDTH
    bg_ref[...] = proj(0, c1).astype(BF16)
    cu_ref[...] = (proj(c1, c2) * proj(c2, c3)).astype(BF16)
    q_lat = proj(c3, c3 + Q_LORA)
    kvk = proj(c3 + Q_LORA, IN_COLS)
    cs = cs_ref[...]

    qn = _rms(q_lat) * qlg_ref[...]
    q = jnp.dot(qn.astype(BF16), w_uq_ref[...], preferred_element_type=F32)
    scale = QK_DIM ** -0.5
    g_nope = gq_ref[0:1, :] * scale
    tq = cs * (gq_ref[1:2, :] * scale)
    for hd in range(N_HEADS):
        lo = q[:, hd * HEAD_W:hd * HEAD_W + QK_NOPE]
        up = q[:, hd * HEAD_W + QK_NOPE:(hd + 1) * HEAD_W]
        ss = _rowsum(lo * lo) + 0.5 * _rowsum(up * up)
        r = lax.rsqrt(ss * (1.0 / QK_DIM) + EPS)
        q_ref[:, hd * HEAD_W:hd * HEAD_W + QK_NOPE] = (lo * r * g_nope).astype(BF16)
        q_ref[:, hd * HEAD_W + QK_NOPE:(hd + 1) * HEAD_W] = (up * r * tq).astype(BF16)

    kv_lat = kvk[:, :KV_LORA]
    kraw = kvk[:, KV_LORA:]
    ckv = _rms(kv_lat) * kvg_ref[...]
    if cache_refs:
        ckv_out_ref, kr_out_ref = cache_refs
        ckv_out_ref[...] = ckv
        kr_out_ref[...] = kraw[:, :QK_ROPE]
    _emit_kv(ckv, kraw, cs, w_ukv_ref, gk_ref, k_ref, v_ref)


def _inproj(x2d, mod3, n1g, w_in_b, qlg, w_uq_b, kvg, w_ukv_b, gq, gk, cs, *,
            cond_base, tiles_per_cond, cs_tiles, emit_cache):
    t = x2d.shape[0]
    tm = TM_IN
    out_shape = [jax.ShapeDtypeStruct((t, CONV_WIDTH), BF16),
                 jax.ShapeDtypeStruct((t, CONV_WIDTH), BF16),
                 jax.ShapeDtypeStruct((t, N_HEADS * HEAD_W), BF16),
                 jax.ShapeDtypeStruct((t, N_HEADS * HEAD_W), BF16),
                 jax.ShapeDtypeStruct((t, N_HEADS * V_DIM), BF16)]
    out_specs = [pl.BlockSpec((tm, CONV_WIDTH), lambda i: (i, 0)),
                 pl.BlockSpec((tm, CONV_WIDTH), lambda i: (i, 0)),
                 pl.BlockSpec((tm, N_HEADS * HEAD_W), lambda i: (i, 0)),
                 pl.BlockSpec((tm, N_HEADS * HEAD_W), lambda i: (i, 0)),
                 pl.BlockSpec((tm, N_HEADS * V_DIM), lambda i: (i, 0))]
    if emit_cache:
        out_shape += [jax.ShapeDtypeStruct((t, KV_LORA), F32), jax.ShapeDtypeStruct((t, QK_ROPE), F32)]
        out_specs += [pl.BlockSpec((tm, KV_LORA), lambda i: (i, 0)),
                      pl.BlockSpec((tm, QK_ROPE), lambda i: (i, 0))]
    return pl.pallas_call(
        _inproj_kernel,
        out_shape=tuple(out_shape),
        grid=(t // tm,),
        in_specs=[pl.BlockSpec((tm, D_MODEL), lambda i: (i, 0)),
                  pl.BlockSpec((1, N_MOD, D_MODEL), lambda i: (cond_base + i // tiles_per_cond, 0, 0)),
                  _const_spec((1, D_MODEL)),
                  _const_spec((D_MODEL, IN_COLS)),
                  _const_spec((1, Q_LORA)),
                  _const_spec((Q_LORA, N_HEADS * HEAD_W)),
                  _const_spec((1, KV_LORA)),
                  _const_spec((KV_LORA, N_HEADS * HEAD_W)),
                  _const_spec((2, LANES)),
                  _const_spec((2, LANES)),
                  pl.BlockSpec((tm, LANES), lambda i: (i % cs_tiles, 0))],
        out_specs=tuple(out_specs),
        compiler_params=_cparams(("arbitrary",)),
        name="inproj_ctx" if emit_cache else "inproj_lat",
    )(x2d, mod3, n1g, w_in_b, qlg, w_uq_b, kvg, w_ukv_b, gq, gk, cs)


def _attn_kernel(*refs, n_kv, heads):
    q_ref = refs[0]
    o_ref = refs[-1]
    for h in range(heads):
        q = q_ref[0, :, h * HEAD_W:(h + 1) * HEAD_W]
        scores = []
        for j in range(n_kv):
            k = refs[1 + 2 * j][0, :, h * HEAD_W:(h + 1) * HEAD_W]
            scores.append(lax.dot_general(q, k, (((1,), (1,)), ((), ())), preferred_element_type=F32))
        m = jnp.max(scores[0], axis=-1, keepdims=True)
        for s in scores[1:]:
            m = jnp.maximum(m, jnp.max(s, axis=-1, keepdims=True))
        l = None
        o = None
        for j in range(n_kv):
            p = jnp.exp(scores[j] - m)
            lj = _rowsum(p)
            v = refs[2 + 2 * j][0, :, h * V_DIM:(h + 1) * V_DIM]
            oj = jnp.dot(p.astype(BF16), v, preferred_element_type=F32)
            l = lj if l is None else l + lj
            o = oj if o is None else o + oj
        o_ref[0, :, h * V_DIM:(h + 1) * V_DIM] = (o / l).astype(BF16)


def _attention(q, kvs, *, tq, heads, name):
    b, s, _ = q.shape
    in_specs = [pl.BlockSpec((1, tq, heads * HEAD_W), lambda bi, hi, qi: (bi, qi, hi))]
    args = [q]
    for k, v in kvs:
        sk = k.shape[1]
        in_specs.append(pl.BlockSpec((1, sk, heads * HEAD_W), lambda bi, hi, qi: (bi, 0, hi)))
        in_specs.append(pl.BlockSpec((1, sk, heads * V_DIM), lambda bi, hi, qi: (bi, 0, hi)))
        args += [k, v]
    return pl.pallas_call(
        functools.partial(_attn_kernel, n_kv=len(kvs), heads=heads),
        out_shape=jax.ShapeDtypeStruct((b, s, N_HEADS * V_DIM), BF16),
        grid=(b, N_HEADS // heads, s // tq),
        in_specs=in_specs,
        out_specs=pl.BlockSpec((1, tq, heads * V_DIM), lambda bi, hi, qi: (bi, qi, hi)),
        compiler_params=_cparams(("arbitrary", "arbitrary", "arbitrary")),
        name=name,
    )(*args)


def _outproj_kernel(x_ref, bg_ref, cu_ref, cup_ref, cun_ref, o_ref, mod_ref, cw_ref, w_out_ref,
                    n2g_ref, wr_ref, x1_ref, h2w_ref, rid_ref, rw_ref, *, tm, seq_len):
    i = pl.program_id(0)
    mod = mod_ref[0]
    cu = cu_ref[...].astype(F32)
    prev_row = cup_ref[...].astype(F32)[15:16, :]
    next_row = cun_ref[...].astype(F32)[0:1, :]
    row = lax.broadcasted_iota(jnp.int32, (tm, 1), 0)
    pos = (i * tm + row) & (seq_len - 1)
    up = jnp.where(row == 0, prev_row, pltpu.roll(cu, 1, axis=0))
    up = jnp.where(pos == 0, 0.0, up)
    dn = jnp.where(row == tm - 1, next_row, pltpu.roll(cu, tm - 1, axis=0))
    dn = jnp.where(pos == seq_len - 1, 0.0, dn)
    cw = cw_ref[...]
    y_conv = bg_ref[...].astype(F32) * (up * cw[0:1, :] + cu * cw[1:2, :] + dn * cw[2:3, :])
    mix = jnp.dot(y_conv.astype(BF16), w_out_ref[:CONV_WIDTH, :], preferred_element_type=F32)
    mix = mix + jnp.dot(o_ref[...], w_out_ref[CONV_WIDTH:, :], preferred_element_type=F32)
    x1 = x_ref[...] + mod[2:3, :] * mix
    x1_ref[...] = x1
    h2 = _rms(x1) * n2g_ref[...] * (1.0 + mod[4:5, :]) + mod[3:4, :]

    bits = pltpu.bitcast(h2.astype(BF16).astype(F32), jnp.uint32)
    half = D_MODEL // 2
    h2w_ref[...] = (bits[:, :half] >> 16) | bits[:, half:]

    logits = jnp.dot(h2, wr_ref[...], precision=HIGHEST, preferred_element_type=F32)
    lane = lax.broadcasted_iota(jnp.int32, logits.shape, 1)
    neg = -jnp.inf
    big = jnp.int32(1 << 20)
    gl = jnp.where(lane < N_EXPERT_GROUPS, logits, neg)
    gmax = jnp.max(gl, axis=-1, keepdims=True)
    p_top = 1.0 / _rowsum(jnp.exp(gl - gmax))
    g_top = jnp.min(jnp.where(gl == gmax, lane, big), axis=-1, keepdims=True)
    e_lo = N_EXPERT_GROUPS + EXPERTS_PER_GROUP * g_top
    el = jnp.where((lane >= e_lo) & (lane < e_lo + EXPERTS_PER_GROUP), logits, neg)
    v1 = jnp.max(el, axis=-1, keepdims=True)
    i1 = jnp.min(jnp.where(el == v1, lane, big), axis=-1, keepdims=True)
    el2 = jnp.where(lane == i1, neg, el)
    v2 = jnp.max(el2, axis=-1, keepdims=True)
    i2 = jnp.min(jnp.where(el2 == v2, lane, big), axis=-1, keepdims=True)
    e21 = jnp.exp(v2 - v1)
    w1 = p_top / (1.0 + e21)
    w2 = w1 * e21
    rid_ref[...] = jnp.where(lane == 0, i1 - N_EXPERT_GROUPS, i2 - N_EXPERT_GROUPS)
    rw_ref[...] = jnp.where(lane == 0, w1, w2)


def _outproj(x2d, bg, cu, o2d, mod3, conv_w, w_out_b, n2g, wr, *, cond_base, tiles_per_cond, seq_len):
    t = x2d.shape[0]
    tm = TM_OUT
    hb = tm // 16
    nhb = t // 16
    return pl.pallas_call(
        functools.partial(_outproj_kernel, tm=tm, seq_len=seq_len),
        out_shape=(jax.ShapeDtypeStruct((t, D_MODEL), F32),
                   jax.ShapeDtypeStruct((t, D_MODEL // 2), jnp.uint32),
                   jax.ShapeDtypeStruct((t, LANES), jnp.int32),
                   jax.ShapeDtypeStruct((t, LANES), F32)),
        grid=(t // tm,),
        in_specs=[pl.BlockSpec((tm, D_MODEL), lambda i: (i, 0)),
                  pl.BlockSpec((tm, CONV_WIDTH), lambda i: (i, 0)),
                  pl.BlockSpec((tm, CONV_WIDTH), lambda i: (i, 0)),
                  pl.BlockSpec((16, CONV_WIDTH), lambda i: (jnp.maximum(i * hb - 1, 0), 0)),
                  pl.BlockSpec((16, CONV_WIDTH), lambda i: (jnp.minimum((i + 1) * hb, nhb - 1), 0)),
                  pl.BlockSpec((tm, N_HEADS * V_DIM), lambda i: (i, 0)),
                  pl.BlockSpec((1, N_MOD, D_MODEL), lambda i: (cond_base + i // tiles_per_cond, 0, 0)),
                  _const_spec((3, CONV_WIDTH)),
                  _const_spec((D_MODEL, D_MODEL)),
                  _const_spec((1, D_MODEL)),
                  _const_spec((D_MODEL, ROUTER_COLS))],
        out_specs=(pl.BlockSpec((tm, D_MODEL), lambda i: (i, 0)),
                   pl.BlockSpec((tm, D_MODEL // 2), lambda i: (i, 0)),
                   pl.BlockSpec((tm, LANES), lambda i: (i, 0)),
                   pl.BlockSpec((tm, LANES), lambda i: (i, 0))),
        compiler_params=_cparams(("arbitrary",)),
        name="outproj_ctx" if cond_base == 0 else "outproj_lat",
    )(x2d, bg, cu, cu, cu, o2d, mod3, conv_w, w_out_b, n2g, wr)


def _moe_kernel(texp_ref, nval_ref, tab_ref, rw_ref, h2w_hbm, wg_ref, wu_ref, wd_ref, y_hbm,
                xbuf, ybuf, wgb, wub, wdb, sem_g, sem_s, *, tm, n_pairs):
    i = pl.program_id(0)

    @pl.when(i < nval_ref[0])
    def _():
        def gather(r, c):
            tok = jnp.maximum(tab_ref[0, 0, r], 0) >> 1
            pltpu.make_async_copy(h2w_hbm.at[pl.ds(tok, 1)], xbuf.at[pl.ds(r, 1)], sem_g).start()
            return c
        lax.fori_loop(0, tm, gather, 0)

        e = texp_ref[i]
        e_prev = texp_ref[jnp.maximum(i - 1, 0)]

        @pl.when((i == 0) | (e != e_prev))
        def _():
            wgb[...] = wg_ref[0].astype(BF16)
            wub[...] = wu_ref[0].astype(BF16)
            wdb[...] = wd_ref[0].astype(BF16)

        pltpu.make_async_copy(h2w_hbm.at[pl.ds(0, tm)], xbuf, sem_g).wait()

        xw = xbuf[...]
        x_lo = pltpu.bitcast(xw << 16, F32).astype(BF16)
        x_hi = pltpu.bitcast(xw & jnp.uint32(0xFFFF0000), F32).astype(BF16)
        half = D_MODEL // 2
        g = (jnp.dot(x_lo, wgb[:half, :], preferred_element_type=F32)
             + jnp.dot(x_hi, wgb[half:, :], preferred_element_type=F32))
        u = (jnp.dot(x_lo, wub[:half, :], preferred_element_type=F32)
             + jnp.dot(x_hi, wub[half:, :], preferred_element_type=F32))
        a = g / (1.0 + jnp.exp(-g)) * u
        y = jnp.dot(a.astype(BF16), wdb[...], preferred_element_type=F32)
        ybuf[...] = y * rw_ref[...]

        def row_copy(r, v):
            return pltpu.make_async_copy(ybuf.at[pl.ds(r, 1)], y_hbm.at[pl.ds(v, 1)], sem_s)

        def scatter(r, c):
            v = tab_ref[0, 0, r]

            @pl.when(v >= 0)
            def _():
                row_copy(r, v).start()
            return c

        def drain(r, c):
            v = tab_ref[0, 0, r]

            @pl.when(v >= 0)
            def _():
                row_copy(r, v).wait()
            return c
        lax.fori_loop(0, tm, scatter, 0)
        lax.fori_loop(0, tm, drain, 0)


def _moe(tile_expert, n_valid, tab, row_w, h2w, w_gate, w_up, w_down, *, n_pairs):
    nt = tile_expert.shape[0]
    tm = TM_MOE
    grid_spec = pltpu.PrefetchScalarGridSpec(
        num_scalar_prefetch=2,
        grid=(nt,),
        in_specs=[pl.BlockSpec((1, 1, tm), lambda i, te, nv: (i, 0, 0), memory_space=pltpu.SMEM),
                  pl.BlockSpec((tm, 1), lambda i, te, nv: (i, 0)),
                  pl.BlockSpec(memory_space=pl.ANY),
                  pl.BlockSpec((1, D_MODEL, D_EXPERT), lambda i, te, nv: (te[i], 0, 0)),
                  pl.BlockSpec((1, D_MODEL, D_EXPERT), lambda i, te, nv: (te[i], 0, 0)),
                  pl.BlockSpec((1, D_EXPERT, D_MODEL), lambda i, te, nv: (te[i], 0, 0))],
        out_specs=pl.BlockSpec(memory_space=pl.ANY),
        scratch_shapes=[pltpu.VMEM((tm, D_MODEL // 2), jnp.uint32),
                        pltpu.VMEM((tm, D_MODEL), F32),
                        pltpu.VMEM((D_MODEL, D_EXPERT), BF16),
                        pltpu.VMEM((D_MODEL, D_EXPERT), BF16),
                        pltpu.VMEM((D_EXPERT, D_MODEL), BF16),
                        pltpu.SemaphoreType.DMA,
                        pltpu.SemaphoreType.DMA])
    return pl.pallas_call(
        functools.partial(_moe_kernel, tm=tm, n_pairs=n_pairs),
        out_shape=jax.ShapeDtypeStruct((n_pairs, D_MODEL), F32),
        grid_spec=grid_spec,
        compiler_params=pltpu.CompilerParams(dimension_semantics=("arbitrary",),
                                             vmem_limit_bytes=VMEM_LIMIT, has_side_effects=True),
        name="moe",
    )(tile_expert, n_valid, tab, row_w, h2w, w_gate, w_up, w_down)


def _route_tables(rid, rw, *, tm):
    n_pairs = rid.shape[0]
    nt = n_pairs // tm + N_EXPERTS
    onehot = (rid[:, None] == jnp.arange(N_EXPERTS, dtype=jnp.int32)[None, :]).astype(jnp.int32)
    counts = jnp.sum(onehot, axis=0)
    order = jnp.argsort(rid, stable=True).astype(jnp.int32)
    start = jnp.cumsum(counts) - counts
    tiles_e = (counts + tm - 1) // tm
    tile_end = jnp.cumsum(tiles_e)
    tile_start = tile_end - tiles_e
    n_valid = tile_end[-1]
    tile_idx = jnp.arange(nt, dtype=jnp.int32)
    te = jnp.searchsorted(tile_end, jnp.minimum(tile_idx, n_valid - 1), side="right").astype(jnp.int32)
    row = jnp.arange(nt * tm, dtype=jnp.int32)
    e_row = jnp.repeat(te, tm)
    local = row - jnp.repeat(tile_start[te], tm) * tm
    valid = (local < counts[e_row]) & (jnp.repeat(tile_idx, tm) < n_valid)
    src = order[jnp.clip(start[e_row] + local, 0, n_pairs - 1)]
    tab = jnp.where(valid, src, -1).astype(jnp.int32)
    row_w = jnp.where(valid, rw[jnp.maximum(tab, 0)], 0.0).astype(F32)
    return te, n_valid.reshape(1).astype(jnp.int32), tab.reshape(nt, 1, tm), row_w.reshape(nt * tm, 1)


def _final_kernel(x1_ref, y_ref, mod_ref, o_ref):
    y = y_ref[...]
    o_ref[...] = x1_ref[...] + mod_ref[0][5:6, :] * (y[:, :D_MODEL] + y[:, D_MODEL:])


def _final(x1, y2, mod3, *, row_base, cond_base, tiles_per_cond):
    t = x1.shape[0]
    tm = TM_FIN
    tb = row_base // tm
    return pl.pallas_call(
        _final_kernel,
        out_shape=jax.ShapeDtypeStruct((t, D_MODEL), F32),
        grid=(t // tm,),
        in_specs=[pl.BlockSpec((tm, D_MODEL), lambda i: (i, 0)),
                  pl.BlockSpec((tm, 2 * D_MODEL), lambda i: (tb + i, 0)),
                  pl.BlockSpec((1, N_MOD, D_MODEL), lambda i: (cond_base + i // tiles_per_cond, 0, 0))],
        out_specs=pl.BlockSpec((tm, D_MODEL), lambda i: (i, 0)),
        compiler_params=_cparams(("arbitrary",)),
        name="final_ctx" if cond_base == 0 else "final_lat",
    )(x1, y2, mod3)


def _rope_table(n_tokens):
    rows = n_tokens // GRID_W
    row = jnp.repeat(jnp.arange(rows, dtype=F32), GRID_W)
    col = jnp.tile(jnp.arange(GRID_W, dtype=F32), rows)
    inv = ROPE_BASE ** (-jnp.arange(ROPE_PAIRS, dtype=F32) / ROPE_PAIRS)
    cr, sr = jnp.cos(row[:, None] * inv), jnp.sin(row[:, None] * inv)
    cc, sc = jnp.cos(col[:, None] * inv), jnp.sin(col[:, None] * inv)
    return jnp.concatenate([cr, cr, cc, cc, -sr, sr, -sc, sc], axis=-1)


def _swap_halves(a):
    p = ROPE_PAIRS
    return jnp.concatenate([a[..., p:2 * p], a[..., :p], a[..., 3 * p:], a[..., 2 * p:3 * p]], axis=-1)


def _head_gains(g):
    rope = g[QK_NOPE:]
    return jnp.stack([g[:QK_NOPE], jnp.concatenate([rope, _swap_halves(rope)])])


def kernel(x_prompt, x_sample, cache_ckv, cache_krope, c, c_ctx, ada_w, ada_b, norm1_g, w_in, conv_w,
           q_lora_g, w_uq, kv_lora_g, w_ukv, q_head_g, k_head_g, w_out, norm2_g, router_g, router_e,
           w_gate, w_up, w_down):
    depth = ada_w.shape[0]
    assert depth == 1
    bp, sp, _ = x_prompt.shape
    bs, ss, _ = x_sample.shape
    past = cache_ckv.shape[2]
    tp, ts = bp * sp, bs * ss
    assert sp & (sp - 1) == 0 and ss & (ss - 1) == 0
    assert tp % TM_IN == 0 and ss % TM_IN == 0 and TM_IN % sp == 0 and ss % TQ == 0
    assert bs + 1 <= 8 and past % 256 == 0 and (2 * (tp + ts)) % TM_MOE == 0

    w_in_b = jnp.concatenate([w_in[0], _swap_halves(w_in[0][:, IN_COLS - 2 * QK_ROPE:])], axis=1).astype(BF16)
    uq = w_uq[0].reshape(Q_LORA, N_HEADS, QK_DIM)
    w_uq_b = jnp.concatenate([uq, _swap_halves(uq[..., QK_NOPE:])], axis=-1)
    w_uq_b = w_uq_b.reshape(Q_LORA, N_HEADS * HEAD_W).astype(BF16)
    w_ukv_b = w_ukv[0].astype(BF16)
    w_out_b = w_out[0].astype(BF16)
    wr = jnp.concatenate([router_g[0], router_e[0],
                          jnp.zeros((D_MODEL, ROUTER_COLS - N_EXPERT_GROUPS - N_EXPERTS), F32)], axis=1)
    gq, gk = _head_gains(q_head_g[0]), _head_gains(k_head_g[0])
    n1g, n2g = norm1_g[0].reshape(1, D_MODEL), norm2_g[0].reshape(1, D_MODEL)
    qlg, kvg = q_lora_g[0].reshape(1, Q_LORA), kv_lora_g[0].reshape(1, KV_LORA)
    cs_lat = _rope_table(ss)
    cs_id = jnp.concatenate([jnp.ones((TM_IN, QK_ROPE), F32), jnp.zeros((TM_IN, QK_ROPE), F32)], axis=1)

    cond8 = jnp.concatenate([c_ctx[None, :], c, jnp.zeros((8 - 1 - bs, D_MODEL), F32)], axis=0)
    mod3 = _modulation(cond8, ada_w[0], ada_b[0]).reshape(8, N_MOD, D_MODEL)

    xp2, xs2 = x_prompt.reshape(tp, D_MODEL), x_sample.reshape(ts, D_MODEL)
    big = 1 << 30

    bg_p, cu_p, q_p, k_p, v_p, ckv_p, kr_p = _inproj(
        xp2, mod3, n1g, w_in_b, qlg, w_uq_b, kvg, w_ukv_b, gq, gk, cs_id,
        cond_base=0, tiles_per_cond=big, cs_tiles=1, emit_cache=True)
    o_p = _attention(q_p.reshape(bp, sp, -1), [(k_p.reshape(bp, sp, -1), v_p.reshape(bp, sp, -1))],
                     tq=sp, heads=N_HEADS, name="attn_ctx")
    x1_p, h2w_p, rid_p, rw_p = _outproj(
        xp2, bg_p, cu_p, o_p.reshape(tp, -1), mod3, conv_w[0], w_out_b, n2g, wr,
        cond_base=0, tiles_per_cond=big, seq_len=sp)

    kr_c = cache_krope[:, 0].reshape(bs * past, QK_ROPE)
    k_c, v_c = _kvcache(cache_ckv[:, 0].reshape(bs * past, KV_LORA), jnp.concatenate([kr_c, kr_c], axis=1),
                        cs_id, w_ukv_b, gk)
    bg_s, cu_s, q_s, k_s, v_s = _inproj(
        xs2, mod3, n1g, w_in_b, qlg, w_uq_b, kvg, w_ukv_b, gq, gk, cs_lat,
        cond_base=1, tiles_per_cond=ss // TM_IN, cs_tiles=ss // TM_IN, emit_cache=False)
    o_s = _attention(q_s.reshape(bs, ss, -1),
                     [(k_c.reshape(bs, past, -1), v_c.reshape(bs, past, -1)),
                      (k_s.reshape(bs, ss, -1), v_s.reshape(bs, ss, -1))],
                     tq=TQ, heads=1, name="attn_lat")
    x1_s, h2w_s, rid_s, rw_s = _outproj(
        xs2, bg_s, cu_s, o_s.reshape(ts, -1), mod3, conv_w[0], w_out_b, n2g, wr,
        cond_base=1, tiles_per_cond=ss // TM_OUT, seq_len=ss)

    n_pairs = 2 * (tp + ts)
    rid = jnp.concatenate([rid_p[:, :2], rid_s[:, :2]], axis=0).reshape(n_pairs)
    rw = jnp.concatenate([rw_p[:, :2], rw_s[:, :2]], axis=0).reshape(n_pairs)
    h2w = jnp.concatenate([h2w_p, h2w_s], axis=0)
    te, n_valid, tab, row_w = _route_tables(rid, rw, tm=TM_MOE)
    y = _moe(te, n_valid, tab, row_w, h2w, w_gate[0], w_up[0], w_down[0], n_pairs=n_pairs)
    y2 = y.reshape(n_pairs // 2, 2 * D_MODEL)

    y_p = _final(x1_p, y2, mod3, row_base=0, cond_base=0, tiles_per_cond=big)
    y_s = _final(x1_s, y2, mod3, row_base=tp, cond_base=1, tiles_per_cond=ss // TM_FIN)

    return (y_p.reshape(bp, sp, D_MODEL), y_s.reshape(bs, ss, D_MODEL),
            ckv_p.reshape(bp, 1, sp, KV_LORA), kr_p.reshape(bp, 1, sp, QK_ROPE))
```

```python
import functools

import jax
import jax.numpy as jnp
from jax import lax
from jax.experimental import pallas as pl
from jax.experimental.pallas import tpu as pltpu

F32 = jnp.float32
BF16 = jnp.bfloat16
HIGHEST = lax.Precision.HIGHEST

D_MODEL = 2048
CONV_WIDTH = 1024
N_HEADS = 8
QK_NOPE = 128
QK_ROPE = 64
V_DIM = 128
QK_DIM = QK_NOPE + QK_ROPE
Q_LORA = 512
KV_LORA = 256
GRID_W = 64
ROPE_PAIRS = QK_ROPE // 4
ROPE_BASE = 10000.0
N_EXPERT_GROUPS = 4
EXPERTS_PER_GROUP = 8
N_EXPERTS = N_EXPERT_GROUPS * EXPERTS_PER_GROUP
D_EXPERT = 512
N_MOD = 6
EPS = 1e-6

HEAD_W = 2 * QK_NOPE
LANES = 128
IN_COLS = 3 * CONV_WIDTH + Q_LORA + KV_LORA + 2 * QK_ROPE
ROUTER_COLS = LANES
VMEM_LIMIT = 56 * 1024 * 1024

TM_IN = 512
TM_OUT = 512
TQ = 256
TM_MOE = 256
TM_FIN = 512
BN_MOD = 1024


def _cparams(sem):
    return pltpu.CompilerParams(dimension_semantics=sem, vmem_limit_bytes=VMEM_LIMIT)


def _const_spec(shape):
    nd = len(shape)
    return pl.BlockSpec(shape, lambda *_: (0,) * nd, pipeline_mode=pl.Buffered(1))


def _rms(x):
    return x * lax.rsqrt(jnp.mean(x * x, axis=-1, keepdims=True) + EPS)


def _rowsum(x):
    return jnp.sum(x, axis=-1, keepdims=True)


def _mod_kernel(c_ref, w_ref, b_ref, o_ref):
    c = c_ref[...]
    s = c / (1.0 + jnp.exp(-c))
    o_ref[...] = jnp.dot(s, w_ref[...], precision=HIGHEST, preferred_element_type=F32) + b_ref[...]


def _modulation(cond8, ada_w, ada_b):
    n = ada_w.shape[1]
    return pl.pallas_call(
        _mod_kernel,
        out_shape=jax.ShapeDtypeStruct((8, n), F32),
        grid=(n // BN_MOD,),
        in_specs=[pl.BlockSpec((8, D_MODEL), lambda j: (0, 0)),
                  pl.BlockSpec((D_MODEL, BN_MOD), lambda j: (0, j)),
                  pl.BlockSpec((1, BN_MOD), lambda j: (0, j))],
        out_specs=pl.BlockSpec((8, BN_MOD), lambda j: (0, j)),
        compiler_params=_cparams(("arbitrary",)),
        name="mod",
    )(cond8, ada_w, ada_b.reshape(1, n))


def _emit_kv(ckv, kraw, cs, w_ukv_ref, gk_ref, k_ref, v_ref):
    kv = jnp.dot(ckv.astype(BF16), w_ukv_ref[...], preferred_element_type=F32)
    ss_rope = 0.5 * _rowsum(kraw * kraw)
    t = kraw * (cs * gk_ref[1:2, :])
    tt = t + pltpu.roll(t, QK_ROPE, axis=1)
    g_nope = gk_ref[0:1, :]
    for h in range(N_HEADS):
        kn = kv[:, h * HEAD_W:h * HEAD_W + QK_NOPE]
        r = lax.rsqrt((_rowsum(kn * kn) + ss_rope) * (1.0 / QK_DIM) + EPS)
        k_ref[:, h * HEAD_W:h * HEAD_W + QK_NOPE] = (kn * r * g_nope).astype(BF16)
        k_ref[:, h * HEAD_W + QK_NOPE:(h + 1) * HEAD_W] = (tt * r).astype(BF16)
        v_ref[:, h * V_DIM:(h + 1) * V_DIM] = kv[:, h * HEAD_W + QK_NOPE:(h + 1) * HEAD_W].astype(BF16)


def _kvcache_kernel(ckv_ref, kraw_ref, cs_ref, w_ukv_ref, gk_ref, k_ref, v_ref):
    _emit_kv(ckv_ref[...], kraw_ref[...], cs_ref[...], w_ukv_ref, gk_ref, k_ref, v_ref)


def _kvcache(ckv, kraw, cs_id, w_ukv_b, gk):
    n = ckv.shape[0]
    tm = 256
    return pl.pallas_call(
        _kvcache_kernel,
        out_shape=(jax.ShapeDtypeStruct((n, N_HEADS * HEAD_W), BF16),
                   jax.ShapeDtypeStruct((n, N_HEADS * V_DIM), BF16)),
        grid=(n // tm,),
        in_specs=[pl.BlockSpec((tm, KV_LORA), lambda i: (i, 0)),
                  pl.BlockSpec((tm, LANES), lambda i: (i, 0)),
                  pl.BlockSpec((tm, LANES), lambda i: (0, 0)),
                  _const_spec((KV_LORA, N_HEADS * HEAD_W)),
                  _const_spec((2, LANES))],
        out_specs=(pl.BlockSpec((tm, N_HEADS * HEAD_W), lambda i: (i, 0)),
                   pl.BlockSpec((tm, N_HEADS * V_DIM), lambda i: (i, 0))),
        compiler_params=_cparams(("arbitrary",)),
        name="kvcache",
    )(ckv, kraw, cs_id, w_ukv_b, gk)


def _inproj_kernel(x_ref, mod_ref, n1g_ref, w_in_ref, qlg_ref, w_uq_ref, kvg_ref, w_ukv_ref,
                   gq_ref, gk_ref, cs_ref, bg_ref, cu_ref, q_ref, k_ref, v_ref, *cache_refs):
    x = x_ref[...]
    mod = mod_ref[0]
    h = _rms(x) * n1g_ref[...] * (1.0 + mod[1:2, :]) + mod[0:1, :]
    hb = h.astype(BF16)

    def proj(a, b):
        return jnp.dot(hb, w_in_ref[:, a:b], preferred_element_type=F32)

    c1, c2, c3 = CONV_WIDTH, 2 * CONV_WIDTH, 3 * CONV_WIDTH
    bg_ref[...] = proj(0, c1).astype(BF16)
    cu_ref[...] = (proj(c1, c2) * proj(c2, c3)).astype(BF16)
    q_lat = proj(c3, c3 + Q_LORA)
    kvk = proj(c3 + Q_LORA, IN_COLS)
    cs = cs_ref[...]

    qn = _rms(q_lat) * qlg_ref[...]
    q = jnp.dot(qn.astype(BF16), w_uq_ref[...], preferred_element_type=F32)
    scale = QK_DIM ** -0.5
    g_nope = gq_ref[0:1, :] * scale
    tq = cs * (gq_ref[1:2, :] * scale)
    for hd in range(N_HEADS):
        lo = q[:, hd * HEAD_W:hd * HEAD_W + QK_NOPE]
        up = q[:, hd * HEAD_W + QK_NOPE:(hd + 1) * HEAD_W]
        ss = _rowsum(lo * lo) + 0.5 * _rowsum(up * up)
        r = lax.rsqrt(ss * (1.0 / QK_DIM) + EPS)
        q_ref[:, hd * HEAD_W:hd * HEAD_W + QK_NOPE] = (lo * r * g_nope).astype(BF16)
        q_ref[:, hd * HEAD_W + QK_NOPE:(hd + 1) * HEAD_W] = (up * r * tq).astype(BF16)

    kv_lat = kvk[:, :KV_LORA]
    kraw = kvk[:, KV_LORA:]
    ckv = _rms(kv_lat) * kvg_ref[...]
    if cache_refs:
        ckv_out_ref, kr_out_ref = cache_refs
        ckv_out_ref[...] = ckv
        kr_out_ref[...] = kraw[:, :QK_ROPE]
    _emit_kv(ckv, kraw, cs, w_ukv_ref, gk_ref, k_ref, v_ref)


def _inproj(x2d, mod3, n1g, w_in_b, qlg, w_uq_b, kvg, w_ukv_b, gq, gk, cs, *,
            cond_base, tiles_per_cond, cs_tiles, emit_cache):
    t = x2d.shape[0]
    tm = TM_IN
    out_shape = [jax.ShapeDtypeStruct((t, CONV_WIDTH), BF16),
                 jax.ShapeDtypeStruct((t, CONV_WIDTH), BF16),
                 jax.ShapeDtypeStruct((t, N_HEADS * HEAD_W), BF16),
                 jax.ShapeDtypeStruct((t, N_HEADS * HEAD_W), BF16),
                 jax.ShapeDtypeStruct((t, N_HEADS * V_DIM), BF16)]
    out_specs = [pl.BlockSpec((tm, CONV_WIDTH), lambda i: (i, 0)),
                 pl.BlockSpec((tm, CONV_WIDTH), lambda i: (i, 0)),
                 pl.BlockSpec((tm, N_HEADS * HEAD_W), lambda i: (i, 0)),
                 pl.BlockSpec((tm, N_HEADS * HEAD_W), lambda i: (i, 0)),
                 pl.BlockSpec((tm, N_HEADS * V_DIM), lambda i: (i, 0))]
    if emit_cache:
        out_shape += [jax.ShapeDtypeStruct((t, KV_LORA), F32), jax.ShapeDtypeStruct((t, QK_ROPE), F32)]
        out_specs += [pl.BlockSpec((tm, KV_LORA), lambda i: (i, 0)),
                      pl.BlockSpec((tm, QK_ROPE), lambda i: (i, 0))]
    return pl.pallas_call(
        _inproj_kernel,
        out_shape=tuple(out_shape),
        grid=(t // tm,),
        in_specs=[pl.BlockSpec((tm, D_MODEL), lambda i: (i, 0)),
                  pl.BlockSpec((1, N_MOD, D_MODEL), lambda i: (cond_base + i // tiles_per_cond, 0, 0)),
                  _const_spec((1, D_MODEL)),
                  _const_spec((D_MODEL, IN_COLS)),
                  _const_spec((1, Q_LORA)),
                  _const_spec((Q_LORA, N_HEADS * HEAD_W)),
                  _const_spec((1, KV_LORA)),
                  _const_spec((KV_LORA, N_HEADS * HEAD_W)),
                  _const_spec((2, LANES)),
                  _const_spec((2, LANES)),
                  pl.BlockSpec((tm, LANES), lambda i: (i % cs_tiles, 0))],
        out_specs=tuple(out_specs),
        compiler_params=_cparams(("arbitrary",)),
        name="inproj_ctx" if emit_cache else "inproj_lat",
    )(x2d, mod3, n1g, w_in_b, qlg, w_uq_b, kvg, w_ukv_b, gq, gk, cs)


def _attn_kernel(*refs, n_kv, heads):
    q_ref = refs[0]
    o_ref = refs[-1]
    for h in range(heads):
        q = q_ref[0, :, h * HEAD_W:(h + 1) * HEAD_W]
        scores = []
        for j in range(n_kv):
            k = refs[1 + 2 * j][0, :, h * HEAD_W:(h + 1) * HEAD_W]
            scores.append(lax.dot_general(q, k, (((1,), (1,)), ((), ())), preferred_element_type=F32))
        m = jnp.max(scores[0], axis=-1, keepdims=True)
        for s in scores[1:]:
            m = jnp.maximum(m, jnp.max(s, axis=-1, keepdims=True))
        l = None
        o = None
        for j in range(n_kv):
            p = jnp.exp(scores[j] - m)
            lj = _rowsum(p)
            v = refs[2 + 2 * j][0, :, h * V_DIM:(h + 1) * V_DIM]
            oj = jnp.dot(p.astype(BF16), v, preferred_element_type=F32)
            l = lj if l is None else l + lj
            o = oj if o is None else o + oj
        o_ref[0, :, h * V_DIM:(h + 1) * V_DIM] = (o / l).astype(BF16)


def _attention(q, kvs, *, tq, heads, name):
    b, s, _ = q.shape
    in_specs = [pl.BlockSpec((1, tq, heads * HEAD_W), lambda bi, hi, qi: (bi, qi, hi))]
    args = [q]
    for k, v in kvs:
        sk = k.shape[1]
        in_specs.append(pl.BlockSpec((1, sk, heads * HEAD_W), lambda bi, hi, qi: (bi, 0, hi)))
        in_specs.append(pl.BlockSpec((1, sk, heads * V_DIM), lambda bi, hi, qi: (bi, 0, hi)))
        args += [k, v]
    return pl.pallas_call(
        functools.partial(_attn_kernel, n_kv=len(kvs), heads=heads),
        out_shape=jax.ShapeDtypeStruct((b, s, N_HEADS * V_DIM), BF16),
        grid=(b, N_HEADS // heads, s // tq),
        in_specs=in_specs,
        out_specs=pl.BlockSpec((1, tq, heads * V_DIM), lambda bi, hi, qi: (bi, qi, hi)),
        compiler_params=_cparams(("arbitrary", "arbitrary", "arbitrary")),
        name=name,
    )(*args)


def _outproj_kernel(x_ref, bg_ref, cu_ref, cup_ref, cun_ref, o_ref, mod_ref, cw_ref, w_out_ref,
                    n2g_ref, wr_ref, x1_ref, h2w_ref, rid_ref, rw_ref, *, tm, seq_len):
    i = pl.program_id(0)
    mod = mod_ref[0]
    cu = cu_ref[...].astype(F32)
    prev_row = cup_ref[...].astype(F32)[15:16, :]
    next_row = cun_ref[...].astype(F32)[0:1, :]
    row = lax.broadcasted_iota(jnp.int32, (tm, 1), 0)
    pos = (i * tm + row) & (seq_len - 1)
    up = jnp.where(row == 0, prev_row, pltpu.roll(cu, 1, axis=0))
    up = jnp.where(pos == 0, 0.0, up)
    dn = jnp.where(row == tm - 1, next_row, pltpu.roll(cu, tm - 1, axis=0))
    dn = jnp.where(pos == seq_len - 1, 0.0, dn)
    cw = cw_ref[...]
    y_conv = bg_ref[...].astype(F32) * (up * cw[0:1, :] + cu * cw[1:2, :] + dn * cw[2:3, :])
    mix = jnp.dot(y_conv.astype(BF16), w_out_ref[:CONV_WIDTH, :], preferred_element_type=F32)
    mix = mix + jnp.dot(o_ref[...], w_out_ref[CONV_WIDTH:, :], preferred_element_type=F32)
    x1 = x_ref[...] + mod[2:3, :] * mix
    x1_ref[...] = x1
    h2 = _rms(x1) * n2g_ref[...] * (1.0 + mod[4:5, :]) + mod[3:4, :]

    bits = pltpu.bitcast(h2.astype(BF16).astype(F32), jnp.uint32)
    half = D_MODEL // 2
    h2w_ref[...] = (bits[:, :half] >> 16) | bits[:, half:]

    logits = jnp.dot(h2, wr_ref[...], precision=HIGHEST, preferred_element_type=F32)
    lane = lax.broadcasted_iota(jnp.int32, logits.shape, 1)
    neg = -jnp.inf
    big = jnp.int32(1 << 20)
    gl = jnp.where(lane < N_EXPERT_GROUPS, logits, neg)
    gmax = jnp.max(gl, axis=-1, keepdims=True)
    p_top = 1.0 / _rowsum(jnp.exp(gl - gmax))
    g_top = jnp.min(jnp.where(gl == gmax, lane, big), axis=-1, keepdims=True)
    e_lo = N_EXPERT_GROUPS + EXPERTS_PER_GROUP * g_top
    el = jnp.where((lane >= e_lo) & (lane < e_lo + EXPERTS_PER_GROUP), logits, neg)
    v1 = jnp.max(el, axis=-1, keepdims=True)
    i1 = jnp.min(jnp.where(el == v1, lane, big), axis=-1, keepdims=True)
    el2 = jnp.where(lane == i1, neg, el)
    v2 = jnp.max(el2, axis=-1, keepdims=True)
    i2 = jnp.min(jnp.where(el2 == v2, lane, big), axis=-1, keepdims=True)
    e21 = jnp.exp(v2 - v1)
    w1 = p_top / (1.0 + e21)
    w2 = w1 * e21
    rid_ref[...] = jnp.where(lane == 0, i1 - N_EXPERT_GROUPS, i2 - N_EXPERT_GROUPS)
    rw_ref[...] = jnp.where(lane == 0, w1, w2)


def _outproj(x2d, bg, cu, o2d, mod3, conv_w, w_out_b, n2g, wr, *, cond_base, tiles_per_cond, seq_len):
    t = x2d.shape[0]
    tm = TM_OUT
    hb = tm // 16
    nhb = t // 16
    return pl.pallas_call(
        functools.partial(_outproj_kernel, tm=tm, seq_len=seq_len),
        out_shape=(jax.ShapeDtypeStruct((t, D_MODEL), F32),
                   jax.ShapeDtypeStruct((t, D_MODEL // 2), jnp.uint32),
                   jax.ShapeDtypeStruct((t, LANES), jnp.int32),
                   jax.ShapeDtypeStruct((t, LANES), F32)),
        grid=(t // tm,),
        in_specs=[pl.BlockSpec((tm, D_MODEL), lambda i: (i, 0)),
                  pl.BlockSpec((tm, CONV_WIDTH), lambda i: (i, 0)),
                  pl.BlockSpec((tm, CONV_WIDTH), lambda i: (i, 0)),
                  pl.BlockSpec((16, CONV_WIDTH), lambda i: (jnp.maximum(i * hb - 1, 0), 0)),
                  pl.BlockSpec((16, CONV_WIDTH), lambda i: (jnp.minimum((i + 1) * hb, nhb - 1), 0)),
                  pl.BlockSpec((tm, N_HEADS * V_DIM), lambda i: (i, 0)),
                  pl.BlockSpec((1, N_MOD, D_MODEL), lambda i: (cond_base + i // tiles_per_cond, 0, 0)),
                  _const_spec((3, CONV_WIDTH)),
                  _const_spec((D_MODEL, D_MODEL)),
                  _const_spec((1, D_MODEL)),
                  _const_spec((D_MODEL, ROUTER_COLS))],
        out_specs=(pl.BlockSpec((tm, D_MODEL), lambda i: (i, 0)),
                   pl.BlockSpec((tm, D_MODEL // 2), lambda i: (i, 0)),
                   pl.BlockSpec((tm, LANES), lambda i: (i, 0)),
                   pl.BlockSpec((tm, LANES), lambda i: (i, 0))),
        compiler_params=_cparams(("arbitrary",)),
        name="outproj_ctx" if cond_base == 0 else "outproj_lat",
    )(x2d, bg, cu, cu, cu, o2d, mod3, conv_w, w_out_b, n2g, wr)


def _moe_kernel(texp_ref, nval_ref, gtab0_ref, gtab_ref, stab_ref, rw_ref, h2w_hbm, wg_ref, wu_ref, wd_ref,
                y_hbm, xbuf, ybuf, wgb, wub, wdb, sem_g, sem_s, *, tm):
    i = pl.program_id(0)
    n_valid = nval_ref[0]
    slot = i & 1
    other = 1 - slot

    def gather_row(tab_ref, r, s):
        return pltpu.make_async_copy(h2w_hbm.at[pl.ds(tab_ref[0, 0, r], 1)], xbuf.at[s, pl.ds(r, 1)],
                                     sem_g.at[s])

    def gather_tile(s):
        return pltpu.make_async_copy(h2w_hbm.at[pl.ds(0, tm)], xbuf.at[s], sem_g.at[s])

    def scatter_row(r, s):
        return pltpu.make_async_copy(ybuf.at[s, pl.ds(r, 1)], y_hbm.at[pl.ds(stab_ref[0, 0, r], 1)],
                                     sem_s.at[s])

    def scatter_tile(s):
        return pltpu.make_async_copy(ybuf.at[s], y_hbm.at[pl.ds(0, tm)], sem_s.at[s])

    @pl.when(i == 0)
    def _():
        ybuf[1] = jnp.zeros((tm, D_MODEL), F32)
        for r in range(tm):
            gather_row(gtab0_ref, r, 0).start()

    @pl.when(i < n_valid)
    def _():
        gather_tile(slot).wait()

        @pl.when(i > 0)
        def _():
            scatter_tile(slot).wait()

        e = texp_ref[i]
        e_prev = texp_ref[jnp.maximum(i - 1, 0)]

        @pl.when((i == 0) | (e != e_prev))
        def _():
            wgb[...] = wg_ref[0].astype(BF16)
            wub[...] = wu_ref[0].astype(BF16)
            wdb[...] = wd_ref[0].astype(BF16)

        for r in range(tm):
            gather_row(gtab_ref, r, other).start()
        for r in range(tm):
            scatter_row(r, other).start()

        xw = xbuf[slot]
        x_lo = pltpu.bitcast(xw << 16, F32).astype(BF16)
        x_hi = pltpu.bitcast(xw & jnp.uint32(0xFFFF0000), F32).astype(BF16)
        half = D_MODEL // 2
        g = (jnp.dot(x_lo, wgb[:half, :], preferred_element_type=F32)
             + jnp.dot(x_hi, wgb[half:, :], preferred_element_type=F32))
        u = (jnp.dot(x_lo, wub[:half, :], preferred_element_type=F32)
             + jnp.dot(x_hi, wub[half:, :], preferred_element_type=F32))
        a = g / (1.0 + jnp.exp(-g)) * u
        y = jnp.dot(a.astype(BF16), wdb[...], preferred_element_type=F32)
        ybuf[slot] = y * rw_ref[...]

    @pl.when(i == n_valid)
    def _():
        gather_tile(slot).wait()
        scatter_tile(slot).wait()
        for r in range(tm):
            scatter_row(r, other).start()
        scatter_tile(other).wait()


def _moe(tile_expert, n_valid, gtab, stab, row_w, h2w, w_gate, w_up, w_down, *, n_pairs):
    nt = tile_expert.shape[0]
    tm = TM_MOE
    smem_tile = functools.partial(pl.BlockSpec, (1, 1, tm), memory_space=pltpu.SMEM)
    grid_spec = pltpu.PrefetchScalarGridSpec(
        num_scalar_prefetch=2,
        grid=(nt,),
        in_specs=[smem_tile(lambda i, te, nv: (0, 0, 0)),
                  smem_tile(lambda i, te, nv: (i + 1, 0, 0)),
                  smem_tile(lambda i, te, nv: (i, 0, 0)),
                  pl.BlockSpec((tm, 1), lambda i, te, nv: (i, 0)),
                  pl.BlockSpec(memory_space=pl.ANY),
                  pl.BlockSpec((1, D_MODEL, D_EXPERT), lambda i, te, nv: (te[i], 0, 0)),
                  pl.BlockSpec((1, D_MODEL, D_EXPERT), lambda i, te, nv: (te[i], 0, 0)),
                  pl.BlockSpec((1, D_EXPERT, D_MODEL), lambda i, te, nv: (te[i], 0, 0))],
        out_specs=pl.BlockSpec(memory_space=pl.ANY),
        scratch_shapes=[pltpu.VMEM((2, tm, D_MODEL // 2), jnp.uint32),
                        pltpu.VMEM((2, tm, D_MODEL), F32),
                        pltpu.VMEM((D_MODEL, D_EXPERT), BF16),
                        pltpu.VMEM((D_MODEL, D_EXPERT), BF16),
                        pltpu.VMEM((D_EXPERT, D_MODEL), BF16),
                        pltpu.SemaphoreType.DMA((2,)),
                        pltpu.SemaphoreType.DMA((2,))])
    return pl.pallas_call(
        functools.partial(_moe_kernel, tm=tm),
        out_shape=jax.ShapeDtypeStruct((n_pairs + tm, D_MODEL), F32),
        grid_spec=grid_spec,
        compiler_params=pltpu.CompilerParams(dimension_semantics=("arbitrary",),
                                             vmem_limit_bytes=VMEM_LIMIT, has_side_effects=True),
        name="moe",
    )(tile_expert, n_valid, gtab, gtab, stab, row_w, h2w, w_gate, w_up, w_down)


def _route_tables(rid, rw, *, tm, n_tok):
    n_pairs = rid.shape[0]
    nt = n_pairs // tm + N_EXPERTS
    experts = jnp.arange(N_EXPERTS, dtype=jnp.int32)
    onehot = (rid[None, :] == experts[:, None]).astype(jnp.int32)
    csum = jnp.cumsum(onehot, axis=1)
    counts = csum[:, -1]
    tiles_e = (counts + tm - 1) // tm
    tile_end = jnp.cumsum(tiles_e)
    tile_start = tile_end - tiles_e
    n_valid = tile_end[-1]
    pos = jnp.sum(onehot * (csum - 1 + tile_start[:, None] * tm), axis=0)
    pair = jnp.arange(n_pairs, dtype=jnp.int32)
    tok = jnp.where(pair >= n_tok, pair - n_tok, pair)
    rows = nt * tm
    dump = n_pairs + jnp.arange(tm, dtype=jnp.int32)
    gtab = jnp.zeros((rows + tm,), jnp.int32).at[pos].set(tok, unique_indices=True)
    stab = jnp.tile(dump, nt).at[pos].set(pair, unique_indices=True)
    stab = jnp.concatenate([dump, stab])
    row_w = jnp.zeros((rows,), F32).at[pos].set(rw, unique_indices=True)
    tile_idx = jnp.minimum(jnp.arange(nt, dtype=jnp.int32), n_valid - 1)
    te = jnp.sum((tile_end[None, :] <= tile_idx[:, None]).astype(jnp.int32), axis=1)
    return (te, n_valid.reshape(1).astype(jnp.int32), gtab.reshape(nt + 1, 1, tm),
            stab.reshape(nt + 1, 1, tm), row_w.reshape(rows, 1))


def _final_kernel(x1_ref, ya_ref, yb_ref, mod_ref, o_ref):
    o_ref[...] = x1_ref[...] + mod_ref[0][5:6, :] * (ya_ref[...] + yb_ref[...])


def _final(x1, y, mod3, *, row_base, n_tok, cond_base, tiles_per_cond):
    t = x1.shape[0]
    tm = TM_FIN
    ta, tb = row_base // tm, (n_tok + row_base) // tm
    return pl.pallas_call(
        _final_kernel,
        out_shape=jax.ShapeDtypeStruct((t, D_MODEL), F32),
        grid=(t // tm,),
        in_specs=[pl.BlockSpec((tm, D_MODEL), lambda i: (i, 0)),
                  pl.BlockSpec((tm, D_MODEL), lambda i: (ta + i, 0)),
                  pl.BlockSpec((tm, D_MODEL), lambda i: (tb + i, 0)),
                  pl.BlockSpec((1, N_MOD, D_MODEL), lambda i: (cond_base + i // tiles_per_cond, 0, 0))],
        out_specs=pl.BlockSpec((tm, D_MODEL), lambda i: (i, 0)),
        compiler_params=_cparams(("arbitrary",)),
        name="final_ctx" if cond_base == 0 else "final_lat",
    )(x1, y, y, mod3)


def _rope_table(n_tokens):
    rows = n_tokens // GRID_W
    row = jnp.repeat(jnp.arange(rows, dtype=F32), GRID_W)
    col = jnp.tile(jnp.arange(GRID_W, dtype=F32), rows)
    inv = ROPE_BASE ** (-jnp.arange(ROPE_PAIRS, dtype=F32) / ROPE_PAIRS)
    cr, sr = jnp.cos(row[:, None] * inv), jnp.sin(row[:, None] * inv)
    cc, sc = jnp.cos(col[:, None] * inv), jnp.sin(col[:, None] * inv)
    return jnp.concatenate([cr, cr, cc, cc, -sr, sr, -sc, sc], axis=-1)


def _swap_halves(a):
    p = ROPE_PAIRS
    return jnp.concatenate([a[..., p:2 * p], a[..., :p], a[..., 3 * p:], a[..., 2 * p:3 * p]], axis=-1)


def _head_gains(g):
    rope = g[QK_NOPE:]
    return jnp.stack([g[:QK_NOPE], jnp.concatenate([rope, _swap_halves(rope)])])


def kernel(x_prompt, x_sample, cache_ckv, cache_krope, c, c_ctx, ada_w, ada_b, norm1_g, w_in, conv_w,
           q_lora_g, w_uq, kv_lora_g, w_ukv, q_head_g, k_head_g, w_out, norm2_g, router_g, router_e,
           w_gate, w_up, w_down):
    depth = ada_w.shape[0]
    assert depth == 1
    bp, sp, _ = x_prompt.shape
    bs, ss, _ = x_sample.shape
    past = cache_ckv.shape[2]
    tp, ts = bp * sp, bs * ss
    assert sp & (sp - 1) == 0 and ss & (ss - 1) == 0
    assert tp % TM_IN == 0 and ss % TM_IN == 0 and TM_IN % sp == 0 and ss % TQ == 0
    assert bs + 1 <= 8 and past % 256 == 0 and (2 * (tp + ts)) % TM_MOE == 0

    w_in_b = jnp.concatenate([w_in[0], _swap_halves(w_in[0][:, IN_COLS - 2 * QK_ROPE:])], axis=1).astype(BF16)
    uq = w_uq[0].reshape(Q_LORA, N_HEADS, QK_DIM)
    w_uq_b = jnp.concatenate([uq, _swap_halves(uq[..., QK_NOPE:])], axis=-1)
    w_uq_b = w_uq_b.reshape(Q_LORA, N_HEADS * HEAD_W).astype(BF16)
    w_ukv_b = w_ukv[0].astype(BF16)
    w_out_b = w_out[0].astype(BF16)
    wr = jnp.concatenate([router_g[0], router_e[0],
                          jnp.zeros((D_MODEL, ROUTER_COLS - N_EXPERT_GROUPS - N_EXPERTS), F32)], axis=1)
    gq, gk = _head_gains(q_head_g[0]), _head_gains(k_head_g[0])
    n1g, n2g = norm1_g[0].reshape(1, D_MODEL), norm2_g[0].reshape(1, D_MODEL)
    qlg, kvg = q_lora_g[0].reshape(1, Q_LORA), kv_lora_g[0].reshape(1, KV_LORA)
    cs_lat = _rope_table(ss)
    cs_id = jnp.concatenate([jnp.ones((TM_IN, QK_ROPE), F32), jnp.zeros((TM_IN, QK_ROPE), F32)], axis=1)

    cond8 = jnp.concatenate([c_ctx[None, :], c, jnp.zeros((8 - 1 - bs, D_MODEL), F32)], axis=0)
    mod3 = _modulation(cond8, ada_w[0], ada_b[0]).reshape(8, N_MOD, D_MODEL)

    xp2, xs2 = x_prompt.reshape(tp, D_MODEL), x_sample.reshape(ts, D_MODEL)
    big = 1 << 30

    bg_p, cu_p, q_p, k_p, v_p, ckv_p, kr_p = _inproj(
        xp2, mod3, n1g, w_in_b, qlg, w_uq_b, kvg, w_ukv_b, gq, gk, cs_id,
        cond_base=0, tiles_per_cond=big, cs_tiles=1, emit_cache=True)
    o_p = _attention(q_p.reshape(bp, sp, -1), [(k_p.reshape(bp, sp, -1), v_p.reshape(bp, sp, -1))],
                     tq=sp, heads=N_HEADS, name="attn_ctx")
    x1_p, h2w_p, rid_p, rw_p = _outproj(
        xp2, bg_p, cu_p, o_p.reshape(tp, -1), mod3, conv_w[0], w_out_b, n2g, wr,
        cond_base=0, tiles_per_cond=big, seq_len=sp)

    kr_c = cache_krope[:, 0].reshape(bs * past, QK_ROPE)
    k_c, v_c = _kvcache(cache_ckv[:, 0].reshape(bs * past, KV_LORA), jnp.concatenate([kr_c, kr_c], axis=1),
                        cs_id, w_ukv_b, gk)
    bg_s, cu_s, q_s, k_s, v_s = _inproj(
        xs2, mod3, n1g, w_in_b, qlg, w_uq_b, kvg, w_ukv_b, gq, gk, cs_lat,
        cond_base=1, tiles_per_cond=ss // TM_IN, cs_tiles=ss // TM_IN, emit_cache=False)
    o_s = _attention(q_s.reshape(bs, ss, -1),
                     [(k_c.reshape(bs, past, -1), v_c.reshape(bs, past, -1)),
                      (k_s.reshape(bs, ss, -1), v_s.reshape(bs, ss, -1))],
                     tq=TQ, heads=1, name="attn_lat")
    x1_s, h2w_s, rid_s, rw_s = _outproj(
        xs2, bg_s, cu_s, o_s.reshape(ts, -1), mod3, conv_w[0], w_out_b, n2g, wr,
        cond_base=1, tiles_per_cond=ss // TM_OUT, seq_len=ss)

    n_tok = tp + ts
    n_pairs = 2 * n_tok
    rid = jnp.concatenate([rid_p[:, :2], rid_s[:, :2]], axis=0).T.reshape(n_pairs)
    rw = jnp.concatenate([rw_p[:, :2], rw_s[:, :2]], axis=0).T.reshape(n_pairs)
    h2w = jnp.concatenate([h2w_p, h2w_s], axis=0)
    te, n_valid, gtab, stab, row_w = _route_tables(rid, rw, tm=TM_MOE, n_tok=n_tok)
    y = _moe(te, n_valid, gtab, stab, row_w, h2w, w_gate[0], w_up[0], w_down[0], n_pairs=n_pairs)

    y_p = _final(x1_p, y, mod3, row_base=0, n_tok=n_tok, cond_base=0, tiles_per_cond=big)
    y_s = _final(x1_s, y, mod3, row_base=tp, n_tok=n_tok, cond_base=1, tiles_per_cond=ss // TM_FIN)

    return (y_p.reshape(bp, sp, D_MODEL), y_s.reshape(bs, ss, D_MODEL),
            ckv_p.reshape(bp, 1, sp, KV_LORA), kr_p.reshape(bp, 1, sp, QK_ROPE))
```

```python
import functools

import jax
import jax.numpy as jnp
from jax import lax
from jax.experimental import pallas as pl
from jax.experimental.pallas import tpu as pltpu

F32 = jnp.float32
BF16 = jnp.bfloat16
HIGHEST = lax.Precision.HIGHEST

D_MODEL = 2048
CONV_WIDTH = 1024
N_HEADS = 8
QK_NOPE = 128
QK_ROPE = 64
V_DIM = 128
QK_DIM = QK_NOPE + QK_ROPE
Q_LORA = 512
KV_LORA = 256
GRID_W = 64
ROPE_PAIRS = QK_ROPE // 4
ROPE_BASE = 10000.0
N_EXPERT_GROUPS = 4
EXPERTS_PER_GROUP = 8
N_EXPERTS = N_EXPERT_GROUPS * EXPERTS_PER_GROUP
D_EXPERT = 512
N_MOD = 6
EPS = 1e-6

HEAD_W = 2 * QK_NOPE
LANES = 128
SUBLANES = 8
IN_COLS = 3 * CONV_WIDTH + Q_LORA + KV_LORA + 2 * QK_ROPE
ROUTER_COLS = LANES
VMEM_LIMIT = 56 * 1024 * 1024

TM_IN = 512
TM_OUT = 512
TQ = 512
TM_MOE = 256
TM_ROW = 256
BN_MOD = 1024
PAD_BITS = tuple(1 << b for b in reversed(range(TM_MOE.bit_length() - 1)))


def _cparams(sem):
    return pltpu.CompilerParams(dimension_semantics=sem, vmem_limit_bytes=VMEM_LIMIT)


def _const_spec(shape):
    nd = len(shape)
    return pl.BlockSpec(shape, lambda *_: (0,) * nd, pipeline_mode=pl.Buffered(1))


def _rms(x):
    return x * lax.rsqrt(jnp.mean(x * x, axis=-1, keepdims=True) + EPS)


def _rowsum(x):
    return jnp.sum(x, axis=-1, keepdims=True)


def _mod_kernel(c_ref, w_ref, b_ref, o_ref):
    c = c_ref[...]
    s = c / (1.0 + jnp.exp(-c))
    o_ref[...] = jnp.dot(s, w_ref[...], precision=HIGHEST, preferred_element_type=F32) + b_ref[...]


def _modulation(cond8, ada_w, ada_b):
    n = ada_w.shape[1]
    return pl.pallas_call(
        _mod_kernel,
        out_shape=jax.ShapeDtypeStruct((8, n), F32),
        grid=(n // BN_MOD,),
        in_specs=[pl.BlockSpec((8, D_MODEL), lambda j: (0, 0)),
                  pl.BlockSpec((D_MODEL, BN_MOD), lambda j: (0, j)),
                  pl.BlockSpec((1, BN_MOD), lambda j: (0, j))],
        out_specs=pl.BlockSpec((8, BN_MOD), lambda j: (0, j)),
        compiler_params=_cparams(("arbitrary",)),
        name="mod",
    )(cond8, ada_w, ada_b.reshape(1, n))


def _emit_kv(ckv, kraw, cs, w_ukv_ref, gk_ref, k_ref, v_ref):
    kv = jnp.dot(ckv.astype(BF16), w_ukv_ref[...], preferred_element_type=F32)
    ss_rope = 0.5 * _rowsum(kraw * kraw)
    t = kraw * (cs * gk_ref[1:2, :])
    tt = t + pltpu.roll(t, QK_ROPE, axis=1)
    g_nope = gk_ref[0:1, :]
    for h in range(N_HEADS):
        kn = kv[:, h * HEAD_W:h * HEAD_W + QK_NOPE]
        r = lax.rsqrt((_rowsum(kn * kn) + ss_rope) * (1.0 / QK_DIM) + EPS)
        k_ref[:, h * HEAD_W:h * HEAD_W + QK_NOPE] = (kn * r * g_nope).astype(BF16)
        k_ref[:, h * HEAD_W + QK_NOPE:(h + 1) * HEAD_W] = (tt * r).astype(BF16)
        v_ref[:, h * V_DIM:(h + 1) * V_DIM] = kv[:, h * HEAD_W + QK_NOPE:(h + 1) * HEAD_W].astype(BF16)


def _kvcache_kernel(ckv_ref, kraw_ref, cs_ref, w_ukv_ref, gk_ref, k_ref, v_ref):
    _emit_kv(ckv_ref[...], kraw_ref[...], cs_ref[...], w_ukv_ref, gk_ref, k_ref, v_ref)


def _kvcache(ckv, kraw, cs_id, w_ukv_b, gk):
    n = ckv.shape[0]
    tm = 256
    return pl.pallas_call(
        _kvcache_kernel,
        out_shape=(jax.ShapeDtypeStruct((n, N_HEADS * HEAD_W), BF16),
                   jax.ShapeDtypeStruct((n, N_HEADS * V_DIM), BF16)),
        grid=(n // tm,),
        in_specs=[pl.BlockSpec((tm, KV_LORA), lambda i: (i, 0)),
                  pl.BlockSpec((tm, LANES), lambda i: (i, 0)),
                  pl.BlockSpec((tm, LANES), lambda i: (0, 0)),
                  _const_spec((KV_LORA, N_HEADS * HEAD_W)),
                  _const_spec((2, LANES))],
        out_specs=(pl.BlockSpec((tm, N_HEADS * HEAD_W), lambda i: (i, 0)),
                   pl.BlockSpec((tm, N_HEADS * V_DIM), lambda i: (i, 0))),
        compiler_params=_cparams(("arbitrary",)),
        name="kvcache",
    )(ckv, kraw, cs_id, w_ukv_b, gk)


def _inproj_kernel(x_ref, mod_ref, n1g_ref, w_in_ref, qlg_ref, w_uq_ref, kvg_ref, w_ukv_ref,
                   gq_ref, gk_ref, cs_ref, bg_ref, cu_ref, q_ref, k_ref, v_ref, *cache_refs):
    x = x_ref[...]
    mod = mod_ref[0]
    h = _rms(x) * n1g_ref[...] * (1.0 + mod[1:2, :]) + mod[0:1, :]
    hb = h.astype(BF16)

    def proj(a, b):
        return jnp.dot(hb, w_in_ref[:, a:b], preferred_element_type=F32)

    c1, c2, c3 = CONV_WIDTH, 2 * CONV_WIDTH, 3 * CONV_WIDTH
    bg_ref[...] = proj(0, c1).astype(BF16)
    cu_ref[...] = (proj(c1, c2) * proj(c2, c3)).astype(BF16)
    q_lat = proj(c3, c3 + Q_LORA)
    kvk = proj(c3 + Q_LORA, IN_COLS)
    cs = cs_ref[...]

    qn = _rms(q_lat) * qlg_ref[...]
    q = jnp.dot(qn.astype(BF16), w_uq_ref[...], preferred_element_type=F32)
    scale = QK_DIM ** -0.5
    g_nope = gq_ref[0:1, :] * scale
    tq = cs * (gq_ref[1:2, :] * scale)
    for hd in range(N_HEADS):
        lo = q[:, hd * HEAD_W:hd * HEAD_W + QK_NOPE]
        up = q[:, hd * HEAD_W + QK_NOPE:(hd + 1) * HEAD_W]
        ss = _rowsum(lo * lo) + 0.5 * _rowsum(up * up)
        r = lax.rsqrt(ss * (1.0 / QK_DIM) + EPS)
        q_ref[:, hd * HEAD_W:hd * HEAD_W + QK_NOPE] = (lo * r * g_nope).astype(BF16)
        q_ref[:, hd * HEAD_W + QK_NOPE:(hd + 1) * HEAD_W] = (up * r * tq).astype(BF16)

    kv_lat = kvk[:, :KV_LORA]
    kraw = kvk[:, KV_LORA:]
    ckv = _rms(kv_lat) * kvg_ref[...]
    if cache_refs:
        ckv_out_ref, kr_out_ref = cache_refs
        ckv_out_ref[...] = ckv
        kr_out_ref[...] = kraw[:, :QK_ROPE]
    _emit_kv(ckv, kraw, cs, w_ukv_ref, gk_ref, k_ref, v_ref)


def _inproj(x2d, mod3, n1g, w_in_b, qlg, w_uq_b, kvg, w_ukv_b, gq, gk, cs, *,
            cond_base, tiles_per_cond, cs_tiles, emit_cache):
    t = x2d.shape[0]
    tm = TM_IN
    out_shape = [jax.ShapeDtypeStruct((t, CONV_WIDTH), BF16),
                 jax.ShapeDtypeStruct((t, CONV_WIDTH), BF16),
                 jax.ShapeDtypeStruct((t, N_HEADS * HEAD_W), BF16),
                 jax.ShapeDtypeStruct((t, N_HEADS * HEAD_W), BF16),
                 jax.ShapeDtypeStruct((t, N_HEADS * V_DIM), BF16)]
    out_specs = [pl.BlockSpec((tm, CONV_WIDTH), lambda i: (i, 0)),
                 pl.BlockSpec((tm, CONV_WIDTH), lambda i: (i, 0)),
                 pl.BlockSpec((tm, N_HEADS * HEAD_W), lambda i: (i, 0)),
                 pl.BlockSpec((tm, N_HEADS * HEAD_W), lambda i: (i, 0)),
                 pl.BlockSpec((tm, N_HEADS * V_DIM), lambda i: (i, 0))]
    if emit_cache:
        out_shape += [jax.ShapeDtypeStruct((t, KV_LORA), F32), jax.ShapeDtypeStruct((t, QK_ROPE), F32)]
        out_specs += [pl.BlockSpec((tm, KV_LORA), lambda i: (i, 0)),
                      pl.BlockSpec((tm, QK_ROPE), lambda i: (i, 0))]
    return pl.pallas_call(
        _inproj_kernel,
        out_shape=tuple(out_shape),
        grid=(t // tm,),
        in_specs=[pl.BlockSpec((tm, D_MODEL), lambda i: (i, 0)),
                  pl.BlockSpec((1, N_MOD, D_MODEL), lambda i: (cond_base + i // tiles_per_cond, 0, 0)),
                  _const_spec((1, D_MODEL)),
                  _const_spec((D_MODEL, IN_COLS)),
                  _const_spec((1, Q_LORA)),
                  _const_spec((Q_LORA, N_HEADS * HEAD_W)),
                  _const_spec((1, KV_LORA)),
                  _const_spec((KV_LORA, N_HEADS * HEAD_W)),
                  _const_spec((2, LANES)),
                  _const_spec((2, LANES)),
                  pl.BlockSpec((tm, LANES), lambda i: (i % cs_tiles, 0))],
        out_specs=tuple(out_specs),
        compiler_params=_cparams(("arbitrary",)),
        name="inproj_ctx" if emit_cache else "inproj_lat",
    )(x2d, mod3, n1g, w_in_b, qlg, w_uq_b, kvg, w_ukv_b, gq, gk, cs)


def _attn_kernel(*refs, n_kv, heads):
    q_ref = refs[0]
    o_ref = refs[-1]
    for h in range(heads):
        q = q_ref[0, :, h * HEAD_W:(h + 1) * HEAD_W]
        scores = []
        for j in range(n_kv):
            k = refs[1 + 2 * j][0, :, h * HEAD_W:(h + 1) * HEAD_W]
            scores.append(lax.dot_general(q, k, (((1,), (1,)), ((), ())), preferred_element_type=F32))
        m = jnp.max(scores[0], axis=-1, keepdims=True)
        for s in scores[1:]:
            m = jnp.maximum(m, jnp.max(s, axis=-1, keepdims=True))
        l = None
        o = None
        for j in range(n_kv):
            p = jnp.exp(scores[j] - m)
            lj = _rowsum(p)
            v = refs[2 + 2 * j][0, :, h * V_DIM:(h + 1) * V_DIM]
            oj = jnp.dot(p.astype(BF16), v, preferred_element_type=F32)
            l = lj if l is None else l + lj
            o = oj if o is None else o + oj
        o_ref[0, :, h * V_DIM:(h + 1) * V_DIM] = (o / l).astype(BF16)


def _attention(q, kvs, *, tq, heads, name):
    b, s, _ = q.shape
    in_specs = [pl.BlockSpec((1, tq, heads * HEAD_W), lambda bi, hi, qi: (bi, qi, hi))]
    args = [q]
    for k, v in kvs:
        sk = k.shape[1]
        in_specs.append(pl.BlockSpec((1, sk, heads * HEAD_W), lambda bi, hi, qi: (bi, 0, hi)))
        in_specs.append(pl.BlockSpec((1, sk, heads * V_DIM), lambda bi, hi, qi: (bi, 0, hi)))
        args += [k, v]
    return pl.pallas_call(
        functools.partial(_attn_kernel, n_kv=len(kvs), heads=heads),
        out_shape=jax.ShapeDtypeStruct((b, s, N_HEADS * V_DIM), BF16),
        grid=(b, N_HEADS // heads, s // tq),
        in_specs=in_specs,
        out_specs=pl.BlockSpec((1, tq, heads * V_DIM), lambda bi, hi, qi: (bi, qi, hi)),
        compiler_params=_cparams(("arbitrary", "arbitrary", "arbitrary")),
        name=name,
    )(*args)


def _outproj_kernel(x_ref, bg_ref, cu_ref, cup_ref, cun_ref, o_ref, mod_ref, cw_ref, w_out_ref,
                    n2g_ref, wr_ref, x1_ref, h2_ref, rid_ref, rw_ref, *, tm, seq_len):
    i = pl.program_id(0)
    mod = mod_ref[0]
    cu = cu_ref[...].astype(F32)
    prev_row = cup_ref[...].astype(F32)[15:16, :]
    next_row = cun_ref[...].astype(F32)[0:1, :]
    row = lax.broadcasted_iota(jnp.int32, (tm, 1), 0)
    pos = (i * tm + row) & (seq_len - 1)
    up = jnp.where(row == 0, prev_row, pltpu.roll(cu, 1, axis=0))
    up = jnp.where(pos == 0, 0.0, up)
    dn = jnp.where(row == tm - 1, next_row, pltpu.roll(cu, tm - 1, axis=0))
    dn = jnp.where(pos == seq_len - 1, 0.0, dn)
    cw = cw_ref[...]
    y_conv = bg_ref[...].astype(F32) * (up * cw[0:1, :] + cu * cw[1:2, :] + dn * cw[2:3, :])
    mix = jnp.dot(y_conv.astype(BF16), w_out_ref[:CONV_WIDTH, :], preferred_element_type=F32)
    mix = mix + jnp.dot(o_ref[...], w_out_ref[CONV_WIDTH:, :], preferred_element_type=F32)
    x1 = x_ref[...] + mod[2:3, :] * mix
    x1_ref[...] = x1
    h2 = _rms(x1) * n2g_ref[...] * (1.0 + mod[4:5, :]) + mod[3:4, :]
    h2_ref[...] = h2

    logits = jnp.dot(h2, wr_ref[...], precision=HIGHEST, preferred_element_type=F32)
    lane = lax.broadcasted_iota(jnp.int32, logits.shape, 1)
    neg = -jnp.inf
    big = jnp.int32(1 << 20)
    gl = jnp.where(lane < N_EXPERT_GROUPS, logits, neg)
    gmax = jnp.max(gl, axis=-1, keepdims=True)
    p_top = 1.0 / _rowsum(jnp.exp(gl - gmax))
    g_top = jnp.min(jnp.where(gl == gmax, lane, big), axis=-1, keepdims=True)
    e_lo = N_EXPERT_GROUPS + EXPERTS_PER_GROUP * g_top
    el = jnp.where((lane >= e_lo) & (lane < e_lo + EXPERTS_PER_GROUP), logits, neg)
    v1 = jnp.max(el, axis=-1, keepdims=True)
    i1 = jnp.min(jnp.where(el == v1, lane, big), axis=-1, keepdims=True)
    el2 = jnp.where(lane == i1, neg, el)
    v2 = jnp.max(el2, axis=-1, keepdims=True)
    i2 = jnp.min(jnp.where(el2 == v2, lane, big), axis=-1, keepdims=True)
    e21 = jnp.exp(v2 - v1)
    w1 = p_top / (1.0 + e21)
    w2 = w1 * e21
    rid_ref[...] = jnp.where(lane == 0, i1 - N_EXPERT_GROUPS, i2 - N_EXPERT_GROUPS)
    rw_ref[...] = jnp.where(lane == 0, w1, w2)


def _outproj(x2d, bg, cu, o2d, mod3, conv_w, w_out_b, n2g, wr, *, cond_base, tiles_per_cond, seq_len):
    t = x2d.shape[0]
    tm = TM_OUT
    hb = tm // 16
    nhb = t // 16
    return pl.pallas_call(
        functools.partial(_outproj_kernel, tm=tm, seq_len=seq_len),
        out_shape=(jax.ShapeDtypeStruct((t, D_MODEL), F32),
                   jax.ShapeDtypeStruct((t, D_MODEL), F32),
                   jax.ShapeDtypeStruct((t, LANES), jnp.int32),
                   jax.ShapeDtypeStruct((t, LANES), F32)),
        grid=(t // tm,),
        in_specs=[pl.BlockSpec((tm, D_MODEL), lambda i: (i, 0)),
                  pl.BlockSpec((tm, CONV_WIDTH), lambda i: (i, 0)),
                  pl.BlockSpec((tm, CONV_WIDTH), lambda i: (i, 0)),
                  pl.BlockSpec((16, CONV_WIDTH), lambda i: (jnp.maximum(i * hb - 1, 0), 0)),
                  pl.BlockSpec((16, CONV_WIDTH), lambda i: (jnp.minimum((i + 1) * hb, nhb - 1), 0)),
                  pl.BlockSpec((tm, N_HEADS * V_DIM), lambda i: (i, 0)),
                  pl.BlockSpec((1, N_MOD, D_MODEL), lambda i: (cond_base + i // tiles_per_cond, 0, 0)),
                  _const_spec((3, CONV_WIDTH)),
                  _const_spec((D_MODEL, D_MODEL)),
                  _const_spec((1, D_MODEL)),
                  _const_spec((D_MODEL, ROUTER_COLS))],
        out_specs=(pl.BlockSpec((tm, D_MODEL), lambda i: (i, 0)),
                   pl.BlockSpec((tm, D_MODEL), lambda i: (i, 0)),
                   pl.BlockSpec((tm, LANES), lambda i: (i, 0)),
                   pl.BlockSpec((tm, LANES), lambda i: (i, 0))),
        compiler_params=_cparams(("arbitrary",)),
        name="outproj_ctx" if cond_base == 0 else "outproj_lat",
    )(x2d, bg, cu, cu, cu, o2d, mod3, conv_w, w_out_b, n2g, wr)


def _route_tables(rid, *, tm):
    n_pairs = rid.shape[0]
    nt = n_pairs // tm + N_EXPERTS
    experts = jnp.arange(N_EXPERTS, dtype=jnp.int32)
    onehot = (rid[None, :] == experts[:, None]).astype(jnp.int32)
    csum = jnp.cumsum(onehot, axis=1)
    counts = csum[:, -1]
    tiles_e = (counts + tm - 1) // tm
    tile_end = jnp.cumsum(tiles_e)
    tile_start = tile_end - tiles_e
    n_valid = tile_end[-1]
    pos = jnp.sum(onehot * (csum - 1 + tile_start[:, None] * tm), axis=0)
    tile_idx = jnp.minimum(jnp.arange(nt, dtype=jnp.int32), n_valid - 1)
    te = jnp.sum((tile_end[None, :] <= tile_idx[:, None]).astype(jnp.int32), axis=1)
    return (pos, te, n_valid.reshape(1).astype(jnp.int32),
            (tile_start * tm + counts).astype(jnp.int32), (tiles_e * tm - counts).astype(jnp.int32))


def _dispatch_kernel(pstart_ref, pcnt_ref, nval_ref, pos0_ref, pos1_ref, h2a_ref, h2b_ref, xs_hbm,
                     buf, zbuf, sem, sem_z, *, tm, nt, n_a):
    i = pl.program_id(0)
    n = pl.num_programs(0)
    slot = i & 1

    def row_copy(r, pos_ref, s):
        return pltpu.make_async_copy(buf.at[s, pl.ds(r, 1)], xs_hbm.at[pl.ds(pos_ref[0, 0, r], 1)], sem.at[s])

    def wait_tile(s):
        for _ in range(2):
            pltpu.make_async_copy(buf.at[s], xs_hbm.at[pl.ds(0, tm)], sem.at[s]).wait()

    def pad_copies(e, fn):
        cnt = pcnt_ref[e]
        start = pstart_ref[e]
        off = start + cnt
        for b in PAD_BITS:
            if b < SUBLANES:
                break
            off = off - (cnt & b)
            dst = pl.multiple_of(off, SUBLANES)

            @pl.when((cnt & b) != 0)
            def _():
                fn(pltpu.make_async_copy(zbuf.at[pl.ds(0, b)], xs_hbm.at[pl.ds(dst, b)], sem_z))
        for j in range(SUBLANES - 1):
            @pl.when(j < (cnt & (SUBLANES - 1)))
            def _():
                fn(pltpu.make_async_copy(zbuf.at[pl.ds(0, 1)], xs_hbm.at[pl.ds(start + j, 1)], sem_z))

    def tail_copy(j):
        return pltpu.make_async_copy(zbuf, xs_hbm.at[pl.ds(pl.multiple_of(j * tm, tm), tm)], sem_z)

    @pl.when(i == 0)
    def _():
        zbuf[...] = jnp.zeros((tm, D_MODEL), F32)
        lax.fori_loop(0, N_EXPERTS, lambda e, c: (pad_copies(e, lambda d: d.start()), c)[1], 0)
        lax.fori_loop(nval_ref[0], nt, lambda j, c: (tail_copy(j).start(), c)[1], 0)
        lax.fori_loop(0, N_EXPERTS, lambda e, c: (pad_copies(e, lambda d: d.wait()), c)[1], 0)
        lax.fori_loop(nval_ref[0], nt, lambda j, c: (tail_copy(j).wait(), c)[1], 0)

    @pl.when(i >= 2)
    def _():
        wait_tile(slot)

    @pl.when(i < n_a)
    def _():
        buf[slot] = h2a_ref[...]

    @pl.when(i >= n_a)
    def _():
        buf[slot] = h2b_ref[...]

    for r in range(tm):
        row_copy(r, pos0_ref, slot).start()
        row_copy(r, pos1_ref, slot).start()

    @pl.when(i == n - 1)
    def _():
        wait_tile(slot)
        wait_tile(1 - slot)


def _dispatch(pad_start, pad_cnt, n_valid, pos3, h2_a, h2_b, *, nt):
    tm = TM_ROW
    n_a, n_b = h2_a.shape[0] // tm, h2_b.shape[0] // tm
    smem_tile = functools.partial(pl.BlockSpec, (1, 1, tm), memory_space=pltpu.SMEM)
    grid_spec = pltpu.PrefetchScalarGridSpec(
        num_scalar_prefetch=3,
        grid=(n_a + n_b,),
        in_specs=[smem_tile(lambda i, *_: (i, 0, 0)),
                  smem_tile(lambda i, *_: (n_a + n_b + i, 0, 0)),
                  pl.BlockSpec((tm, D_MODEL), lambda i, *_: (jnp.minimum(i, n_a - 1), 0)),
                  pl.BlockSpec((tm, D_MODEL), lambda i, *_: (jnp.maximum(i - n_a, 0), 0))],
        out_specs=pl.BlockSpec(memory_space=pl.ANY),
        scratch_shapes=[pltpu.VMEM((2, tm, D_MODEL), F32),
                        pltpu.VMEM((tm, D_MODEL), F32),
                        pltpu.SemaphoreType.DMA((2,)),
                        pltpu.SemaphoreType.DMA])
    return pl.pallas_call(
        functools.partial(_dispatch_kernel, tm=tm, nt=nt, n_a=n_a),
        out_shape=jax.ShapeDtypeStruct((nt * TM_MOE, D_MODEL), F32),
        grid_spec=grid_spec,
        compiler_params=pltpu.CompilerParams(dimension_semantics=("arbitrary",),
                                             vmem_limit_bytes=VMEM_LIMIT, has_side_effects=True),
        name="dispatch",
    )(pad_start, pad_cnt, n_valid, pos3, pos3, h2_a, h2_b)


def _moe_kernel(texp_ref, nval_ref, xs_ref, wg_ref, wu_ref, wd_ref, y_ref, wgb, wub, wdb):
    i = pl.program_id(0)

    @pl.when(i < nval_ref[0])
    def _():
        e = texp_ref[i]
        e_prev = texp_ref[jnp.maximum(i - 1, 0)]

        @pl.when((i == 0) | (e != e_prev))
        def _():
            wgb[...] = wg_ref[0].astype(BF16)
            wub[...] = wu_ref[0].astype(BF16)
            wdb[...] = wd_ref[0].astype(BF16)

        x = xs_ref[...].astype(BF16)
        g = jnp.dot(x, wgb[...], preferred_element_type=F32)
        u = jnp.dot(x, wub[...], preferred_element_type=F32)
        a = g / (1.0 + jnp.exp(-g)) * u
        y_ref[...] = jnp.dot(a.astype(BF16), wdb[...], preferred_element_type=F32)

    @pl.when(i >= nval_ref[0])
    def _():
        y_ref[...] = jnp.zeros(y_ref.shape, F32)


def _moe(tile_expert, n_valid, xs, w_gate, w_up, w_down):
    nt = tile_expert.shape[0]
    tm = TM_MOE
    grid_spec = pltpu.PrefetchScalarGridSpec(
        num_scalar_prefetch=2,
        grid=(nt,),
        in_specs=[pl.BlockSpec((tm, D_MODEL), lambda i, te, nv: (jnp.minimum(i, nv[0] - 1), 0)),
                  pl.BlockSpec((1, D_MODEL, D_EXPERT), lambda i, te, nv: (te[i], 0, 0)),
                  pl.BlockSpec((1, D_MODEL, D_EXPERT), lambda i, te, nv: (te[i], 0, 0)),
                  pl.BlockSpec((1, D_EXPERT, D_MODEL), lambda i, te, nv: (te[i], 0, 0))],
        out_specs=pl.BlockSpec((tm, D_MODEL), lambda i, te, nv: (i, 0)),
        scratch_shapes=[pltpu.VMEM((D_MODEL, D_EXPERT), BF16),
                        pltpu.VMEM((D_MODEL, D_EXPERT), BF16),
                        pltpu.VMEM((D_EXPERT, D_MODEL), BF16)])
    return pl.pallas_call(
        _moe_kernel,
        out_shape=jax.ShapeDtypeStruct((nt * tm, D_MODEL), F32),
        grid_spec=grid_spec,
        compiler_params=_cparams(("arbitrary",)),
        name="moe",
    )(tile_expert, n_valid, xs, w_gate, w_up, w_down)


def _final_kernel(pa0_ref, pb0_ref, pa_ref, pb_ref, x1_ref, rw_ref, mod_ref, y_hbm, o_ref, ybuf, sem, *, tm):
    i = pl.program_id(0)
    n = pl.num_programs(0)
    slot = i & 1

    def start_tile(pa, pb, s):
        for r in range(tm):
            pltpu.make_async_copy(y_hbm.at[pl.ds(pa[0, 0, r], 1)], ybuf.at[s, 0, pl.ds(r, 1)], sem.at[s]).start()
            pltpu.make_async_copy(y_hbm.at[pl.ds(pb[0, 0, r], 1)], ybuf.at[s, 1, pl.ds(r, 1)], sem.at[s]).start()

    @pl.when(i == 0)
    def _():
        start_tile(pa0_ref, pb0_ref, 0)

    @pl.when(i + 1 < n)
    def _():
        start_tile(pa_ref, pb_ref, 1 - slot)

    for k in range(2):
        pltpu.make_async_copy(y_hbm.at[pl.ds(0, tm)], ybuf.at[slot, k], sem.at[slot]).wait()
    w = rw_ref[...]
    moe = w[:, 0:1] * ybuf[slot, 0] + w[:, 1:2] * ybuf[slot, 1]
    o_ref[...] = x1_ref[...] + mod_ref[0][5:6, :] * moe


def _final(x1, rw, y, pos3, mod3, *, tile_base, slot_tiles, cond_base, tiles_per_cond):
    t = x1.shape[0]
    tm = TM_ROW
    n = t // tm
    smem_tile = functools.partial(pl.BlockSpec, (1, 1, tm), memory_space=pltpu.SMEM)
    return pl.pallas_call(
        functools.partial(_final_kernel, tm=tm),
        out_shape=jax.ShapeDtypeStruct((t, D_MODEL), F32),
        grid=(n,),
        in_specs=[smem_tile(lambda i: (tile_base, 0, 0)),
                  smem_tile(lambda i: (slot_tiles + tile_base, 0, 0)),
                  smem_tile(lambda i: (tile_base + jnp.minimum(i + 1, n - 1), 0, 0)),
                  smem_tile(lambda i: (slot_tiles + tile_base + jnp.minimum(i + 1, n - 1), 0, 0)),
                  pl.BlockSpec((tm, D_MODEL), lambda i: (i, 0)),
                  pl.BlockSpec((tm, LANES), lambda i: (i, 0)),
                  pl.BlockSpec((1, N_MOD, D_MODEL), lambda i: (cond_base + i // tiles_per_cond, 0, 0)),
                  pl.BlockSpec(memory_space=pl.ANY)],
        out_specs=pl.BlockSpec((tm, D_MODEL), lambda i: (i, 0)),
        scratch_shapes=[pltpu.VMEM((2, 2, tm, D_MODEL), F32), pltpu.SemaphoreType.DMA((2,))],
        compiler_params=_cparams(("arbitrary",)),
        name="final_ctx" if cond_base == 0 else "final_lat",
    )(pos3, pos3, pos3, pos3, x1, rw, mod3, y)


def _rope_table(n_tokens):
    rows = n_tokens // GRID_W
    row = jnp.repeat(jnp.arange(rows, dtype=F32), GRID_W)
    col = jnp.tile(jnp.arange(GRID_W, dtype=F32), rows)
    inv = ROPE_BASE ** (-jnp.arange(ROPE_PAIRS, dtype=F32) / ROPE_PAIRS)
    cr, sr = jnp.cos(row[:, None] * inv), jnp.sin(row[:, None] * inv)
    cc, sc = jnp.cos(col[:, None] * inv), jnp.sin(col[:, None] * inv)
    return jnp.concatenate([cr, cr, cc, cc, -sr, sr, -sc, sc], axis=-1)


def _swap_halves(a):
    p = ROPE_PAIRS
    return jnp.concatenate([a[..., p:2 * p], a[..., :p], a[..., 3 * p:], a[..., 2 * p:3 * p]], axis=-1)


def _head_gains(g):
    rope = g[QK_NOPE:]
    return jnp.stack([g[:QK_NOPE], jnp.concatenate([rope, _swap_halves(rope)])])


def kernel(x_prompt, x_sample, cache_ckv, cache_krope, c, c_ctx, ada_w, ada_b, norm1_g, w_in, conv_w,
           q_lora_g, w_uq, kv_lora_g, w_ukv, q_head_g, k_head_g, w_out, norm2_g, router_g, router_e,
           w_gate, w_up, w_down):
    depth = ada_w.shape[0]
    assert depth == 1
    bp, sp, _ = x_prompt.shape
    bs, ss, _ = x_sample.shape
    past = cache_ckv.shape[2]
    tp, ts = bp * sp, bs * ss
    assert sp & (sp - 1) == 0 and ss & (ss - 1) == 0
    assert tp % TM_IN == 0 and ss % TM_IN == 0 and TM_IN % sp == 0 and ss % TQ == 0
    assert bs + 1 <= 8 and past % 256 == 0 and TM_ROW == TM_MOE and tp % TM_ROW == 0 and ts % TM_ROW == 0

    w_in_b = jnp.concatenate([w_in[0], _swap_halves(w_in[0][:, IN_COLS - 2 * QK_ROPE:])], axis=1).astype(BF16)
    uq = w_uq[0].reshape(Q_LORA, N_HEADS, QK_DIM)
    w_uq_b = jnp.concatenate([uq, _swap_halves(uq[..., QK_NOPE:])], axis=-1)
    w_uq_b = w_uq_b.reshape(Q_LORA, N_HEADS * HEAD_W).astype(BF16)
    w_ukv_b = w_ukv[0].astype(BF16)
    w_out_b = w_out[0].astype(BF16)
    wr = jnp.concatenate([router_g[0], router_e[0],
                          jnp.zeros((D_MODEL, ROUTER_COLS - N_EXPERT_GROUPS - N_EXPERTS), F32)], axis=1)
    gq, gk = _head_gains(q_head_g[0]), _head_gains(k_head_g[0])
    n1g, n2g = norm1_g[0].reshape(1, D_MODEL), norm2_g[0].reshape(1, D_MODEL)
    qlg, kvg = q_lora_g[0].reshape(1, Q_LORA), kv_lora_g[0].reshape(1, KV_LORA)
    cs_lat = _rope_table(ss)
    cs_id = jnp.concatenate([jnp.ones((TM_IN, QK_ROPE), F32), jnp.zeros((TM_IN, QK_ROPE), F32)], axis=1)

    cond8 = jnp.concatenate([c_ctx[None, :], c, jnp.zeros((8 - 1 - bs, D_MODEL), F32)], axis=0)
    mod3 = _modulation(cond8, ada_w[0], ada_b[0]).reshape(8, N_MOD, D_MODEL)

    xp2, xs2 = x_prompt.reshape(tp, D_MODEL), x_sample.reshape(ts, D_MODEL)
    big = 1 << 30

    bg_p, cu_p, q_p, k_p, v_p, ckv_p, kr_p = _inproj(
        xp2, mod3, n1g, w_in_b, qlg, w_uq_b, kvg, w_ukv_b, gq, gk, cs_id,
        cond_base=0, tiles_per_cond=big, cs_tiles=1, emit_cache=True)
    o_p = _attention(q_p.reshape(bp, sp, -1), [(k_p.reshape(bp, sp, -1), v_p.reshape(bp, sp, -1))],
                     tq=sp, heads=N_HEADS, name="attn_ctx")
    x1_p, h2_p, rid_p, rw_p = _outproj(
        xp2, bg_p, cu_p, o_p.reshape(tp, -1), mod3, conv_w[0], w_out_b, n2g, wr,
        cond_base=0, tiles_per_cond=big, seq_len=sp)

    kr_c = cache_krope[:, 0].reshape(bs * past, QK_ROPE)
    k_c, v_c = _kvcache(cache_ckv[:, 0].reshape(bs * past, KV_LORA), jnp.concatenate([kr_c, kr_c], axis=1),
                        cs_id, w_ukv_b, gk)
    bg_s, cu_s, q_s, k_s, v_s = _inproj(
        xs2, mod3, n1g, w_in_b, qlg, w_uq_b, kvg, w_ukv_b, gq, gk, cs_lat,
        cond_base=1, tiles_per_cond=ss // TM_IN, cs_tiles=ss // TM_IN, emit_cache=False)
    o_s = _attention(q_s.reshape(bs, ss, -1),
                     [(k_c.reshape(bs, past, -1), v_c.reshape(bs, past, -1)),
                      (k_s.reshape(bs, ss, -1), v_s.reshape(bs, ss, -1))],
                     tq=TQ, heads=1, name="attn_lat")
    x1_s, h2_s, rid_s, rw_s = _outproj(
        xs2, bg_s, cu_s, o_s.reshape(ts, -1), mod3, conv_w[0], w_out_b, n2g, wr,
        cond_base=1, tiles_per_cond=ss // TM_OUT, seq_len=ss)

    n_tok = tp + ts
    slot_tiles = n_tok // TM_ROW
    nt = 2 * n_tok // TM_MOE + N_EXPERTS
    rid = jnp.concatenate([rid_p[:, :2], rid_s[:, :2]], axis=0).T.reshape(2 * n_tok)
    pos, te, n_valid, pad_start, pad_cnt = _route_tables(rid, tm=TM_MOE)
    pos3 = pos.reshape(2 * slot_tiles, 1, TM_ROW)
    xs = _dispatch(pad_start, pad_cnt, n_valid, pos3, h2_p, h2_s, nt=nt)
    y = _moe(te, n_valid, xs, w_gate[0], w_up[0], w_down[0])

    y_p = _final(x1_p, rw_p, y, pos3, mod3, tile_base=0, slot_tiles=slot_tiles, cond_base=0, tiles_per_cond=big)
    y_s = _final(x1_s, rw_s, y, pos3, mod3, tile_base=tp // TM_ROW, slot_tiles=slot_tiles, cond_base=1,
                 tiles_per_cond=ss // TM_ROW)

    return (y_p.reshape(bp, sp, D_MODEL), y_s.reshape(bs, ss, D_MODEL),
            ckv_p.reshape(bp, 1, sp, KV_LORA), kr_p.reshape(bp, 1, sp, QK_ROPE))
```

```python
import functools

import jax
import jax.numpy as jnp
from jax import lax
from jax.experimental import pallas as pl
from jax.experimental.pallas import tpu as pltpu

F32 = jnp.float32
BF16 = jnp.bfloat16
HIGHEST = lax.Precision.HIGHEST

D_MODEL = 2048
CONV_WIDTH = 1024
N_HEADS = 8
QK_NOPE = 128
QK_ROPE = 64
V_DIM = 128
QK_DIM = QK_NOPE + QK_ROPE
Q_LORA = 512
KV_LORA = 256
GRID_W = 64
ROPE_PAIRS = QK_ROPE // 4
ROPE_BASE = 10000.0
N_EXPERT_GROUPS = 4
EXPERTS_PER_GROUP = 8
N_EXPERTS = N_EXPERT_GROUPS * EXPERTS_PER_GROUP
D_EXPERT = 512
N_MOD = 6
EPS = 1e-6

HEAD_W = 2 * QK_NOPE
LANES = 128
SUBLANES = 8
IN_COLS = 3 * CONV_WIDTH + Q_LORA + KV_LORA + 2 * QK_ROPE
ROUTER_COLS = LANES
VMEM_LIMIT = 56 * 1024 * 1024

TM_IN = 512
TM_OUT = 512
TQ = 256
TM_MOE = 256
TM_ROW = 256
BN_MOD = 1024
PAD_BITS = tuple(1 << b for b in reversed(range(TM_MOE.bit_length() - 1)))


def _cparams(sem):
    return pltpu.CompilerParams(dimension_semantics=sem, vmem_limit_bytes=VMEM_LIMIT)


def _const_spec(shape):
    nd = len(shape)
    return pl.BlockSpec(shape, lambda *_: (0,) * nd, pipeline_mode=pl.Buffered(1))


def _rms(x):
    return x * lax.rsqrt(jnp.mean(x * x, axis=-1, keepdims=True) + EPS)


def _rowsum(x):
    return jnp.sum(x, axis=-1, keepdims=True)


def _mod_kernel(c_ref, w_ref, b_ref, o_ref):
    c = c_ref[...]
    s = c / (1.0 + jnp.exp(-c))
    o_ref[...] = jnp.dot(s, w_ref[...], precision=HIGHEST, preferred_element_type=F32) + b_ref[...]


def _modulation(cond8, ada_w, ada_b):
    n = ada_w.shape[1]
    return pl.pallas_call(
        _mod_kernel,
        out_shape=jax.ShapeDtypeStruct((8, n), F32),
        grid=(n // BN_MOD,),
        in_specs=[pl.BlockSpec((8, D_MODEL), lambda j: (0, 0)),
                  pl.BlockSpec((D_MODEL, BN_MOD), lambda j: (0, j)),
                  pl.BlockSpec((1, BN_MOD), lambda j: (0, j))],
        out_specs=pl.BlockSpec((8, BN_MOD), lambda j: (0, j)),
        compiler_params=_cparams(("arbitrary",)),
        name="mod",
    )(cond8, ada_w, ada_b.reshape(1, n))


def _emit_kv(ckv, kraw, cs, w_uk_ref, w_uvt_ref, gk_ref, k_ref, vt_ref):
    cb = ckv.astype(BF16)
    kn_all = jnp.dot(cb, w_uk_ref[...], preferred_element_type=F32)
    vt = lax.dot_general(w_uvt_ref[...], cb, (((1,), (1,)), ((), ())), preferred_element_type=F32)
    vt_ref[...] = vt.astype(BF16)
    ss_rope = 0.5 * _rowsum(kraw * kraw)
    t = kraw * (cs * gk_ref[1:2, :])
    tt = t + pltpu.roll(t, QK_ROPE, axis=1)
    g_nope = gk_ref[0:1, :]
    for h in range(N_HEADS):
        kn = kn_all[:, h * QK_NOPE:(h + 1) * QK_NOPE]
        r = lax.rsqrt((_rowsum(kn * kn) + ss_rope) * (1.0 / QK_DIM) + EPS)
        k_ref[:, h * HEAD_W:h * HEAD_W + QK_NOPE] = (kn * r * g_nope).astype(BF16)
        k_ref[:, h * HEAD_W + QK_NOPE:(h + 1) * HEAD_W] = (tt * r).astype(BF16)


def _kvcache_kernel(ckv_ref, kraw_ref, cs_ref, w_uk_ref, w_uvt_ref, gk_ref, k_ref, vt_ref):
    _emit_kv(ckv_ref[...], kraw_ref[...], cs_ref[...], w_uk_ref, w_uvt_ref, gk_ref, k_ref, vt_ref)


def _kvcache(ckv, kraw, cs_id, w_uk_b, w_uvt_b, gk):
    n = ckv.shape[0]
    tm = 256
    return pl.pallas_call(
        _kvcache_kernel,
        out_shape=(jax.ShapeDtypeStruct((n, N_HEADS * HEAD_W), BF16),
                   jax.ShapeDtypeStruct((N_HEADS * V_DIM, n), BF16)),
        grid=(n // tm,),
        in_specs=[pl.BlockSpec((tm, KV_LORA), lambda i: (i, 0)),
                  pl.BlockSpec((tm, LANES), lambda i: (i, 0)),
                  pl.BlockSpec((tm, LANES), lambda i: (0, 0)),
                  _const_spec((KV_LORA, N_HEADS * QK_NOPE)),
                  _const_spec((N_HEADS * V_DIM, KV_LORA)),
                  _const_spec((2, LANES))],
        out_specs=(pl.BlockSpec((tm, N_HEADS * HEAD_W), lambda i: (i, 0)),
                   pl.BlockSpec((N_HEADS * V_DIM, tm), lambda i: (0, i))),
        compiler_params=_cparams(("arbitrary",)),
        name="kvcache",
    )(ckv, kraw, cs_id, w_uk_b, w_uvt_b, gk)


def _inproj_kernel(x_ref, mod_ref, n1g_ref, w_in_ref, qlg_ref, w_uq_ref, kvg_ref, w_uk_ref, w_uvt_ref,
                   gq_ref, gk_ref, cs_ref, bg_ref, cu_ref, q_ref, k_ref, vt_ref, *cache_refs):
    x = x_ref[...]
    mod = mod_ref[0]
    h = _rms(x) * n1g_ref[...] * (1.0 + mod[1:2, :]) + mod[0:1, :]
    hb = h.astype(BF16)

    def proj(a, b):
        return jnp.dot(hb, w_in_ref[:, a:b], preferred_element_type=F32)

    c1, c2, c3 = CONV_WIDTH, 2 * CONV_WIDTH, 3 * CONV_WIDTH
    bg_ref[...] = proj(0, c1).astype(BF16)
    cu_ref[...] = (proj(c1, c2) * proj(c2, c3)).astype(BF16)
    q_lat = proj(c3, c3 + Q_LORA)
    kvk = proj(c3 + Q_LORA, IN_COLS)
    cs = cs_ref[...]

    qn = _rms(q_lat) * qlg_ref[...]
    q = jnp.dot(qn.astype(BF16), w_uq_ref[...], preferred_element_type=F32)
    scale = QK_DIM ** -0.5
    g_nope = gq_ref[0:1, :] * scale
    tq = cs * (gq_ref[1:2, :] * scale)
    for hd in range(N_HEADS):
        lo = q[:, hd * HEAD_W:hd * HEAD_W + QK_NOPE]
        up = q[:, hd * HEAD_W + QK_NOPE:(hd + 1) * HEAD_W]
        ss = _rowsum(lo * lo) + 0.5 * _rowsum(up * up)
        r = lax.rsqrt(ss * (1.0 / QK_DIM) + EPS)
        q_ref[:, hd * HEAD_W:hd * HEAD_W + QK_NOPE] = (lo * r * g_nope).astype(BF16)
        q_ref[:, hd * HEAD_W + QK_NOPE:(hd + 1) * HEAD_W] = (up * r * tq).astype(BF16)

    kv_lat = kvk[:, :KV_LORA]
    kraw = kvk[:, KV_LORA:]
    ckv = _rms(kv_lat) * kvg_ref[...]
    if cache_refs:
        ckv_out_ref, kr_out_ref = cache_refs
        ckv_out_ref[...] = ckv
        kr_out_ref[...] = kraw[:, :QK_ROPE]
    _emit_kv(ckv, kraw, cs, w_uk_ref, w_uvt_ref, gk_ref, k_ref, vt_ref)


def _inproj(x2d, mod3, n1g, w_in_b, qlg, w_uq_b, kvg, w_uk_b, w_uvt_b, gq, gk, cs, *,
            cond_base, tiles_per_cond, cs_tiles, emit_cache):
    t = x2d.shape[0]
    tm = TM_IN
    out_shape = [jax.ShapeDtypeStruct((t, CONV_WIDTH), BF16),
                 jax.ShapeDtypeStruct((t, CONV_WIDTH), BF16),
                 jax.ShapeDtypeStruct((t, N_HEADS * HEAD_W), BF16),
                 jax.ShapeDtypeStruct((t, N_HEADS * HEAD_W), BF16),
                 jax.ShapeDtypeStruct((N_HEADS * V_DIM, t), BF16)]
    out_specs = [pl.BlockSpec((tm, CONV_WIDTH), lambda i: (i, 0)),
                 pl.BlockSpec((tm, CONV_WIDTH), lambda i: (i, 0)),
                 pl.BlockSpec((tm, N_HEADS * HEAD_W), lambda i: (i, 0)),
                 pl.BlockSpec((tm, N_HEADS * HEAD_W), lambda i: (i, 0)),
                 pl.BlockSpec((N_HEADS * V_DIM, tm), lambda i: (0, i))]
    if emit_cache:
        out_shape += [jax.ShapeDtypeStruct((t, KV_LORA), F32), jax.ShapeDtypeStruct((t, QK_ROPE), F32)]
        out_specs += [pl.BlockSpec((tm, KV_LORA), lambda i: (i, 0)),
                      pl.BlockSpec((tm, QK_ROPE), lambda i: (i, 0))]
    return pl.pallas_call(
        _inproj_kernel,
        out_shape=tuple(out_shape),
        grid=(t // tm,),
        in_specs=[pl.BlockSpec((tm, D_MODEL), lambda i: (i, 0)),
                  pl.BlockSpec((1, N_MOD, D_MODEL), lambda i: (cond_base + i // tiles_per_cond, 0, 0)),
                  _const_spec((1, D_MODEL)),
                  _const_spec((D_MODEL, IN_COLS)),
                  _const_spec((1, Q_LORA)),
                  _const_spec((Q_LORA, N_HEADS * HEAD_W)),
                  _const_spec((1, KV_LORA)),
                  _const_spec((KV_LORA, N_HEADS * QK_NOPE)),
                  _const_spec((N_HEADS * V_DIM, KV_LORA)),
                  _const_spec((2, LANES)),
                  _const_spec((2, LANES)),
                  pl.BlockSpec((tm, LANES), lambda i: (i % cs_tiles, 0))],
        out_specs=tuple(out_specs),
        compiler_params=_cparams(("arbitrary",)),
        name="inproj_ctx" if emit_cache else "inproj_lat",
    )(x2d, mod3, n1g, w_in_b, qlg, w_uq_b, kvg, w_uk_b, w_uvt_b, gq, gk, cs)


def _attn_kernel(*refs, n_kv, heads):
    q_ref = refs[0]
    o_ref = refs[-1]
    for h in range(heads):
        q = q_ref[0, :, h * HEAD_W:(h + 1) * HEAD_W]
        scores = []
        for j in range(n_kv):
            k = refs[1 + 2 * j][0, :, h * HEAD_W:(h + 1) * HEAD_W]
            scores.append(lax.dot_general(k, q, (((1,), (1,)), ((), ())), preferred_element_type=F32))
        m = jnp.max(scores[0], axis=0, keepdims=True)
        for s in scores[1:]:
            m = jnp.maximum(m, jnp.max(s, axis=0, keepdims=True))
        l = None
        ot = None
        for j in range(n_kv):
            p = jnp.exp(scores[j] - m)
            lj = jnp.sum(p, axis=0, keepdims=True)
            vt = refs[2 + 2 * j][h * V_DIM:(h + 1) * V_DIM, :]
            oj = jnp.dot(vt, p.astype(BF16), preferred_element_type=F32)
            l = lj if l is None else l + lj
            ot = oj if ot is None else ot + oj
        o_ref[0, :, h * V_DIM:(h + 1) * V_DIM] = (ot / l).T.astype(BF16)


def _attention(q, kvs, *, tq, heads, name):
    b, s, _ = q.shape
    in_specs = [pl.BlockSpec((1, tq, heads * HEAD_W), lambda bi, hi, qi: (bi, qi, hi))]
    args = [q]
    for k, v in kvs:
        sk = k.shape[1]
        in_specs.append(pl.BlockSpec((1, sk, heads * HEAD_W), lambda bi, hi, qi: (bi, 0, hi)))
        in_specs.append(pl.BlockSpec((heads * V_DIM, sk), lambda bi, hi, qi: (hi, bi)))
        args += [k, v]
    return pl.pallas_call(
        functools.partial(_attn_kernel, n_kv=len(kvs), heads=heads),
        out_shape=jax.ShapeDtypeStruct((b, s, N_HEADS * V_DIM), BF16),
        grid=(b, N_HEADS // heads, s // tq),
        in_specs=in_specs,
        out_specs=pl.BlockSpec((1, tq, heads * V_DIM), lambda bi, hi, qi: (bi, qi, hi)),
        compiler_params=_cparams(("arbitrary", "arbitrary", "arbitrary")),
        name=name,
    )(*args)


def _outproj_kernel(x_ref, bg_ref, cu_ref, cup_ref, cun_ref, o_ref, mod_ref, cw_ref, w_out_ref,
                    n2g_ref, wr_ref, x1_ref, h2_ref, rid_ref, rw_ref, *, tm, seq_len):
    i = pl.program_id(0)
    mod = mod_ref[0]
    cu = cu_ref[...].astype(F32)
    prev_row = cup_ref[...].astype(F32)[15:16, :]
    next_row = cun_ref[...].astype(F32)[0:1, :]
    row = lax.broadcasted_iota(jnp.int32, (tm, 1), 0)
    pos = (i * tm + row) & (seq_len - 1)
    up = jnp.where(row == 0, prev_row, pltpu.roll(cu, 1, axis=0))
    up = jnp.where(pos == 0, 0.0, up)
    dn = jnp.where(row == tm - 1, next_row, pltpu.roll(cu, tm - 1, axis=0))
    dn = jnp.where(pos == seq_len - 1, 0.0, dn)
    cw = cw_ref[...]
    y_conv = bg_ref[...].astype(F32) * (up * cw[0:1, :] + cu * cw[1:2, :] + dn * cw[2:3, :])
    mix = jnp.dot(y_conv.astype(BF16), w_out_ref[:CONV_WIDTH, :], preferred_element_type=F32)
    mix = mix + jnp.dot(o_ref[...], w_out_ref[CONV_WIDTH:, :], preferred_element_type=F32)
    x1 = x_ref[...] + mod[2:3, :] * mix
    x1_ref[...] = x1
    h2 = _rms(x1) * n2g_ref[...] * (1.0 + mod[4:5, :]) + mod[3:4, :]
    h2_ref[...] = h2

    h2_hi = h2.astype(BF16)
    h2_lo = (h2 - h2_hi.astype(F32)).astype(BF16)
    hh_hl = jnp.dot(h2_hi, wr_ref[...], preferred_element_type=F32)
    lh = jnp.dot(h2_lo, wr_ref[:, :ROUTER_COLS], preferred_element_type=F32)
    logits = hh_hl[:, :ROUTER_COLS] + (hh_hl[:, ROUTER_COLS:] + lh)
    lane = lax.broadcasted_iota(jnp.int32, logits.shape, 1)
    neg = -jnp.inf
    big = jnp.int32(1 << 20)
    gl = jnp.where(lane < N_EXPERT_GROUPS, logits, neg)
    gmax = jnp.max(gl, axis=-1, keepdims=True)
    p_top = 1.0 / _rowsum(jnp.exp(gl - gmax))
    g_top = jnp.min(jnp.where(gl == gmax, lane, big), axis=-1, keepdims=True)
    e_lo = N_EXPERT_GROUPS + EXPERTS_PER_GROUP * g_top
    el = jnp.where((lane >= e_lo) & (lane < e_lo + EXPERTS_PER_GROUP), logits, neg)
    v1 = jnp.max(el, axis=-1, keepdims=True)
    i1 = jnp.min(jnp.where(el == v1, lane, big), axis=-1, keepdims=True)
    el2 = jnp.where(lane == i1, neg, el)
    v2 = jnp.max(el2, axis=-1, keepdims=True)
    i2 = jnp.min(jnp.where(el2 == v2, lane, big), axis=-1, keepdims=True)
    e21 = jnp.exp(v2 - v1)
    w1 = p_top / (1.0 + e21)
    w2 = w1 * e21
    rid_ref[...] = jnp.where(lane == 0, i1 - N_EXPERT_GROUPS, i2 - N_EXPERT_GROUPS)
    rw_ref[...] = jnp.where(lane == 0, w1, w2)


def _outproj(x2d, bg, cu, o2d, mod3, conv_w, w_out_b, n2g, wr, *, cond_base, tiles_per_cond, seq_len):
    t = x2d.shape[0]
    tm = TM_OUT
    hb = tm // 16
    nhb = t // 16
    return pl.pallas_call(
        functools.partial(_outproj_kernel, tm=tm, seq_len=seq_len),
        out_shape=(jax.ShapeDtypeStruct((t, D_MODEL), F32),
                   jax.ShapeDtypeStruct((t, D_MODEL), F32),
                   jax.ShapeDtypeStruct((t, LANES), jnp.int32),
                   jax.ShapeDtypeStruct((t, LANES), F32)),
        grid=(t // tm,),
        in_specs=[pl.BlockSpec((tm, D_MODEL), lambda i: (i, 0)),
                  pl.BlockSpec((tm, CONV_WIDTH), lambda i: (i, 0)),
                  pl.BlockSpec((tm, CONV_WIDTH), lambda i: (i, 0)),
                  pl.BlockSpec((16, CONV_WIDTH), lambda i: (jnp.maximum(i * hb - 1, 0), 0)),
                  pl.BlockSpec((16, CONV_WIDTH), lambda i: (jnp.minimum((i + 1) * hb, nhb - 1), 0)),
                  pl.BlockSpec((tm, N_HEADS * V_DIM), lambda i: (i, 0)),
                  pl.BlockSpec((1, N_MOD, D_MODEL), lambda i: (cond_base + i // tiles_per_cond, 0, 0)),
                  _const_spec((3, CONV_WIDTH)),
                  _const_spec((D_MODEL, D_MODEL)),
                  _const_spec((1, D_MODEL)),
                  _const_spec((D_MODEL, 2 * ROUTER_COLS))],
        out_specs=(pl.BlockSpec((tm, D_MODEL), lambda i: (i, 0)),
                   pl.BlockSpec((tm, D_MODEL), lambda i: (i, 0)),
                   pl.BlockSpec((tm, LANES), lambda i: (i, 0)),
                   pl.BlockSpec((tm, LANES), lambda i: (i, 0))),
        compiler_params=_cparams(("arbitrary",)),
        name="outproj_ctx" if cond_base == 0 else "outproj_lat",
    )(x2d, bg, cu, cu, cu, o2d, mod3, conv_w, w_out_b, n2g, wr)


def _route_tables(rid, *, tm):
    n_pairs = rid.shape[0]
    nt = n_pairs // tm + N_EXPERTS
    experts = jnp.arange(N_EXPERTS, dtype=jnp.int32)
    onehot = (rid[None, :] == experts[:, None]).astype(jnp.int32)
    csum = jnp.cumsum(onehot, axis=1)
    counts = csum[:, -1]
    tiles_e = (counts + tm - 1) // tm
    tile_end = jnp.cumsum(tiles_e)
    tile_start = tile_end - tiles_e
    n_valid = tile_end[-1]
    pos = jnp.sum(onehot * (csum - 1 + tile_start[:, None] * tm), axis=0)
    tile_idx = jnp.minimum(jnp.arange(nt, dtype=jnp.int32), n_valid - 1)
    te = jnp.sum((tile_end[None, :] <= tile_idx[:, None]).astype(jnp.int32), axis=1)
    return (pos, te, n_valid.reshape(1).astype(jnp.int32),
            (tile_start * tm + counts).astype(jnp.int32), (tiles_e * tm - counts).astype(jnp.int32))


def _dispatch_kernel(pstart_ref, pcnt_ref, nval_ref, pos0_ref, pos1_ref, h2a_ref, h2b_ref, xs_hbm,
                     buf, zbuf, sem, sem_z, *, tm, nt, n_a):
    i = pl.program_id(0)
    n = pl.num_programs(0)
    slot = i & 1

    def row_copy(r, pos_ref, s):
        return pltpu.make_async_copy(buf.at[s, pl.ds(r, 1)], xs_hbm.at[pl.ds(pos_ref[0, 0, r], 1)], sem.at[s])

    def wait_tile(s):
        for _ in range(2):
            pltpu.make_async_copy(buf.at[s], xs_hbm.at[pl.ds(0, tm)], sem.at[s]).wait()

    def pad_copies(e, fn):
        cnt = pcnt_ref[e]
        start = pstart_ref[e]
        off = start + cnt
        for b in PAD_BITS:
            if b < SUBLANES:
                break
            off = off - (cnt & b)
            dst = pl.multiple_of(off, SUBLANES)

            @pl.when((cnt & b) != 0)
            def _():
                fn(pltpu.make_async_copy(zbuf.at[pl.ds(0, b)], xs_hbm.at[pl.ds(dst, b)], sem_z))
        for j in range(SUBLANES - 1):
            @pl.when(j < (cnt & (SUBLANES - 1)))
            def _():
                fn(pltpu.make_async_copy(zbuf.at[pl.ds(0, 1)], xs_hbm.at[pl.ds(start + j, 1)], sem_z))

    def tail_copy(j):
        return pltpu.make_async_copy(zbuf, xs_hbm.at[pl.ds(pl.multiple_of(j * tm, tm), tm)], sem_z)

    @pl.when(i == 0)
    def _():
        zbuf[...] = jnp.zeros((tm, D_MODEL), F32)
        lax.fori_loop(0, N_EXPERTS, lambda e, c: (pad_copies(e, lambda d: d.start()), c)[1], 0)
        lax.fori_loop(nval_ref[0], nt, lambda j, c: (tail_copy(j).start(), c)[1], 0)
        lax.fori_loop(0, N_EXPERTS, lambda e, c: (pad_copies(e, lambda d: d.wait()), c)[1], 0)
        lax.fori_loop(nval_ref[0], nt, lambda j, c: (tail_copy(j).wait(), c)[1], 0)

    @pl.when(i >= 2)
    def _():
        wait_tile(slot)

    @pl.when(i < n_a)
    def _():
        buf[slot] = h2a_ref[...]

    @pl.when(i >= n_a)
    def _():
        buf[slot] = h2b_ref[...]

    for r in range(tm):
        row_copy(r, pos0_ref, slot).start()
        row_copy(r, pos1_ref, slot).start()

    @pl.when(i == n - 1)
    def _():
        wait_tile(slot)
        wait_tile(1 - slot)


def _dispatch(pad_start, pad_cnt, n_valid, pos3, h2_a, h2_b, *, nt):
    tm = TM_ROW
    n_a, n_b = h2_a.shape[0] // tm, h2_b.shape[0] // tm
    smem_tile = functools.partial(pl.BlockSpec, (1, 1, tm), memory_space=pltpu.SMEM)
    grid_spec = pltpu.PrefetchScalarGridSpec(
        num_scalar_prefetch=3,
        grid=(n_a + n_b,),
        in_specs=[smem_tile(lambda i, *_: (i, 0, 0)),
                  smem_tile(lambda i, *_: (n_a + n_b + i, 0, 0)),
                  pl.BlockSpec((tm, D_MODEL), lambda i, *_: (jnp.minimum(i, n_a - 1), 0)),
                  pl.BlockSpec((tm, D_MODEL), lambda i, *_: (jnp.maximum(i - n_a, 0), 0))],
        out_specs=pl.BlockSpec(memory_space=pl.ANY),
        scratch_shapes=[pltpu.VMEM((2, tm, D_MODEL), F32),
                        pltpu.VMEM((tm, D_MODEL), F32),
                        pltpu.SemaphoreType.DMA((2,)),
                        pltpu.SemaphoreType.DMA])
    return pl.pallas_call(
        functools.partial(_dispatch_kernel, tm=tm, nt=nt, n_a=n_a),
        out_shape=jax.ShapeDtypeStruct((nt * TM_MOE, D_MODEL), F32),
        grid_spec=grid_spec,
        compiler_params=pltpu.CompilerParams(dimension_semantics=("arbitrary",),
                                             vmem_limit_bytes=VMEM_LIMIT, has_side_effects=True),
        name="dispatch",
    )(pad_start, pad_cnt, n_valid, pos3, pos3, h2_a, h2_b)


def _moe_kernel(texp_ref, nval_ref, xs_ref, wg_ref, wu_ref, wd_ref, y_ref, wgb, wub, wdb):
    i = pl.program_id(0)

    @pl.when(i < nval_ref[0])
    def _():
        e = texp_ref[i]
        e_prev = texp_ref[jnp.maximum(i - 1, 0)]

        @pl.when((i == 0) | (e != e_prev))
        def _():
            wgb[...] = wg_ref[0].astype(BF16)
            wub[...] = wu_ref[0].astype(BF16)
            wdb[...] = wd_ref[0].astype(BF16)

        x = xs_ref[...].astype(BF16)
        g = jnp.dot(x, wgb[...], preferred_element_type=F32)
        u = jnp.dot(x, wub[...], preferred_element_type=F32)
        a = g / (1.0 + jnp.exp(-g)) * u
        y_ref[...] = jnp.dot(a.astype(BF16), wdb[...], preferred_element_type=F32)

    @pl.when(i >= nval_ref[0])
    def _():
        y_ref[...] = jnp.zeros(y_ref.shape, F32)


def _moe(tile_expert, n_valid, xs, w_gate, w_up, w_down):
    nt = tile_expert.shape[0]
    tm = TM_MOE
    grid_spec = pltpu.PrefetchScalarGridSpec(
        num_scalar_prefetch=2,
        grid=(nt,),
        in_specs=[pl.BlockSpec((tm, D_MODEL), lambda i, te, nv: (jnp.minimum(i, nv[0] - 1), 0)),
                  pl.BlockSpec((1, D_MODEL, D_EXPERT), lambda i, te, nv: (te[i], 0, 0)),
                  pl.BlockSpec((1, D_MODEL, D_EXPERT), lambda i, te, nv: (te[i], 0, 0)),
                  pl.BlockSpec((1, D_EXPERT, D_MODEL), lambda i, te, nv: (te[i], 0, 0))],
        out_specs=pl.BlockSpec((tm, D_MODEL), lambda i, te, nv: (i, 0)),
        scratch_shapes=[pltpu.VMEM((D_MODEL, D_EXPERT), BF16),
                        pltpu.VMEM((D_MODEL, D_EXPERT), BF16),
                        pltpu.VMEM((D_EXPERT, D_MODEL), BF16)])
    return pl.pallas_call(
        _moe_kernel,
        out_shape=jax.ShapeDtypeStruct((nt * tm, D_MODEL), F32),
        grid_spec=grid_spec,
        compiler_params=_cparams(("arbitrary",)),
        name="moe",
    )(tile_expert, n_valid, xs, w_gate, w_up, w_down)


def _final_kernel(pa0_ref, pb0_ref, pa_ref, pb_ref, x1_ref, rw_ref, mod_ref, y_hbm, o_ref, ybuf, sem, *, tm):
    i = pl.program_id(0)
    n = pl.num_programs(0)
    slot = i & 1

    def start_tile(pa, pb, s):
        for r in range(tm):
            pltpu.make_async_copy(y_hbm.at[pl.ds(pa[0, 0, r], 1)], ybuf.at[s, 0, pl.ds(r, 1)], sem.at[s]).start()
            pltpu.make_async_copy(y_hbm.at[pl.ds(pb[0, 0, r], 1)], ybuf.at[s, 1, pl.ds(r, 1)], sem.at[s]).start()

    @pl.when(i == 0)
    def _():
        start_tile(pa0_ref, pb0_ref, 0)

    @pl.when(i + 1 < n)
    def _():
        start_tile(pa_ref, pb_ref, 1 - slot)

    for k in range(2):
        pltpu.make_async_copy(y_hbm.at[pl.ds(0, tm)], ybuf.at[slot, k], sem.at[slot]).wait()
    w = rw_ref[...]
    moe = w[:, 0:1] * ybuf[slot, 0] + w[:, 1:2] * ybuf[slot, 1]
    o_ref[...] = x1_ref[...] + mod_ref[0][5:6, :] * moe


def _final(x1, rw, y, pos3, mod3, *, tile_base, slot_tiles, cond_base, tiles_per_cond):
    t = x1.shape[0]
    tm = TM_ROW
    n = t // tm
    smem_tile = functools.partial(pl.BlockSpec, (1, 1, tm), memory_space=pltpu.SMEM)
    return pl.pallas_call(
        functools.partial(_final_kernel, tm=tm),
        out_shape=jax.ShapeDtypeStruct((t, D_MODEL), F32),
        grid=(n,),
        in_specs=[smem_tile(lambda i: (tile_base, 0, 0)),
                  smem_tile(lambda i: (slot_tiles + tile_base, 0, 0)),
                  smem_tile(lambda i: (tile_base + jnp.minimum(i + 1, n - 1), 0, 0)),
                  smem_tile(lambda i: (slot_tiles + tile_base + jnp.minimum(i + 1, n - 1), 0, 0)),
                  pl.BlockSpec((tm, D_MODEL), lambda i: (i, 0)),
                  pl.BlockSpec((tm, LANES), lambda i: (i, 0)),
                  pl.BlockSpec((1, N_MOD, D_MODEL), lambda i: (cond_base + i // tiles_per_cond, 0, 0)),
                  pl.BlockSpec(memory_space=pl.ANY)],
        out_specs=pl.BlockSpec((tm, D_MODEL), lambda i: (i, 0)),
        scratch_shapes=[pltpu.VMEM((2, 2, tm, D_MODEL), F32), pltpu.SemaphoreType.DMA((2,))],
        compiler_params=_cparams(("arbitrary",)),
        name="final_ctx" if cond_base == 0 else "final_lat",
    )(pos3, pos3, pos3, pos3, x1, rw, mod3, y)


def _rope_table(n_tokens):
    rows = n_tokens // GRID_W
    inv = ROPE_BASE ** (-jnp.arange(ROPE_PAIRS, dtype=F32) / ROPE_PAIRS)
    row_ang = jnp.arange(rows, dtype=F32)[:, None] * inv
    col_ang = jnp.arange(GRID_W, dtype=F32)[:, None] * inv
    cr, sr = (jnp.repeat(f(row_ang), GRID_W, axis=0) for f in (jnp.cos, jnp.sin))
    cc, sc = (jnp.tile(f(col_ang), (rows, 1)) for f in (jnp.cos, jnp.sin))
    return jnp.concatenate([cr, cr, cc, cc, -sr, sr, -sc, sc], axis=-1)


def _swap_halves(a):
    p = ROPE_PAIRS
    return jnp.concatenate([a[..., p:2 * p], a[..., :p], a[..., 3 * p:], a[..., 2 * p:3 * p]], axis=-1)


def _head_gains(g):
    rope = g[QK_NOPE:]
    return jnp.stack([g[:QK_NOPE], jnp.concatenate([rope, _swap_halves(rope)])])


def kernel(x_prompt, x_sample, cache_ckv, cache_krope, c, c_ctx, ada_w, ada_b, norm1_g, w_in, conv_w,
           q_lora_g, w_uq, kv_lora_g, w_ukv, q_head_g, k_head_g, w_out, norm2_g, router_g, router_e,
           w_gate, w_up, w_down):
    depth = ada_w.shape[0]
    assert depth == 1
    bp, sp, _ = x_prompt.shape
    bs, ss, _ = x_sample.shape
    past = cache_ckv.shape[2]
    tp, ts = bp * sp, bs * ss
    assert sp & (sp - 1) == 0 and ss & (ss - 1) == 0
    assert tp % TM_IN == 0 and ss % TM_IN == 0 and TM_IN % sp == 0 and ss % TQ == 0
    assert bs + 1 <= 8 and past % 256 == 0 and TM_ROW == TM_MOE and tp % TM_ROW == 0 and ts % TM_ROW == 0

    w_in_b = jnp.concatenate([w_in[0], _swap_halves(w_in[0][:, IN_COLS - 2 * QK_ROPE:])], axis=1).astype(BF16)
    uq = w_uq[0].reshape(Q_LORA, N_HEADS, QK_DIM)
    w_uq_b = jnp.concatenate([uq, _swap_halves(uq[..., QK_NOPE:])], axis=-1)
    w_uq_b = w_uq_b.reshape(Q_LORA, N_HEADS * HEAD_W).astype(BF16)
    ukv = w_ukv[0].reshape(KV_LORA, N_HEADS, QK_NOPE + V_DIM)
    w_uk_b = ukv[..., :QK_NOPE].reshape(KV_LORA, N_HEADS * QK_NOPE).astype(BF16)
    w_uvt_b = ukv[..., QK_NOPE:].reshape(KV_LORA, N_HEADS * V_DIM).T.astype(BF16)
    w_out_b = w_out[0].astype(BF16)
    wr = jnp.concatenate([router_g[0], router_e[0],
                          jnp.zeros((D_MODEL, ROUTER_COLS - N_EXPERT_GROUPS - N_EXPERTS), F32)], axis=1)
    wr_hi = lax.bitcast_convert_type(lax.bitcast_convert_type(wr, jnp.uint32) & jnp.uint32(0xFFFF0000), F32)
    wr = jnp.concatenate([wr_hi, wr - wr_hi], axis=1).astype(BF16)
    gq, gk = _head_gains(q_head_g[0]), _head_gains(k_head_g[0])
    n1g, n2g = norm1_g[0].reshape(1, D_MODEL), norm2_g[0].reshape(1, D_MODEL)
    qlg, kvg = q_lora_g[0].reshape(1, Q_LORA), kv_lora_g[0].reshape(1, KV_LORA)
    cs_lat = _rope_table(ss)
    cs_id = jnp.concatenate([jnp.ones((TM_IN, QK_ROPE), F32), jnp.zeros((TM_IN, QK_ROPE), F32)], axis=1)

    cond8 = jnp.concatenate([c_ctx[None, :], c, jnp.zeros((8 - 1 - bs, D_MODEL), F32)], axis=0)
    mod3 = _modulation(cond8, ada_w[0], ada_b[0]).reshape(8, N_MOD, D_MODEL)

    xp2, xs2 = x_prompt.reshape(tp, D_MODEL), x_sample.reshape(ts, D_MODEL)
    big = 1 << 30

    bg_p, cu_p, q_p, k_p, vt_p, ckv_p, kr_p = _inproj(
        xp2, mod3, n1g, w_in_b, qlg, w_uq_b, kvg, w_uk_b, w_uvt_b, gq, gk, cs_id,
        cond_base=0, tiles_per_cond=big, cs_tiles=1, emit_cache=True)
    o_p = _attention(q_p.reshape(bp, sp, -1), [(k_p.reshape(bp, sp, -1), vt_p)],
                     tq=sp, heads=N_HEADS, name="attn_ctx")
    x1_p, h2_p, rid_p, rw_p = _outproj(
        xp2, bg_p, cu_p, o_p.reshape(tp, -1), mod3, conv_w[0], w_out_b, n2g, wr,
        cond_base=0, tiles_per_cond=big, seq_len=sp)

    kr_c = cache_krope[:, 0].reshape(bs * past, QK_ROPE)
    k_c, vt_c = _kvcache(cache_ckv[:, 0].reshape(bs * past, KV_LORA), jnp.concatenate([kr_c, kr_c], axis=1),
                         cs_id, w_uk_b, w_uvt_b, gk)
    bg_s, cu_s, q_s, k_s, vt_s = _inproj(
        xs2, mod3, n1g, w_in_b, qlg, w_uq_b, kvg, w_uk_b, w_uvt_b, gq, gk, cs_lat,
        cond_base=1, tiles_per_cond=ss // TM_IN, cs_tiles=ss // TM_IN, emit_cache=False)
    o_s = _attention(q_s.reshape(bs, ss, -1),
                     [(k_c.reshape(bs, past, -1), vt_c), (k_s.reshape(bs, ss, -1), vt_s)],
                     tq=TQ, heads=1, name="attn_lat")
    x1_s, h2_s, rid_s, rw_s = _outproj(
        xs2, bg_s, cu_s, o_s.reshape(ts, -1), mod3, conv_w[0], w_out_b, n2g, wr,
        cond_base=1, tiles_per_cond=ss // TM_OUT, seq_len=ss)

    n_tok = tp + ts
    slot_tiles = n_tok // TM_ROW
    nt = 2 * n_tok // TM_MOE + N_EXPERTS
    rid = jnp.concatenate([rid_p[:, :2], rid_s[:, :2]], axis=0).T.reshape(2 * n_tok)
    pos, te, n_valid, pad_start, pad_cnt = _route_tables(rid, tm=TM_MOE)
    pos3 = pos.reshape(2 * slot_tiles, 1, TM_ROW)
    xs = _dispatch(pad_start, pad_cnt, n_valid, pos3, h2_p, h2_s, nt=nt)
    y = _moe(te, n_valid, xs, w_gate[0], w_up[0], w_down[0])

    y_p = _final(x1_p, rw_p, y, pos3, mod3, tile_base=0, slot_tiles=slot_tiles, cond_base=0, tiles_per_cond=big)
    y_s = _final(x1_s, rw_s, y, pos3, mod3, tile_base=tp // TM_ROW, slot_tiles=slot_tiles, cond_base=1,
                 tiles_per_cond=ss // TM_ROW)

    return (y_p.reshape(bp, sp, D_MODEL), y_s.reshape(bs, ss, D_MODEL),
            ckv_p.reshape(bp, 1, sp, KV_LORA), kr_p.reshape(bp, 1, sp, QK_ROPE))
```

```python
import functools

import jax
import jax.numpy as jnp
from jax import lax
from jax.experimental import pallas as pl
from jax.experimental.pallas import tpu as pltpu

F32 = jnp.float32
BF16 = jnp.bfloat16
HIGHEST = lax.Precision.HIGHEST

D_MODEL = 2048
CONV_WIDTH = 1024
N_HEADS = 8
QK_NOPE = 128
QK_ROPE = 64
V_DIM = 128
QK_DIM = QK_NOPE + QK_ROPE
Q_LORA = 512
KV_LORA = 256
GRID_W = 64
ROPE_PAIRS = QK_ROPE // 4
ROPE_BASE = 10000.0
N_EXPERT_GROUPS = 4
EXPERTS_PER_GROUP = 8
N_EXPERTS = N_EXPERT_GROUPS * EXPERTS_PER_GROUP
D_EXPERT = 512
N_MOD = 6
EPS = 1e-6
LOG2_E = 1.4426950408889634

HEAD_W = 2 * QK_NOPE
LANES = 128
SUBLANES = 8
IN_COLS = 3 * CONV_WIDTH + Q_LORA + KV_LORA + 2 * QK_ROPE
ROUTER_COLS = LANES
VMEM_LIMIT = 56 * 1024 * 1024

TM_IN = 512
TM_OUT = 512
TQ = 512
ATTN_CHUNK = 1024
TM_MOE = 256
TM_ROW = 256
BN_MOD = 1024
PAD_BITS = tuple(1 << b for b in reversed(range(TM_MOE.bit_length() - 1)))


def _cparams(sem):
    return pltpu.CompilerParams(dimension_semantics=sem, vmem_limit_bytes=VMEM_LIMIT)


def _const_spec(shape):
    nd = len(shape)
    return pl.BlockSpec(shape, lambda *_: (0,) * nd, pipeline_mode=pl.Buffered(1))


def _rms(x):
    return x * lax.rsqrt(jnp.mean(x * x, axis=-1, keepdims=True) + EPS)


def _rowsum(x):
    return jnp.sum(x, axis=-1, keepdims=True)


def _mod_kernel(c_ref, w_ref, b_ref, o_ref):
    c = c_ref[...]
    s = c / (1.0 + jnp.exp(-c))
    o_ref[...] = jnp.dot(s, w_ref[...], precision=HIGHEST, preferred_element_type=F32) + b_ref[...]


def _modulation(cond8, ada_w, ada_b):
    n = ada_w.shape[1]
    return pl.pallas_call(
        _mod_kernel,
        out_shape=jax.ShapeDtypeStruct((8, n), F32),
        grid=(n // BN_MOD,),
        in_specs=[pl.BlockSpec((8, D_MODEL), lambda j: (0, 0)),
                  pl.BlockSpec((D_MODEL, BN_MOD), lambda j: (0, j)),
                  pl.BlockSpec((1, BN_MOD), lambda j: (0, j))],
        out_specs=pl.BlockSpec((8, BN_MOD), lambda j: (0, j)),
        compiler_params=_cparams(("arbitrary",)),
        name="mod",
    )(cond8, ada_w, ada_b.reshape(1, n))


def _emit_kv(ckv, kraw, cs, w_uk_ref, w_uvt_ref, gk_ref, k_ref, vt_ref):
    cb = ckv.astype(BF16)
    kn_all = jnp.dot(cb, w_uk_ref[...], preferred_element_type=F32)
    vt = lax.dot_general(w_uvt_ref[...], cb, (((1,), (1,)), ((), ())), preferred_element_type=F32)
    vt_ref[...] = vt.astype(BF16)
    ss_rope = 0.5 * _rowsum(kraw * kraw)
    t = kraw * (cs * gk_ref[1:2, :])
    tt = t + pltpu.roll(t, QK_ROPE, axis=1)
    g_nope = gk_ref[0:1, :]
    for h in range(N_HEADS):
        kn = kn_all[:, h * QK_NOPE:(h + 1) * QK_NOPE]
        r = lax.rsqrt((_rowsum(kn * kn) + ss_rope) * (1.0 / QK_DIM) + EPS)
        k_ref[:, h * HEAD_W:h * HEAD_W + QK_NOPE] = (kn * r * g_nope).astype(BF16)
        k_ref[:, h * HEAD_W + QK_NOPE:(h + 1) * HEAD_W] = (tt * r).astype(BF16)


def _kvcache_kernel(ckv_ref, kraw_ref, cs_ref, w_uk_ref, w_uvt_ref, gk_ref, k_ref, vt_ref):
    _emit_kv(ckv_ref[...], kraw_ref[...], cs_ref[...], w_uk_ref, w_uvt_ref, gk_ref, k_ref, vt_ref)


def _kvcache(ckv, kraw, cs_id, w_uk_b, w_uvt_b, gk):
    n = ckv.shape[0]
    tm = 256
    return pl.pallas_call(
        _kvcache_kernel,
        out_shape=(jax.ShapeDtypeStruct((n, N_HEADS * HEAD_W), BF16),
                   jax.ShapeDtypeStruct((N_HEADS * V_DIM, n), BF16)),
        grid=(n // tm,),
        in_specs=[pl.BlockSpec((tm, KV_LORA), lambda i: (i, 0)),
                  pl.BlockSpec((tm, LANES), lambda i: (i, 0)),
                  pl.BlockSpec((tm, LANES), lambda i: (0, 0)),
                  _const_spec((KV_LORA, N_HEADS * QK_NOPE)),
                  _const_spec((N_HEADS * V_DIM, KV_LORA)),
                  _const_spec((2, LANES))],
        out_specs=(pl.BlockSpec((tm, N_HEADS * HEAD_W), lambda i: (i, 0)),
                   pl.BlockSpec((N_HEADS * V_DIM, tm), lambda i: (0, i))),
        compiler_params=_cparams(("arbitrary",)),
        name="kvcache",
    )(ckv, kraw, cs_id, w_uk_b, w_uvt_b, gk)


def _inproj_kernel(x_ref, mod_ref, n1g_ref, w_in_ref, qlg_ref, w_uq_ref, kvg_ref, w_uk_ref, w_uvt_ref,
                   gq_ref, gk_ref, cs_ref, bg_ref, cu_ref, q_ref, k_ref, vt_ref, *cache_refs):
    x = x_ref[...]
    mod = mod_ref[0]
    h = _rms(x) * n1g_ref[...] * (1.0 + mod[1:2, :]) + mod[0:1, :]
    hb = h.astype(BF16)

    def proj(a, b):
        return jnp.dot(hb, w_in_ref[:, a:b], preferred_element_type=F32)

    c1, c2, c3 = CONV_WIDTH, 2 * CONV_WIDTH, 3 * CONV_WIDTH
    bg_ref[...] = proj(0, c1).astype(BF16)
    cu_ref[...] = (proj(c1, c2) * proj(c2, c3)).astype(BF16)
    q_lat = proj(c3, c3 + Q_LORA)
    kvk = proj(c3 + Q_LORA, IN_COLS)
    cs = cs_ref[...]

    qn = _rms(q_lat) * qlg_ref[...]
    q = jnp.dot(qn.astype(BF16), w_uq_ref[...], preferred_element_type=F32)
    scale = QK_DIM ** -0.5 * LOG2_E
    g_nope = gq_ref[0:1, :] * scale
    tq = cs * (gq_ref[1:2, :] * scale)
    for hd in range(N_HEADS):
        lo = q[:, hd * HEAD_W:hd * HEAD_W + QK_NOPE]
        up = q[:, hd * HEAD_W + QK_NOPE:(hd + 1) * HEAD_W]
        ss = _rowsum(lo * lo) + 0.5 * _rowsum(up * up)
        r = lax.rsqrt(ss * (1.0 / QK_DIM) + EPS)
        q_ref[:, hd * HEAD_W:hd * HEAD_W + QK_NOPE] = (lo * r * g_nope).astype(BF16)
        q_ref[:, hd * HEAD_W + QK_NOPE:(hd + 1) * HEAD_W] = (up * r * tq).astype(BF16)

    kv_lat = kvk[:, :KV_LORA]
    kraw = kvk[:, KV_LORA:]
    ckv = _rms(kv_lat) * kvg_ref[...]
    if cache_refs:
        ckv_out_ref, kr_out_ref = cache_refs
        ckv_out_ref[...] = ckv
        kr_out_ref[...] = kraw[:, :QK_ROPE]
    _emit_kv(ckv, kraw, cs, w_uk_ref, w_uvt_ref, gk_ref, k_ref, vt_ref)


def _inproj(x2d, mod3, n1g, w_in_b, qlg, w_uq_b, kvg, w_uk_b, w_uvt_b, gq, gk, cs, *,
            cond_base, tiles_per_cond, cs_tiles, emit_cache):
    t = x2d.shape[0]
    tm = TM_IN
    out_shape = [jax.ShapeDtypeStruct((t, CONV_WIDTH), BF16),
                 jax.ShapeDtypeStruct((t, CONV_WIDTH), BF16),
                 jax.ShapeDtypeStruct((t, N_HEADS * HEAD_W), BF16),
                 jax.ShapeDtypeStruct((t, N_HEADS * HEAD_W), BF16),
                 jax.ShapeDtypeStruct((N_HEADS * V_DIM, t), BF16)]
    out_specs = [pl.BlockSpec((tm, CONV_WIDTH), lambda i: (i, 0)),
                 pl.BlockSpec((tm, CONV_WIDTH), lambda i: (i, 0)),
                 pl.BlockSpec((tm, N_HEADS * HEAD_W), lambda i: (i, 0)),
                 pl.BlockSpec((tm, N_HEADS * HEAD_W), lambda i: (i, 0)),
                 pl.BlockSpec((N_HEADS * V_DIM, tm), lambda i: (0, i))]
    if emit_cache:
        out_shape += [jax.ShapeDtypeStruct((t, KV_LORA), F32), jax.ShapeDtypeStruct((t, QK_ROPE), F32)]
        out_specs += [pl.BlockSpec((tm, KV_LORA), lambda i: (i, 0)),
                      pl.BlockSpec((tm, QK_ROPE), lambda i: (i, 0))]
    return pl.pallas_call(
        _inproj_kernel,
        out_shape=tuple(out_shape),
        grid=(t // tm,),
        in_specs=[pl.BlockSpec((tm, D_MODEL), lambda i: (i, 0)),
                  pl.BlockSpec((1, N_MOD, D_MODEL), lambda i: (cond_base + i // tiles_per_cond, 0, 0)),
                  _const_spec((1, D_MODEL)),
                  _const_spec((D_MODEL, IN_COLS)),
                  _const_spec((1, Q_LORA)),
                  _const_spec((Q_LORA, N_HEADS * HEAD_W)),
                  _const_spec((1, KV_LORA)),
                  _const_spec((KV_LORA, N_HEADS * QK_NOPE)),
                  _const_spec((N_HEADS * V_DIM, KV_LORA)),
                  _const_spec((2, LANES)),
                  _const_spec((2, LANES)),
                  pl.BlockSpec((tm, LANES), lambda i: (i % cs_tiles, 0))],
        out_specs=tuple(out_specs),
        compiler_params=_cparams(("arbitrary",)),
        name="inproj_ctx" if emit_cache else "inproj_lat",
    )(x2d, mod3, n1g, w_in_b, qlg, w_uq_b, kvg, w_uk_b, w_uvt_b, gq, gk, cs)


def _attn_kernel(*refs, n_kv, heads):
    q_ref = refs[0]
    o_ref = refs[-1]
    chunks = []
    for j in range(n_kv):
        sk = refs[1 + 2 * j].shape[1]
        step = min(sk, ATTN_CHUNK)
        chunks += [(j, lo, lo + step) for lo in range(0, sk, step)]
    for h in range(heads):
        q = q_ref[0, :, h * HEAD_W:(h + 1) * HEAD_W]
        def score(c):
            j, lo, hi = chunks[c]
            k = refs[1 + 2 * j][0, lo:hi, h * HEAD_W:(h + 1) * HEAD_W]
            return lax.dot_general(k, q, (((1,), (1,)), ((), ())), preferred_element_type=F32)

        m = None
        acc = None
        s_next = score(0)
        for c, (j, lo, hi) in enumerate(chunks):
            s = s_next
            if c + 1 < len(chunks):
                s_next = score(c + 1)
            mc = jnp.max(s, axis=0, keepdims=True)
            m_new = mc if m is None else jnp.maximum(m, mc)
            p = jnp.exp2(s - m_new).astype(BF16)
            vt = refs[2 + 2 * j][h * V_DIM:(h + 1) * V_DIM, lo:hi]
            vt1 = jnp.concatenate([vt, jnp.ones((2 * SUBLANES, hi - lo), BF16)], axis=0)
            part = jnp.dot(vt1, p, preferred_element_type=F32)
            acc = part if acc is None else acc * jnp.exp2(m - m_new) + part
            m = m_new
        ot = acc[:V_DIM, :] / acc[V_DIM:V_DIM + 1, :]
        o_ref[0, :, h * V_DIM:(h + 1) * V_DIM] = ot.T.astype(BF16)


def _attention(q, kvs, *, tq, heads, name):
    b, s, _ = q.shape
    in_specs = [pl.BlockSpec((1, tq, heads * HEAD_W), lambda bi, hi, qi: (bi, qi, hi))]
    args = [q]
    for k, v in kvs:
        sk = k.shape[1]
        in_specs.append(pl.BlockSpec((1, sk, heads * HEAD_W), lambda bi, hi, qi: (bi, 0, hi)))
        in_specs.append(pl.BlockSpec((heads * V_DIM, sk), lambda bi, hi, qi: (hi, bi)))
        args += [k, v]
    return pl.pallas_call(
        functools.partial(_attn_kernel, n_kv=len(kvs), heads=heads),
        out_shape=jax.ShapeDtypeStruct((b, s, N_HEADS * V_DIM), BF16),
        grid=(b, N_HEADS // heads, s // tq),
        in_specs=in_specs,
        out_specs=pl.BlockSpec((1, tq, heads * V_DIM), lambda bi, hi, qi: (bi, qi, hi)),
        compiler_params=_cparams(("arbitrary", "arbitrary", "arbitrary")),
        name=name,
    )(*args)


def _outproj_kernel(x_ref, bg_ref, cu_ref, cup_ref, cun_ref, o_ref, mod_ref, cw_ref, w_out_ref,
                    n2g_ref, wr_ref, x1_ref, h2_ref, rid_ref, rw_ref, *, tm, seq_len):
    i = pl.program_id(0)
    mod = mod_ref[0]
    cu = cu_ref[...].astype(F32)
    prev_row = cup_ref[...].astype(F32)[15:16, :]
    next_row = cun_ref[...].astype(F32)[0:1, :]
    row = lax.broadcasted_iota(jnp.int32, (tm, 1), 0)
    pos = (i * tm + row) & (seq_len - 1)
    up = jnp.where(row == 0, prev_row, pltpu.roll(cu, 1, axis=0))
    up = jnp.where(pos == 0, 0.0, up)
    dn = jnp.where(row == tm - 1, next_row, pltpu.roll(cu, tm - 1, axis=0))
    dn = jnp.where(pos == seq_len - 1, 0.0, dn)
    cw = cw_ref[...]
    y_conv = bg_ref[...].astype(F32) * (up * cw[0:1, :] + cu * cw[1:2, :] + dn * cw[2:3, :])
    mix = jnp.dot(y_conv.astype(BF16), w_out_ref[:CONV_WIDTH, :], preferred_element_type=F32)
    mix = mix + jnp.dot(o_ref[...], w_out_ref[CONV_WIDTH:, :], preferred_element_type=F32)
    x1 = x_ref[...] + mod[2:3, :] * mix
    x1_ref[...] = x1
    h2 = _rms(x1) * n2g_ref[...] * (1.0 + mod[4:5, :]) + mod[3:4, :]
    h2_ref[...] = h2

    h2_hi = h2.astype(BF16)
    h2_lo = (h2 - h2_hi.astype(F32)).astype(BF16)
    hh_hl = jnp.dot(h2_hi, wr_ref[...], preferred_element_type=F32)
    lh = jnp.dot(h2_lo, wr_ref[:, :ROUTER_COLS], preferred_element_type=F32)
    logits = hh_hl[:, :ROUTER_COLS] + (hh_hl[:, ROUTER_COLS:] + lh)
    lane = lax.broadcasted_iota(jnp.int32, logits.shape, 1)
    neg = -jnp.inf
    big = jnp.int32(1 << 20)
    gl = jnp.where(lane < N_EXPERT_GROUPS, logits, neg)
    gmax = jnp.max(gl, axis=-1, keepdims=True)
    p_top = 1.0 / _rowsum(jnp.exp(gl - gmax))
    g_top = jnp.min(jnp.where(gl == gmax, lane, big), axis=-1, keepdims=True)
    e_lo = N_EXPERT_GROUPS + EXPERTS_PER_GROUP * g_top
    el = jnp.where((lane >= e_lo) & (lane < e_lo + EXPERTS_PER_GROUP), logits, neg)
    v1 = jnp.max(el, axis=-1, keepdims=True)
    i1 = jnp.min(jnp.where(el == v1, lane, big), axis=-1, keepdims=True)
    el2 = jnp.where(lane == i1, neg, el)
    v2 = jnp.max(el2, axis=-1, keepdims=True)
    i2 = jnp.min(jnp.where(el2 == v2, lane, big), axis=-1, keepdims=True)
    e21 = jnp.exp(v2 - v1)
    w1 = p_top / (1.0 + e21)
    w2 = w1 * e21
    rid_ref[...] = jnp.where(lane == 0, i1 - N_EXPERT_GROUPS, i2 - N_EXPERT_GROUPS)
    rw_ref[...] = jnp.where(lane == 0, w1, w2)


def _outproj(x2d, bg, cu, o2d, mod3, conv_w, w_out_b, n2g, wr, *, cond_base, tiles_per_cond, seq_len):
    t = x2d.shape[0]
    tm = TM_OUT
    hb = tm // 16
    nhb = t // 16
    return pl.pallas_call(
        functools.partial(_outproj_kernel, tm=tm, seq_len=seq_len),
        out_shape=(jax.ShapeDtypeStruct((t, D_MODEL), F32),
                   jax.ShapeDtypeStruct((t, D_MODEL), F32),
                   jax.ShapeDtypeStruct((t, LANES), jnp.int32),
                   jax.ShapeDtypeStruct((t, LANES), F32)),
        grid=(t // tm,),
        in_specs=[pl.BlockSpec((tm, D_MODEL), lambda i: (i, 0)),
                  pl.BlockSpec((tm, CONV_WIDTH), lambda i: (i, 0)),
                  pl.BlockSpec((tm, CONV_WIDTH), lambda i: (i, 0)),
                  pl.BlockSpec((16, CONV_WIDTH), lambda i: (jnp.maximum(i * hb - 1, 0), 0)),
                  pl.BlockSpec((16, CONV_WIDTH), lambda i: (jnp.minimum((i + 1) * hb, nhb - 1), 0)),
                  pl.BlockSpec((tm, N_HEADS * V_DIM), lambda i: (i, 0)),
                  pl.BlockSpec((1, N_MOD, D_MODEL), lambda i: (cond_base + i // tiles_per_cond, 0, 0)),
                  _const_spec((3, CONV_WIDTH)),
                  _const_spec((D_MODEL, D_MODEL)),
                  _const_spec((1, D_MODEL)),
                  _const_spec((D_MODEL, 2 * ROUTER_COLS))],
        out_specs=(pl.BlockSpec((tm, D_MODEL), lambda i: (i, 0)),
                   pl.BlockSpec((tm, D_MODEL), lambda i: (i, 0)),
                   pl.BlockSpec((tm, LANES), lambda i: (i, 0)),
                   pl.BlockSpec((tm, LANES), lambda i: (i, 0))),
        compiler_params=_cparams(("arbitrary",)),
        name="outproj_ctx" if cond_base == 0 else "outproj_lat",
    )(x2d, bg, cu, cu, cu, o2d, mod3, conv_w, w_out_b, n2g, wr)


def _route_tables(rid, *, tm):
    n_pairs = rid.shape[0]
    nt = n_pairs // tm + N_EXPERTS
    experts = jnp.arange(N_EXPERTS, dtype=jnp.int32)
    onehot = (rid[None, :] == experts[:, None]).astype(jnp.int32)
    csum = jnp.cumsum(onehot, axis=1)
    counts = csum[:, -1]
    tiles_e = (counts + tm - 1) // tm
    tile_end = jnp.cumsum(tiles_e)
    tile_start = tile_end - tiles_e
    n_valid = tile_end[-1]
    pos = jnp.sum(onehot * (csum - 1 + tile_start[:, None] * tm), axis=0)
    tile_idx = jnp.minimum(jnp.arange(nt, dtype=jnp.int32), n_valid - 1)
    te = jnp.sum((tile_end[None, :] <= tile_idx[:, None]).astype(jnp.int32), axis=1)
    return (pos, te, n_valid.reshape(1).astype(jnp.int32),
            (tile_start * tm + counts).astype(jnp.int32), (tiles_e * tm - counts).astype(jnp.int32))


def _dispatch_kernel(pstart_ref, pcnt_ref, nval_ref, pos0_ref, pos1_ref, h2a_ref, h2b_ref, xs_hbm,
                     buf, zbuf, sem, sem_z, *, tm, nt, n_a):
    i = pl.program_id(0)
    n = pl.num_programs(0)
    slot = i & 1

    def row_copy(r, pos_ref, s):
        return pltpu.make_async_copy(buf.at[s, pl.ds(r, 1)], xs_hbm.at[pl.ds(pos_ref[0, 0, r], 1)], sem.at[s])

    def wait_tile(s):
        for _ in range(2):
            pltpu.make_async_copy(buf.at[s], xs_hbm.at[pl.ds(0, tm)], sem.at[s]).wait()

    def pad_copies(e, fn):
        cnt = pcnt_ref[e]
        start = pstart_ref[e]
        off = start + cnt
        for b in PAD_BITS:
            if b < SUBLANES:
                break
            off = off - (cnt & b)
            dst = pl.multiple_of(off, SUBLANES)

            @pl.when((cnt & b) != 0)
            def _():
                fn(pltpu.make_async_copy(zbuf.at[pl.ds(0, b)], xs_hbm.at[pl.ds(dst, b)], sem_z))
        for j in range(SUBLANES - 1):
            @pl.when(j < (cnt & (SUBLANES - 1)))
            def _():
                fn(pltpu.make_async_copy(zbuf.at[pl.ds(0, 1)], xs_hbm.at[pl.ds(start + j, 1)], sem_z))

    def tail_copy(j):
        return pltpu.make_async_copy(zbuf, xs_hbm.at[pl.ds(pl.multiple_of(j * tm, tm), tm)], sem_z)

    @pl.when(i == 0)
    def _():
        zbuf[...] = jnp.zeros((tm, D_MODEL), F32)
        lax.fori_loop(0, N_EXPERTS, lambda e, c: (pad_copies(e, lambda d: d.start()), c)[1], 0)
        lax.fori_loop(nval_ref[0], nt, lambda j, c: (tail_copy(j).start(), c)[1], 0)
        lax.fori_loop(0, N_EXPERTS, lambda e, c: (pad_copies(e, lambda d: d.wait()), c)[1], 0)
        lax.fori_loop(nval_ref[0], nt, lambda j, c: (tail_copy(j).wait(), c)[1], 0)

    @pl.when(i >= 2)
    def _():
        wait_tile(slot)

    @pl.when(i < n_a)
    def _():
        buf[slot] = h2a_ref[...]

    @pl.when(i >= n_a)
    def _():
        buf[slot] = h2b_ref[...]

    for r in range(tm):
        row_copy(r, pos0_ref, slot).start()
        row_copy(r, pos1_ref, slot).start()

    @pl.when(i == n - 1)
    def _():
        wait_tile(slot)
        wait_tile(1 - slot)


def _dispatch(pad_start, pad_cnt, n_valid, pos3, h2_a, h2_b, *, nt):
    tm = TM_ROW
    n_a, n_b = h2_a.shape[0] // tm, h2_b.shape[0] // tm
    smem_tile = functools.partial(pl.BlockSpec, (1, 1, tm), memory_space=pltpu.SMEM)
    grid_spec = pltpu.PrefetchScalarGridSpec(
        num_scalar_prefetch=3,
        grid=(n_a + n_b,),
        in_specs=[smem_tile(lambda i, *_: (i, 0, 0)),
                  smem_tile(lambda i, *_: (n_a + n_b + i, 0, 0)),
                  pl.BlockSpec((tm, D_MODEL), lambda i, *_: (jnp.minimum(i, n_a - 1), 0)),
                  pl.BlockSpec((tm, D_MODEL), lambda i, *_: (jnp.maximum(i - n_a, 0), 0))],
        out_specs=pl.BlockSpec(memory_space=pl.ANY),
        scratch_shapes=[pltpu.VMEM((2, tm, D_MODEL), F32),
                        pltpu.VMEM((tm, D_MODEL), F32),
                        pltpu.SemaphoreType.DMA((2,)),
                        pltpu.SemaphoreType.DMA])
    return pl.pallas_call(
        functools.partial(_dispatch_kernel, tm=tm, nt=nt, n_a=n_a),
        out_shape=jax.ShapeDtypeStruct((nt * TM_MOE, D_MODEL), F32),
        grid_spec=grid_spec,
        compiler_params=pltpu.CompilerParams(dimension_semantics=("arbitrary",),
                                             vmem_limit_bytes=VMEM_LIMIT, has_side_effects=True),
        name="dispatch",
    )(pad_start, pad_cnt, n_valid, pos3, pos3, h2_a, h2_b)


def _moe_kernel(texp_ref, nval_ref, xs_ref, wg_ref, wu_ref, wd_ref, y_ref, wgb, wub, wdb):
    i = pl.program_id(0)

    @pl.when(i < nval_ref[0])
    def _():
        e = texp_ref[i]
        e_prev = texp_ref[jnp.maximum(i - 1, 0)]

        @pl.when((i == 0) | (e != e_prev))
        def _():
            wgb[...] = wg_ref[0].astype(BF16)
            wub[...] = wu_ref[0].astype(BF16)
            wdb[...] = wd_ref[0].astype(BF16)

        x = xs_ref[...].astype(BF16)
        g = jnp.dot(x, wgb[...], preferred_element_type=F32)
        u = jnp.dot(x, wub[...], preferred_element_type=F32)
        a = g / (1.0 + jnp.exp(-g)) * u
        y_ref[...] = jnp.dot(a.astype(BF16), wdb[...], preferred_element_type=F32)

    @pl.when(i >= nval_ref[0])
    def _():
        y_ref[...] = jnp.zeros(y_ref.shape, F32)


def _moe(tile_expert, n_valid, xs, w_gate, w_up, w_down):
    nt = tile_expert.shape[0]
    tm = TM_MOE
    grid_spec = pltpu.PrefetchScalarGridSpec(
        num_scalar_prefetch=2,
        grid=(nt,),
        in_specs=[pl.BlockSpec((tm, D_MODEL), lambda i, te, nv: (jnp.minimum(i, nv[0] - 1), 0)),
                  pl.BlockSpec((1, D_MODEL, D_EXPERT), lambda i, te, nv: (te[i], 0, 0)),
                  pl.BlockSpec((1, D_MODEL, D_EXPERT), lambda i, te, nv: (te[i], 0, 0)),
                  pl.BlockSpec((1, D_EXPERT, D_MODEL), lambda i, te, nv: (te[i], 0, 0))],
        out_specs=pl.BlockSpec((tm, D_MODEL), lambda i, te, nv: (i, 0)),
        scratch_shapes=[pltpu.VMEM((D_MODEL, D_EXPERT), BF16),
                        pltpu.VMEM((D_MODEL, D_EXPERT), BF16),
                        pltpu.VMEM((D_EXPERT, D_MODEL), BF16)])
    return pl.pallas_call(
        _moe_kernel,
        out_shape=jax.ShapeDtypeStruct((nt * tm, D_MODEL), F32),
        grid_spec=grid_spec,
        compiler_params=_cparams(("arbitrary",)),
        name="moe",
    )(tile_expert, n_valid, xs, w_gate, w_up, w_down)


def _final_kernel(pa0_ref, pb0_ref, pa_ref, pb_ref, x1_ref, rw_ref, mod_ref, y_hbm, o_ref, ybuf, sem, *, tm):
    i = pl.program_id(0)
    n = pl.num_programs(0)
    slot = i & 1

    def start_tile(pa, pb, s):
        for r in range(tm):
            pltpu.make_async_copy(y_hbm.at[pl.ds(pa[0, 0, r], 1)], ybuf.at[s, 0, pl.ds(r, 1)], sem.at[s]).start()
            pltpu.make_async_copy(y_hbm.at[pl.ds(pb[0, 0, r], 1)], ybuf.at[s, 1, pl.ds(r, 1)], sem.at[s]).start()

    @pl.when(i == 0)
    def _():
        start_tile(pa0_ref, pb0_ref, 0)

    @pl.when(i + 1 < n)
    def _():
        start_tile(pa_ref, pb_ref, 1 - slot)

    for k in range(2):
        pltpu.make_async_copy(y_hbm.at[pl.ds(0, tm)], ybuf.at[slot, k], sem.at[slot]).wait()
    w = rw_ref[...]
    moe = w[:, 0:1] * ybuf[slot, 0] + w[:, 1:2] * ybuf[slot, 1]
    o_ref[...] = x1_ref[...] + mod_ref[0][5:6, :] * moe


def _final(x1, rw, y, pos3, mod3, *, tile_base, slot_tiles, cond_base, tiles_per_cond):
    t = x1.shape[0]
    tm = TM_ROW
    n = t // tm
    smem_tile = functools.partial(pl.BlockSpec, (1, 1, tm), memory_space=pltpu.SMEM)
    return pl.pallas_call(
        functools.partial(_final_kernel, tm=tm),
        out_shape=jax.ShapeDtypeStruct((t, D_MODEL), F32),
        grid=(n,),
        in_specs=[smem_tile(lambda i: (tile_base, 0, 0)),
                  smem_tile(lambda i: (slot_tiles + tile_base, 0, 0)),
                  smem_tile(lambda i: (tile_base + jnp.minimum(i + 1, n - 1), 0, 0)),
                  smem_tile(lambda i: (slot_tiles + tile_base + jnp.minimum(i + 1, n - 1), 0, 0)),
                  pl.BlockSpec((tm, D_MODEL), lambda i: (i, 0)),
                  pl.BlockSpec((tm, LANES), lambda i: (i, 0)),
                  pl.BlockSpec((1, N_MOD, D_MODEL), lambda i: (cond_base + i // tiles_per_cond, 0, 0)),
                  pl.BlockSpec(memory_space=pl.ANY)],
        out_specs=pl.BlockSpec((tm, D_MODEL), lambda i: (i, 0)),
        scratch_shapes=[pltpu.VMEM((2, 2, tm, D_MODEL), F32), pltpu.SemaphoreType.DMA((2,))],
        compiler_params=_cparams(("arbitrary",)),
        name="final_ctx" if cond_base == 0 else "final_lat",
    )(pos3, pos3, pos3, pos3, x1, rw, mod3, y)


def _rope_table(n_tokens):
    rows = n_tokens // GRID_W
    inv = ROPE_BASE ** (-jnp.arange(ROPE_PAIRS, dtype=F32) / ROPE_PAIRS)
    row_ang = jnp.arange(rows, dtype=F32)[:, None] * inv
    col_ang = jnp.arange(GRID_W, dtype=F32)[:, None] * inv
    cr, sr = (jnp.repeat(f(row_ang), GRID_W, axis=0) for f in (jnp.cos, jnp.sin))
    cc, sc = (jnp.tile(f(col_ang), (rows, 1)) for f in (jnp.cos, jnp.sin))
    return jnp.concatenate([cr, cr, cc, cc, -sr, sr, -sc, sc], axis=-1)


def _swap_halves(a):
    p = ROPE_PAIRS
    return jnp.concatenate([a[..., p:2 * p], a[..., :p], a[..., 3 * p:], a[..., 2 * p:3 * p]], axis=-1)


def _head_gains(g):
    rope = g[QK_NOPE:]
    return jnp.stack([g[:QK_NOPE], jnp.concatenate([rope, _swap_halves(rope)])])


def kernel(x_prompt, x_sample, cache_ckv, cache_krope, c, c_ctx, ada_w, ada_b, norm1_g, w_in, conv_w,
           q_lora_g, w_uq, kv_lora_g, w_ukv, q_head_g, k_head_g, w_out, norm2_g, router_g, router_e,
           w_gate, w_up, w_down):
    depth = ada_w.shape[0]
    assert depth == 1
    bp, sp, _ = x_prompt.shape
    bs, ss, _ = x_sample.shape
    past = cache_ckv.shape[2]
    tp, ts = bp * sp, bs * ss
    assert sp & (sp - 1) == 0 and ss & (ss - 1) == 0
    assert tp % TM_IN == 0 and ss % TM_IN == 0 and TM_IN % sp == 0 and ss % TQ == 0
    assert bs + 1 <= 8 and past % 256 == 0 and TM_ROW == TM_MOE and tp % TM_ROW == 0 and ts % TM_ROW == 0

    w_in_b = jnp.concatenate([w_in[0], _swap_halves(w_in[0][:, IN_COLS - 2 * QK_ROPE:])], axis=1).astype(BF16)
    uq = w_uq[0].reshape(Q_LORA, N_HEADS, QK_DIM)
    w_uq_b = jnp.concatenate([uq, _swap_halves(uq[..., QK_NOPE:])], axis=-1)
    w_uq_b = w_uq_b.reshape(Q_LORA, N_HEADS * HEAD_W).astype(BF16)
    ukv = w_ukv[0].reshape(KV_LORA, N_HEADS, QK_NOPE + V_DIM)
    w_uk_b = ukv[..., :QK_NOPE].reshape(KV_LORA, N_HEADS * QK_NOPE).astype(BF16)
    w_uvt_b = ukv[..., QK_NOPE:].reshape(KV_LORA, N_HEADS * V_DIM).T.astype(BF16)
    w_out_b = w_out[0].astype(BF16)
    wr = jnp.concatenate([router_g[0], router_e[0],
                          jnp.zeros((D_MODEL, ROUTER_COLS - N_EXPERT_GROUPS - N_EXPERTS), F32)], axis=1)
    wr_hi = lax.bitcast_convert_type(lax.bitcast_convert_type(wr, jnp.uint32) & jnp.uint32(0xFFFF0000), F32)
    wr = jnp.concatenate([wr_hi, wr - wr_hi], axis=1).astype(BF16)
    gq, gk = _head_gains(q_head_g[0]), _head_gains(k_head_g[0])
    n1g, n2g = norm1_g[0].reshape(1, D_MODEL), norm2_g[0].reshape(1, D_MODEL)
    qlg, kvg = q_lora_g[0].reshape(1, Q_LORA), kv_lora_g[0].reshape(1, KV_LORA)
    cs_lat = _rope_table(ss)
    cs_id = jnp.concatenate([jnp.ones((TM_IN, QK_ROPE), F32), jnp.zeros((TM_IN, QK_ROPE), F32)], axis=1)

    cond8 = jnp.concatenate([c_ctx[None, :], c, jnp.zeros((8 - 1 - bs, D_MODEL), F32)], axis=0)
    mod3 = _modulation(cond8, ada_w[0], ada_b[0]).reshape(8, N_MOD, D_MODEL)

    xp2, xs2 = x_prompt.reshape(tp, D_MODEL), x_sample.reshape(ts, D_MODEL)
    big = 1 << 30

    bg_p, cu_p, q_p, k_p, vt_p, ckv_p, kr_p = _inproj(
        xp2, mod3, n1g, w_in_b, qlg, w_uq_b, kvg, w_uk_b, w_uvt_b, gq, gk, cs_id,
        cond_base=0, tiles_per_cond=big, cs_tiles=1, emit_cache=True)
    o_p = _attention(q_p.reshape(bp, sp, -1), [(k_p.reshape(bp, sp, -1), vt_p)],
                     tq=sp, heads=N_HEADS, name="attn_ctx")
    x1_p, h2_p, rid_p, rw_p = _outproj(
        xp2, bg_p, cu_p, o_p.reshape(tp, -1), mod3, conv_w[0], w_out_b, n2g, wr,
        cond_base=0, tiles_per_cond=big, seq_len=sp)

    kr_c = cache_krope[:, 0].reshape(bs * past, QK_ROPE)
    k_c, vt_c = _kvcache(cache_ckv[:, 0].reshape(bs * past, KV_LORA), jnp.concatenate([kr_c, kr_c], axis=1),
                         cs_id, w_uk_b, w_uvt_b, gk)
    bg_s, cu_s, q_s, k_s, vt_s = _inproj(
        xs2, mod3, n1g, w_in_b, qlg, w_uq_b, kvg, w_uk_b, w_uvt_b, gq, gk, cs_lat,
        cond_base=1, tiles_per_cond=ss // TM_IN, cs_tiles=ss // TM_IN, emit_cache=False)
    o_s = _attention(q_s.reshape(bs, ss, -1),
                     [(k_c.reshape(bs, past, -1), vt_c), (k_s.reshape(bs, ss, -1), vt_s)],
                     tq=TQ, heads=1, name="attn_lat")
    x1_s, h2_s, rid_s, rw_s = _outproj(
        xs2, bg_s, cu_s, o_s.reshape(ts, -1), mod3, conv_w[0], w_out_b, n2g, wr,
        cond_base=1, tiles_per_cond=ss // TM_OUT, seq_len=ss)

    n_tok = tp + ts
    slot_tiles = n_tok // TM_ROW
    nt = 2 * n_tok // TM_MOE + N_EXPERTS
    rid = jnp.concatenate([rid_p[:, :2], rid_s[:, :2]], axis=0).T.reshape(2 * n_tok)
    pos, te, n_valid, pad_start, pad_cnt = _route_tables(rid, tm=TM_MOE)
    pos3 = pos.reshape(2 * slot_tiles, 1, TM_ROW)
    xs = _dispatch(pad_start, pad_cnt, n_valid, pos3, h2_p, h2_s, nt=nt)
    y = _moe(te, n_valid, xs, w_gate[0], w_up[0], w_down[0])

    y_p = _final(x1_p, rw_p, y, pos3, mod3, tile_base=0, slot_tiles=slot_tiles, cond_base=0, tiles_per_cond=big)
    y_s = _final(x1_s, rw_s, y, pos3, mod3, tile_base=tp // TM_ROW, slot_tiles=slot_tiles, cond_base=1,
                 tiles_per_cond=ss // TM_ROW)

    return (y_p.reshape(bp, sp, D_MODEL), y_s.reshape(bs, ss, D_MODEL),
            ckv_p.reshape(bp, 1, sp, KV_LORA), kr_p.reshape(bp, 1, sp, QK_ROPE))
```

```python
import functools

import jax
import jax.numpy as jnp
from jax import lax
from jax.experimental import pallas as pl
from jax.experimental.pallas import tpu as pltpu

F32 = jnp.float32
BF16 = jnp.bfloat16
HIGHEST = lax.Precision.HIGHEST

D_MODEL = 2048
CONV_WIDTH = 1024
N_HEADS = 8
QK_NOPE = 128
QK_ROPE = 64
V_DIM = 128
QK_DIM = QK_NOPE + QK_ROPE
Q_LORA = 512
KV_LORA = 256
GRID_W = 64
ROPE_PAIRS = QK_ROPE // 4
ROPE_BASE = 10000.0
N_EXPERT_GROUPS = 4
EXPERTS_PER_GROUP = 8
N_EXPERTS = N_EXPERT_GROUPS * EXPERTS_PER_GROUP
D_EXPERT = 512
N_MOD = 6
EPS = 1e-6
LOG2_E = 1.4426950408889634

HEAD_W = 2 * QK_NOPE
LANES = 128
SUBLANES = 8
IN_COLS = 3 * CONV_WIDTH + Q_LORA + KV_LORA + 2 * QK_ROPE
ROUTER_COLS = LANES
VMEM_LIMIT = 56 * 1024 * 1024

TM_IN = 512
TM_OUT = 512
TQ = 512
ATTN_CHUNK = 1024
TM_MOE = 256
TM_ROW = 256
BN_MOD = 1024
PAD_BITS = tuple(1 << b for b in reversed(range(TM_MOE.bit_length() - 1)))


def _cparams(sem):
    return pltpu.CompilerParams(dimension_semantics=sem, vmem_limit_bytes=VMEM_LIMIT)


def _const_spec(shape):
    nd = len(shape)
    return pl.BlockSpec(shape, lambda *_: (0,) * nd, pipeline_mode=pl.Buffered(1))


def _rms(x):
    return x * lax.rsqrt(jnp.mean(x * x, axis=-1, keepdims=True) + EPS)


def _rowsum(x):
    return jnp.sum(x, axis=-1, keepdims=True)


def _mod_kernel(c_ref, w_ref, b_ref, o_ref):
    c = c_ref[...]
    s = c / (1.0 + jnp.exp(-c))
    o_ref[...] = jnp.dot(s, w_ref[...], precision=HIGHEST, preferred_element_type=F32) + b_ref[...]


def _modulation(cond8, ada_w, ada_b):
    n = ada_w.shape[1]
    return pl.pallas_call(
        _mod_kernel,
        out_shape=jax.ShapeDtypeStruct((8, n), F32),
        grid=(n // BN_MOD,),
        in_specs=[pl.BlockSpec((8, D_MODEL), lambda j: (0, 0)),
                  pl.BlockSpec((D_MODEL, BN_MOD), lambda j: (0, j)),
                  pl.BlockSpec((1, BN_MOD), lambda j: (0, j))],
        out_specs=pl.BlockSpec((8, BN_MOD), lambda j: (0, j)),
        compiler_params=_cparams(("arbitrary",)),
        name="mod",
    )(cond8, ada_w, ada_b.reshape(1, n))


def _winprep_kernel(w_ref, sw_ref, o_ref):
    n = w_ref.shape[1]
    o_ref[:, :n] = w_ref[...].astype(BF16)
    o_ref[:, n:] = sw_ref[...]


def _winprep(w_in, rope_swapped):
    k, n = w_in.shape
    tk = 256
    return pl.pallas_call(
        _winprep_kernel,
        out_shape=jax.ShapeDtypeStruct((k, IN_COLS), BF16),
        grid=(k // tk,),
        in_specs=[pl.BlockSpec((tk, n), lambda i: (i, 0)),
                  pl.BlockSpec((tk, QK_ROPE), lambda i: (i, 0))],
        out_specs=pl.BlockSpec((tk, IN_COLS), lambda i: (i, 0)),
        compiler_params=_cparams(("arbitrary",)),
        name="winprep",
    )(w_in, rope_swapped)


def _emit_kv(ckv, kraw, cs, w_uk_ref, w_uvt_ref, gk_ref, k_ref, vt_ref):
    cb = ckv.astype(BF16)
    kn_all = jnp.dot(cb, w_uk_ref[...], preferred_element_type=F32)
    vt = lax.dot_general(w_uvt_ref[...], cb, (((1,), (1,)), ((), ())), preferred_element_type=F32)
    vt_ref[...] = vt.astype(BF16)
    ss_rope = 0.5 * _rowsum(kraw * kraw)
    t = kraw * (cs * gk_ref[1:2, :])
    tt = t + pltpu.roll(t, QK_ROPE, axis=1)
    g_nope = gk_ref[0:1, :]
    for h in range(N_HEADS):
        kn = kn_all[:, h * QK_NOPE:(h + 1) * QK_NOPE]
        r = lax.rsqrt((_rowsum(kn * kn) + ss_rope) * (1.0 / QK_DIM) + EPS)
        k_ref[:, h * HEAD_W:h * HEAD_W + QK_NOPE] = (kn * r * g_nope).astype(BF16)
        k_ref[:, h * HEAD_W + QK_NOPE:(h + 1) * HEAD_W] = (tt * r).astype(BF16)


def _kvcache_kernel(ckv_ref, kraw_ref, cs_ref, w_uk_ref, w_uvt_ref, gk_ref, k_ref, vt_ref):
    _emit_kv(ckv_ref[...], kraw_ref[...], cs_ref[...], w_uk_ref, w_uvt_ref, gk_ref, k_ref, vt_ref)


def _kvcache(ckv, kraw, cs_id, w_uk_b, w_uvt_b, gk):
    n = ckv.shape[0]
    tm = 256
    return pl.pallas_call(
        _kvcache_kernel,
        out_shape=(jax.ShapeDtypeStruct((n, N_HEADS * HEAD_W), BF16),
                   jax.ShapeDtypeStruct((N_HEADS * V_DIM, n), BF16)),
        grid=(n // tm,),
        in_specs=[pl.BlockSpec((tm, KV_LORA), lambda i: (i, 0)),
                  pl.BlockSpec((tm, LANES), lambda i: (i, 0)),
                  pl.BlockSpec((tm, LANES), lambda i: (0, 0)),
                  _const_spec((KV_LORA, N_HEADS * QK_NOPE)),
                  _const_spec((N_HEADS * V_DIM, KV_LORA)),
                  _const_spec((2, LANES))],
        out_specs=(pl.BlockSpec((tm, N_HEADS * HEAD_W), lambda i: (i, 0)),
                   pl.BlockSpec((N_HEADS * V_DIM, tm), lambda i: (0, i))),
        compiler_params=_cparams(("arbitrary",)),
        name="kvcache",
    )(ckv, kraw, cs_id, w_uk_b, w_uvt_b, gk)


def _inproj_kernel(x_ref, mod_ref, n1g_ref, w_in_ref, qlg_ref, w_uq_ref, kvg_ref, w_uk_ref, w_uvt_ref,
                   gq_ref, gk_ref, cs_ref, bg_ref, cu_ref, q_ref, k_ref, vt_ref, *cache_refs):
    x = x_ref[...]
    mod = mod_ref[0]
    h = _rms(x) * n1g_ref[...] * (1.0 + mod[1:2, :]) + mod[0:1, :]
    hb = h.astype(BF16)

    def proj(a, b):
        return jnp.dot(hb, w_in_ref[:, a:b], preferred_element_type=F32)

    c1, c2, c3 = CONV_WIDTH, 2 * CONV_WIDTH, 3 * CONV_WIDTH
    bg_ref[...] = proj(0, c1).astype(BF16)
    cu_ref[...] = (proj(c1, c2) * proj(c2, c3)).astype(BF16)
    q_lat = proj(c3, c3 + Q_LORA)
    kvk = proj(c3 + Q_LORA, IN_COLS)
    cs = cs_ref[...]

    qn = _rms(q_lat) * qlg_ref[...]
    q = jnp.dot(qn.astype(BF16), w_uq_ref[...], preferred_element_type=F32)
    scale = QK_DIM ** -0.5 * LOG2_E
    g_nope = gq_ref[0:1, :] * scale
    tq = cs * (gq_ref[1:2, :] * scale)
    for hd in range(N_HEADS):
        lo = q[:, hd * HEAD_W:hd * HEAD_W + QK_NOPE]
        up = q[:, hd * HEAD_W + QK_NOPE:(hd + 1) * HEAD_W]
        ss = _rowsum(lo * lo) + 0.5 * _rowsum(up * up)
        r = lax.rsqrt(ss * (1.0 / QK_DIM) + EPS)
        q_ref[:, hd * HEAD_W:hd * HEAD_W + QK_NOPE] = (lo * r * g_nope).astype(BF16)
        q_ref[:, hd * HEAD_W + QK_NOPE:(hd + 1) * HEAD_W] = (up * r * tq).astype(BF16)

    kv_lat = kvk[:, :KV_LORA]
    kraw = kvk[:, KV_LORA:]
    ckv = _rms(kv_lat) * kvg_ref[...]
    if cache_refs:
        ckv_out_ref, kr_out_ref = cache_refs
        ckv_out_ref[...] = ckv
        kr_out_ref[...] = kraw[:, :QK_ROPE]
    _emit_kv(ckv, kraw, cs, w_uk_ref, w_uvt_ref, gk_ref, k_ref, vt_ref)


def _inproj(x2d, mod3, n1g, w_in_b, qlg, w_uq_b, kvg, w_uk_b, w_uvt_b, gq, gk, cs, *,
            cond_base, tiles_per_cond, cs_tiles, emit_cache):
    t = x2d.shape[0]
    tm = TM_IN
    out_shape = [jax.ShapeDtypeStruct((t, CONV_WIDTH), BF16),
                 jax.ShapeDtypeStruct((t, CONV_WIDTH), BF16),
                 jax.ShapeDtypeStruct((t, N_HEADS * HEAD_W), BF16),
                 jax.ShapeDtypeStruct((t, N_HEADS * HEAD_W), BF16),
                 jax.ShapeDtypeStruct((N_HEADS * V_DIM, t), BF16)]
    out_specs = [pl.BlockSpec((tm, CONV_WIDTH), lambda i: (i, 0)),
                 pl.BlockSpec((tm, CONV_WIDTH), lambda i: (i, 0)),
                 pl.BlockSpec((tm, N_HEADS * HEAD_W), lambda i: (i, 0)),
                 pl.BlockSpec((tm, N_HEADS * HEAD_W), lambda i: (i, 0)),
                 pl.BlockSpec((N_HEADS * V_DIM, tm), lambda i: (0, i))]
    if emit_cache:
        out_shape += [jax.ShapeDtypeStruct((t, KV_LORA), F32), jax.ShapeDtypeStruct((t, QK_ROPE), F32)]
        out_specs += [pl.BlockSpec((tm, KV_LORA), lambda i: (i, 0)),
                      pl.BlockSpec((tm, QK_ROPE), lambda i: (i, 0))]
    return pl.pallas_call(
        _inproj_kernel,
        out_shape=tuple(out_shape),
        grid=(t // tm,),
        in_specs=[pl.BlockSpec((tm, D_MODEL), lambda i: (i, 0)),
                  pl.BlockSpec((1, N_MOD, D_MODEL), lambda i: (cond_base + i // tiles_per_cond, 0, 0)),
                  _const_spec((1, D_MODEL)),
                  _const_spec((D_MODEL, IN_COLS)),
                  _const_spec((1, Q_LORA)),
                  _const_spec((Q_LORA, N_HEADS * HEAD_W)),
                  _const_spec((1, KV_LORA)),
                  _const_spec((KV_LORA, N_HEADS * QK_NOPE)),
                  _const_spec((N_HEADS * V_DIM, KV_LORA)),
                  _const_spec((2, LANES)),
                  _const_spec((2, LANES)),
                  pl.BlockSpec((tm, LANES), lambda i: (i % cs_tiles, 0))],
        out_specs=tuple(out_specs),
        compiler_params=_cparams(("arbitrary",)),
        name="inproj_ctx" if emit_cache else "inproj_lat",
    )(x2d, mod3, n1g, w_in_b, qlg, w_uq_b, kvg, w_uk_b, w_uvt_b, gq, gk, cs)


def _attn_kernel(*refs, n_kv, heads):
    q_ref = refs[0]
    o_ref = refs[-1]
    chunks = []
    for j in range(n_kv):
        sk = refs[1 + 2 * j].shape[1]
        step = min(sk, ATTN_CHUNK)
        chunks += [(j, lo, lo + step) for lo in range(0, sk, step)]
    for h in range(heads):
        q = q_ref[0, :, h * HEAD_W:(h + 1) * HEAD_W]
        def score(c):
            j, lo, hi = chunks[c]
            k = refs[1 + 2 * j][0, lo:hi, h * HEAD_W:(h + 1) * HEAD_W]
            return lax.dot_general(k, q, (((1,), (1,)), ((), ())), preferred_element_type=F32)

        m = None
        acc = None
        s_next = score(0)
        for c, (j, lo, hi) in enumerate(chunks):
            s = s_next
            if c + 1 < len(chunks):
                s_next = score(c + 1)
            mc = jnp.max(s, axis=0, keepdims=True)
            m_new = mc if m is None else jnp.maximum(m, mc)
            p = jnp.exp2(s - m_new).astype(BF16)
            vt = refs[2 + 2 * j][h * V_DIM:(h + 1) * V_DIM, lo:hi]
            vt1 = jnp.concatenate([vt, jnp.ones((2 * SUBLANES, hi - lo), BF16)], axis=0)
            part = jnp.dot(vt1, p, preferred_element_type=F32)
            acc = part if acc is None else acc * jnp.exp2(m - m_new) + part
            m = m_new
        ot = acc[:V_DIM, :] / acc[V_DIM:V_DIM + 1, :]
        o_ref[0, :, h * V_DIM:(h + 1) * V_DIM] = ot.T.astype(BF16)


def _attention(q, kvs, *, tq, heads, name):
    b, s, _ = q.shape
    in_specs = [pl.BlockSpec((1, tq, heads * HEAD_W), lambda bi, hi, qi: (bi, qi, hi))]
    args = [q]
    for k, v in kvs:
        sk = k.shape[1]
        in_specs.append(pl.BlockSpec((1, sk, heads * HEAD_W), lambda bi, hi, qi: (bi, 0, hi)))
        in_specs.append(pl.BlockSpec((heads * V_DIM, sk), lambda bi, hi, qi: (hi, bi)))
        args += [k, v]
    return pl.pallas_call(
        functools.partial(_attn_kernel, n_kv=len(kvs), heads=heads),
        out_shape=jax.ShapeDtypeStruct((b, s, N_HEADS * V_DIM), BF16),
        grid=(b, N_HEADS // heads, s // tq),
        in_specs=in_specs,
        out_specs=pl.BlockSpec((1, tq, heads * V_DIM), lambda bi, hi, qi: (bi, qi, hi)),
        compiler_params=_cparams(("arbitrary", "arbitrary", "arbitrary")),
        name=name,
    )(*args)


def _outproj_kernel(x_ref, bg_ref, cu_ref, cup_ref, cun_ref, o_ref, mod_ref, cw_ref, w_out_ref,
                    n2g_ref, wr_ref, x1_ref, h2_ref, rid_ref, rw_ref, *, tm, seq_len):
    i = pl.program_id(0)
    mod = mod_ref[0]
    cu = cu_ref[...].astype(F32)
    prev_row = cup_ref[...].astype(F32)[15:16, :]
    next_row = cun_ref[...].astype(F32)[0:1, :]
    row = lax.broadcasted_iota(jnp.int32, (tm, 1), 0)
    pos = (i * tm + row) & (seq_len - 1)
    up = jnp.where(row == 0, prev_row, pltpu.roll(cu, 1, axis=0))
    up = jnp.where(pos == 0, 0.0, up)
    dn = jnp.where(row == tm - 1, next_row, pltpu.roll(cu, tm - 1, axis=0))
    dn = jnp.where(pos == seq_len - 1, 0.0, dn)
    cw = cw_ref[...]
    y_conv = bg_ref[...].astype(F32) * (up * cw[0:1, :] + cu * cw[1:2, :] + dn * cw[2:3, :])
    mix = jnp.dot(y_conv.astype(BF16), w_out_ref[:CONV_WIDTH, :], preferred_element_type=F32)
    mix = mix + jnp.dot(o_ref[...], w_out_ref[CONV_WIDTH:, :], preferred_element_type=F32)
    x1 = x_ref[...] + mod[2:3, :] * mix
    x1_ref[...] = x1
    h2 = _rms(x1) * n2g_ref[...] * (1.0 + mod[4:5, :]) + mod[3:4, :]
    h2_ref[...] = h2

    h2_hi = h2.astype(BF16)
    h2_lo = (h2 - h2_hi.astype(F32)).astype(BF16)
    hh_hl = jnp.dot(h2_hi, wr_ref[...], preferred_element_type=F32)
    lh = jnp.dot(h2_lo, wr_ref[:, :ROUTER_COLS], preferred_element_type=F32)
    logits = hh_hl[:, :ROUTER_COLS] + (hh_hl[:, ROUTER_COLS:] + lh)
    lane = lax.broadcasted_iota(jnp.int32, logits.shape, 1)
    neg = -jnp.inf
    big = jnp.int32(1 << 20)
    gl = jnp.where(lane < N_EXPERT_GROUPS, logits, neg)
    gmax = jnp.max(gl, axis=-1, keepdims=True)
    p_top = 1.0 / _rowsum(jnp.exp(gl - gmax))
    g_top = jnp.min(jnp.where(gl == gmax, lane, big), axis=-1, keepdims=True)
    e_lo = N_EXPERT_GROUPS + EXPERTS_PER_GROUP * g_top
    el = jnp.where((lane >= e_lo) & (lane < e_lo + EXPERTS_PER_GROUP), logits, neg)
    v1 = jnp.max(el, axis=-1, keepdims=True)
    i1 = jnp.min(jnp.where(el == v1, lane, big), axis=-1, keepdims=True)
    el2 = jnp.where(lane == i1, neg, el)
    v2 = jnp.max(el2, axis=-1, keepdims=True)
    i2 = jnp.min(jnp.where(el2 == v2, lane, big), axis=-1, keepdims=True)
    e21 = jnp.exp(v2 - v1)
    w1 = p_top / (1.0 + e21)
    w2 = w1 * e21
    rid_ref[...] = jnp.where(lane == 0, i1 - N_EXPERT_GROUPS, i2 - N_EXPERT_GROUPS)
    rw_ref[...] = jnp.where(lane == 0, w1, w2)


def _outproj(x2d, bg, cu, o2d, mod3, conv_w, w_out_b, n2g, wr, *, cond_base, tiles_per_cond, seq_len):
    t = x2d.shape[0]
    tm = TM_OUT
    hb = tm // 16
    nhb = t // 16
    return pl.pallas_call(
        functools.partial(_outproj_kernel, tm=tm, seq_len=seq_len),
        out_shape=(jax.ShapeDtypeStruct((t, D_MODEL), F32),
                   jax.ShapeDtypeStruct((t, D_MODEL), F32),
                   jax.ShapeDtypeStruct((t, LANES), jnp.int32),
                   jax.ShapeDtypeStruct((t, LANES), F32)),
        grid=(t // tm,),
        in_specs=[pl.BlockSpec((tm, D_MODEL), lambda i: (i, 0)),
                  pl.BlockSpec((tm, CONV_WIDTH), lambda i: (i, 0)),
                  pl.BlockSpec((tm, CONV_WIDTH), lambda i: (i, 0)),
                  pl.BlockSpec((16, CONV_WIDTH), lambda i: (jnp.maximum(i * hb - 1, 0), 0)),
                  pl.BlockSpec((16, CONV_WIDTH), lambda i: (jnp.minimum((i + 1) * hb, nhb - 1), 0)),
                  pl.BlockSpec((tm, N_HEADS * V_DIM), lambda i: (i, 0)),
                  pl.BlockSpec((1, N_MOD, D_MODEL), lambda i: (cond_base + i // tiles_per_cond, 0, 0)),
                  _const_spec((3, CONV_WIDTH)),
                  _const_spec((D_MODEL, D_MODEL)),
                  _const_spec((1, D_MODEL)),
                  _const_spec((D_MODEL, 2 * ROUTER_COLS))],
        out_specs=(pl.BlockSpec((tm, D_MODEL), lambda i: (i, 0)),
                   pl.BlockSpec((tm, D_MODEL), lambda i: (i, 0)),
                   pl.BlockSpec((tm, LANES), lambda i: (i, 0)),
                   pl.BlockSpec((tm, LANES), lambda i: (i, 0))),
        compiler_params=_cparams(("arbitrary",)),
        name="outproj_ctx" if cond_base == 0 else "outproj_lat",
    )(x2d, bg, cu, cu, cu, o2d, mod3, conv_w, w_out_b, n2g, wr)


def _route_tables(rid, *, tm):
    n_pairs = rid.shape[0]
    nt = n_pairs // tm + N_EXPERTS
    experts = jnp.arange(N_EXPERTS, dtype=jnp.int32)
    onehot = (rid[None, :] == experts[:, None]).astype(jnp.int32)
    csum = jnp.cumsum(onehot, axis=1)
    counts = csum[:, -1]
    tiles_e = (counts + tm - 1) // tm
    tile_end = jnp.cumsum(tiles_e)
    tile_start = tile_end - tiles_e
    n_valid = tile_end[-1]
    pos = jnp.sum(onehot * (csum - 1 + tile_start[:, None] * tm), axis=0)
    tile_raw = jnp.arange(nt, dtype=jnp.int32)
    tile_idx = jnp.minimum(tile_raw, n_valid - 1)
    te = jnp.sum((tile_end[None, :] <= tile_idx[:, None]).astype(jnp.int32), axis=1)
    used = tiles_e > 0
    first = ((tile_raw == tile_start[te]) & (tile_raw < n_valid)).astype(jnp.int32)
    nxt = lax.cummin(jnp.where(used, experts, N_EXPERTS), axis=0, reverse=True)
    nxt = jnp.concatenate([nxt[1:], jnp.full((1,), N_EXPERTS, jnp.int32)])
    next_expert = jnp.where(nxt[te] < N_EXPERTS, nxt[te], -1).astype(jnp.int32)
    wslot = ((jnp.cumsum(used.astype(jnp.int32)) - 1)[te] & 1).astype(jnp.int32)
    return (pos, te, n_valid.reshape(1).astype(jnp.int32),
            (tile_start * tm + counts).astype(jnp.int32), (tiles_e * tm - counts).astype(jnp.int32),
            first, next_expert, wslot)


def _dispatch_kernel(pstart_ref, pcnt_ref, nval_ref, pos0_ref, pos1_ref, h2a_ref, h2b_ref, xs_hbm,
                     buf, zbuf, sem, sem_z, *, tm, nt, n_a):
    i = pl.program_id(0)
    n = pl.num_programs(0)
    slot = i & 1

    def row_copy(r, pos_ref, s):
        return pltpu.make_async_copy(buf.at[s, pl.ds(r, 1)], xs_hbm.at[pl.ds(pos_ref[0, 0, r], 1)], sem.at[s])

    def wait_tile(s):
        for _ in range(2):
            pltpu.make_async_copy(buf.at[s], xs_hbm.at[pl.ds(0, tm)], sem.at[s]).wait()

    def pad_copies(e, fn):
        cnt = pcnt_ref[e]
        start = pstart_ref[e]
        off = start + cnt
        for b in PAD_BITS:
            if b < SUBLANES:
                break
            off = off - (cnt & b)
            dst = pl.multiple_of(off, SUBLANES)

            @pl.when((cnt & b) != 0)
            def _():
                fn(pltpu.make_async_copy(zbuf.at[pl.ds(0, b)], xs_hbm.at[pl.ds(dst, b)], sem_z))
        for j in range(SUBLANES - 1):
            @pl.when(j < (cnt & (SUBLANES - 1)))
            def _():
                fn(pltpu.make_async_copy(zbuf.at[pl.ds(0, 1)], xs_hbm.at[pl.ds(start + j, 1)], sem_z))

    def tail_copy(j):
        return pltpu.make_async_copy(zbuf, xs_hbm.at[pl.ds(pl.multiple_of(j * tm, tm), tm)], sem_z)

    @pl.when(i == 0)
    def _():
        zbuf[...] = jnp.zeros((tm, D_MODEL), F32)
        lax.fori_loop(0, N_EXPERTS, lambda e, c: (pad_copies(e, lambda d: d.start()), c)[1], 0)
        lax.fori_loop(nval_ref[0], nt, lambda j, c: (tail_copy(j).start(), c)[1], 0)
        lax.fori_loop(0, N_EXPERTS, lambda e, c: (pad_copies(e, lambda d: d.wait()), c)[1], 0)
        lax.fori_loop(nval_ref[0], nt, lambda j, c: (tail_copy(j).wait(), c)[1], 0)

    @pl.when(i >= 2)
    def _():
        wait_tile(slot)

    @pl.when(i < n_a)
    def _():
        buf[slot] = h2a_ref[...]

    @pl.when(i >= n_a)
    def _():
        buf[slot] = h2b_ref[...]

    for r in range(tm):
        row_copy(r, pos0_ref, slot).start()
        row_copy(r, pos1_ref, slot).start()

    @pl.when(i == n - 1)
    def _():
        wait_tile(slot)
        wait_tile(1 - slot)


def _dispatch(pad_start, pad_cnt, n_valid, pos3, h2_a, h2_b, *, nt):
    tm = TM_ROW
    n_a, n_b = h2_a.shape[0] // tm, h2_b.shape[0] // tm
    smem_tile = functools.partial(pl.BlockSpec, (1, 1, tm), memory_space=pltpu.SMEM)
    grid_spec = pltpu.PrefetchScalarGridSpec(
        num_scalar_prefetch=3,
        grid=(n_a + n_b,),
        in_specs=[smem_tile(lambda i, *_: (i, 0, 0)),
                  smem_tile(lambda i, *_: (n_a + n_b + i, 0, 0)),
                  pl.BlockSpec((tm, D_MODEL), lambda i, *_: (jnp.minimum(i, n_a - 1), 0)),
                  pl.BlockSpec((tm, D_MODEL), lambda i, *_: (jnp.maximum(i - n_a, 0), 0))],
        out_specs=pl.BlockSpec(memory_space=pl.ANY),
        scratch_shapes=[pltpu.VMEM((2, tm, D_MODEL), F32),
                        pltpu.VMEM((tm, D_MODEL), F32),
                        pltpu.SemaphoreType.DMA((2,)),
                        pltpu.SemaphoreType.DMA])
    return pl.pallas_call(
        functools.partial(_dispatch_kernel, tm=tm, nt=nt, n_a=n_a),
        out_shape=jax.ShapeDtypeStruct((nt * TM_MOE, D_MODEL), F32),
        grid_spec=grid_spec,
        compiler_params=pltpu.CompilerParams(dimension_semantics=("arbitrary",),
                                             vmem_limit_bytes=VMEM_LIMIT, has_side_effects=True),
        name="dispatch",
    )(pad_start, pad_cnt, n_valid, pos3, pos3, h2_a, h2_b)


def _moe_kernel(texp_ref, nval_ref, first_ref, next_ref, wslot_ref, xs_ref, wg_hbm, wu_hbm, wd_hbm, y_ref,
                wg32, wu32, wd32, wgb, wub, wdb, sem_w):
    i = pl.program_id(0)

    def weight_copies(e, s):
        return (pltpu.make_async_copy(wg_hbm.at[e], wg32.at[s], sem_w.at[s]),
                pltpu.make_async_copy(wu_hbm.at[e], wu32.at[s], sem_w.at[s]),
                pltpu.make_async_copy(wd_hbm.at[e], wd32.at[s], sem_w.at[s]))

    @pl.when(i == 0)
    def _():
        for d in weight_copies(texp_ref[0], 0):
            d.start()

    @pl.when(i < nval_ref[0])
    def _():
        @pl.when(first_ref[i] == 1)
        def _():
            s = wslot_ref[i]
            for d in weight_copies(texp_ref[i], s):
                d.wait()
            e_next = next_ref[i]

            @pl.when(e_next >= 0)
            def _():
                for d in weight_copies(e_next, 1 - s):
                    d.start()
            wgb[...] = wg32[s].astype(BF16)
            wub[...] = wu32[s].astype(BF16)
            wdb[...] = wd32[s].astype(BF16)

        x = xs_ref[...].astype(BF16)
        g = jnp.dot(x, wgb[...], preferred_element_type=F32)
        u = jnp.dot(x, wub[...], preferred_element_type=F32)
        a = g / (1.0 + jnp.exp(-g)) * u
        y_ref[...] = jnp.dot(a.astype(BF16), wdb[...], preferred_element_type=F32)

    @pl.when(i >= nval_ref[0])
    def _():
        y_ref[...] = jnp.zeros(y_ref.shape, F32)


def _moe(tile_expert, n_valid, first, next_expert, wslot, xs, w_gate, w_up, w_down):
    nt = tile_expert.shape[0]
    tm = TM_MOE
    grid_spec = pltpu.PrefetchScalarGridSpec(
        num_scalar_prefetch=5,
        grid=(nt,),
        in_specs=[pl.BlockSpec((tm, D_MODEL), lambda i, te, nv, *_: (jnp.minimum(i, nv[0] - 1), 0)),
                  pl.BlockSpec(memory_space=pl.ANY),
                  pl.BlockSpec(memory_space=pl.ANY),
                  pl.BlockSpec(memory_space=pl.ANY)],
        out_specs=pl.BlockSpec((tm, D_MODEL), lambda i, *_: (i, 0)),
        scratch_shapes=[pltpu.VMEM((2, D_MODEL, D_EXPERT), F32),
                        pltpu.VMEM((2, D_MODEL, D_EXPERT), F32),
                        pltpu.VMEM((2, D_EXPERT, D_MODEL), F32),
                        pltpu.VMEM((D_MODEL, D_EXPERT), BF16),
                        pltpu.VMEM((D_MODEL, D_EXPERT), BF16),
                        pltpu.VMEM((D_EXPERT, D_MODEL), BF16),
                        pltpu.SemaphoreType.DMA((2,))])
    return pl.pallas_call(
        _moe_kernel,
        out_shape=jax.ShapeDtypeStruct((nt * tm, D_MODEL), F32),
        grid_spec=grid_spec,
        compiler_params=_cparams(("arbitrary",)),
        name="moe",
    )(tile_expert, n_valid, first, next_expert, wslot, xs, w_gate, w_up, w_down)


def _final_kernel(pa0_ref, pb0_ref, pa_ref, pb_ref, x1_ref, rw_ref, mod_ref, y_hbm, o_ref, ybuf, sem, *, tm):
    i = pl.program_id(0)
    n = pl.num_programs(0)
    slot = i & 1

    def start_tile(pa, pb, s):
        for r in range(tm):
            pltpu.make_async_copy(y_hbm.at[pl.ds(pa[0, 0, r], 1)], ybuf.at[s, 0, pl.ds(r, 1)], sem.at[s]).start()
            pltpu.make_async_copy(y_hbm.at[pl.ds(pb[0, 0, r], 1)], ybuf.at[s, 1, pl.ds(r, 1)], sem.at[s]).start()

    @pl.when(i == 0)
    def _():
        start_tile(pa0_ref, pb0_ref, 0)

    @pl.when(i + 1 < n)
    def _():
        start_tile(pa_ref, pb_ref, 1 - slot)

    for k in range(2):
        pltpu.make_async_copy(y_hbm.at[pl.ds(0, tm)], ybuf.at[slot, k], sem.at[slot]).wait()
    w = rw_ref[...]
    moe = w[:, 0:1] * ybuf[slot, 0] + w[:, 1:2] * ybuf[slot, 1]
    o_ref[...] = x1_ref[...] + mod_ref[0][5:6, :] * moe


def _final(x1, rw, y, pos3, mod3, *, tile_base, slot_tiles, cond_base, tiles_per_cond):
    t = x1.shape[0]
    tm = TM_ROW
    n = t // tm
    smem_tile = functools.partial(pl.BlockSpec, (1, 1, tm), memory_space=pltpu.SMEM)
    return pl.pallas_call(
        functools.partial(_final_kernel, tm=tm),
        out_shape=jax.ShapeDtypeStruct((t, D_MODEL), F32),
        grid=(n,),
        in_specs=[smem_tile(lambda i: (tile_base, 0, 0)),
                  smem_tile(lambda i: (slot_tiles + tile_base, 0, 0)),
                  smem_tile(lambda i: (tile_base + jnp.minimum(i + 1, n - 1), 0, 0)),
                  smem_tile(lambda i: (slot_tiles + tile_base + jnp.minimum(i + 1, n - 1), 0, 0)),
                  pl.BlockSpec((tm, D_MODEL), lambda i: (i, 0)),
                  pl.BlockSpec((tm, LANES), lambda i: (i, 0)),
                  pl.BlockSpec((1, N_MOD, D_MODEL), lambda i: (cond_base + i // tiles_per_cond, 0, 0)),
                  pl.BlockSpec(memory_space=pl.ANY)],
        out_specs=pl.BlockSpec((tm, D_MODEL), lambda i: (i, 0)),
        scratch_shapes=[pltpu.VMEM((2, 2, tm, D_MODEL), F32), pltpu.SemaphoreType.DMA((2,))],
        compiler_params=_cparams(("arbitrary",)),
        name="final_ctx" if cond_base == 0 else "final_lat",
    )(pos3, pos3, pos3, pos3, x1, rw, mod3, y)


def _rope_table(n_tokens):
    rows = n_tokens // GRID_W
    inv = ROPE_BASE ** (-jnp.arange(ROPE_PAIRS, dtype=F32) / ROPE_PAIRS)
    row_ang = jnp.arange(rows, dtype=F32)[:, None] * inv
    col_ang = jnp.arange(GRID_W, dtype=F32)[:, None] * inv
    cr, sr = (jnp.repeat(f(row_ang), GRID_W, axis=0) for f in (jnp.cos, jnp.sin))
    cc, sc = (jnp.tile(f(col_ang), (rows, 1)) for f in (jnp.cos, jnp.sin))
    return jnp.concatenate([cr, cr, cc, cc, -sr, sr, -sc, sc], axis=-1)


def _swap_halves(a):
    p = ROPE_PAIRS
    return jnp.concatenate([a[..., p:2 * p], a[..., :p], a[..., 3 * p:], a[..., 2 * p:3 * p]], axis=-1)


def _head_gains(g):
    rope = g[QK_NOPE:]
    return jnp.stack([g[:QK_NOPE], jnp.concatenate([rope, _swap_halves(rope)])])


def kernel(x_prompt, x_sample, cache_ckv, cache_krope, c, c_ctx, ada_w, ada_b, norm1_g, w_in, conv_w,
           q_lora_g, w_uq, kv_lora_g, w_ukv, q_head_g, k_head_g, w_out, norm2_g, router_g, router_e,
           w_gate, w_up, w_down):
    depth = ada_w.shape[0]
    assert depth == 1
    bp, sp, _ = x_prompt.shape
    bs, ss, _ = x_sample.shape
    past = cache_ckv.shape[2]
    tp, ts = bp * sp, bs * ss
    assert sp & (sp - 1) == 0 and ss & (ss - 1) == 0
    assert tp % TM_IN == 0 and ss % TM_IN == 0 and TM_IN % sp == 0 and ss % TQ == 0
    assert bs + 1 <= 8 and past % 256 == 0 and TM_ROW == TM_MOE and tp % TM_ROW == 0 and ts % TM_ROW == 0

    w_in_b = _winprep(w_in[0], _swap_halves(w_in[0][:, IN_COLS - 2 * QK_ROPE:]).astype(BF16))
    uq = w_uq[0].reshape(Q_LORA, N_HEADS, QK_DIM)
    w_uq_b = jnp.concatenate([uq, _swap_halves(uq[..., QK_NOPE:])], axis=-1)
    w_uq_b = w_uq_b.reshape(Q_LORA, N_HEADS * HEAD_W).astype(BF16)
    ukv = w_ukv[0].reshape(KV_LORA, N_HEADS, QK_NOPE + V_DIM)
    w_uk_b = ukv[..., :QK_NOPE].reshape(KV_LORA, N_HEADS * QK_NOPE).astype(BF16)
    w_uvt_b = ukv[..., QK_NOPE:].reshape(KV_LORA, N_HEADS * V_DIM).T.astype(BF16)
    w_out_b = w_out[0].astype(BF16)
    wr = jnp.concatenate([router_g[0], router_e[0],
                          jnp.zeros((D_MODEL, ROUTER_COLS - N_EXPERT_GROUPS - N_EXPERTS), F32)], axis=1)
    wr_hi = lax.bitcast_convert_type(lax.bitcast_convert_type(wr, jnp.uint32) & jnp.uint32(0xFFFF0000), F32)
    wr = jnp.concatenate([wr_hi, wr - wr_hi], axis=1).astype(BF16)
    gq, gk = _head_gains(q_head_g[0]), _head_gains(k_head_g[0])
    n1g, n2g = norm1_g[0].reshape(1, D_MODEL), norm2_g[0].reshape(1, D_MODEL)
    qlg, kvg = q_lora_g[0].reshape(1, Q_LORA), kv_lora_g[0].reshape(1, KV_LORA)
    cs_lat = _rope_table(ss)
    cs_id = jnp.concatenate([jnp.ones((TM_IN, QK_ROPE), F32), jnp.zeros((TM_IN, QK_ROPE), F32)], axis=1)

    cond8 = jnp.concatenate([c_ctx[None, :], c, jnp.zeros((8 - 1 - bs, D_MODEL), F32)], axis=0)
    mod3 = _modulation(cond8, ada_w[0], ada_b[0]).reshape(8, N_MOD, D_MODEL)

    xp2, xs2 = x_prompt.reshape(tp, D_MODEL), x_sample.reshape(ts, D_MODEL)
    big = 1 << 30

    bg_p, cu_p, q_p, k_p, vt_p, ckv_p, kr_p = _inproj(
        xp2, mod3, n1g, w_in_b, qlg, w_uq_b, kvg, w_uk_b, w_uvt_b, gq, gk, cs_id,
        cond_base=0, tiles_per_cond=big, cs_tiles=1, emit_cache=True)
    o_p = _attention(q_p.reshape(bp, sp, -1), [(k_p.reshape(bp, sp, -1), vt_p)],
                     tq=sp, heads=N_HEADS, name="attn_ctx")
    x1_p, h2_p, rid_p, rw_p = _outproj(
        xp2, bg_p, cu_p, o_p.reshape(tp, -1), mod3, conv_w[0], w_out_b, n2g, wr,
        cond_base=0, tiles_per_cond=big, seq_len=sp)

    kr_c = cache_krope[:, 0].reshape(bs * past, QK_ROPE)
    k_c, vt_c = _kvcache(cache_ckv[:, 0].reshape(bs * past, KV_LORA), jnp.concatenate([kr_c, kr_c], axis=1),
                         cs_id, w_uk_b, w_uvt_b, gk)
    bg_s, cu_s, q_s, k_s, vt_s = _inproj(
        xs2, mod3, n1g, w_in_b, qlg, w_uq_b, kvg, w_uk_b, w_uvt_b, gq, gk, cs_lat,
        cond_base=1, tiles_per_cond=ss // TM_IN, cs_tiles=ss // TM_IN, emit_cache=False)
    o_s = _attention(q_s.reshape(bs, ss, -1),
                     [(k_c.reshape(bs, past, -1), vt_c), (k_s.reshape(bs, ss, -1), vt_s)],
                     tq=TQ, heads=1, name="attn_lat")
    x1_s, h2_s, rid_s, rw_s = _outproj(
        xs2, bg_s, cu_s, o_s.reshape(ts, -1), mod3, conv_w[0], w_out_b, n2g, wr,
        cond_base=1, tiles_per_cond=ss // TM_OUT, seq_len=ss)

    n_tok = tp + ts
    slot_tiles = n_tok // TM_ROW
    nt = 2 * n_tok // TM_MOE + N_EXPERTS
    rid = jnp.concatenate([rid_p[:, :2], rid_s[:, :2]], axis=0).T.reshape(2 * n_tok)
    pos, te, n_valid, pad_start, pad_cnt, first, next_expert, wslot = _route_tables(rid, tm=TM_MOE)
    pos3 = pos.reshape(2 * slot_tiles, 1, TM_ROW)
    xs = _dispatch(pad_start, pad_cnt, n_valid, pos3, h2_p, h2_s, nt=nt)
    y = _moe(te, n_valid, first, next_expert, wslot, xs, w_gate[0], w_up[0], w_down[0])

    y_p = _final(x1_p, rw_p, y, pos3, mod3, tile_base=0, slot_tiles=slot_tiles, cond_base=0, tiles_per_cond=big)
    y_s = _final(x1_s, rw_s, y, pos3, mod3, tile_base=tp // TM_ROW, slot_tiles=slot_tiles, cond_base=1,
                 tiles_per_cond=ss // TM_ROW)

    return (y_p.reshape(bp, sp, D_MODEL), y_s.reshape(bs, ss, D_MODEL),
            ckv_p.reshape(bp, 1, sp, KV_LORA), kr_p.reshape(bp, 1, sp, QK_ROPE))
```

```python
import functools

import jax
import jax.numpy as jnp
from jax import lax
from jax.experimental import pallas as pl
from jax.experimental.pallas import tpu as pltpu

F32 = jnp.float32
BF16 = jnp.bfloat16
HIGHEST = lax.Precision.HIGHEST

D_MODEL = 2048
CONV_WIDTH = 1024
N_HEADS = 8
QK_NOPE = 128
QK_ROPE = 64
V_DIM = 128
QK_DIM = QK_NOPE + QK_ROPE
Q_LORA = 512
KV_LORA = 256
GRID_W = 64
ROPE_PAIRS = QK_ROPE // 4
ROPE_BASE = 10000.0
N_EXPERT_GROUPS = 4
EXPERTS_PER_GROUP = 8
N_EXPERTS = N_EXPERT_GROUPS * EXPERTS_PER_GROUP
D_EXPERT = 512
N_MOD = 6
EPS = 1e-6
LOG2_E = 1.4426950408889634

HEAD_W = 2 * QK_NOPE
LANES = 128
SUBLANES = 8
IN_COLS = 3 * CONV_WIDTH + Q_LORA + KV_LORA + 2 * QK_ROPE
ROUTER_COLS = LANES
VMEM_LIMIT = 56 * 1024 * 1024

TM_IN = 512
TM_OUT = 512
TQ = 512
ATTN_CHUNK = 512
TM_MOE = 256
TM_ROW = 256
BN_MOD = 1024
PAD_BITS = tuple(1 << b for b in reversed(range(TM_MOE.bit_length() - 1)))


def _cparams(sem):
    return pltpu.CompilerParams(dimension_semantics=sem, vmem_limit_bytes=VMEM_LIMIT)


def _const_spec(shape):
    nd = len(shape)
    return pl.BlockSpec(shape, lambda *_: (0,) * nd, pipeline_mode=pl.Buffered(1))


def _rms(x):
    return x * lax.rsqrt(jnp.mean(x * x, axis=-1, keepdims=True) + EPS)


def _rowsum(x):
    return jnp.sum(x, axis=-1, keepdims=True)


def _mod_kernel(c_ref, w_ref, b_ref, o_ref):
    c = c_ref[...]
    s = c / (1.0 + jnp.exp(-c))
    s_hi = s.astype(BF16)
    s_lo = (s - s_hi.astype(F32)).astype(BF16)
    w = w_ref[...]
    w_hi = w.astype(BF16)
    w_lo = (w - w_hi.astype(F32)).astype(BF16)
    rows = s.shape[0]
    a = jnp.dot(jnp.concatenate([s_hi, s_lo], axis=0), w_hi, preferred_element_type=F32)
    b = jnp.dot(s_hi, w_lo, preferred_element_type=F32)
    o_ref[...] = a[:rows] + (a[rows:] + b) + b_ref[...]


def _modulation(cond8, ada_w, ada_b):
    n = ada_w.shape[1]
    return pl.pallas_call(
        _mod_kernel,
        out_shape=jax.ShapeDtypeStruct((8, n), F32),
        grid=(n // BN_MOD,),
        in_specs=[pl.BlockSpec((8, D_MODEL), lambda j: (0, 0)),
                  pl.BlockSpec((D_MODEL, BN_MOD), lambda j: (0, j)),
                  pl.BlockSpec((1, BN_MOD), lambda j: (0, j))],
        out_specs=pl.BlockSpec((8, BN_MOD), lambda j: (0, j)),
        compiler_params=_cparams(("arbitrary",)),
        name="mod",
    )(cond8, ada_w, ada_b.reshape(1, n))


def _winprep_kernel(w_ref, o_ref):
    n = w_ref.shape[0]
    o_ref[:n, :] = w_ref[...].astype(BF16)
    p = ROPE_PAIRS
    for dst, src in ((0, p), (p, 0), (2 * p, 3 * p), (3 * p, 2 * p)):
        o_ref[n + dst:n + dst + p, :] = w_ref[n - QK_ROPE + src:n - QK_ROPE + src + p, :].astype(BF16)


def _winprep(w_in_t):
    n, k = w_in_t.shape
    tk = 512
    return pl.pallas_call(
        _winprep_kernel,
        out_shape=jax.ShapeDtypeStruct((IN_COLS, k), BF16),
        grid=(k // tk,),
        in_specs=[pl.BlockSpec((n, tk), lambda i: (0, i))],
        out_specs=pl.BlockSpec((IN_COLS, tk), lambda i: (0, i)),
        compiler_params=_cparams(("arbitrary",)),
        name="winprep",
    )(w_in_t)


def _emit_kv(ckv, kraw, cs, w_uk_ref, w_uvt_ref, gk_ref, k_ref, vt_ref):
    cb = ckv.astype(BF16)
    kn_all = jnp.dot(cb, w_uk_ref[...], preferred_element_type=F32)
    vt = lax.dot_general(w_uvt_ref[...], cb, (((1,), (1,)), ((), ())), preferred_element_type=F32)
    vt_ref[...] = vt.astype(BF16)
    ss_rope = 0.5 * _rowsum(kraw * kraw)
    t = kraw * (cs * gk_ref[1:2, :])
    tt = t + pltpu.roll(t, QK_ROPE, axis=1)
    g_nope = gk_ref[0:1, :]
    for h in range(N_HEADS):
        kn = kn_all[:, h * QK_NOPE:(h + 1) * QK_NOPE]
        r = lax.rsqrt((_rowsum(kn * kn) + ss_rope) * (1.0 / QK_DIM) + EPS)
        k_ref[:, h * HEAD_W:h * HEAD_W + QK_NOPE] = (kn * r * g_nope).astype(BF16)
        k_ref[:, h * HEAD_W + QK_NOPE:(h + 1) * HEAD_W] = (tt * r).astype(BF16)


def _kvcache_kernel(ckv_ref, kraw_ref, cs_ref, w_uk_ref, w_uvt_ref, gk_ref, k_ref, vt_ref):
    _emit_kv(ckv_ref[...], kraw_ref[...], cs_ref[...], w_uk_ref, w_uvt_ref, gk_ref, k_ref, vt_ref)


def _kvcache(ckv, kraw, cs_id, w_uk_b, w_uvt_b, gk):
    n = ckv.shape[0]
    tm = 256
    return pl.pallas_call(
        _kvcache_kernel,
        out_shape=(jax.ShapeDtypeStruct((n, N_HEADS * HEAD_W), BF16),
                   jax.ShapeDtypeStruct((N_HEADS * V_DIM, n), BF16)),
        grid=(n // tm,),
        in_specs=[pl.BlockSpec((tm, KV_LORA), lambda i: (i, 0)),
                  pl.BlockSpec((tm, LANES), lambda i: (i, 0)),
                  pl.BlockSpec((tm, LANES), lambda i: (0, 0)),
                  _const_spec((KV_LORA, N_HEADS * QK_NOPE)),
                  _const_spec((N_HEADS * V_DIM, KV_LORA)),
                  _const_spec((2, LANES))],
        out_specs=(pl.BlockSpec((tm, N_HEADS * HEAD_W), lambda i: (i, 0)),
                   pl.BlockSpec((N_HEADS * V_DIM, tm), lambda i: (0, i))),
        compiler_params=_cparams(("arbitrary",)),
        name="kvcache",
    )(ckv, kraw, cs_id, w_uk_b, w_uvt_b, gk)


def _inproj_kernel(x_ref, mod_ref, n1g_ref, w_in_ref, qlg_ref, w_uq_ref, kvg_ref, w_uk_ref, w_uvt_ref,
                   gq_ref, gk_ref, cs_ref, bg_ref, cu_ref, q_ref, k_ref, vt_ref, *cache_refs):
    x = x_ref[...]
    mod = mod_ref[0]
    h = _rms(x) * n1g_ref[...] * (1.0 + mod[1:2, :]) + mod[0:1, :]
    hb = h.astype(BF16)

    def proj(a, b):
        return lax.dot_general(hb, w_in_ref[a:b, :], (((1,), (1,)), ((), ())), preferred_element_type=F32)

    c1, c2, c3 = CONV_WIDTH, 2 * CONV_WIDTH, 3 * CONV_WIDTH
    bg_ref[...] = proj(0, c1).astype(BF16)
    cu_ref[...] = (proj(c1, c2) * proj(c2, c3)).astype(BF16)
    q_lat = proj(c3, c3 + Q_LORA)
    kvk = proj(c3 + Q_LORA, IN_COLS)
    cs = cs_ref[...]

    qn = _rms(q_lat) * qlg_ref[...]
    q = jnp.dot(qn.astype(BF16), w_uq_ref[...], preferred_element_type=F32)
    scale = QK_DIM ** -0.5 * LOG2_E
    g_nope = gq_ref[0:1, :] * scale
    tq = cs * (gq_ref[1:2, :] * scale)
    for hd in range(N_HEADS):
        lo = q[:, hd * HEAD_W:hd * HEAD_W + QK_NOPE]
        up = q[:, hd * HEAD_W + QK_NOPE:(hd + 1) * HEAD_W]
        ss = _rowsum(lo * lo) + 0.5 * _rowsum(up * up)
        r = lax.rsqrt(ss * (1.0 / QK_DIM) + EPS)
        q_ref[:, hd * HEAD_W:hd * HEAD_W + QK_NOPE] = (lo * r * g_nope).astype(BF16)
        q_ref[:, hd * HEAD_W + QK_NOPE:(hd + 1) * HEAD_W] = (up * r * tq).astype(BF16)

    kv_lat = kvk[:, :KV_LORA]
    kraw = kvk[:, KV_LORA:]
    ckv = _rms(kv_lat) * kvg_ref[...]
    if cache_refs:
        ckv_out_ref, kr_out_ref = cache_refs
        ckv_out_ref[...] = ckv
        kr_out_ref[...] = kraw[:, :QK_ROPE]
    _emit_kv(ckv, kraw, cs, w_uk_ref, w_uvt_ref, gk_ref, k_ref, vt_ref)


def _inproj(x2d, mod3, n1g, w_in_b, qlg, w_uq_b, kvg, w_uk_b, w_uvt_b, gq, gk, cs, *,
            cond_base, tiles_per_cond, cs_tiles, emit_cache):
    t = x2d.shape[0]
    tm = TM_IN
    out_shape = [jax.ShapeDtypeStruct((t, CONV_WIDTH), BF16),
                 jax.ShapeDtypeStruct((t, CONV_WIDTH), BF16),
                 jax.ShapeDtypeStruct((t, N_HEADS * HEAD_W), BF16),
                 jax.ShapeDtypeStruct((t, N_HEADS * HEAD_W), BF16),
                 jax.ShapeDtypeStruct((N_HEADS * V_DIM, t), BF16)]
    out_specs = [pl.BlockSpec((tm, CONV_WIDTH), lambda i: (i, 0)),
                 pl.BlockSpec((tm, CONV_WIDTH), lambda i: (i, 0)),
                 pl.BlockSpec((tm, N_HEADS * HEAD_W), lambda i: (i, 0)),
                 pl.BlockSpec((tm, N_HEADS * HEAD_W), lambda i: (i, 0)),
                 pl.BlockSpec((N_HEADS * V_DIM, tm), lambda i: (0, i))]
    if emit_cache:
        out_shape += [jax.ShapeDtypeStruct((t, KV_LORA), F32), jax.ShapeDtypeStruct((t, QK_ROPE), F32)]
        out_specs += [pl.BlockSpec((tm, KV_LORA), lambda i: (i, 0)),
                      pl.BlockSpec((tm, QK_ROPE), lambda i: (i, 0))]
    return pl.pallas_call(
        _inproj_kernel,
        out_shape=tuple(out_shape),
        grid=(t // tm,),
        in_specs=[pl.BlockSpec((tm, D_MODEL), lambda i: (i, 0)),
                  pl.BlockSpec((1, N_MOD, D_MODEL), lambda i: (cond_base + i // tiles_per_cond, 0, 0)),
                  _const_spec((1, D_MODEL)),
                  _const_spec((IN_COLS, D_MODEL)),
                  _const_spec((1, Q_LORA)),
                  _const_spec((Q_LORA, N_HEADS * HEAD_W)),
                  _const_spec((1, KV_LORA)),
                  _const_spec((KV_LORA, N_HEADS * QK_NOPE)),
                  _const_spec((N_HEADS * V_DIM, KV_LORA)),
                  _const_spec((2, LANES)),
                  _const_spec((2, LANES)),
                  pl.BlockSpec((tm, LANES), lambda i: (i % cs_tiles, 0))],
        out_specs=tuple(out_specs),
        compiler_params=_cparams(("arbitrary",)),
        name="inproj_ctx" if emit_cache else "inproj_lat",
    )(x2d, mod3, n1g, w_in_b, qlg, w_uq_b, kvg, w_uk_b, w_uvt_b, gq, gk, cs)


def _attn_kernel(*refs, n_kv, heads):
    q_ref = refs[0]
    o_ref = refs[-1]
    chunks = []
    for j in range(n_kv):
        sk = refs[1 + 2 * j].shape[1]
        step = min(sk, ATTN_CHUNK)
        chunks += [(j, lo, lo + step) for lo in range(0, sk, step)]
    for h in range(heads):
        q = q_ref[0, :, h * HEAD_W:(h + 1) * HEAD_W]
        def score(c):
            j, lo, hi = chunks[c]
            k = refs[1 + 2 * j][0, lo:hi, h * HEAD_W:(h + 1) * HEAD_W]
            return lax.dot_general(k, q, (((1,), (1,)), ((), ())), preferred_element_type=F32)

        m = None
        acc = None
        s_next = score(0)
        for c, (j, lo, hi) in enumerate(chunks):
            s = s_next
            if c + 1 < len(chunks):
                s_next = score(c + 1)
            mc = jnp.max(s, axis=0, keepdims=True)
            m_new = mc if m is None else jnp.maximum(m, mc)
            p = jnp.exp2(s - m_new).astype(BF16)
            vt = refs[2 + 2 * j][h * V_DIM:(h + 1) * V_DIM, lo:hi]
            vt1 = jnp.concatenate([vt, jnp.ones((2 * SUBLANES, hi - lo), BF16)], axis=0)
            part = jnp.dot(vt1, p, preferred_element_type=F32)
            acc = part if acc is None else acc * jnp.exp2(m - m_new) + part
            m = m_new
        ot = acc[:V_DIM, :] / acc[V_DIM:V_DIM + 1, :]
        o_ref[0, :, h * V_DIM:(h + 1) * V_DIM] = ot.T.astype(BF16)


def _attention(q, kvs, *, tq, heads, name):
    b, s, _ = q.shape
    in_specs = [pl.BlockSpec((1, tq, heads * HEAD_W), lambda bi, hi, qi: (bi, qi, hi))]
    args = [q]
    for k, v in kvs:
        sk = k.shape[1]
        in_specs.append(pl.BlockSpec((1, sk, heads * HEAD_W), lambda bi, hi, qi: (bi, 0, hi)))
        in_specs.append(pl.BlockSpec((heads * V_DIM, sk), lambda bi, hi, qi: (hi, bi)))
        args += [k, v]
    return pl.pallas_call(
        functools.partial(_attn_kernel, n_kv=len(kvs), heads=heads),
        out_shape=jax.ShapeDtypeStruct((b, s, N_HEADS * V_DIM), BF16),
        grid=(b, N_HEADS // heads, s // tq),
        in_specs=in_specs,
        out_specs=pl.BlockSpec((1, tq, heads * V_DIM), lambda bi, hi, qi: (bi, qi, hi)),
        compiler_params=_cparams(("arbitrary", "arbitrary", "arbitrary")),
        name=name,
    )(*args)


def _outproj_kernel(x_ref, bg_ref, cu_ref, cup_ref, cun_ref, o_ref, mod_ref, cw_ref, w_out_ref,
                    n2g_ref, wr_ref, x1_ref, h2_ref, rid_ref, rw_ref, *, tm, seq_len):
    i = pl.program_id(0)
    mod = mod_ref[0]
    cu = cu_ref[...].astype(F32)
    prev_row = cup_ref[...].astype(F32)[15:16, :]
    next_row = cun_ref[...].astype(F32)[0:1, :]
    row = lax.broadcasted_iota(jnp.int32, (tm, 1), 0)
    pos = (i * tm + row) & (seq_len - 1)
    up = jnp.where(row == 0, prev_row, pltpu.roll(cu, 1, axis=0))
    up = jnp.where(pos == 0, 0.0, up)
    dn = jnp.where(row == tm - 1, next_row, pltpu.roll(cu, tm - 1, axis=0))
    dn = jnp.where(pos == seq_len - 1, 0.0, dn)
    cw = cw_ref[...]
    y_conv = bg_ref[...].astype(F32) * (up * cw[0:1, :] + cu * cw[1:2, :] + dn * cw[2:3, :])
    mix = jnp.dot(y_conv.astype(BF16), w_out_ref[:CONV_WIDTH, :], preferred_element_type=F32)
    mix = mix + jnp.dot(o_ref[...], w_out_ref[CONV_WIDTH:, :], preferred_element_type=F32)
    x1 = x_ref[...] + mod[2:3, :] * mix
    x1_ref[...] = x1
    h2 = _rms(x1) * n2g_ref[...] * (1.0 + mod[4:5, :]) + mod[3:4, :]
    h2_ref[...] = h2

    h2_hi = h2.astype(BF16)
    h2_lo = (h2 - h2_hi.astype(F32)).astype(BF16)
    hh_hl = jnp.dot(h2_hi, wr_ref[...], preferred_element_type=F32)
    lh = jnp.dot(h2_lo, wr_ref[:, :ROUTER_COLS], preferred_element_type=F32)
    logits = hh_hl[:, :ROUTER_COLS] + (hh_hl[:, ROUTER_COLS:] + lh)
    lane = lax.broadcasted_iota(jnp.int32, logits.shape, 1)
    neg = -jnp.inf
    big = jnp.int32(1 << 20)
    gl = jnp.where(lane < N_EXPERT_GROUPS, logits, neg)
    gmax = jnp.max(gl, axis=-1, keepdims=True)
    p_top = 1.0 / _rowsum(jnp.exp(gl - gmax))
    g_top = jnp.min(jnp.where(gl == gmax, lane, big), axis=-1, keepdims=True)
    e_lo = N_EXPERT_GROUPS + EXPERTS_PER_GROUP * g_top
    el = jnp.where((lane >= e_lo) & (lane < e_lo + EXPERTS_PER_GROUP), logits, neg)
    v1 = jnp.max(el, axis=-1, keepdims=True)
    i1 = jnp.min(jnp.where(el == v1, lane, big), axis=-1, keepdims=True)
    el2 = jnp.where(lane == i1, neg, el)
    v2 = jnp.max(el2, axis=-1, keepdims=True)
    i2 = jnp.min(jnp.where(el2 == v2, lane, big), axis=-1, keepdims=True)
    e21 = jnp.exp(v2 - v1)
    w1 = p_top / (1.0 + e21)
    w2 = w1 * e21
    rid_ref[...] = jnp.where(lane == 0, i1 - N_EXPERT_GROUPS, i2 - N_EXPERT_GROUPS)
    rw_ref[...] = jnp.where(lane == 0, w1, w2)


def _outproj(x2d, bg, cu, o2d, mod3, conv_w, w_out_b, n2g, wr, *, cond_base, tiles_per_cond, seq_len):
    t = x2d.shape[0]
    tm = TM_OUT
    hb = tm // 16
    nhb = t // 16
    return pl.pallas_call(
        functools.partial(_outproj_kernel, tm=tm, seq_len=seq_len),
        out_shape=(jax.ShapeDtypeStruct((t, D_MODEL), F32),
                   jax.ShapeDtypeStruct((t, D_MODEL), F32),
                   jax.ShapeDtypeStruct((t, LANES), jnp.int32),
                   jax.ShapeDtypeStruct((t, LANES), F32)),
        grid=(t // tm,),
        in_specs=[pl.BlockSpec((tm, D_MODEL), lambda i: (i, 0)),
                  pl.BlockSpec((tm, CONV_WIDTH), lambda i: (i, 0)),
                  pl.BlockSpec((tm, CONV_WIDTH), lambda i: (i, 0)),
                  pl.BlockSpec((16, CONV_WIDTH), lambda i: (jnp.maximum(i * hb - 1, 0), 0)),
                  pl.BlockSpec((16, CONV_WIDTH), lambda i: (jnp.minimum((i + 1) * hb, nhb - 1), 0)),
                  pl.BlockSpec((tm, N_HEADS * V_DIM), lambda i: (i, 0)),
                  pl.BlockSpec((1, N_MOD, D_MODEL), lambda i: (cond_base + i // tiles_per_cond, 0, 0)),
                  _const_spec((3, CONV_WIDTH)),
                  _const_spec((D_MODEL, D_MODEL)),
                  _const_spec((1, D_MODEL)),
                  _const_spec((D_MODEL, 2 * ROUTER_COLS))],
        out_specs=(pl.BlockSpec((tm, D_MODEL), lambda i: (i, 0)),
                   pl.BlockSpec((tm, D_MODEL), lambda i: (i, 0)),
                   pl.BlockSpec((tm, LANES), lambda i: (i, 0)),
                   pl.BlockSpec((tm, LANES), lambda i: (i, 0))),
        compiler_params=_cparams(("arbitrary",)),
        name="outproj_ctx" if cond_base == 0 else "outproj_lat",
    )(x2d, bg, cu, cu, cu, o2d, mod3, conv_w, w_out_b, n2g, wr)


def _route_tables(rid, *, tm):
    n_pairs = rid.shape[0]
    nt = n_pairs // tm + N_EXPERTS
    experts = jnp.arange(N_EXPERTS, dtype=jnp.int32)
    onehot = (rid[None, :] == experts[:, None]).astype(jnp.int32)
    csum = jnp.cumsum(onehot, axis=1)
    counts = csum[:, -1]
    tiles_e = (counts + tm - 1) // tm
    tile_end = jnp.cumsum(tiles_e)
    tile_start = tile_end - tiles_e
    n_valid = tile_end[-1]
    pos = jnp.sum(onehot * (csum - 1 + tile_start[:, None] * tm), axis=0)
    tile_raw = jnp.arange(nt, dtype=jnp.int32)
    tile_idx = jnp.minimum(tile_raw, n_valid - 1)
    te = jnp.sum((tile_end[None, :] <= tile_idx[:, None]).astype(jnp.int32), axis=1)
    used = tiles_e > 0
    first = ((tile_raw == tile_start[te]) & (tile_raw < n_valid)).astype(jnp.int32)
    nxt = lax.cummin(jnp.where(used, experts, N_EXPERTS), axis=0, reverse=True)
    nxt = jnp.concatenate([nxt[1:], jnp.full((1,), N_EXPERTS, jnp.int32)])
    next_expert = jnp.where(nxt[te] < N_EXPERTS, nxt[te], -1).astype(jnp.int32)
    wslot = ((jnp.cumsum(used.astype(jnp.int32)) - 1)[te] & 1).astype(jnp.int32)
    return (pos, te, n_valid.reshape(1).astype(jnp.int32),
            (tile_start * tm + counts).astype(jnp.int32), (tiles_e * tm - counts).astype(jnp.int32),
            first, next_expert, wslot)


def _dispatch_kernel(pstart_ref, pcnt_ref, nval_ref, pos0_ref, pos1_ref, h2a_ref, h2b_ref, xs_hbm,
                     buf, zbuf, sem, sem_z, *, tm, nt, n_a):
    i = pl.program_id(0)
    n = pl.num_programs(0)
    slot = i & 1

    def row_copy(r, pos_ref, s):
        return pltpu.make_async_copy(buf.at[s, pl.ds(r, 1)], xs_hbm.at[pl.ds(pos_ref[0, 0, r], 1)], sem.at[s])

    def wait_tile(s):
        for _ in range(2):
            pltpu.make_async_copy(buf.at[s], xs_hbm.at[pl.ds(0, tm)], sem.at[s]).wait()

    def pad_copies(e, fn):
        cnt = pcnt_ref[e]
        start = pstart_ref[e]
        off = start + cnt
        for b in PAD_BITS:
            if b < SUBLANES:
                break
            off = off - (cnt & b)
            dst = pl.multiple_of(off, SUBLANES)

            @pl.when((cnt & b) != 0)
            def _():
                fn(pltpu.make_async_copy(zbuf.at[pl.ds(0, b)], xs_hbm.at[pl.ds(dst, b)], sem_z))
        for j in range(SUBLANES - 1):
            @pl.when(j < (cnt & (SUBLANES - 1)))
            def _():
                fn(pltpu.make_async_copy(zbuf.at[pl.ds(0, 1)], xs_hbm.at[pl.ds(start + j, 1)], sem_z))

    def tail_copy(j):
        return pltpu.make_async_copy(zbuf, xs_hbm.at[pl.ds(pl.multiple_of(j * tm, tm), tm)], sem_z)

    @pl.when(i == 0)
    def _():
        zbuf[...] = jnp.zeros((tm, D_MODEL), F32)
        lax.fori_loop(0, N_EXPERTS, lambda e, c: (pad_copies(e, lambda d: d.start()), c)[1], 0)
        lax.fori_loop(nval_ref[0], nt, lambda j, c: (tail_copy(j).start(), c)[1], 0)
        lax.fori_loop(0, N_EXPERTS, lambda e, c: (pad_copies(e, lambda d: d.wait()), c)[1], 0)
        lax.fori_loop(nval_ref[0], nt, lambda j, c: (tail_copy(j).wait(), c)[1], 0)

    @pl.when(i >= 2)
    def _():
        wait_tile(slot)

    @pl.when(i < n_a)
    def _():
        buf[slot] = h2a_ref[...]

    @pl.when(i >= n_a)
    def _():
        buf[slot] = h2b_ref[...]

    for r in range(tm):
        row_copy(r, pos0_ref, slot).start()
        row_copy(r, pos1_ref, slot).start()

    @pl.when(i == n - 1)
    def _():
        wait_tile(slot)
        wait_tile(1 - slot)


def _dispatch(pad_start, pad_cnt, n_valid, pos3, h2_a, h2_b, *, nt):
    tm = TM_ROW
    n_a, n_b = h2_a.shape[0] // tm, h2_b.shape[0] // tm
    smem_tile = functools.partial(pl.BlockSpec, (1, 1, tm), memory_space=pltpu.SMEM)
    grid_spec = pltpu.PrefetchScalarGridSpec(
        num_scalar_prefetch=3,
        grid=(n_a + n_b,),
        in_specs=[smem_tile(lambda i, *_: (i, 0, 0)),
                  smem_tile(lambda i, *_: (n_a + n_b + i, 0, 0)),
                  pl.BlockSpec((tm, D_MODEL), lambda i, *_: (jnp.minimum(i, n_a - 1), 0)),
                  pl.BlockSpec((tm, D_MODEL), lambda i, *_: (jnp.maximum(i - n_a, 0), 0))],
        out_specs=pl.BlockSpec(memory_space=pl.ANY),
        scratch_shapes=[pltpu.VMEM((2, tm, D_MODEL), F32),
                        pltpu.VMEM((tm, D_MODEL), F32),
                        pltpu.SemaphoreType.DMA((2,)),
                        pltpu.SemaphoreType.DMA])
    return pl.pallas_call(
        functools.partial(_dispatch_kernel, tm=tm, nt=nt, n_a=n_a),
        out_shape=jax.ShapeDtypeStruct((nt * TM_MOE, D_MODEL), F32),
        grid_spec=grid_spec,
        compiler_params=pltpu.CompilerParams(dimension_semantics=("arbitrary",),
                                             vmem_limit_bytes=VMEM_LIMIT, has_side_effects=True),
        name="dispatch",
    )(pad_start, pad_cnt, n_valid, pos3, pos3, h2_a, h2_b)


def _moe_kernel(texp_ref, nval_ref, first_ref, next_ref, wslot_ref, xs_ref, wg_hbm, wu_hbm, wd_hbm, y_ref,
                wg32, wu32, wd32, wgb, wub, wdb, sem_w):
    i = pl.program_id(0)

    def weight_copies(e, s):
        return (pltpu.make_async_copy(wg_hbm.at[e], wg32.at[s], sem_w.at[s]),
                pltpu.make_async_copy(wu_hbm.at[e], wu32.at[s], sem_w.at[s]),
                pltpu.make_async_copy(wd_hbm.at[e], wd32.at[s], sem_w.at[s]))

    @pl.when(i == 0)
    def _():
        for d in weight_copies(texp_ref[0], 0):
            d.start()

    @pl.when(i < nval_ref[0])
    def _():
        @pl.when(first_ref[i] == 1)
        def _():
            s = wslot_ref[i]
            for d in weight_copies(texp_ref[i], s):
                d.wait()
            e_next = next_ref[i]

            @pl.when(e_next >= 0)
            def _():
                for d in weight_copies(e_next, 1 - s):
                    d.start()
            wgb[...] = wg32[s].astype(BF16)
            wub[...] = wu32[s].astype(BF16)
            wdb[...] = wd32[s].astype(BF16)

        x = xs_ref[...].astype(BF16)
        g = jnp.dot(x, wgb[...], preferred_element_type=F32)
        u = jnp.dot(x, wub[...], preferred_element_type=F32)
        a = g / (1.0 + jnp.exp(-g)) * u
        y_ref[...] = jnp.dot(a.astype(BF16), wdb[...], preferred_element_type=F32)

    @pl.when(i >= nval_ref[0])
    def _():
        y_ref[...] = jnp.zeros(y_ref.shape, F32)


def _moe(tile_expert, n_valid, first, next_expert, wslot, xs, w_gate, w_up, w_down):
    nt = tile_expert.shape[0]
    tm = TM_MOE
    grid_spec = pltpu.PrefetchScalarGridSpec(
        num_scalar_prefetch=5,
        grid=(nt,),
        in_specs=[pl.BlockSpec((tm, D_MODEL), lambda i, te, nv, *_: (jnp.minimum(i, nv[0] - 1), 0)),
                  pl.BlockSpec(memory_space=pl.ANY),
                  pl.BlockSpec(memory_space=pl.ANY),
                  pl.BlockSpec(memory_space=pl.ANY)],
        out_specs=pl.BlockSpec((tm, D_MODEL), lambda i, *_: (i, 0)),
        scratch_shapes=[pltpu.VMEM((2, D_MODEL, D_EXPERT), F32),
                        pltpu.VMEM((2, D_MODEL, D_EXPERT), F32),
                        pltpu.VMEM((2, D_EXPERT, D_MODEL), F32),
                        pltpu.VMEM((D_MODEL, D_EXPERT), BF16),
                        pltpu.VMEM((D_MODEL, D_EXPERT), BF16),
                        pltpu.VMEM((D_EXPERT, D_MODEL), BF16),
                        pltpu.SemaphoreType.DMA((2,))])
    return pl.pallas_call(
        _moe_kernel,
        out_shape=jax.ShapeDtypeStruct((nt * tm, D_MODEL), F32),
        grid_spec=grid_spec,
        compiler_params=_cparams(("arbitrary",)),
        name="moe",
    )(tile_expert, n_valid, first, next_expert, wslot, xs, w_gate, w_up, w_down)


def _final_kernel(pa0_ref, pb0_ref, pa_ref, pb_ref, x1_ref, rw_ref, mod_ref, y_hbm, o_ref, ybuf, sem, *, tm):
    i = pl.program_id(0)
    n = pl.num_programs(0)
    slot = i & 1

    def start_tile(pa, pb, s):
        for r in range(tm):
            pltpu.make_async_copy(y_hbm.at[pl.ds(pa[0, 0, r], 1)], ybuf.at[s, 0, pl.ds(r, 1)], sem.at[s]).start()
            pltpu.make_async_copy(y_hbm.at[pl.ds(pb[0, 0, r], 1)], ybuf.at[s, 1, pl.ds(r, 1)], sem.at[s]).start()

    @pl.when(i == 0)
    def _():
        start_tile(pa0_ref, pb0_ref, 0)

    @pl.when(i + 1 < n)
    def _():
        start_tile(pa_ref, pb_ref, 1 - slot)

    for k in range(2):
        pltpu.make_async_copy(y_hbm.at[pl.ds(0, tm)], ybuf.at[slot, k], sem.at[slot]).wait()
    w = rw_ref[...]
    moe = w[:, 0:1] * ybuf[slot, 0] + w[:, 1:2] * ybuf[slot, 1]
    o_ref[...] = x1_ref[...] + mod_ref[0][5:6, :] * moe


def _final(x1, rw, y, pos3, mod3, *, tile_base, slot_tiles, cond_base, tiles_per_cond):
    t = x1.shape[0]
    tm = TM_ROW
    n = t // tm
    smem_tile = functools.partial(pl.BlockSpec, (1, 1, tm), memory_space=pltpu.SMEM)
    return pl.pallas_call(
        functools.partial(_final_kernel, tm=tm),
        out_shape=jax.ShapeDtypeStruct((t, D_MODEL), F32),
        grid=(n,),
        in_specs=[smem_tile(lambda i: (tile_base, 0, 0)),
                  smem_tile(lambda i: (slot_tiles + tile_base, 0, 0)),
                  smem_tile(lambda i: (tile_base + jnp.minimum(i + 1, n - 1), 0, 0)),
                  smem_tile(lambda i: (slot_tiles + tile_base + jnp.minimum(i + 1, n - 1), 0, 0)),
                  pl.BlockSpec((tm, D_MODEL), lambda i: (i, 0)),
                  pl.BlockSpec((tm, LANES), lambda i: (i, 0)),
                  pl.BlockSpec((1, N_MOD, D_MODEL), lambda i: (cond_base + i // tiles_per_cond, 0, 0)),
                  pl.BlockSpec(memory_space=pl.ANY)],
        out_specs=pl.BlockSpec((tm, D_MODEL), lambda i: (i, 0)),
        scratch_shapes=[pltpu.VMEM((2, 2, tm, D_MODEL), F32), pltpu.SemaphoreType.DMA((2,))],
        compiler_params=_cparams(("arbitrary",)),
        name="final_ctx" if cond_base == 0 else "final_lat",
    )(pos3, pos3, pos3, pos3, x1, rw, mod3, y)


def _rope_table(n_tokens):
    rows = n_tokens // GRID_W
    inv = ROPE_BASE ** (-jnp.arange(ROPE_PAIRS, dtype=F32) / ROPE_PAIRS)
    row_ang = jnp.arange(rows, dtype=F32)[:, None] * inv
    col_ang = jnp.arange(GRID_W, dtype=F32)[:, None] * inv
    cr, sr = (jnp.repeat(f(row_ang), GRID_W, axis=0) for f in (jnp.cos, jnp.sin))
    cc, sc = (jnp.tile(f(col_ang), (rows, 1)) for f in (jnp.cos, jnp.sin))
    return jnp.concatenate([cr, cr, cc, cc, -sr, sr, -sc, sc], axis=-1)


def _swap_halves(a):
    p = ROPE_PAIRS
    return jnp.concatenate([a[..., p:2 * p], a[..., :p], a[..., 3 * p:], a[..., 2 * p:3 * p]], axis=-1)


def _head_gains(g):
    rope = g[QK_NOPE:]
    return jnp.stack([g[:QK_NOPE], jnp.concatenate([rope, _swap_halves(rope)])])


def kernel(x_prompt, x_sample, cache_ckv, cache_krope, c, c_ctx, ada_w, ada_b, norm1_g, w_in, conv_w,
           q_lora_g, w_uq, kv_lora_g, w_ukv, q_head_g, k_head_g, w_out, norm2_g, router_g, router_e,
           w_gate, w_up, w_down):
    depth = ada_w.shape[0]
    assert depth == 1
    bp, sp, _ = x_prompt.shape
    bs, ss, _ = x_sample.shape
    past = cache_ckv.shape[2]
    tp, ts = bp * sp, bs * ss
    assert sp & (sp - 1) == 0 and ss & (ss - 1) == 0
    assert tp % TM_IN == 0 and ss % TM_IN == 0 and TM_IN % sp == 0 and ss % TQ == 0
    assert bs + 1 <= 8 and past % 256 == 0 and TM_ROW == TM_MOE and tp % TM_ROW == 0 and ts % TM_ROW == 0

    w_in_b = _winprep(w_in[0].T)
    uq = w_uq[0].reshape(Q_LORA, N_HEADS, QK_DIM)
    w_uq_b = jnp.concatenate([uq, _swap_halves(uq[..., QK_NOPE:])], axis=-1)
    w_uq_b = w_uq_b.reshape(Q_LORA, N_HEADS * HEAD_W).astype(BF16)
    ukv = w_ukv[0].reshape(KV_LORA, N_HEADS, QK_NOPE + V_DIM)
    w_uk_b = ukv[..., :QK_NOPE].reshape(KV_LORA, N_HEADS * QK_NOPE).astype(BF16)
    w_uvt_b = ukv[..., QK_NOPE:].reshape(KV_LORA, N_HEADS * V_DIM).T.astype(BF16)
    w_out_b = w_out[0].astype(BF16)
    wr = jnp.concatenate([router_g[0], router_e[0],
                          jnp.zeros((D_MODEL, ROUTER_COLS - N_EXPERT_GROUPS - N_EXPERTS), F32)], axis=1)
    wr_hi = lax.bitcast_convert_type(lax.bitcast_convert_type(wr, jnp.uint32) & jnp.uint32(0xFFFF0000), F32)
    wr = jnp.concatenate([wr_hi, wr - wr_hi], axis=1).astype(BF16)
    gq, gk = _head_gains(q_head_g[0]), _head_gains(k_head_g[0])
    n1g, n2g = norm1_g[0].reshape(1, D_MODEL), norm2_g[0].reshape(1, D_MODEL)
    qlg, kvg = q_lora_g[0].reshape(1, Q_LORA), kv_lora_g[0].reshape(1, KV_LORA)
    cs_lat = _rope_table(ss)
    cs_id = jnp.concatenate([jnp.ones((TM_IN, QK_ROPE), F32), jnp.zeros((TM_IN, QK_ROPE), F32)], axis=1)

    cond8 = jnp.concatenate([c_ctx[None, :], c, jnp.zeros((8 - 1 - bs, D_MODEL), F32)], axis=0)
    mod3 = _modulation(cond8, ada_w[0], ada_b[0]).reshape(8, N_MOD, D_MODEL)

    xp2, xs2 = x_prompt.reshape(tp, D_MODEL), x_sample.reshape(ts, D_MODEL)
    big = 1 << 30

    bg_p, cu_p, q_p, k_p, vt_p, ckv_p, kr_p = _inproj(
        xp2, mod3, n1g, w_in_b, qlg, w_uq_b, kvg, w_uk_b, w_uvt_b, gq, gk, cs_id,
        cond_base=0, tiles_per_cond=big, cs_tiles=1, emit_cache=True)
    o_p = _attention(q_p.reshape(bp, sp, -1), [(k_p.reshape(bp, sp, -1), vt_p)],
                     tq=sp, heads=N_HEADS, name="attn_ctx")
    x1_p, h2_p, rid_p, rw_p = _outproj(
        xp2, bg_p, cu_p, o_p.reshape(tp, -1), mod3, conv_w[0], w_out_b, n2g, wr,
        cond_base=0, tiles_per_cond=big, seq_len=sp)

    kr_c = cache_krope[:, 0].reshape(bs * past, QK_ROPE)
    k_c, vt_c = _kvcache(cache_ckv[:, 0].reshape(bs * past, KV_LORA), jnp.concatenate([kr_c, kr_c], axis=1),
                         cs_id, w_uk_b, w_uvt_b, gk)
    bg_s, cu_s, q_s, k_s, vt_s = _inproj(
        xs2, mod3, n1g, w_in_b, qlg, w_uq_b, kvg, w_uk_b, w_uvt_b, gq, gk, cs_lat,
        cond_base=1, tiles_per_cond=ss // TM_IN, cs_tiles=ss // TM_IN, emit_cache=False)
    o_s = _attention(q_s.reshape(bs, ss, -1),
                     [(k_c.reshape(bs, past, -1), vt_c), (k_s.reshape(bs, ss, -1), vt_s)],
                     tq=TQ, heads=1, name="attn_lat")
    x1_s, h2_s, rid_s, rw_s = _outproj(
        xs2, bg_s, cu_s, o_s.reshape(ts, -1), mod3, conv_w[0], w_out_b, n2g, wr,
        cond_base=1, tiles_per_cond=ss // TM_OUT, seq_len=ss)

    n_tok = tp + ts
    slot_tiles = n_tok // TM_ROW
    nt = 2 * n_tok // TM_MOE + N_EXPERTS
    rid = jnp.concatenate([rid_p[:, :2], rid_s[:, :2]], axis=0).T.reshape(2 * n_tok)
    pos, te, n_valid, pad_start, pad_cnt, first, next_expert, wslot = _route_tables(rid, tm=TM_MOE)
    pos3 = pos.reshape(2 * slot_tiles, 1, TM_ROW)
    xs = _dispatch(pad_start, pad_cnt, n_valid, pos3, h2_p, h2_s, nt=nt)
    y = _moe(te, n_valid, first, next_expert, wslot, xs, w_gate[0], w_up[0], w_down[0])

    y_p = _final(x1_p, rw_p, y, pos3, mod3, tile_base=0, slot_tiles=slot_tiles, cond_base=0, tiles_per_cond=big)
    y_s = _final(x1_s, rw_s, y, pos3, mod3, tile_base=tp // TM_ROW, slot_tiles=slot_tiles, cond_base=1,
                 tiles_per_cond=ss // TM_ROW)

    return (y_p.reshape(bp, sp, D_MODEL), y_s.reshape(bs, ss, D_MODEL),
            ckv_p.reshape(bp, 1, sp, KV_LORA), kr_p.reshape(bp, 1, sp, QK_ROPE))
```

```python
import functools

import jax
import jax.numpy as jnp
from jax import lax
from jax.experimental import pallas as pl
from jax.experimental.pallas import tpu as pltpu

F32 = jnp.float32
BF16 = jnp.bfloat16
HIGHEST = lax.Precision.HIGHEST

D_MODEL = 2048
CONV_WIDTH = 1024
N_HEADS = 8
QK_NOPE = 128
QK_ROPE = 64
V_DIM = 128
QK_DIM = QK_NOPE + QK_ROPE
Q_LORA = 512
KV_LORA = 256
GRID_W = 64
ROPE_PAIRS = QK_ROPE // 4
ROPE_BASE = 10000.0
N_EXPERT_GROUPS = 4
EXPERTS_PER_GROUP = 8
N_EXPERTS = N_EXPERT_GROUPS * EXPERTS_PER_GROUP
D_EXPERT = 512
N_MOD = 6
EPS = 1e-6
LOG2_E = 1.4426950408889634

HEAD_W = 2 * QK_NOPE
LANES = 128
SUBLANES = 8
IN_COLS = 3 * CONV_WIDTH + Q_LORA + KV_LORA + 2 * QK_ROPE
ROUTER_COLS = LANES
VMEM_LIMIT = 56 * 1024 * 1024

TM_IN = 512
TM_OUT = 512
TQ = 512
ATTN_CHUNK = 512
ATTN_AHEAD = 2
TM_MOE = 256
TM_ROW = 256
BN_MOD = 1024
PAD_BITS = tuple(1 << b for b in reversed(range(TM_MOE.bit_length() - 1)))


def _cparams(sem):
    return pltpu.CompilerParams(dimension_semantics=sem, vmem_limit_bytes=VMEM_LIMIT)


def _const_spec(shape):
    nd = len(shape)
    return pl.BlockSpec(shape, lambda *_: (0,) * nd, pipeline_mode=pl.Buffered(1))


def _rms(x):
    return x * lax.rsqrt(jnp.mean(x * x, axis=-1, keepdims=True) + EPS)


def _rowsum(x):
    return jnp.sum(x, axis=-1, keepdims=True)


def _mod_kernel(c_ref, w_ref, b_ref, o_ref):
    c = c_ref[...]
    s = c / (1.0 + jnp.exp(-c))
    s_hi = s.astype(BF16)
    s_lo = (s - s_hi.astype(F32)).astype(BF16)
    w = w_ref[...]
    w_hi = w.astype(BF16)
    w_lo = (w - w_hi.astype(F32)).astype(BF16)
    rows = s.shape[0]
    a = jnp.dot(jnp.concatenate([s_hi, s_lo], axis=0), w_hi, preferred_element_type=F32)
    b = jnp.dot(s_hi, w_lo, preferred_element_type=F32)
    o_ref[...] = a[:rows] + (a[rows:] + b) + b_ref[...]


def _modulation(cond8, ada_w, ada_b):
    n = ada_w.shape[1]
    return pl.pallas_call(
        _mod_kernel,
        out_shape=jax.ShapeDtypeStruct((8, n), F32),
        grid=(n // BN_MOD,),
        in_specs=[pl.BlockSpec((8, D_MODEL), lambda j: (0, 0)),
                  pl.BlockSpec((D_MODEL, BN_MOD), lambda j: (0, j)),
                  pl.BlockSpec((1, BN_MOD), lambda j: (0, j))],
        out_specs=pl.BlockSpec((8, BN_MOD), lambda j: (0, j)),
        compiler_params=_cparams(("arbitrary",)),
        name="mod",
    )(cond8, ada_w, ada_b.reshape(1, n))


def _winprep_kernel(w_ref, o_ref):
    n = w_ref.shape[0]
    o_ref[:n, :] = w_ref[...].astype(BF16)
    p = ROPE_PAIRS
    for dst, src in ((0, p), (p, 0), (2 * p, 3 * p), (3 * p, 2 * p)):
        o_ref[n + dst:n + dst + p, :] = w_ref[n - QK_ROPE + src:n - QK_ROPE + src + p, :].astype(BF16)


def _winprep(w_in_t):
    n, k = w_in_t.shape
    tk = 512
    return pl.pallas_call(
        _winprep_kernel,
        out_shape=jax.ShapeDtypeStruct((IN_COLS, k), BF16),
        grid=(k // tk,),
        in_specs=[pl.BlockSpec((n, tk), lambda i: (0, i))],
        out_specs=pl.BlockSpec((IN_COLS, tk), lambda i: (0, i)),
        compiler_params=_cparams(("arbitrary",)),
        name="winprep",
    )(w_in_t)


def _emit_kv(ckv, kraw, cs, w_uk_ref, w_uvt_ref, gk_ref, k_ref, vt_ref):
    cb = ckv.astype(BF16)
    kn_all = jnp.dot(cb, w_uk_ref[...], preferred_element_type=F32)
    vt = lax.dot_general(w_uvt_ref[...], cb, (((1,), (1,)), ((), ())), preferred_element_type=F32)
    vt_ref[...] = vt.astype(BF16)
    ss_rope = 0.5 * _rowsum(kraw * kraw)
    t = kraw * (cs * gk_ref[1:2, :])
    tt = t + pltpu.roll(t, QK_ROPE, axis=1)
    g_nope = gk_ref[0:1, :]
    for h in range(N_HEADS):
        kn = kn_all[:, h * QK_NOPE:(h + 1) * QK_NOPE]
        r = lax.rsqrt((_rowsum(kn * kn) + ss_rope) * (1.0 / QK_DIM) + EPS)
        k_ref[:, h * HEAD_W:h * HEAD_W + QK_NOPE] = (kn * r * g_nope).astype(BF16)
        k_ref[:, h * HEAD_W + QK_NOPE:(h + 1) * HEAD_W] = (tt * r).astype(BF16)


def _kvcache_kernel(ckv_ref, kraw_ref, cs_ref, w_uk_ref, w_uvt_ref, gk_ref, k_ref, vt_ref):
    _emit_kv(ckv_ref[...], kraw_ref[...], cs_ref[...], w_uk_ref, w_uvt_ref, gk_ref, k_ref, vt_ref)


def _kvcache(ckv, kraw, cs_id, w_uk_b, w_uvt_b, gk):
    n = ckv.shape[0]
    tm = 256
    return pl.pallas_call(
        _kvcache_kernel,
        out_shape=(jax.ShapeDtypeStruct((n, N_HEADS * HEAD_W), BF16),
                   jax.ShapeDtypeStruct((N_HEADS * V_DIM, n), BF16)),
        grid=(n // tm,),
        in_specs=[pl.BlockSpec((tm, KV_LORA), lambda i: (i, 0)),
                  pl.BlockSpec((tm, LANES), lambda i: (i, 0)),
                  pl.BlockSpec((tm, LANES), lambda i: (0, 0)),
                  _const_spec((KV_LORA, N_HEADS * QK_NOPE)),
                  _const_spec((N_HEADS * V_DIM, KV_LORA)),
                  _const_spec((2, LANES))],
        out_specs=(pl.BlockSpec((tm, N_HEADS * HEAD_W), lambda i: (i, 0)),
                   pl.BlockSpec((N_HEADS * V_DIM, tm), lambda i: (0, i))),
        compiler_params=_cparams(("arbitrary",)),
        name="kvcache",
    )(ckv, kraw, cs_id, w_uk_b, w_uvt_b, gk)


def _inproj_kernel(x_ref, mod_ref, n1g_ref, w_in_ref, qlg_ref, w_uq_ref, kvg_ref, w_uk_ref, w_uvt_ref,
                   gq_ref, gk_ref, cs_ref, bg_ref, cu_ref, q_ref, k_ref, vt_ref, *cache_refs):
    x = x_ref[...]
    mod = mod_ref[0]
    h = _rms(x) * n1g_ref[...] * (1.0 + mod[1:2, :]) + mod[0:1, :]
    hb = h.astype(BF16)

    def proj(a, b):
        return lax.dot_general(hb, w_in_ref[a:b, :], (((1,), (1,)), ((), ())), preferred_element_type=F32)

    c1, c2, c3 = CONV_WIDTH, 2 * CONV_WIDTH, 3 * CONV_WIDTH
    bg_ref[...] = proj(0, c1).astype(BF16)
    cu_ref[...] = (proj(c1, c2) * proj(c2, c3)).astype(BF16)
    q_lat = proj(c3, c3 + Q_LORA)
    kvk = proj(c3 + Q_LORA, IN_COLS)
    cs = cs_ref[...]

    qn = _rms(q_lat) * qlg_ref[...]
    q = jnp.dot(qn.astype(BF16), w_uq_ref[...], preferred_element_type=F32)
    scale = QK_DIM ** -0.5 * LOG2_E
    g_nope = gq_ref[0:1, :] * scale
    tq = cs * (gq_ref[1:2, :] * scale)
    for hd in range(N_HEADS):
        lo = q[:, hd * HEAD_W:hd * HEAD_W + QK_NOPE]
        up = q[:, hd * HEAD_W + QK_NOPE:(hd + 1) * HEAD_W]
        ss = _rowsum(lo * lo) + 0.5 * _rowsum(up * up)
        r = lax.rsqrt(ss * (1.0 / QK_DIM) + EPS)
        q_ref[:, hd * HEAD_W:hd * HEAD_W + QK_NOPE] = (lo * r * g_nope).astype(BF16)
        q_ref[:, hd * HEAD_W + QK_NOPE:(hd + 1) * HEAD_W] = (up * r * tq).astype(BF16)

    kv_lat = kvk[:, :KV_LORA]
    kraw = kvk[:, KV_LORA:]
    ckv = _rms(kv_lat) * kvg_ref[...]
    if cache_refs:
        ckv_out_ref, kr_out_ref = cache_refs
        ckv_out_ref[...] = ckv
        kr_out_ref[...] = kraw[:, :QK_ROPE]
    _emit_kv(ckv, kraw, cs, w_uk_ref, w_uvt_ref, gk_ref, k_ref, vt_ref)


def _inproj(x2d, mod3, n1g, w_in_b, qlg, w_uq_b, kvg, w_uk_b, w_uvt_b, gq, gk, cs, *,
            cond_base, tiles_per_cond, cs_tiles, emit_cache):
    t = x2d.shape[0]
    tm = TM_IN
    out_shape = [jax.ShapeDtypeStruct((t, CONV_WIDTH), BF16),
                 jax.ShapeDtypeStruct((t, CONV_WIDTH), BF16),
                 jax.ShapeDtypeStruct((t, N_HEADS * HEAD_W), BF16),
                 jax.ShapeDtypeStruct((t, N_HEADS * HEAD_W), BF16),
                 jax.ShapeDtypeStruct((N_HEADS * V_DIM, t), BF16)]
    out_specs = [pl.BlockSpec((tm, CONV_WIDTH), lambda i: (i, 0)),
                 pl.BlockSpec((tm, CONV_WIDTH), lambda i: (i, 0)),
                 pl.BlockSpec((tm, N_HEADS * HEAD_W), lambda i: (i, 0)),
                 pl.BlockSpec((tm, N_HEADS * HEAD_W), lambda i: (i, 0)),
                 pl.BlockSpec((N_HEADS * V_DIM, tm), lambda i: (0, i))]
    if emit_cache:
        out_shape += [jax.ShapeDtypeStruct((t, KV_LORA), F32), jax.ShapeDtypeStruct((t, QK_ROPE), F32)]
        out_specs += [pl.BlockSpec((tm, KV_LORA), lambda i: (i, 0)),
                      pl.BlockSpec((tm, QK_ROPE), lambda i: (i, 0))]
    return pl.pallas_call(
        _inproj_kernel,
        out_shape=tuple(out_shape),
        grid=(t // tm,),
        in_specs=[pl.BlockSpec((tm, D_MODEL), lambda i: (i, 0)),
                  pl.BlockSpec((1, N_MOD, D_MODEL), lambda i: (cond_base + i // tiles_per_cond, 0, 0)),
                  _const_spec((1, D_MODEL)),
                  _const_spec((IN_COLS, D_MODEL)),
                  _const_spec((1, Q_LORA)),
                  _const_spec((Q_LORA, N_HEADS * HEAD_W)),
                  _const_spec((1, KV_LORA)),
                  _const_spec((KV_LORA, N_HEADS * QK_NOPE)),
                  _const_spec((N_HEADS * V_DIM, KV_LORA)),
                  _const_spec((2, LANES)),
                  _const_spec((2, LANES)),
                  pl.BlockSpec((tm, LANES), lambda i: (i % cs_tiles, 0))],
        out_specs=tuple(out_specs),
        compiler_params=_cparams(("arbitrary",)),
        name="inproj_ctx" if emit_cache else "inproj_lat",
    )(x2d, mod3, n1g, w_in_b, qlg, w_uq_b, kvg, w_uk_b, w_uvt_b, gq, gk, cs)


def _attn_kernel(*refs, n_kv, heads):
    q_ref = refs[0]
    o_ref = refs[-1]
    chunks = []
    for j in range(n_kv):
        sk = refs[1 + 2 * j].shape[1]
        step = min(sk, ATTN_CHUNK)
        chunks += [(j, lo, lo + step) for lo in range(0, sk, step)]
    for h in range(heads):
        q = q_ref[0, :, h * HEAD_W:(h + 1) * HEAD_W]
        def score(c):
            j, lo, hi = chunks[c]
            k = refs[1 + 2 * j][0, lo:hi, h * HEAD_W:(h + 1) * HEAD_W]
            return lax.dot_general(k, q, (((1,), (1,)), ((), ())), preferred_element_type=F32)

        m = None
        acc = None
        ahead = [score(c) for c in range(min(ATTN_AHEAD, len(chunks)))]
        for c, (j, lo, hi) in enumerate(chunks):
            s = ahead.pop(0)
            if c + ATTN_AHEAD < len(chunks):
                ahead.append(score(c + ATTN_AHEAD))
            mc = jnp.max(s, axis=0, keepdims=True)
            m_new = mc if m is None else jnp.maximum(m, mc)
            p = jnp.exp2(s - m_new).astype(BF16)
            vt = refs[2 + 2 * j][h * V_DIM:(h + 1) * V_DIM, lo:hi]
            vt1 = jnp.concatenate([vt, jnp.ones((2 * SUBLANES, hi - lo), BF16)], axis=0)
            part = jnp.dot(vt1, p, preferred_element_type=F32)
            acc = part if acc is None else acc * jnp.exp2(m - m_new) + part
            m = m_new
        ot = acc[:V_DIM, :] / acc[V_DIM:V_DIM + 1, :]
        o_ref[0, :, h * V_DIM:(h + 1) * V_DIM] = ot.T.astype(BF16)


def _attention(q, kvs, *, tq, heads, name):
    b, s, _ = q.shape
    in_specs = [pl.BlockSpec((1, tq, heads * HEAD_W), lambda bi, hi, qi: (bi, qi, hi))]
    args = [q]
    for k, v in kvs:
        sk = k.shape[1]
        in_specs.append(pl.BlockSpec((1, sk, heads * HEAD_W), lambda bi, hi, qi: (bi, 0, hi)))
        in_specs.append(pl.BlockSpec((heads * V_DIM, sk), lambda bi, hi, qi: (hi, bi)))
        args += [k, v]
    return pl.pallas_call(
        functools.partial(_attn_kernel, n_kv=len(kvs), heads=heads),
        out_shape=jax.ShapeDtypeStruct((b, s, N_HEADS * V_DIM), BF16),
        grid=(b, N_HEADS // heads, s // tq),
        in_specs=in_specs,
        out_specs=pl.BlockSpec((1, tq, heads * V_DIM), lambda bi, hi, qi: (bi, qi, hi)),
        compiler_params=_cparams(("arbitrary", "arbitrary", "arbitrary")),
        name=name,
    )(*args)


def _outproj_kernel(x_ref, bg_ref, cu_ref, cup_ref, cun_ref, o_ref, mod_ref, cw_ref, w_out_ref,
                    n2g_ref, wr_ref, x1_ref, h2_ref, rid_ref, rw_ref, *, tm, seq_len):
    i = pl.program_id(0)
    mod = mod_ref[0]
    cu = cu_ref[...].astype(F32)
    prev_row = cup_ref[...].astype(F32)[15:16, :]
    next_row = cun_ref[...].astype(F32)[0:1, :]
    row = lax.broadcasted_iota(jnp.int32, (tm, 1), 0)
    pos = (i * tm + row) & (seq_len - 1)
    up = jnp.where(row == 0, prev_row, pltpu.roll(cu, 1, axis=0))
    up = jnp.where(pos == 0, 0.0, up)
    dn = jnp.where(row == tm - 1, next_row, pltpu.roll(cu, tm - 1, axis=0))
    dn = jnp.where(pos == seq_len - 1, 0.0, dn)
    cw = cw_ref[...]
    y_conv = bg_ref[...].astype(F32) * (up * cw[0:1, :] + cu * cw[1:2, :] + dn * cw[2:3, :])
    mix = jnp.dot(y_conv.astype(BF16), w_out_ref[:CONV_WIDTH, :], preferred_element_type=F32)
    mix = mix + jnp.dot(o_ref[...], w_out_ref[CONV_WIDTH:, :], preferred_element_type=F32)
    x1 = x_ref[...] + mod[2:3, :] * mix
    x1_ref[...] = x1
    h2 = _rms(x1) * n2g_ref[...] * (1.0 + mod[4:5, :]) + mod[3:4, :]
    h2_ref[...] = h2.astype(BF16)

    h2_hi = h2.astype(BF16)
    h2_lo = (h2 - h2_hi.astype(F32)).astype(BF16)
    hh_hl = jnp.dot(h2_hi, wr_ref[...], preferred_element_type=F32)
    lh = jnp.dot(h2_lo, wr_ref[:, :ROUTER_COLS], preferred_element_type=F32)
    logits = hh_hl[:, :ROUTER_COLS] + (hh_hl[:, ROUTER_COLS:] + lh)
    lane = lax.broadcasted_iota(jnp.int32, logits.shape, 1)
    neg = -jnp.inf
    big = jnp.int32(1 << 20)
    gl = jnp.where(lane < N_EXPERT_GROUPS, logits, neg)
    gmax = jnp.max(gl, axis=-1, keepdims=True)
    p_top = 1.0 / _rowsum(jnp.exp(gl - gmax))
    g_top = jnp.min(jnp.where(gl == gmax, lane, big), axis=-1, keepdims=True)
    e_lo = N_EXPERT_GROUPS + EXPERTS_PER_GROUP * g_top
    el = jnp.where((lane >= e_lo) & (lane < e_lo + EXPERTS_PER_GROUP), logits, neg)
    v1 = jnp.max(el, axis=-1, keepdims=True)
    i1 = jnp.min(jnp.where(el == v1, lane, big), axis=-1, keepdims=True)
    el2 = jnp.where(lane == i1, neg, el)
    v2 = jnp.max(el2, axis=-1, keepdims=True)
    i2 = jnp.min(jnp.where(el2 == v2, lane, big), axis=-1, keepdims=True)
    e21 = jnp.exp(v2 - v1)
    w1 = p_top / (1.0 + e21)
    w2 = w1 * e21
    rid_ref[...] = jnp.where(lane == 0, i1 - N_EXPERT_GROUPS, i2 - N_EXPERT_GROUPS)
    rw_ref[...] = jnp.where(lane == 0, w1, w2)


def _outproj(x2d, bg, cu, o2d, mod3, conv_w, w_out_b, n2g, wr, *, cond_base, tiles_per_cond, seq_len):
    t = x2d.shape[0]
    tm = TM_OUT
    hb = tm // 16
    nhb = t // 16
    return pl.pallas_call(
        functools.partial(_outproj_kernel, tm=tm, seq_len=seq_len),
        out_shape=(jax.ShapeDtypeStruct((t, D_MODEL), F32),
                   jax.ShapeDtypeStruct((t, D_MODEL), BF16),
                   jax.ShapeDtypeStruct((t, LANES), jnp.int32),
                   jax.ShapeDtypeStruct((t, LANES), F32)),
        grid=(t // tm,),
        in_specs=[pl.BlockSpec((tm, D_MODEL), lambda i: (i, 0)),
                  pl.BlockSpec((tm, CONV_WIDTH), lambda i: (i, 0)),
                  pl.BlockSpec((tm, CONV_WIDTH), lambda i: (i, 0)),
                  pl.BlockSpec((16, CONV_WIDTH), lambda i: (jnp.maximum(i * hb - 1, 0), 0)),
                  pl.BlockSpec((16, CONV_WIDTH), lambda i: (jnp.minimum((i + 1) * hb, nhb - 1), 0)),
                  pl.BlockSpec((tm, N_HEADS * V_DIM), lambda i: (i, 0)),
                  pl.BlockSpec((1, N_MOD, D_MODEL), lambda i: (cond_base + i // tiles_per_cond, 0, 0)),
                  _const_spec((3, CONV_WIDTH)),
                  _const_spec((D_MODEL, D_MODEL)),
                  _const_spec((1, D_MODEL)),
                  _const_spec((D_MODEL, 2 * ROUTER_COLS))],
        out_specs=(pl.BlockSpec((tm, D_MODEL), lambda i: (i, 0)),
                   pl.BlockSpec((tm, D_MODEL), lambda i: (i, 0)),
                   pl.BlockSpec((tm, LANES), lambda i: (i, 0)),
                   pl.BlockSpec((tm, LANES), lambda i: (i, 0))),
        compiler_params=_cparams(("arbitrary",)),
        name="outproj_ctx" if cond_base == 0 else "outproj_lat",
    )(x2d, bg, cu, cu, cu, o2d, mod3, conv_w, w_out_b, n2g, wr)


def _route_tables(rid, *, tm):
    n_pairs = rid.shape[0]
    nt = n_pairs // tm + N_EXPERTS
    experts = jnp.arange(N_EXPERTS, dtype=jnp.int32)
    onehot = (rid[None, :] == experts[:, None]).astype(jnp.int32)
    csum = jnp.cumsum(onehot, axis=1)
    counts = csum[:, -1]
    tiles_e = (counts + tm - 1) // tm
    tile_end = jnp.cumsum(tiles_e)
    tile_start = tile_end - tiles_e
    n_valid = tile_end[-1]
    pos = jnp.sum(onehot * (csum - 1 + tile_start[:, None] * tm), axis=0)
    tile_raw = jnp.arange(nt, dtype=jnp.int32)
    tile_idx = jnp.minimum(tile_raw, n_valid - 1)
    te = jnp.sum((tile_end[None, :] <= tile_idx[:, None]).astype(jnp.int32), axis=1)
    used = tiles_e > 0
    first = ((tile_raw == tile_start[te]) & (tile_raw < n_valid)).astype(jnp.int32)
    nxt = lax.cummin(jnp.where(used, experts, N_EXPERTS), axis=0, reverse=True)
    nxt = jnp.concatenate([nxt[1:], jnp.full((1,), N_EXPERTS, jnp.int32)])
    next_expert = jnp.where(nxt[te] < N_EXPERTS, nxt[te], -1).astype(jnp.int32)
    wslot = ((jnp.cumsum(used.astype(jnp.int32)) - 1)[te] & 1).astype(jnp.int32)
    return (pos, te, n_valid.reshape(1).astype(jnp.int32),
            (tile_start * tm + counts).astype(jnp.int32), (tiles_e * tm - counts).astype(jnp.int32),
            first, next_expert, wslot)


def _dispatch_kernel(pstart_ref, pcnt_ref, nval_ref, pos0_ref, pos1_ref, h2a_ref, h2b_ref, xs_hbm,
                     buf, zbuf, sem, sem_z, *, tm, nt, n_a):
    i = pl.program_id(0)
    n = pl.num_programs(0)
    slot = i & 1

    def row_copy(r, pos_ref, s):
        return pltpu.make_async_copy(buf.at[s, pl.ds(r, 1)], xs_hbm.at[pl.ds(pos_ref[0, 0, r], 1)], sem.at[s])

    def wait_tile(s):
        for _ in range(2):
            pltpu.make_async_copy(buf.at[s], xs_hbm.at[pl.ds(0, tm)], sem.at[s]).wait()

    def pad_copies(e, fn):
        cnt = pcnt_ref[e]
        start = pstart_ref[e]
        off = start + cnt
        for b in PAD_BITS:
            if b < SUBLANES:
                break
            off = off - (cnt & b)
            dst = pl.multiple_of(off, SUBLANES)

            @pl.when((cnt & b) != 0)
            def _():
                fn(pltpu.make_async_copy(zbuf.at[pl.ds(0, b)], xs_hbm.at[pl.ds(dst, b)], sem_z))
        for j in range(SUBLANES - 1):
            @pl.when(j < (cnt & (SUBLANES - 1)))
            def _():
                fn(pltpu.make_async_copy(zbuf.at[pl.ds(0, 1)], xs_hbm.at[pl.ds(start + j, 1)], sem_z))

    def tail_copy(j):
        return pltpu.make_async_copy(zbuf, xs_hbm.at[pl.ds(pl.multiple_of(j * tm, tm), tm)], sem_z)

    @pl.when(i == 0)
    def _():
        zbuf[...] = jnp.zeros((tm, D_MODEL), F32)
        lax.fori_loop(0, N_EXPERTS, lambda e, c: (pad_copies(e, lambda d: d.start()), c)[1], 0)
        lax.fori_loop(nval_ref[0], nt, lambda j, c: (tail_copy(j).start(), c)[1], 0)
        lax.fori_loop(0, N_EXPERTS, lambda e, c: (pad_copies(e, lambda d: d.wait()), c)[1], 0)
        lax.fori_loop(nval_ref[0], nt, lambda j, c: (tail_copy(j).wait(), c)[1], 0)

    @pl.when(i >= 2)
    def _():
        wait_tile(slot)

    @pl.when(i < n_a)
    def _():
        buf[slot] = h2a_ref[...].astype(F32)

    @pl.when(i >= n_a)
    def _():
        buf[slot] = h2b_ref[...].astype(F32)

    for r in range(tm):
        row_copy(r, pos0_ref, slot).start()
        row_copy(r, pos1_ref, slot).start()

    @pl.when(i == n - 1)
    def _():
        wait_tile(slot)
        wait_tile(1 - slot)


def _dispatch(pad_start, pad_cnt, n_valid, pos3, h2_a, h2_b, *, nt):
    tm = TM_ROW
    n_a, n_b = h2_a.shape[0] // tm, h2_b.shape[0] // tm
    smem_tile = functools.partial(pl.BlockSpec, (1, 1, tm), memory_space=pltpu.SMEM)
    grid_spec = pltpu.PrefetchScalarGridSpec(
        num_scalar_prefetch=3,
        grid=(n_a + n_b,),
        in_specs=[smem_tile(lambda i, *_: (i, 0, 0)),
                  smem_tile(lambda i, *_: (n_a + n_b + i, 0, 0)),
                  pl.BlockSpec((tm, D_MODEL), lambda i, *_: (jnp.minimum(i, n_a - 1), 0)),
                  pl.BlockSpec((tm, D_MODEL), lambda i, *_: (jnp.maximum(i - n_a, 0), 0))],
        out_specs=pl.BlockSpec(memory_space=pl.ANY),
        scratch_shapes=[pltpu.VMEM((2, tm, D_MODEL), F32),
                        pltpu.VMEM((tm, D_MODEL), F32),
                        pltpu.SemaphoreType.DMA((2,)),
                        pltpu.SemaphoreType.DMA])
    return pl.pallas_call(
        functools.partial(_dispatch_kernel, tm=tm, nt=nt, n_a=n_a),
        out_shape=jax.ShapeDtypeStruct((nt * TM_MOE, D_MODEL), F32),
        grid_spec=grid_spec,
        compiler_params=pltpu.CompilerParams(dimension_semantics=("arbitrary",),
                                             vmem_limit_bytes=VMEM_LIMIT, has_side_effects=True),
        name="dispatch",
    )(pad_start, pad_cnt, n_valid, pos3, pos3, h2_a, h2_b)


def _moe_kernel(texp_ref, nval_ref, first_ref, next_ref, wslot_ref, xs_ref, wg_hbm, wu_hbm, wd_hbm, y_ref,
                wg32, wu32, wd32, wgb, wub, wdb, sem_w):
    i = pl.program_id(0)

    def weight_copies(e, s):
        return (pltpu.make_async_copy(wg_hbm.at[e], wg32.at[s], sem_w.at[s]),
                pltpu.make_async_copy(wu_hbm.at[e], wu32.at[s], sem_w.at[s]),
                pltpu.make_async_copy(wd_hbm.at[e], wd32.at[s], sem_w.at[s]))

    @pl.when(i == 0)
    def _():
        for d in weight_copies(texp_ref[0], 0):
            d.start()

    @pl.when(i < nval_ref[0])
    def _():
        @pl.when(first_ref[i] == 1)
        def _():
            s = wslot_ref[i]
            for d in weight_copies(texp_ref[i], s):
                d.wait()
            e_next = next_ref[i]

            @pl.when(e_next >= 0)
            def _():
                for d in weight_copies(e_next, 1 - s):
                    d.start()
            wgb[...] = wg32[s].astype(BF16)
            wub[...] = wu32[s].astype(BF16)
            wdb[...] = wd32[s].astype(BF16)

        x = xs_ref[...].astype(BF16)
        g = jnp.dot(x, wgb[...], preferred_element_type=F32)
        u = jnp.dot(x, wub[...], preferred_element_type=F32)
        a = g / (1.0 + jnp.exp(-g)) * u
        y_ref[...] = jnp.dot(a.astype(BF16), wdb[...], preferred_element_type=F32)

    @pl.when(i >= nval_ref[0])
    def _():
        y_ref[...] = jnp.zeros(y_ref.shape, F32)


def _moe(tile_expert, n_valid, first, next_expert, wslot, xs, w_gate, w_up, w_down):
    nt = tile_expert.shape[0]
    tm = TM_MOE
    grid_spec = pltpu.PrefetchScalarGridSpec(
        num_scalar_prefetch=5,
        grid=(nt,),
        in_specs=[pl.BlockSpec((tm, D_MODEL), lambda i, te, nv, *_: (jnp.minimum(i, nv[0] - 1), 0)),
                  pl.BlockSpec(memory_space=pl.ANY),
                  pl.BlockSpec(memory_space=pl.ANY),
                  pl.BlockSpec(memory_space=pl.ANY)],
        out_specs=pl.BlockSpec((tm, D_MODEL), lambda i, *_: (i, 0)),
        scratch_shapes=[pltpu.VMEM((2, D_MODEL, D_EXPERT), F32),
                        pltpu.VMEM((2, D_MODEL, D_EXPERT), F32),
                        pltpu.VMEM((2, D_EXPERT, D_MODEL), F32),
                        pltpu.VMEM((D_MODEL, D_EXPERT), BF16),
                        pltpu.VMEM((D_MODEL, D_EXPERT), BF16),
                        pltpu.VMEM((D_EXPERT, D_MODEL), BF16),
                        pltpu.SemaphoreType.DMA((2,))])
    return pl.pallas_call(
        _moe_kernel,
        out_shape=jax.ShapeDtypeStruct((nt * tm, D_MODEL), F32),
        grid_spec=grid_spec,
        compiler_params=_cparams(("arbitrary",)),
        name="moe",
    )(tile_expert, n_valid, first, next_expert, wslot, xs, w_gate, w_up, w_down)


def _final_kernel(pa0_ref, pb0_ref, pa_ref, pb_ref, x1_ref, rw_ref, mod_ref, y_hbm, o_ref, ybuf, sem, *, tm):
    i = pl.program_id(0)
    n = pl.num_programs(0)
    slot = i & 1

    def start_tile(pa, pb, s):
        for r in range(tm):
            pltpu.make_async_copy(y_hbm.at[pl.ds(pa[0, 0, r], 1)], ybuf.at[s, 0, pl.ds(r, 1)], sem.at[s]).start()
            pltpu.make_async_copy(y_hbm.at[pl.ds(pb[0, 0, r], 1)], ybuf.at[s, 1, pl.ds(r, 1)], sem.at[s]).start()

    @pl.when(i == 0)
    def _():
        start_tile(pa0_ref, pb0_ref, 0)

    @pl.when(i + 1 < n)
    def _():
        start_tile(pa_ref, pb_ref, 1 - slot)

    for k in range(2):
        pltpu.make_async_copy(y_hbm.at[pl.ds(0, tm)], ybuf.at[slot, k], sem.at[slot]).wait()
    w = rw_ref[...]
    moe = w[:, 0:1] * ybuf[slot, 0] + w[:, 1:2] * ybuf[slot, 1]
    o_ref[...] = x1_ref[...] + mod_ref[0][5:6, :] * moe


def _final(x1, rw, y, pos3, mod3, *, tile_base, slot_tiles, cond_base, tiles_per_cond):
    t = x1.shape[0]
    tm = TM_ROW
    n = t // tm
    smem_tile = functools.partial(pl.BlockSpec, (1, 1, tm), memory_space=pltpu.SMEM)
    return pl.pallas_call(
        functools.partial(_final_kernel, tm=tm),
        out_shape=jax.ShapeDtypeStruct((t, D_MODEL), F32),
        grid=(n,),
        in_specs=[smem_tile(lambda i: (tile_base, 0, 0)),
                  smem_tile(lambda i: (slot_tiles + tile_base, 0, 0)),
                  smem_tile(lambda i: (tile_base + jnp.minimum(i + 1, n - 1), 0, 0)),
                  smem_tile(lambda i: (slot_tiles + tile_base + jnp.minimum(i + 1, n - 1), 0, 0)),
                  pl.BlockSpec((tm, D_MODEL), lambda i: (i, 0)),
                  pl.BlockSpec((tm, LANES), lambda i: (i, 0)),
                  pl.BlockSpec((1, N_MOD, D_MODEL), lambda i: (cond_base + i // tiles_per_cond, 0, 0)),
                  pl.BlockSpec(memory_space=pl.ANY)],
        out_specs=pl.BlockSpec((tm, D_MODEL), lambda i: (i, 0)),
        scratch_shapes=[pltpu.VMEM((2, 2, tm, D_MODEL), F32), pltpu.SemaphoreType.DMA((2,))],
        compiler_params=_cparams(("arbitrary",)),
        name="final_ctx" if cond_base == 0 else "final_lat",
    )(pos3, pos3, pos3, pos3, x1, rw, mod3, y)


def _rope_table(n_tokens):
    rows = n_tokens // GRID_W
    inv = ROPE_BASE ** (-jnp.arange(ROPE_PAIRS, dtype=F32) / ROPE_PAIRS)
    row_ang = jnp.arange(rows, dtype=F32)[:, None] * inv
    col_ang = jnp.arange(GRID_W, dtype=F32)[:, None] * inv
    cr, sr = (jnp.repeat(f(row_ang), GRID_W, axis=0) for f in (jnp.cos, jnp.sin))
    cc, sc = (jnp.tile(f(col_ang), (rows, 1)) for f in (jnp.cos, jnp.sin))
    return jnp.concatenate([cr, cr, cc, cc, -sr, sr, -sc, sc], axis=-1)


def _swap_halves(a):
    p = ROPE_PAIRS
    return jnp.concatenate([a[..., p:2 * p], a[..., :p], a[..., 3 * p:], a[..., 2 * p:3 * p]], axis=-1)


def _head_gains(g):
    rope = g[QK_NOPE:]
    return jnp.stack([g[:QK_NOPE], jnp.concatenate([rope, _swap_halves(rope)])])


def kernel(x_prompt, x_sample, cache_ckv, cache_krope, c, c_ctx, ada_w, ada_b, norm1_g, w_in, conv_w,
           q_lora_g, w_uq, kv_lora_g, w_ukv, q_head_g, k_head_g, w_out, norm2_g, router_g, router_e,
           w_gate, w_up, w_down):
    depth = ada_w.shape[0]
    assert depth == 1
    bp, sp, _ = x_prompt.shape
    bs, ss, _ = x_sample.shape
    past = cache_ckv.shape[2]
    tp, ts = bp * sp, bs * ss
    assert sp & (sp - 1) == 0 and ss & (ss - 1) == 0
    assert tp % TM_IN == 0 and ss % TM_IN == 0 and TM_IN % sp == 0 and ss % TQ == 0
    assert bs + 1 <= 8 and past % 256 == 0 and TM_ROW == TM_MOE and tp % TM_ROW == 0 and ts % TM_ROW == 0

    w_in_b = _winprep(w_in[0].T)
    uq = w_uq[0].reshape(Q_LORA, N_HEADS, QK_DIM)
    w_uq_b = jnp.concatenate([uq, _swap_halves(uq[..., QK_NOPE:])], axis=-1)
    w_uq_b = w_uq_b.reshape(Q_LORA, N_HEADS * HEAD_W).astype(BF16)
    ukv = w_ukv[0].reshape(KV_LORA, N_HEADS, QK_NOPE + V_DIM)
    w_uk_b = ukv[..., :QK_NOPE].reshape(KV_LORA, N_HEADS * QK_NOPE).astype(BF16)
    w_uvt_b = ukv[..., QK_NOPE:].reshape(KV_LORA, N_HEADS * V_DIM).T.astype(BF16)
    w_out_b = w_out[0].astype(BF16)
    wr = jnp.concatenate([router_g[0], router_e[0],
                          jnp.zeros((D_MODEL, ROUTER_COLS - N_EXPERT_GROUPS - N_EXPERTS), F32)], axis=1)
    wr_hi = lax.bitcast_convert_type(lax.bitcast_convert_type(wr, jnp.uint32) & jnp.uint32(0xFFFF0000), F32)
    wr = jnp.concatenate([wr_hi, wr - wr_hi], axis=1).astype(BF16)
    gq, gk = _head_gains(q_head_g[0]), _head_gains(k_head_g[0])
    n1g, n2g = norm1_g[0].reshape(1, D_MODEL), norm2_g[0].reshape(1, D_MODEL)
    qlg, kvg = q_lora_g[0].reshape(1, Q_LORA), kv_lora_g[0].reshape(1, KV_LORA)
    cs_lat = _rope_table(ss)
    cs_id = jnp.concatenate([jnp.ones((TM_IN, QK_ROPE), F32), jnp.zeros((TM_IN, QK_ROPE), F32)], axis=1)

    cond8 = jnp.concatenate([c_ctx[None, :], c, jnp.zeros((8 - 1 - bs, D_MODEL), F32)], axis=0)
    mod3 = _modulation(cond8, ada_w[0], ada_b[0]).reshape(8, N_MOD, D_MODEL)

    xp2, xs2 = x_prompt.reshape(tp, D_MODEL), x_sample.reshape(ts, D_MODEL)
    big = 1 << 30

    bg_p, cu_p, q_p, k_p, vt_p, ckv_p, kr_p = _inproj(
        xp2, mod3, n1g, w_in_b, qlg, w_uq_b, kvg, w_uk_b, w_uvt_b, gq, gk, cs_id,
        cond_base=0, tiles_per_cond=big, cs_tiles=1, emit_cache=True)
    o_p = _attention(q_p.reshape(bp, sp, -1), [(k_p.reshape(bp, sp, -1), vt_p)],
                     tq=sp, heads=N_HEADS, name="attn_ctx")
    x1_p, h2_p, rid_p, rw_p = _outproj(
        xp2, bg_p, cu_p, o_p.reshape(tp, -1), mod3, conv_w[0], w_out_b, n2g, wr,
        cond_base=0, tiles_per_cond=big, seq_len=sp)

    kr_c = cache_krope[:, 0].reshape(bs * past, QK_ROPE)
    k_c, vt_c = _kvcache(cache_ckv[:, 0].reshape(bs * past, KV_LORA), jnp.concatenate([kr_c, kr_c], axis=1),
                         cs_id, w_uk_b, w_uvt_b, gk)
    bg_s, cu_s, q_s, k_s, vt_s = _inproj(
        xs2, mod3, n1g, w_in_b, qlg, w_uq_b, kvg, w_uk_b, w_uvt_b, gq, gk, cs_lat,
        cond_base=1, tiles_per_cond=ss // TM_IN, cs_tiles=ss // TM_IN, emit_cache=False)
    o_s = _attention(q_s.reshape(bs, ss, -1),
                     [(k_c.reshape(bs, past, -1), vt_c), (k_s.reshape(bs, ss, -1), vt_s)],
                     tq=TQ, heads=1, name="attn_lat")
    x1_s, h2_s, rid_s, rw_s = _outproj(
        xs2, bg_s, cu_s, o_s.reshape(ts, -1), mod3, conv_w[0], w_out_b, n2g, wr,
        cond_base=1, tiles_per_cond=ss // TM_OUT, seq_len=ss)

    n_tok = tp + ts
    slot_tiles = n_tok // TM_ROW
    nt = 2 * n_tok // TM_MOE + N_EXPERTS
    rid = jnp.concatenate([rid_p[:, :2], rid_s[:, :2]], axis=0).T.reshape(2 * n_tok)
    pos, te, n_valid, pad_start, pad_cnt, first, next_expert, wslot = _route_tables(rid, tm=TM_MOE)
    pos3 = pos.reshape(2 * slot_tiles, 1, TM_ROW)
    xs = _dispatch(pad_start, pad_cnt, n_valid, pos3, h2_p, h2_s, nt=nt)
    y = _moe(te, n_valid, first, next_expert, wslot, xs, w_gate[0], w_up[0], w_down[0])

    y_p = _final(x1_p, rw_p, y, pos3, mod3, tile_base=0, slot_tiles=slot_tiles, cond_base=0, tiles_per_cond=big)
    y_s = _final(x1_s, rw_s, y, pos3, mod3, tile_base=tp // TM_ROW, slot_tiles=slot_tiles, cond_base=1,
                 tiles_per_cond=ss // TM_ROW)

    return (y_p.reshape(bp, sp, D_MODEL), y_s.reshape(bs, ss, D_MODEL),
            ckv_p.reshape(bp, 1, sp, KV_LORA), kr_p.reshape(bp, 1, sp, QK_ROPE))
```

```python
import functools

import jax
import jax.numpy as jnp
from jax import lax
from jax.experimental import pallas as pl
from jax.experimental.pallas import tpu as pltpu

F32 = jnp.float32
BF16 = jnp.bfloat16
HIGHEST = lax.Precision.HIGHEST

D_MODEL = 2048
CONV_WIDTH = 1024
N_HEADS = 8
QK_NOPE = 128
QK_ROPE = 64
V_DIM = 128
QK_DIM = QK_NOPE + QK_ROPE
Q_LORA = 512
KV_LORA = 256
GRID_W = 64
ROPE_PAIRS = QK_ROPE // 4
ROPE_BASE = 10000.0
N_EXPERT_GROUPS = 4
EXPERTS_PER_GROUP = 8
N_EXPERTS = N_EXPERT_GROUPS * EXPERTS_PER_GROUP
D_EXPERT = 512
N_MOD = 6
EPS = 1e-6
LOG2_E = 1.4426950408889634

HEAD_W = 2 * QK_NOPE
LANES = 128
SUBLANES = 8
IN_COLS = 3 * CONV_WIDTH + Q_LORA + KV_LORA + 2 * QK_ROPE
ROUTER_COLS = LANES
VMEM_LIMIT = 56 * 1024 * 1024

TM_IN = 512
TM_OUT = 512
OUT_SLABS = 2
TQ = 512
ATTN_CHUNK = 512
ATTN_AHEAD = 2
TM_MOE = 256
TM_ROW = 256
BN_MOD = 1024
PAD_BITS = tuple(1 << b for b in reversed(range(TM_MOE.bit_length() - 1)))


def _cparams(sem):
    return pltpu.CompilerParams(dimension_semantics=sem, vmem_limit_bytes=VMEM_LIMIT)


def _const_spec(shape):
    nd = len(shape)
    return pl.BlockSpec(shape, lambda *_: (0,) * nd, pipeline_mode=pl.Buffered(1))


def _rms(x):
    return x * lax.rsqrt(jnp.mean(x * x, axis=-1, keepdims=True) + EPS)


def _rowsum(x):
    return jnp.sum(x, axis=-1, keepdims=True)


def _mod_kernel(c_ref, w_ref, b_ref, o_ref):
    c = c_ref[...]
    s = c / (1.0 + jnp.exp(-c))
    s_hi = s.astype(BF16)
    s_lo = (s - s_hi.astype(F32)).astype(BF16)
    w = w_ref[...]
    w_hi = w.astype(BF16)
    w_lo = (w - w_hi.astype(F32)).astype(BF16)
    rows = s.shape[0]
    a = jnp.dot(jnp.concatenate([s_hi, s_lo], axis=0), w_hi, preferred_element_type=F32)
    b = jnp.dot(s_hi, w_lo, preferred_element_type=F32)
    o_ref[...] = a[:rows] + (a[rows:] + b) + b_ref[...]


def _modulation(cond8, ada_w, ada_b):
    n = ada_w.shape[1]
    return pl.pallas_call(
        _mod_kernel,
        out_shape=jax.ShapeDtypeStruct((8, n), F32),
        grid=(n // BN_MOD,),
        in_specs=[pl.BlockSpec((8, D_MODEL), lambda j: (0, 0)),
                  pl.BlockSpec((D_MODEL, BN_MOD), lambda j: (0, j)),
                  pl.BlockSpec((1, BN_MOD), lambda j: (0, j))],
        out_specs=pl.BlockSpec((8, BN_MOD), lambda j: (0, j)),
        compiler_params=_cparams(("arbitrary",)),
        name="mod",
    )(cond8, ada_w, ada_b.reshape(1, n))


def _winprep_kernel(w_ref, o_ref):
    n = w_ref.shape[0]
    o_ref[:n, :] = w_ref[...].astype(BF16)
    p = ROPE_PAIRS
    for dst, src in ((0, p), (p, 0), (2 * p, 3 * p), (3 * p, 2 * p)):
        o_ref[n + dst:n + dst + p, :] = w_ref[n - QK_ROPE + src:n - QK_ROPE + src + p, :].astype(BF16)


def _winprep(w_in_t):
    n, k = w_in_t.shape
    tk = 512
    return pl.pallas_call(
        _winprep_kernel,
        out_shape=jax.ShapeDtypeStruct((IN_COLS, k), BF16),
        grid=(k // tk,),
        in_specs=[pl.BlockSpec((n, tk), lambda i: (0, i))],
        out_specs=pl.BlockSpec((IN_COLS, tk), lambda i: (0, i)),
        compiler_params=_cparams(("arbitrary",)),
        name="winprep",
    )(w_in_t)


def _emit_kv(ckv, kraw, cs, w_uk_ref, w_uvt_ref, gk_ref, k_ref, vt_ref):
    cb = ckv.astype(BF16)
    kn_all = jnp.dot(cb, w_uk_ref[...], preferred_element_type=F32)
    vt = lax.dot_general(w_uvt_ref[...], cb, (((1,), (1,)), ((), ())), preferred_element_type=F32)
    vt_ref[...] = vt.astype(BF16)
    ss_rope = 0.5 * _rowsum(kraw * kraw)
    t = kraw * (cs * gk_ref[1:2, :])
    tt = t + pltpu.roll(t, QK_ROPE, axis=1)
    g_nope = gk_ref[0:1, :]
    for h in range(N_HEADS):
        kn = kn_all[:, h * QK_NOPE:(h + 1) * QK_NOPE]
        r = lax.rsqrt((_rowsum(kn * kn) + ss_rope) * (1.0 / QK_DIM) + EPS)
        k_ref[:, h * HEAD_W:h * HEAD_W + QK_NOPE] = (kn * r * g_nope).astype(BF16)
        k_ref[:, h * HEAD_W + QK_NOPE:(h + 1) * HEAD_W] = (tt * r).astype(BF16)


def _kvcache_kernel(ckv_ref, kraw_ref, cs_ref, w_uk_ref, w_uvt_ref, gk_ref, k_ref, vt_ref):
    _emit_kv(ckv_ref[...], kraw_ref[...], cs_ref[...], w_uk_ref, w_uvt_ref, gk_ref, k_ref, vt_ref)


def _kvcache(ckv, kraw, cs_id, w_uk_b, w_uvt_b, gk):
    n = ckv.shape[0]
    tm = 256
    return pl.pallas_call(
        _kvcache_kernel,
        out_shape=(jax.ShapeDtypeStruct((n, N_HEADS * HEAD_W), BF16),
                   jax.ShapeDtypeStruct((N_HEADS * V_DIM, n), BF16)),
        grid=(n // tm,),
        in_specs=[pl.BlockSpec((tm, KV_LORA), lambda i: (i, 0)),
                  pl.BlockSpec((tm, LANES), lambda i: (i, 0)),
                  pl.BlockSpec((tm, LANES), lambda i: (0, 0)),
                  _const_spec((KV_LORA, N_HEADS * QK_NOPE)),
                  _const_spec((N_HEADS * V_DIM, KV_LORA)),
                  _const_spec((2, LANES))],
        out_specs=(pl.BlockSpec((tm, N_HEADS * HEAD_W), lambda i: (i, 0)),
                   pl.BlockSpec((N_HEADS * V_DIM, tm), lambda i: (0, i))),
        compiler_params=_cparams(("arbitrary",)),
        name="kvcache",
    )(ckv, kraw, cs_id, w_uk_b, w_uvt_b, gk)


def _inproj_kernel(x_ref, mod_ref, n1g_ref, w_in_ref, qlg_ref, w_uq_ref, kvg_ref, w_uk_ref, w_uvt_ref,
                   gq_ref, gk_ref, cs_ref, bg_ref, cu_ref, q_ref, k_ref, vt_ref, *cache_refs):
    x = x_ref[...]
    mod = mod_ref[0]
    h = _rms(x) * n1g_ref[...] * (1.0 + mod[1:2, :]) + mod[0:1, :]
    hb = h.astype(BF16)

    def proj(a, b):
        return lax.dot_general(hb, w_in_ref[a:b, :], (((1,), (1,)), ((), ())), preferred_element_type=F32)

    c1, c2, c3 = CONV_WIDTH, 2 * CONV_WIDTH, 3 * CONV_WIDTH
    bg_ref[...] = proj(0, c1).astype(BF16)
    cu_ref[...] = (proj(c1, c2) * proj(c2, c3)).astype(BF16)
    q_lat = proj(c3, c3 + Q_LORA)
    kvk = proj(c3 + Q_LORA, IN_COLS)
    cs = cs_ref[...]

    qn = _rms(q_lat) * qlg_ref[...]
    q = jnp.dot(qn.astype(BF16), w_uq_ref[...], preferred_element_type=F32)
    scale = QK_DIM ** -0.5 * LOG2_E
    g_nope = gq_ref[0:1, :] * scale
    tq = cs * (gq_ref[1:2, :] * scale)
    for hd in range(N_HEADS):
        lo = q[:, hd * HEAD_W:hd * HEAD_W + QK_NOPE]
        up = q[:, hd * HEAD_W + QK_NOPE:(hd + 1) * HEAD_W]
        ss = _rowsum(lo * lo) + 0.5 * _rowsum(up * up)
        r = lax.rsqrt(ss * (1.0 / QK_DIM) + EPS)
        q_ref[:, hd * HEAD_W:hd * HEAD_W + QK_NOPE] = (lo * r * g_nope).astype(BF16)
        q_ref[:, hd * HEAD_W + QK_NOPE:(hd + 1) * HEAD_W] = (up * r * tq).astype(BF16)

    kv_lat = kvk[:, :KV_LORA]
    kraw = kvk[:, KV_LORA:]
    ckv = _rms(kv_lat) * kvg_ref[...]
    if cache_refs:
        ckv_out_ref, kr_out_ref = cache_refs
        ckv_out_ref[...] = ckv
        kr_out_ref[...] = kraw[:, :QK_ROPE]
    _emit_kv(ckv, kraw, cs, w_uk_ref, w_uvt_ref, gk_ref, k_ref, vt_ref)


def _inproj(x2d, mod3, n1g, w_in_b, qlg, w_uq_b, kvg, w_uk_b, w_uvt_b, gq, gk, cs, *,
            cond_base, tiles_per_cond, cs_tiles, emit_cache):
    t = x2d.shape[0]
    tm = TM_IN
    out_shape = [jax.ShapeDtypeStruct((t, CONV_WIDTH), BF16),
                 jax.ShapeDtypeStruct((t, CONV_WIDTH), BF16),
                 jax.ShapeDtypeStruct((t, N_HEADS * HEAD_W), BF16),
                 jax.ShapeDtypeStruct((t, N_HEADS * HEAD_W), BF16),
                 jax.ShapeDtypeStruct((N_HEADS * V_DIM, t), BF16)]
    out_specs = [pl.BlockSpec((tm, CONV_WIDTH), lambda i: (i, 0)),
                 pl.BlockSpec((tm, CONV_WIDTH), lambda i: (i, 0)),
                 pl.BlockSpec((tm, N_HEADS * HEAD_W), lambda i: (i, 0)),
                 pl.BlockSpec((tm, N_HEADS * HEAD_W), lambda i: (i, 0)),
                 pl.BlockSpec((N_HEADS * V_DIM, tm), lambda i: (0, i))]
    if emit_cache:
        out_shape += [jax.ShapeDtypeStruct((t, KV_LORA), F32), jax.ShapeDtypeStruct((t, QK_ROPE), F32)]
        out_specs += [pl.BlockSpec((tm, KV_LORA), lambda i: (i, 0)),
                      pl.BlockSpec((tm, QK_ROPE), lambda i: (i, 0))]
    return pl.pallas_call(
        _inproj_kernel,
        out_shape=tuple(out_shape),
        grid=(t // tm,),
        in_specs=[pl.BlockSpec((tm, D_MODEL), lambda i: (i, 0)),
                  pl.BlockSpec((1, N_MOD, D_MODEL), lambda i: (cond_base + i // tiles_per_cond, 0, 0)),
                  _const_spec((1, D_MODEL)),
                  _const_spec((IN_COLS, D_MODEL)),
                  _const_spec((1, Q_LORA)),
                  _const_spec((Q_LORA, N_HEADS * HEAD_W)),
                  _const_spec((1, KV_LORA)),
                  _const_spec((KV_LORA, N_HEADS * QK_NOPE)),
                  _const_spec((N_HEADS * V_DIM, KV_LORA)),
                  _const_spec((2, LANES)),
                  _const_spec((2, LANES)),
                  pl.BlockSpec((tm, LANES), lambda i: (i % cs_tiles, 0))],
        out_specs=tuple(out_specs),
        compiler_params=_cparams(("arbitrary",)),
        name="inproj_ctx" if emit_cache else "inproj_lat",
    )(x2d, mod3, n1g, w_in_b, qlg, w_uq_b, kvg, w_uk_b, w_uvt_b, gq, gk, cs)


def _attn_kernel(*refs, n_kv, heads):
    q_ref = refs[0]
    o_ref = refs[-1]
    chunks = []
    for j in range(n_kv):
        sk = refs[1 + 2 * j].shape[1]
        step = min(sk, ATTN_CHUNK)
        chunks += [(j, lo, lo + step) for lo in range(0, sk, step)]
    q = [q_ref[0, :, h * HEAD_W:(h + 1) * HEAD_W] for h in range(heads)]

    def score(h, c):
        j, lo, hi = chunks[c]
        k = refs[1 + 2 * j][0, lo:hi, h * HEAD_W:(h + 1) * HEAD_W]
        return lax.dot_general(k, q[h], (((1,), (1,)), ((), ())), preferred_element_type=F32)

    m = [None] * heads
    acc = [None] * heads
    ahead = [[score(h, c) for c in range(min(ATTN_AHEAD, len(chunks)))] for h in range(heads)]
    for c, (j, lo, hi) in enumerate(chunks):
        for h in range(heads):
            s = ahead[h].pop(0)
            if c + ATTN_AHEAD < len(chunks):
                ahead[h].append(score(h, c + ATTN_AHEAD))
            mc = jnp.max(s, axis=0, keepdims=True)
            m_new = mc if m[h] is None else jnp.maximum(m[h], mc)
            p = jnp.exp2(s - m_new).astype(BF16)
            vt = refs[2 + 2 * j][h * V_DIM:(h + 1) * V_DIM, lo:hi]
            vt1 = jnp.concatenate([vt, jnp.ones((2 * SUBLANES, hi - lo), BF16)], axis=0)
            part = jnp.dot(vt1, p, preferred_element_type=F32)
            acc[h] = part if acc[h] is None else acc[h] * jnp.exp2(m[h] - m_new) + part
            m[h] = m_new
    for h in range(heads):
        ot = acc[h][:V_DIM, :] / acc[h][V_DIM:V_DIM + 1, :]
        o_ref[0, :, h * V_DIM:(h + 1) * V_DIM] = ot.T.astype(BF16)


def _attention(q, kvs, *, tq, heads, name):
    b, s, _ = q.shape
    in_specs = [pl.BlockSpec((1, tq, heads * HEAD_W), lambda bi, hi, qi: (bi, qi, hi))]
    args = [q]
    for k, v in kvs:
        sk = k.shape[1]
        in_specs.append(pl.BlockSpec((1, sk, heads * HEAD_W), lambda bi, hi, qi: (bi, 0, hi)))
        in_specs.append(pl.BlockSpec((heads * V_DIM, sk), lambda bi, hi, qi: (hi, bi)))
        args += [k, v]
    return pl.pallas_call(
        functools.partial(_attn_kernel, n_kv=len(kvs), heads=heads),
        out_shape=jax.ShapeDtypeStruct((b, s, N_HEADS * V_DIM), BF16),
        grid=(b, N_HEADS // heads, s // tq),
        in_specs=in_specs,
        out_specs=pl.BlockSpec((1, tq, heads * V_DIM), lambda bi, hi, qi: (bi, qi, hi)),
        compiler_params=_cparams(("arbitrary", "arbitrary", "arbitrary")),
        name=name,
    )(*args)


def _outproj_kernel(x_ref, bg_ref, cu_ref, cup_ref, cun_ref, o_ref, mod_ref, cw_ref, w_out_ref,
                    n2g_ref, wr_ref, x1_ref, h2_ref, rid_ref, rw_ref, *, tm, seq_len):
    i = pl.program_id(0)
    mod = mod_ref[0]
    cu = cu_ref[...].astype(F32)
    prev_row = cup_ref[...].astype(F32)[15:16, :]
    next_row = cun_ref[...].astype(F32)[0:1, :]
    row = lax.broadcasted_iota(jnp.int32, (tm, 1), 0)
    pos = (i * tm + row) & (seq_len - 1)
    up = jnp.where(row == 0, prev_row, pltpu.roll(cu, 1, axis=0))
    up = jnp.where(pos == 0, 0.0, up)
    dn = jnp.where(row == tm - 1, next_row, pltpu.roll(cu, tm - 1, axis=0))
    dn = jnp.where(pos == seq_len - 1, 0.0, dn)
    cw = cw_ref[...]
    y_conv = bg_ref[...].astype(F32) * (up * cw[0:1, :] + cu * cw[1:2, :] + dn * cw[2:3, :])
    y_conv = y_conv.astype(BF16)

    slabs = [(r, r + tm // OUT_SLABS) for r in range(0, tm, tm // OUT_SLABS)]
    mixes = []
    for lo, hi in slabs:
        mix = jnp.dot(y_conv[lo:hi], w_out_ref[:CONV_WIDTH, :], preferred_element_type=F32)
        mixes.append(mix + jnp.dot(o_ref[lo:hi, :], w_out_ref[CONV_WIDTH:, :], preferred_element_type=F32))
    for (lo, hi), mix in zip(slabs, mixes):
        x1 = x_ref[lo:hi, :] + mod[2:3, :] * mix
        x1_ref[lo:hi, :] = x1
        h2 = _rms(x1) * n2g_ref[...] * (1.0 + mod[4:5, :]) + mod[3:4, :]
        h2_hi = h2.astype(BF16)
        h2_ref[lo:hi, :] = h2_hi

        h2_lo = (h2 - h2_hi.astype(F32)).astype(BF16)
        hh_hl = jnp.dot(h2_hi, wr_ref[...], preferred_element_type=F32)
        lh = jnp.dot(h2_lo, wr_ref[:, :ROUTER_COLS], preferred_element_type=F32)
        logits = hh_hl[:, :ROUTER_COLS] + (hh_hl[:, ROUTER_COLS:] + lh)
        lane = lax.broadcasted_iota(jnp.int32, logits.shape, 1)
        neg = -jnp.inf
        big = jnp.int32(1 << 20)
        gl = jnp.where(lane < N_EXPERT_GROUPS, logits, neg)
        gmax = jnp.max(gl, axis=-1, keepdims=True)
        p_top = 1.0 / _rowsum(jnp.exp(gl - gmax))
        g_top = jnp.min(jnp.where(gl == gmax, lane, big), axis=-1, keepdims=True)
        e_lo = N_EXPERT_GROUPS + EXPERTS_PER_GROUP * g_top
        el = jnp.where((lane >= e_lo) & (lane < e_lo + EXPERTS_PER_GROUP), logits, neg)
        v1 = jnp.max(el, axis=-1, keepdims=True)
        i1 = jnp.min(jnp.where(el == v1, lane, big), axis=-1, keepdims=True)
        el2 = jnp.where(lane == i1, neg, el)
        v2 = jnp.max(el2, axis=-1, keepdims=True)
        i2 = jnp.min(jnp.where(el2 == v2, lane, big), axis=-1, keepdims=True)
        e21 = jnp.exp(v2 - v1)
        w1 = p_top / (1.0 + e21)
        w2 = w1 * e21
        rid_ref[lo:hi, :] = jnp.where(lane == 0, i1 - N_EXPERT_GROUPS, i2 - N_EXPERT_GROUPS)
        rw_ref[lo:hi, :] = jnp.where(lane == 0, w1, w2)


def _outproj(x2d, bg, cu, o2d, mod3, conv_w, w_out_b, n2g, wr, *, cond_base, tiles_per_cond, seq_len):
    t = x2d.shape[0]
    tm = TM_OUT
    hb = tm // 16
    nhb = t // 16
    return pl.pallas_call(
        functools.partial(_outproj_kernel, tm=tm, seq_len=seq_len),
        out_shape=(jax.ShapeDtypeStruct((t, D_MODEL), F32),
                   jax.ShapeDtypeStruct((t, D_MODEL), BF16),
                   jax.ShapeDtypeStruct((t, LANES), jnp.int32),
                   jax.ShapeDtypeStruct((t, LANES), F32)),
        grid=(t // tm,),
        in_specs=[pl.BlockSpec((tm, D_MODEL), lambda i: (i, 0)),
                  pl.BlockSpec((tm, CONV_WIDTH), lambda i: (i, 0)),
                  pl.BlockSpec((tm, CONV_WIDTH), lambda i: (i, 0)),
                  pl.BlockSpec((16, CONV_WIDTH), lambda i: (jnp.maximum(i * hb - 1, 0), 0)),
                  pl.BlockSpec((16, CONV_WIDTH), lambda i: (jnp.minimum((i + 1) * hb, nhb - 1), 0)),
                  pl.BlockSpec((tm, N_HEADS * V_DIM), lambda i: (i, 0)),
                  pl.BlockSpec((1, N_MOD, D_MODEL), lambda i: (cond_base + i // tiles_per_cond, 0, 0)),
                  _const_spec((3, CONV_WIDTH)),
                  _const_spec((D_MODEL, D_MODEL)),
                  _const_spec((1, D_MODEL)),
                  _const_spec((D_MODEL, 2 * ROUTER_COLS))],
        out_specs=(pl.BlockSpec((tm, D_MODEL), lambda i: (i, 0)),
                   pl.BlockSpec((tm, D_MODEL), lambda i: (i, 0)),
                   pl.BlockSpec((tm, LANES), lambda i: (i, 0)),
                   pl.BlockSpec((tm, LANES), lambda i: (i, 0))),
        compiler_params=_cparams(("arbitrary",)),
        name="outproj_ctx" if cond_base == 0 else "outproj_lat",
    )(x2d, bg, cu, cu, cu, o2d, mod3, conv_w, w_out_b, n2g, wr)


def _route_tables(rid, *, tm):
    n_pairs = rid.shape[0]
    nt = n_pairs // tm + N_EXPERTS
    experts = jnp.arange(N_EXPERTS, dtype=jnp.int32)
    onehot = (rid[None, :] == experts[:, None]).astype(jnp.int32)
    csum = jnp.cumsum(onehot, axis=1)
    counts = csum[:, -1]
    tiles_e = (counts + tm - 1) // tm
    tile_end = jnp.cumsum(tiles_e)
    tile_start = tile_end - tiles_e
    n_valid = tile_end[-1]
    pos = jnp.sum(onehot * (csum - 1 + tile_start[:, None] * tm), axis=0)
    tile_raw = jnp.arange(nt, dtype=jnp.int32)
    tile_idx = jnp.minimum(tile_raw, n_valid - 1)
    te = jnp.sum((tile_end[None, :] <= tile_idx[:, None]).astype(jnp.int32), axis=1)
    used = tiles_e > 0
    first = ((tile_raw == tile_start[te]) & (tile_raw < n_valid)).astype(jnp.int32)
    nxt = lax.cummin(jnp.where(used, experts, N_EXPERTS), axis=0, reverse=True)
    nxt = jnp.concatenate([nxt[1:], jnp.full((1,), N_EXPERTS, jnp.int32)])
    next_expert = jnp.where(nxt[te] < N_EXPERTS, nxt[te], -1).astype(jnp.int32)
    wslot = ((jnp.cumsum(used.astype(jnp.int32)) - 1)[te] & 1).astype(jnp.int32)
    return (pos, te, n_valid.reshape(1).astype(jnp.int32),
            (tile_start * tm + counts).astype(jnp.int32), (tiles_e * tm - counts).astype(jnp.int32),
            first, next_expert, wslot)


def _dispatch_kernel(pstart_ref, pcnt_ref, nval_ref, pos0_ref, pos1_ref, h2a_ref, h2b_ref, xs_hbm,
                     buf, zbuf, sem, sem_z, *, tm, nt, n_a):
    i = pl.program_id(0)
    n = pl.num_programs(0)
    slot = i & 1

    def row_copy(r, pos_ref, s):
        return pltpu.make_async_copy(buf.at[s, pl.ds(r, 1)], xs_hbm.at[pl.ds(pos_ref[0, 0, r], 1)], sem.at[s])

    def wait_tile(s):
        for _ in range(2):
            pltpu.make_async_copy(buf.at[s], xs_hbm.at[pl.ds(0, tm)], sem.at[s]).wait()

    def pad_copies(e, fn):
        cnt = pcnt_ref[e]
        start = pstart_ref[e]
        off = start + cnt
        for b in PAD_BITS:
            if b < SUBLANES:
                break
            off = off - (cnt & b)
            dst = pl.multiple_of(off, SUBLANES)

            @pl.when((cnt & b) != 0)
            def _():
                fn(pltpu.make_async_copy(zbuf.at[pl.ds(0, b)], xs_hbm.at[pl.ds(dst, b)], sem_z))
        for j in range(SUBLANES - 1):
            @pl.when(j < (cnt & (SUBLANES - 1)))
            def _():
                fn(pltpu.make_async_copy(zbuf.at[pl.ds(0, 1)], xs_hbm.at[pl.ds(start + j, 1)], sem_z))

    def tail_copy(j):
        return pltpu.make_async_copy(zbuf, xs_hbm.at[pl.ds(pl.multiple_of(j * tm, tm), tm)], sem_z)

    @pl.when(i == 0)
    def _():
        zbuf[...] = jnp.zeros((tm, D_MODEL), F32)
        lax.fori_loop(0, N_EXPERTS, lambda e, c: (pad_copies(e, lambda d: d.start()), c)[1], 0)
        lax.fori_loop(nval_ref[0], nt, lambda j, c: (tail_copy(j).start(), c)[1], 0)
        lax.fori_loop(0, N_EXPERTS, lambda e, c: (pad_copies(e, lambda d: d.wait()), c)[1], 0)
        lax.fori_loop(nval_ref[0], nt, lambda j, c: (tail_copy(j).wait(), c)[1], 0)

    @pl.when(i >= 2)
    def _():
        wait_tile(slot)

    @pl.when(i < n_a)
    def _():
        buf[slot] = h2a_ref[...].astype(F32)

    @pl.when(i >= n_a)
    def _():
        buf[slot] = h2b_ref[...].astype(F32)

    for r in range(tm):
        row_copy(r, pos0_ref, slot).start()
        row_copy(r, pos1_ref, slot).start()

    @pl.when(i == n - 1)
    def _():
        wait_tile(slot)
        wait_tile(1 - slot)


def _dispatch(pad_start, pad_cnt, n_valid, pos3, h2_a, h2_b, *, nt):
    tm = TM_ROW
    n_a, n_b = h2_a.shape[0] // tm, h2_b.shape[0] // tm
    smem_tile = functools.partial(pl.BlockSpec, (1, 1, tm), memory_space=pltpu.SMEM)
    grid_spec = pltpu.PrefetchScalarGridSpec(
        num_scalar_prefetch=3,
        grid=(n_a + n_b,),
        in_specs=[smem_tile(lambda i, *_: (i, 0, 0)),
                  smem_tile(lambda i, *_: (n_a + n_b + i, 0, 0)),
                  pl.BlockSpec((tm, D_MODEL), lambda i, *_: (jnp.minimum(i, n_a - 1), 0)),
                  pl.BlockSpec((tm, D_MODEL), lambda i, *_: (jnp.maximum(i - n_a, 0), 0))],
        out_specs=pl.BlockSpec(memory_space=pl.ANY),
        scratch_shapes=[pltpu.VMEM((2, tm, D_MODEL), F32),
                        pltpu.VMEM((tm, D_MODEL), F32),
                        pltpu.SemaphoreType.DMA((2,)),
                        pltpu.SemaphoreType.DMA])
    return pl.pallas_call(
        functools.partial(_dispatch_kernel, tm=tm, nt=nt, n_a=n_a),
        out_shape=jax.ShapeDtypeStruct((nt * TM_MOE, D_MODEL), F32),
        grid_spec=grid_spec,
        compiler_params=pltpu.CompilerParams(dimension_semantics=("arbitrary",),
                                             vmem_limit_bytes=VMEM_LIMIT, has_side_effects=True),
        name="dispatch",
    )(pad_start, pad_cnt, n_valid, pos3, pos3, h2_a, h2_b)


def _moe_kernel(texp_ref, nval_ref, first_ref, next_ref, wslot_ref, xs_ref, wg_hbm, wu_hbm, wd_hbm, y_ref,
                wg32, wu32, wd32, wgb, wub, wdb, sem_w):
    i = pl.program_id(0)

    def weight_copies(e, s):
        return (pltpu.make_async_copy(wg_hbm.at[e], wg32.at[s], sem_w.at[s]),
                pltpu.make_async_copy(wu_hbm.at[e], wu32.at[s], sem_w.at[s]),
                pltpu.make_async_copy(wd_hbm.at[e], wd32.at[s], sem_w.at[s]))

    @pl.when(i == 0)
    def _():
        for d in weight_copies(texp_ref[0], 0):
            d.start()

    @pl.when(i < nval_ref[0])
    def _():
        @pl.when(first_ref[i] == 1)
        def _():
            s = wslot_ref[i]
            for d in weight_copies(texp_ref[i], s):
                d.wait()
            e_next = next_ref[i]

            @pl.when(e_next >= 0)
            def _():
                for d in weight_copies(e_next, 1 - s):
                    d.start()
            wgb[...] = wg32[s].astype(BF16)
            wub[...] = wu32[s].astype(BF16)
            wdb[...] = wd32[s].astype(BF16)

        x = xs_ref[...].astype(BF16)
        g = jnp.dot(x, wgb[...], preferred_element_type=F32)
        u = jnp.dot(x, wub[...], preferred_element_type=F32)
        a = g / (1.0 + jnp.exp(-g)) * u
        y_ref[...] = jnp.dot(a.astype(BF16), wdb[...], preferred_element_type=F32)

    @pl.when(i >= nval_ref[0])
    def _():
        y_ref[...] = jnp.zeros(y_ref.shape, F32)


def _moe(tile_expert, n_valid, first, next_expert, wslot, xs, w_gate, w_up, w_down):
    nt = tile_expert.shape[0]
    tm = TM_MOE
    grid_spec = pltpu.PrefetchScalarGridSpec(
        num_scalar_prefetch=5,
        grid=(nt,),
        in_specs=[pl.BlockSpec((tm, D_MODEL), lambda i, te, nv, *_: (jnp.minimum(i, nv[0] - 1), 0)),
                  pl.BlockSpec(memory_space=pl.ANY),
                  pl.BlockSpec(memory_space=pl.ANY),
                  pl.BlockSpec(memory_space=pl.ANY)],
        out_specs=pl.BlockSpec((tm, D_MODEL), lambda i, *_: (i, 0)),
        scratch_shapes=[pltpu.VMEM((2, D_MODEL, D_EXPERT), F32),
                        pltpu.VMEM((2, D_MODEL, D_EXPERT), F32),
                        pltpu.VMEM((2, D_EXPERT, D_MODEL), F32),
                        pltpu.VMEM((D_MODEL, D_EXPERT), BF16),
                        pltpu.VMEM((D_MODEL, D_EXPERT), BF16),
                        pltpu.VMEM((D_EXPERT, D_MODEL), BF16),
                        pltpu.SemaphoreType.DMA((2,))])
    return pl.pallas_call(
        _moe_kernel,
        out_shape=jax.ShapeDtypeStruct((nt * tm, D_MODEL), F32),
        grid_spec=grid_spec,
        compiler_params=_cparams(("arbitrary",)),
        name="moe",
    )(tile_expert, n_valid, first, next_expert, wslot, xs, w_gate, w_up, w_down)


def _final_kernel(pa0_ref, pb0_ref, pa_ref, pb_ref, x1_ref, rw_ref, mod_ref, y_hbm, o_ref, ybuf, sem, *, tm):
    i = pl.program_id(0)
    n = pl.num_programs(0)
    slot = i & 1

    def start_tile(pa, pb, s):
        for r in range(tm):
            pltpu.make_async_copy(y_hbm.at[pl.ds(pa[0, 0, r], 1)], ybuf.at[s, 0, pl.ds(r, 1)], sem.at[s]).start()
            pltpu.make_async_copy(y_hbm.at[pl.ds(pb[0, 0, r], 1)], ybuf.at[s, 1, pl.ds(r, 1)], sem.at[s]).start()

    @pl.when(i == 0)
    def _():
        start_tile(pa0_ref, pb0_ref, 0)

    @pl.when(i + 1 < n)
    def _():
        start_tile(pa_ref, pb_ref, 1 - slot)

    for k in range(2):
        pltpu.make_async_copy(y_hbm.at[pl.ds(0, tm)], ybuf.at[slot, k], sem.at[slot]).wait()
    w = rw_ref[...]
    moe = w[:, 0:1] * ybuf[slot, 0] + w[:, 1:2] * ybuf[slot, 1]
    o_ref[...] = x1_ref[...] + mod_ref[0][5:6, :] * moe


def _final(x1, rw, y, pos3, mod3, *, tile_base, slot_tiles, cond_base, tiles_per_cond):
    t = x1.shape[0]
    tm = TM_ROW
    n = t // tm
    smem_tile = functools.partial(pl.BlockSpec, (1, 1, tm), memory_space=pltpu.SMEM)
    return pl.pallas_call(
        functools.partial(_final_kernel, tm=tm),
        out_shape=jax.ShapeDtypeStruct((t, D_MODEL), F32),
        grid=(n,),
        in_specs=[smem_tile(lambda i: (tile_base, 0, 0)),
                  smem_tile(lambda i: (slot_tiles + tile_base, 0, 0)),
                  smem_tile(lambda i: (tile_base + jnp.minimum(i + 1, n - 1), 0, 0)),
                  smem_tile(lambda i: (slot_tiles + tile_base + jnp.minimum(i + 1, n - 1), 0, 0)),
                  pl.BlockSpec((tm, D_MODEL), lambda i: (i, 0)),
                  pl.BlockSpec((tm, LANES), lambda i: (i, 0)),
                  pl.BlockSpec((1, N_MOD, D_MODEL), lambda i: (cond_base + i // tiles_per_cond, 0, 0)),
                  pl.BlockSpec(memory_space=pl.ANY)],
        out_specs=pl.BlockSpec((tm, D_MODEL), lambda i: (i, 0)),
        scratch_shapes=[pltpu.VMEM((2, 2, tm, D_MODEL), F32), pltpu.SemaphoreType.DMA((2,))],
        compiler_params=_cparams(("arbitrary",)),
        name="final_ctx" if cond_base == 0 else "final_lat",
    )(pos3, pos3, pos3, pos3, x1, rw, mod3, y)


def _rope_table(n_tokens):
    rows = n_tokens // GRID_W
    inv = ROPE_BASE ** (-jnp.arange(ROPE_PAIRS, dtype=F32) / ROPE_PAIRS)
    row_ang = jnp.arange(rows, dtype=F32)[:, None] * inv
    col_ang = jnp.arange(GRID_W, dtype=F32)[:, None] * inv
    cr, sr = (jnp.repeat(f(row_ang), GRID_W, axis=0) for f in (jnp.cos, jnp.sin))
    cc, sc = (jnp.tile(f(col_ang), (rows, 1)) for f in (jnp.cos, jnp.sin))
    return jnp.concatenate([cr, cr, cc, cc, -sr, sr, -sc, sc], axis=-1)


def _swap_halves(a):
    p = ROPE_PAIRS
    return jnp.concatenate([a[..., p:2 * p], a[..., :p], a[..., 3 * p:], a[..., 2 * p:3 * p]], axis=-1)


def _head_gains(g):
    rope = g[QK_NOPE:]
    return jnp.stack([g[:QK_NOPE], jnp.concatenate([rope, _swap_halves(rope)])])


def kernel(x_prompt, x_sample, cache_ckv, cache_krope, c, c_ctx, ada_w, ada_b, norm1_g, w_in, conv_w,
           q_lora_g, w_uq, kv_lora_g, w_ukv, q_head_g, k_head_g, w_out, norm2_g, router_g, router_e,
           w_gate, w_up, w_down):
    depth = ada_w.shape[0]
    assert depth == 1
    bp, sp, _ = x_prompt.shape
    bs, ss, _ = x_sample.shape
    past = cache_ckv.shape[2]
    tp, ts = bp * sp, bs * ss
    assert sp & (sp - 1) == 0 and ss & (ss - 1) == 0
    assert tp % TM_IN == 0 and ss % TM_IN == 0 and TM_IN % sp == 0 and ss % TQ == 0
    assert bs + 1 <= 8 and past % 256 == 0 and TM_ROW == TM_MOE and tp % TM_ROW == 0 and ts % TM_ROW == 0

    w_in_b = _winprep(w_in[0].T)
    uq = w_uq[0].reshape(Q_LORA, N_HEADS, QK_DIM)
    w_uq_b = jnp.concatenate([uq, _swap_halves(uq[..., QK_NOPE:])], axis=-1)
    w_uq_b = w_uq_b.reshape(Q_LORA, N_HEADS * HEAD_W).astype(BF16)
    ukv = w_ukv[0].reshape(KV_LORA, N_HEADS, QK_NOPE + V_DIM)
    w_uk_b = ukv[..., :QK_NOPE].reshape(KV_LORA, N_HEADS * QK_NOPE).astype(BF16)
    w_uvt_b = ukv[..., QK_NOPE:].reshape(KV_LORA, N_HEADS * V_DIM).T.astype(BF16)
    w_out_b = w_out[0].astype(BF16)
    wr = jnp.concatenate([router_g[0], router_e[0],
                          jnp.zeros((D_MODEL, ROUTER_COLS - N_EXPERT_GROUPS - N_EXPERTS), F32)], axis=1)
    wr_hi = lax.bitcast_convert_type(lax.bitcast_convert_type(wr, jnp.uint32) & jnp.uint32(0xFFFF0000), F32)
    wr = jnp.concatenate([wr_hi, wr - wr_hi], axis=1).astype(BF16)
    gq, gk = _head_gains(q_head_g[0]), _head_gains(k_head_g[0])
    n1g, n2g = norm1_g[0].reshape(1, D_MODEL), norm2_g[0].reshape(1, D_MODEL)
    qlg, kvg = q_lora_g[0].reshape(1, Q_LORA), kv_lora_g[0].reshape(1, KV_LORA)
    cs_lat = _rope_table(ss)
    cs_id = jnp.concatenate([jnp.ones((TM_IN, QK_ROPE), F32), jnp.zeros((TM_IN, QK_ROPE), F32)], axis=1)

    cond8 = jnp.concatenate([c_ctx[None, :], c, jnp.zeros((8 - 1 - bs, D_MODEL), F32)], axis=0)
    mod3 = _modulation(cond8, ada_w[0], ada_b[0]).reshape(8, N_MOD, D_MODEL)

    xp2, xs2 = x_prompt.reshape(tp, D_MODEL), x_sample.reshape(ts, D_MODEL)
    big = 1 << 30

    bg_p, cu_p, q_p, k_p, vt_p, ckv_p, kr_p = _inproj(
        xp2, mod3, n1g, w_in_b, qlg, w_uq_b, kvg, w_uk_b, w_uvt_b, gq, gk, cs_id,
        cond_base=0, tiles_per_cond=big, cs_tiles=1, emit_cache=True)
    o_p = _attention(q_p.reshape(bp, sp, -1), [(k_p.reshape(bp, sp, -1), vt_p)],
                     tq=sp, heads=N_HEADS, name="attn_ctx")
    x1_p, h2_p, rid_p, rw_p = _outproj(
        xp2, bg_p, cu_p, o_p.reshape(tp, -1), mod3, conv_w[0], w_out_b, n2g, wr,
        cond_base=0, tiles_per_cond=big, seq_len=sp)

    kr_c = cache_krope[:, 0].reshape(bs * past, QK_ROPE)
    k_c, vt_c = _kvcache(cache_ckv[:, 0].reshape(bs * past, KV_LORA), jnp.concatenate([kr_c, kr_c], axis=1),
                         cs_id, w_uk_b, w_uvt_b, gk)
    bg_s, cu_s, q_s, k_s, vt_s = _inproj(
        xs2, mod3, n1g, w_in_b, qlg, w_uq_b, kvg, w_uk_b, w_uvt_b, gq, gk, cs_lat,
        cond_base=1, tiles_per_cond=ss // TM_IN, cs_tiles=ss // TM_IN, emit_cache=False)
    o_s = _attention(q_s.reshape(bs, ss, -1),
                     [(k_c.reshape(bs, past, -1), vt_c), (k_s.reshape(bs, ss, -1), vt_s)],
                     tq=TQ, heads=2, name="attn_lat")
    x1_s, h2_s, rid_s, rw_s = _outproj(
        xs2, bg_s, cu_s, o_s.reshape(ts, -1), mod3, conv_w[0], w_out_b, n2g, wr,
        cond_base=1, tiles_per_cond=ss // TM_OUT, seq_len=ss)

    n_tok = tp + ts
    slot_tiles = n_tok // TM_ROW
    nt = 2 * n_tok // TM_MOE + N_EXPERTS
    rid = jnp.concatenate([rid_p[:, :2], rid_s[:, :2]], axis=0).T.reshape(2 * n_tok)
    pos, te, n_valid, pad_start, pad_cnt, first, next_expert, wslot = _route_tables(rid, tm=TM_MOE)
    pos3 = pos.reshape(2 * slot_tiles, 1, TM_ROW)
    xs = _dispatch(pad_start, pad_cnt, n_valid, pos3, h2_p, h2_s, nt=nt)
    y = _moe(te, n_valid, first, next_expert, wslot, xs, w_gate[0], w_up[0], w_down[0])

    y_p = _final(x1_p, rw_p, y, pos3, mod3, tile_base=0, slot_tiles=slot_tiles, cond_base=0, tiles_per_cond=big)
    y_s = _final(x1_s, rw_s, y, pos3, mod3, tile_base=tp // TM_ROW, slot_tiles=slot_tiles, cond_base=1,
                 tiles_per_cond=ss // TM_ROW)

    return (y_p.reshape(bp, sp, D_MODEL), y_s.reshape(bs, ss, D_MODEL),
            ckv_p.reshape(bp, 1, sp, KV_LORA), kr_p.reshape(bp, 1, sp, QK_ROPE))
```

```python
import functools

import jax
import jax.numpy as jnp
from jax import lax
from jax.experimental import pallas as pl
from jax.experimental.pallas import tpu as pltpu

F32 = jnp.float32
BF16 = jnp.bfloat16
HIGHEST = lax.Precision.HIGHEST

D_MODEL = 2048
CONV_WIDTH = 1024
N_HEADS = 8
QK_NOPE = 128
QK_ROPE = 64
V_DIM = 128
QK_DIM = QK_NOPE + QK_ROPE
Q_LORA = 512
KV_LORA = 256
GRID_W = 64
ROPE_PAIRS = QK_ROPE // 4
ROPE_BASE = 10000.0
N_EXPERT_GROUPS = 4
EXPERTS_PER_GROUP = 8
N_EXPERTS = N_EXPERT_GROUPS * EXPERTS_PER_GROUP
D_EXPERT = 512
N_MOD = 6
EPS = 1e-6
LOG2_E = 1.4426950408889634

HEAD_W = 2 * QK_NOPE
LANES = 128
SUBLANES = 8
IN_COLS = 3 * CONV_WIDTH + Q_LORA + KV_LORA + 2 * QK_ROPE
ROUTER_COLS = LANES
VMEM_LIMIT = 56 * 1024 * 1024

TM_IN = 512
TM_OUT = 512
OUT_SLABS = 2
TQ = 256
ATTN_CHUNK = 512
ATTN_AHEAD = 2
TM_MOE = 256
TM_ROW = 256
BN_MOD = 1024
PAD_BITS = tuple(1 << b for b in reversed(range(TM_MOE.bit_length() - 1)))


def _cparams(sem):
    return pltpu.CompilerParams(dimension_semantics=sem, vmem_limit_bytes=VMEM_LIMIT)


def _const_spec(shape):
    nd = len(shape)
    return pl.BlockSpec(shape, lambda *_: (0,) * nd, pipeline_mode=pl.Buffered(1))


def _rms(x):
    return x * lax.rsqrt(jnp.mean(x * x, axis=-1, keepdims=True) + EPS)


def _rowsum(x):
    return jnp.sum(x, axis=-1, keepdims=True)


def _mod_kernel(c_ref, w_ref, b_ref, o_ref):
    c = c_ref[...]
    s = c / (1.0 + jnp.exp(-c))
    s_hi = s.astype(BF16)
    s_lo = (s - s_hi.astype(F32)).astype(BF16)
    w = w_ref[...]
    w_hi = w.astype(BF16)
    w_lo = (w - w_hi.astype(F32)).astype(BF16)
    rows = s.shape[0]
    a = jnp.dot(jnp.concatenate([s_hi, s_lo], axis=0), w_hi, preferred_element_type=F32)
    b = jnp.dot(s_hi, w_lo, preferred_element_type=F32)
    o_ref[...] = a[:rows] + (a[rows:] + b) + b_ref[...]


def _modulation(cond8, ada_w, ada_b):
    n = ada_w.shape[1]
    return pl.pallas_call(
        _mod_kernel,
        out_shape=jax.ShapeDtypeStruct((8, n), F32),
        grid=(n // BN_MOD,),
        in_specs=[pl.BlockSpec((8, D_MODEL), lambda j: (0, 0)),
                  pl.BlockSpec((D_MODEL, BN_MOD), lambda j: (0, j)),
                  pl.BlockSpec((1, BN_MOD), lambda j: (0, j))],
        out_specs=pl.BlockSpec((8, BN_MOD), lambda j: (0, j)),
        compiler_params=_cparams(("arbitrary",)),
        name="mod",
    )(cond8, ada_w, ada_b.reshape(1, n))


def _winprep_kernel(w_ref, o_ref):
    n = w_ref.shape[0]
    o_ref[:n, :] = w_ref[...].astype(BF16)
    p = ROPE_PAIRS
    for dst, src in ((0, p), (p, 0), (2 * p, 3 * p), (3 * p, 2 * p)):
        o_ref[n + dst:n + dst + p, :] = w_ref[n - QK_ROPE + src:n - QK_ROPE + src + p, :].astype(BF16)


def _winprep(w_in_t):
    n, k = w_in_t.shape
    tk = 512
    return pl.pallas_call(
        _winprep_kernel,
        out_shape=jax.ShapeDtypeStruct((IN_COLS, k), BF16),
        grid=(k // tk,),
        in_specs=[pl.BlockSpec((n, tk), lambda i: (0, i))],
        out_specs=pl.BlockSpec((IN_COLS, tk), lambda i: (0, i)),
        compiler_params=_cparams(("arbitrary",)),
        name="winprep",
    )(w_in_t)


def _emit_kv(ckv, kraw, cs, w_uk_ref, w_uvt_ref, gk_ref, k_ref, vt_ref):
    cb = ckv.astype(BF16)
    kn_all = jnp.dot(cb, w_uk_ref[...], preferred_element_type=F32)
    vt = lax.dot_general(w_uvt_ref[...], cb, (((1,), (1,)), ((), ())), preferred_element_type=F32)
    vt_ref[...] = vt.astype(BF16)
    ss_rope = 0.5 * _rowsum(kraw * kraw)
    t = kraw * (cs * gk_ref[1:2, :])
    tt = t + pltpu.roll(t, QK_ROPE, axis=1)
    g_nope = gk_ref[0:1, :]
    for h in range(N_HEADS):
        kn = kn_all[:, h * QK_NOPE:(h + 1) * QK_NOPE]
        r = lax.rsqrt((_rowsum(kn * kn) + ss_rope) * (1.0 / QK_DIM) + EPS)
        k_ref[:, h * HEAD_W:h * HEAD_W + QK_NOPE] = (kn * r * g_nope).astype(BF16)
        k_ref[:, h * HEAD_W + QK_NOPE:(h + 1) * HEAD_W] = (tt * r).astype(BF16)


def _kvcache_kernel(ckv_ref, kraw_ref, cs_ref, w_uk_ref, w_uvt_ref, gk_ref, k_ref, vt_ref):
    _emit_kv(ckv_ref[...], kraw_ref[...], cs_ref[...], w_uk_ref, w_uvt_ref, gk_ref, k_ref, vt_ref)


def _kvcache(ckv, kraw, cs_id, w_uk_b, w_uvt_b, gk):
    n = ckv.shape[0]
    tm = 256
    return pl.pallas_call(
        _kvcache_kernel,
        out_shape=(jax.ShapeDtypeStruct((n, N_HEADS * HEAD_W), BF16),
                   jax.ShapeDtypeStruct((N_HEADS * V_DIM, n), BF16)),
        grid=(n // tm,),
        in_specs=[pl.BlockSpec((tm, KV_LORA), lambda i: (i, 0)),
                  pl.BlockSpec((tm, LANES), lambda i: (i, 0)),
                  pl.BlockSpec((tm, LANES), lambda i: (0, 0)),
                  _const_spec((KV_LORA, N_HEADS * QK_NOPE)),
                  _const_spec((N_HEADS * V_DIM, KV_LORA)),
                  _const_spec((2, LANES))],
        out_specs=(pl.BlockSpec((tm, N_HEADS * HEAD_W), lambda i: (i, 0)),
                   pl.BlockSpec((N_HEADS * V_DIM, tm), lambda i: (0, i))),
        compiler_params=_cparams(("arbitrary",)),
        name="kvcache",
    )(ckv, kraw, cs_id, w_uk_b, w_uvt_b, gk)


def _inproj_kernel(x_ref, mod_ref, n1g_ref, w_in_ref, qlg_ref, w_uq_ref, kvg_ref, w_uk_ref, w_uvt_ref,
                   gq_ref, gk_ref, cs_ref, bg_ref, cu_ref, q_ref, k_ref, vt_ref, *cache_refs):
    x = x_ref[...]
    mod = mod_ref[0]
    h = _rms(x) * n1g_ref[...] * (1.0 + mod[1:2, :]) + mod[0:1, :]
    hb = h.astype(BF16)

    def proj(a, b):
        return lax.dot_general(hb, w_in_ref[a:b, :], (((1,), (1,)), ((), ())), preferred_element_type=F32)

    c1, c2, c3 = CONV_WIDTH, 2 * CONV_WIDTH, 3 * CONV_WIDTH
    q_lat = proj(c3, c3 + Q_LORA)
    kvk = proj(c3 + Q_LORA, IN_COLS)
    bg_ref[...] = proj(0, c1).astype(BF16)
    cs = cs_ref[...]

    qn = _rms(q_lat) * qlg_ref[...]
    q = jnp.dot(qn.astype(BF16), w_uq_ref[...], preferred_element_type=F32)
    scale = QK_DIM ** -0.5 * LOG2_E
    g_nope = gq_ref[0:1, :] * scale
    tq = cs * (gq_ref[1:2, :] * scale)
    for hd in range(N_HEADS):
        lo = q[:, hd * HEAD_W:hd * HEAD_W + QK_NOPE]
        up = q[:, hd * HEAD_W + QK_NOPE:(hd + 1) * HEAD_W]
        ss = _rowsum(lo * lo) + 0.5 * _rowsum(up * up)
        r = lax.rsqrt(ss * (1.0 / QK_DIM) + EPS)
        q_ref[:, hd * HEAD_W:hd * HEAD_W + QK_NOPE] = (lo * r * g_nope).astype(BF16)
        q_ref[:, hd * HEAD_W + QK_NOPE:(hd + 1) * HEAD_W] = (up * r * tq).astype(BF16)

    cu_ref[...] = (proj(c1, c2) * proj(c2, c3)).astype(BF16)

    kv_lat = kvk[:, :KV_LORA]
    kraw = kvk[:, KV_LORA:]
    ckv = _rms(kv_lat) * kvg_ref[...]
    if cache_refs:
        ckv_out_ref, kr_out_ref = cache_refs
        ckv_out_ref[...] = ckv
        kr_out_ref[...] = kraw[:, :QK_ROPE]
    _emit_kv(ckv, kraw, cs, w_uk_ref, w_uvt_ref, gk_ref, k_ref, vt_ref)


def _inproj(x2d, mod3, n1g, w_in_b, qlg, w_uq_b, kvg, w_uk_b, w_uvt_b, gq, gk, cs, *,
            cond_base, tiles_per_cond, cs_tiles, emit_cache):
    t = x2d.shape[0]
    tm = TM_IN
    out_shape = [jax.ShapeDtypeStruct((t, CONV_WIDTH), BF16),
                 jax.ShapeDtypeStruct((t, CONV_WIDTH), BF16),
                 jax.ShapeDtypeStruct((t, N_HEADS * HEAD_W), BF16),
                 jax.ShapeDtypeStruct((t, N_HEADS * HEAD_W), BF16),
                 jax.ShapeDtypeStruct((N_HEADS * V_DIM, t), BF16)]
    out_specs = [pl.BlockSpec((tm, CONV_WIDTH), lambda i: (i, 0)),
                 pl.BlockSpec((tm, CONV_WIDTH), lambda i: (i, 0)),
                 pl.BlockSpec((tm, N_HEADS * HEAD_W), lambda i: (i, 0)),
                 pl.BlockSpec((tm, N_HEADS * HEAD_W), lambda i: (i, 0)),
                 pl.BlockSpec((N_HEADS * V_DIM, tm), lambda i: (0, i))]
    if emit_cache:
        out_shape += [jax.ShapeDtypeStruct((t, KV_LORA), F32), jax.ShapeDtypeStruct((t, QK_ROPE), F32)]
        out_specs += [pl.BlockSpec((tm, KV_LORA), lambda i: (i, 0)),
                      pl.BlockSpec((tm, QK_ROPE), lambda i: (i, 0))]
    return pl.pallas_call(
        _inproj_kernel,
        out_shape=tuple(out_shape),
        grid=(t // tm,),
        in_specs=[pl.BlockSpec((tm, D_MODEL), lambda i: (i, 0)),
                  pl.BlockSpec((1, N_MOD, D_MODEL), lambda i: (cond_base + i // tiles_per_cond, 0, 0)),
                  _const_spec((1, D_MODEL)),
                  _const_spec((IN_COLS, D_MODEL)),
                  _const_spec((1, Q_LORA)),
                  _const_spec((Q_LORA, N_HEADS * HEAD_W)),
                  _const_spec((1, KV_LORA)),
                  _const_spec((KV_LORA, N_HEADS * QK_NOPE)),
                  _const_spec((N_HEADS * V_DIM, KV_LORA)),
                  _const_spec((2, LANES)),
                  _const_spec((2, LANES)),
                  pl.BlockSpec((tm, LANES), lambda i: (i % cs_tiles, 0))],
        out_specs=tuple(out_specs),
        compiler_params=_cparams(("arbitrary",)),
        name="inproj_ctx" if emit_cache else "inproj_lat",
    )(x2d, mod3, n1g, w_in_b, qlg, w_uq_b, kvg, w_uk_b, w_uvt_b, gq, gk, cs)


def _attn_kernel(*refs, n_kv, heads):
    q_ref = refs[0]
    o_ref = refs[-1]
    chunks = []
    for j in range(n_kv):
        sk = refs[1 + 2 * j].shape[1]
        step = min(sk, ATTN_CHUNK)
        chunks += [(j, lo, lo + step) for lo in range(0, sk, step)]
    q = [q_ref[0, :, h * HEAD_W:(h + 1) * HEAD_W] for h in range(heads)]

    def score(h, c):
        j, lo, hi = chunks[c]
        k = refs[1 + 2 * j][0, lo:hi, h * HEAD_W:(h + 1) * HEAD_W]
        return lax.dot_general(k, q[h], (((1,), (1,)), ((), ())), preferred_element_type=F32)

    m = [None] * heads
    acc = [None] * heads
    ahead = [[score(h, c) for c in range(min(ATTN_AHEAD, len(chunks)))] for h in range(heads)]
    for c, (j, lo, hi) in enumerate(chunks):
        for h in range(heads):
            s = ahead[h].pop(0)
            if c + ATTN_AHEAD < len(chunks):
                ahead[h].append(score(h, c + ATTN_AHEAD))
            mc = jnp.max(s, axis=0, keepdims=True)
            m_new = mc if m[h] is None else jnp.maximum(m[h], mc)
            p = jnp.exp2(s - m_new).astype(BF16)
            vt = refs[2 + 2 * j][h * V_DIM:(h + 1) * V_DIM, lo:hi]
            vt1 = jnp.concatenate([vt, jnp.ones((2 * SUBLANES, hi - lo), BF16)], axis=0)
            part = jnp.dot(vt1, p, preferred_element_type=F32)
            acc[h] = part if acc[h] is None else acc[h] * jnp.exp2(m[h] - m_new) + part
            m[h] = m_new
    for h in range(heads):
        ot = acc[h][:V_DIM, :] / acc[h][V_DIM:V_DIM + 1, :]
        o_ref[0, :, h * V_DIM:(h + 1) * V_DIM] = ot.T.astype(BF16)


def _attention(q, kvs, *, tq, heads, name):
    b, s, _ = q.shape
    in_specs = [pl.BlockSpec((1, tq, heads * HEAD_W), lambda bi, hi, qi: (bi, qi, hi))]
    args = [q]
    for k, v in kvs:
        sk = k.shape[1]
        in_specs.append(pl.BlockSpec((1, sk, heads * HEAD_W), lambda bi, hi, qi: (bi, 0, hi)))
        in_specs.append(pl.BlockSpec((heads * V_DIM, sk), lambda bi, hi, qi: (hi, bi)))
        args += [k, v]
    return pl.pallas_call(
        functools.partial(_attn_kernel, n_kv=len(kvs), heads=heads),
        out_shape=jax.ShapeDtypeStruct((b, s, N_HEADS * V_DIM), BF16),
        grid=(b, N_HEADS // heads, s // tq),
        in_specs=in_specs,
        out_specs=pl.BlockSpec((1, tq, heads * V_DIM), lambda bi, hi, qi: (bi, qi, hi)),
        compiler_params=_cparams(("arbitrary", "arbitrary", "arbitrary")),
        name=name,
    )(*args)


def _outproj_kernel(x_ref, bg_ref, cu_ref, cup_ref, cun_ref, o_ref, mod_ref, cw_ref, w_out_ref,
                    n2g_ref, wr_ref, x1_ref, h2_ref, rid_ref, rw_ref, *, tm, seq_len):
    i = pl.program_id(0)
    mod = mod_ref[0]
    cu = cu_ref[...].astype(F32)
    prev_row = cup_ref[...].astype(F32)[15:16, :]
    next_row = cun_ref[...].astype(F32)[0:1, :]
    row = lax.broadcasted_iota(jnp.int32, (tm, 1), 0)
    pos = (i * tm + row) & (seq_len - 1)
    up = jnp.where(row == 0, prev_row, pltpu.roll(cu, 1, axis=0))
    up = jnp.where(pos == 0, 0.0, up)
    dn = jnp.where(row == tm - 1, next_row, pltpu.roll(cu, tm - 1, axis=0))
    dn = jnp.where(pos == seq_len - 1, 0.0, dn)
    cw = cw_ref[...]
    y_conv = bg_ref[...].astype(F32) * (up * cw[0:1, :] + cu * cw[1:2, :] + dn * cw[2:3, :])
    y_conv = y_conv.astype(BF16)

    slabs = [(r, r + tm // OUT_SLABS) for r in range(0, tm, tm // OUT_SLABS)]
    mixes = []
    for lo, hi in slabs:
        mix = jnp.dot(y_conv[lo:hi], w_out_ref[:CONV_WIDTH, :], preferred_element_type=F32)
        mixes.append(mix + jnp.dot(o_ref[lo:hi, :], w_out_ref[CONV_WIDTH:, :], preferred_element_type=F32))
    for (lo, hi), mix in zip(slabs, mixes):
        x1 = x_ref[lo:hi, :] + mod[2:3, :] * mix
        x1_ref[lo:hi, :] = x1
        h2 = _rms(x1) * n2g_ref[...] * (1.0 + mod[4:5, :]) + mod[3:4, :]
        h2_hi = h2.astype(BF16)
        h2_ref[lo:hi, :] = h2_hi

        h2_lo = (h2 - h2_hi.astype(F32)).astype(BF16)
        hh_hl = jnp.dot(h2_hi, wr_ref[...], preferred_element_type=F32)
        lh = jnp.dot(h2_lo, wr_ref[:, :ROUTER_COLS], preferred_element_type=F32)
        logits = hh_hl[:, :ROUTER_COLS] + (hh_hl[:, ROUTER_COLS:] + lh)
        lane = lax.broadcasted_iota(jnp.int32, logits.shape, 1)
        neg = -jnp.inf
        big = jnp.int32(1 << 20)
        gl = jnp.where(lane < N_EXPERT_GROUPS, logits, neg)
        gmax = jnp.max(gl, axis=-1, keepdims=True)
        p_top = 1.0 / _rowsum(jnp.exp(gl - gmax))
        g_top = jnp.min(jnp.where(gl == gmax, lane, big), axis=-1, keepdims=True)
        e_lo = N_EXPERT_GROUPS + EXPERTS_PER_GROUP * g_top
        el = jnp.where((lane >= e_lo) & (lane < e_lo + EXPERTS_PER_GROUP), logits, neg)
        v1 = jnp.max(el, axis=-1, keepdims=True)
        i1 = jnp.min(jnp.where(el == v1, lane, big), axis=-1, keepdims=True)
        el2 = jnp.where(lane == i1, neg, el)
        v2 = jnp.max(el2, axis=-1, keepdims=True)
        i2 = jnp.min(jnp.where(el2 == v2, lane, big), axis=-1, keepdims=True)
        e21 = jnp.exp(v2 - v1)
        w1 = p_top / (1.0 + e21)
        w2 = w1 * e21
        rid_ref[lo:hi, :] = jnp.where(lane == 0, i1 - N_EXPERT_GROUPS, i2 - N_EXPERT_GROUPS)
        rw_ref[lo:hi, :] = jnp.where(lane == 0, w1, w2)


def _outproj(x2d, bg, cu, o2d, mod3, conv_w, w_out_b, n2g, wr, *, cond_base, tiles_per_cond, seq_len):
    t = x2d.shape[0]
    tm = TM_OUT
    hb = tm // 16
    nhb = t // 16
    return pl.pallas_call(
        functools.partial(_outproj_kernel, tm=tm, seq_len=seq_len),
        out_shape=(jax.ShapeDtypeStruct((t, D_MODEL), F32),
                   jax.ShapeDtypeStruct((t, D_MODEL), BF16),
                   jax.ShapeDtypeStruct((t, LANES), jnp.int32),
                   jax.ShapeDtypeStruct((t, LANES), F32)),
        grid=(t // tm,),
        in_specs=[pl.BlockSpec((tm, D_MODEL), lambda i: (i, 0)),
                  pl.BlockSpec((tm, CONV_WIDTH), lambda i: (i, 0)),
                  pl.BlockSpec((tm, CONV_WIDTH), lambda i: (i, 0)),
                  pl.BlockSpec((16, CONV_WIDTH), lambda i: (jnp.maximum(i * hb - 1, 0), 0)),
                  pl.BlockSpec((16, CONV_WIDTH), lambda i: (jnp.minimum((i + 1) * hb, nhb - 1), 0)),
                  pl.BlockSpec((tm, N_HEADS * V_DIM), lambda i: (i, 0)),
                  pl.BlockSpec((1, N_MOD, D_MODEL), lambda i: (cond_base + i // tiles_per_cond, 0, 0)),
                  _const_spec((3, CONV_WIDTH)),
                  _const_spec((D_MODEL, D_MODEL)),
                  _const_spec((1, D_MODEL)),
                  _const_spec((D_MODEL, 2 * ROUTER_COLS))],
        out_specs=(pl.BlockSpec((tm, D_MODEL), lambda i: (i, 0)),
                   pl.BlockSpec((tm, D_MODEL), lambda i: (i, 0)),
                   pl.BlockSpec((tm, LANES), lambda i: (i, 0)),
                   pl.BlockSpec((tm, LANES), lambda i: (i, 0))),
        compiler_params=_cparams(("arbitrary",)),
        name="outproj_ctx" if cond_base == 0 else "outproj_lat",
    )(x2d, bg, cu, cu, cu, o2d, mod3, conv_w, w_out_b, n2g, wr)


def _route_tables(rid, *, tm):
    n_pairs = rid.shape[0]
    nt = n_pairs // tm + N_EXPERTS
    experts = jnp.arange(N_EXPERTS, dtype=jnp.int32)
    onehot = (rid[None, :] == experts[:, None]).astype(jnp.int32)
    csum = jnp.cumsum(onehot, axis=1)
    counts = csum[:, -1]
    tiles_e = (counts + tm - 1) // tm
    tile_end = jnp.cumsum(tiles_e)
    tile_start = tile_end - tiles_e
    n_valid = tile_end[-1]
    pos = jnp.sum(onehot * (csum - 1 + tile_start[:, None] * tm), axis=0)
    tile_raw = jnp.arange(nt, dtype=jnp.int32)
    tile_idx = jnp.minimum(tile_raw, n_valid - 1)
    te = jnp.sum((tile_end[None, :] <= tile_idx[:, None]).astype(jnp.int32), axis=1)
    used = tiles_e > 0
    first = ((tile_raw == tile_start[te]) & (tile_raw < n_valid)).astype(jnp.int32)
    nxt = lax.cummin(jnp.where(used, experts, N_EXPERTS), axis=0, reverse=True)
    nxt = jnp.concatenate([nxt[1:], jnp.full((1,), N_EXPERTS, jnp.int32)])
    next_expert = jnp.where(nxt[te] < N_EXPERTS, nxt[te], -1).astype(jnp.int32)
    wslot = ((jnp.cumsum(used.astype(jnp.int32)) - 1)[te] & 1).astype(jnp.int32)
    return (pos, te, n_valid.reshape(1).astype(jnp.int32),
            (tile_start * tm + counts).astype(jnp.int32), (tiles_e * tm - counts).astype(jnp.int32),
            first, next_expert, wslot)


def _dispatch_kernel(pstart_ref, pcnt_ref, nval_ref, pos0_ref, pos1_ref, h2a_ref, h2b_ref, xs_hbm,
                     buf, zbuf, sem, sem_z, *, tm, nt, n_a):
    i = pl.program_id(0)
    n = pl.num_programs(0)
    slot = i & 1

    def row_copy(r, pos_ref, s):
        return pltpu.make_async_copy(buf.at[s, pl.ds(r, 1)], xs_hbm.at[pl.ds(pos_ref[0, 0, r], 1)], sem.at[s])

    def wait_tile(s):
        for _ in range(2):
            pltpu.make_async_copy(buf.at[s], xs_hbm.at[pl.ds(0, tm)], sem.at[s]).wait()

    def pad_copies(e, fn):
        cnt = pcnt_ref[e]
        start = pstart_ref[e]
        off = start + cnt
        for b in PAD_BITS:
            if b < SUBLANES:
                break
            off = off - (cnt & b)
            dst = pl.multiple_of(off, SUBLANES)

            @pl.when((cnt & b) != 0)
            def _():
                fn(pltpu.make_async_copy(zbuf.at[pl.ds(0, b)], xs_hbm.at[pl.ds(dst, b)], sem_z))
        for j in range(SUBLANES - 1):
            @pl.when(j < (cnt & (SUBLANES - 1)))
            def _():
                fn(pltpu.make_async_copy(zbuf.at[pl.ds(0, 1)], xs_hbm.at[pl.ds(start + j, 1)], sem_z))

    def tail_copy(j):
        return pltpu.make_async_copy(zbuf, xs_hbm.at[pl.ds(pl.multiple_of(j * tm, tm), tm)], sem_z)

    @pl.when(i == 0)
    def _():
        zbuf[...] = jnp.zeros((tm, D_MODEL), F32)
        lax.fori_loop(0, N_EXPERTS, lambda e, c: (pad_copies(e, lambda d: d.start()), c)[1], 0)
        lax.fori_loop(nval_ref[0], nt, lambda j, c: (tail_copy(j).start(), c)[1], 0)
        lax.fori_loop(0, N_EXPERTS, lambda e, c: (pad_copies(e, lambda d: d.wait()), c)[1], 0)
        lax.fori_loop(nval_ref[0], nt, lambda j, c: (tail_copy(j).wait(), c)[1], 0)

    @pl.when(i >= 2)
    def _():
        wait_tile(slot)

    @pl.when(i < n_a)
    def _():
        buf[slot] = h2a_ref[...].astype(F32)

    @pl.when(i >= n_a)
    def _():
        buf[slot] = h2b_ref[...].astype(F32)

    for r in range(tm):
        row_copy(r, pos0_ref, slot).start()
        row_copy(r, pos1_ref, slot).start()

    @pl.when(i == n - 1)
    def _():
        wait_tile(slot)
        wait_tile(1 - slot)


def _dispatch(pad_start, pad_cnt, n_valid, pos3, h2_a, h2_b, *, nt):
    tm = TM_ROW
    n_a, n_b = h2_a.shape[0] // tm, h2_b.shape[0] // tm
    smem_tile = functools.partial(pl.BlockSpec, (1, 1, tm), memory_space=pltpu.SMEM)
    grid_spec = pltpu.PrefetchScalarGridSpec(
        num_scalar_prefetch=3,
        grid=(n_a + n_b,),
        in_specs=[smem_tile(lambda i, *_: (i, 0, 0)),
                  smem_tile(lambda i, *_: (n_a + n_b + i, 0, 0)),
                  pl.BlockSpec((tm, D_MODEL), lambda i, *_: (jnp.minimum(i, n_a - 1), 0)),
                  pl.BlockSpec((tm, D_MODEL), lambda i, *_: (jnp.maximum(i - n_a, 0), 0))],
        out_specs=pl.BlockSpec(memory_space=pl.ANY),
        scratch_shapes=[pltpu.VMEM((2, tm, D_MODEL), F32),
                        pltpu.VMEM((tm, D_MODEL), F32),
                        pltpu.SemaphoreType.DMA((2,)),
                        pltpu.SemaphoreType.DMA])
    return pl.pallas_call(
        functools.partial(_dispatch_kernel, tm=tm, nt=nt, n_a=n_a),
        out_shape=jax.ShapeDtypeStruct((nt * TM_MOE, D_MODEL), F32),
        grid_spec=grid_spec,
        compiler_params=pltpu.CompilerParams(dimension_semantics=("arbitrary",),
                                             vmem_limit_bytes=VMEM_LIMIT, has_side_effects=True),
        name="dispatch",
    )(pad_start, pad_cnt, n_valid, pos3, pos3, h2_a, h2_b)


def _moe_kernel(texp_ref, nval_ref, first_ref, next_ref, wslot_ref, xs_ref, wg_hbm, wu_hbm, wd_hbm, y_ref,
                wg32, wu32, wd32, wgb, wub, wdb, sem_w):
    i = pl.program_id(0)

    def weight_copies(e, s):
        return (pltpu.make_async_copy(wg_hbm.at[e], wg32.at[s], sem_w.at[s]),
                pltpu.make_async_copy(wu_hbm.at[e], wu32.at[s], sem_w.at[s]),
                pltpu.make_async_copy(wd_hbm.at[e], wd32.at[s], sem_w.at[s]))

    @pl.when(i == 0)
    def _():
        for d in weight_copies(texp_ref[0], 0):
            d.start()

    @pl.when(i < nval_ref[0])
    def _():
        @pl.when(first_ref[i] == 1)
        def _():
            s = wslot_ref[i]
            for d in weight_copies(texp_ref[i], s):
                d.wait()
            e_next = next_ref[i]

            @pl.when(e_next >= 0)
            def _():
                for d in weight_copies(e_next, 1 - s):
                    d.start()
            wgb[...] = wg32[s].astype(BF16)
            wub[...] = wu32[s].astype(BF16)
            wdb[...] = wd32[s].astype(BF16)

        x = xs_ref[...].astype(BF16)
        g = jnp.dot(x, wgb[...], preferred_element_type=F32)
        u = jnp.dot(x, wub[...], preferred_element_type=F32)
        a = g / (1.0 + jnp.exp(-g)) * u
        y_ref[...] = jnp.dot(a.astype(BF16), wdb[...], preferred_element_type=F32)

    @pl.when(i >= nval_ref[0])
    def _():
        y_ref[...] = jnp.zeros(y_ref.shape, F32)


def _moe(tile_expert, n_valid, first, next_expert, wslot, xs, w_gate, w_up, w_down):
    nt = tile_expert.shape[0]
    tm = TM_MOE
    grid_spec = pltpu.PrefetchScalarGridSpec(
        num_scalar_prefetch=5,
        grid=(nt,),
        in_specs=[pl.BlockSpec((tm, D_MODEL), lambda i, te, nv, *_: (jnp.minimum(i, nv[0] - 1), 0)),
                  pl.BlockSpec(memory_space=pl.ANY),
                  pl.BlockSpec(memory_space=pl.ANY),
                  pl.BlockSpec(memory_space=pl.ANY)],
        out_specs=pl.BlockSpec((tm, D_MODEL), lambda i, *_: (i, 0)),
        scratch_shapes=[pltpu.VMEM((2, D_MODEL, D_EXPERT), F32),
                        pltpu.VMEM((2, D_MODEL, D_EXPERT), F32),
                        pltpu.VMEM((2, D_EXPERT, D_MODEL), F32),
                        pltpu.VMEM((D_MODEL, D_EXPERT), BF16),
                        pltpu.VMEM((D_MODEL, D_EXPERT), BF16),
                        pltpu.VMEM((D_EXPERT, D_MODEL), BF16),
                        pltpu.SemaphoreType.DMA((2,))])
    return pl.pallas_call(
        _moe_kernel,
        out_shape=jax.ShapeDtypeStruct((nt * tm, D_MODEL), F32),
        grid_spec=grid_spec,
        compiler_params=_cparams(("arbitrary",)),
        name="moe",
    )(tile_expert, n_valid, first, next_expert, wslot, xs, w_gate, w_up, w_down)


def _final_kernel(pa0_ref, pb0_ref, pa_ref, pb_ref, x1_ref, rw_ref, mod_ref, y_hbm, o_ref, ybuf, sem, *, tm):
    i = pl.program_id(0)
    n = pl.num_programs(0)
    slot = i & 1

    def start_tile(pa, pb, s):
        for r in range(tm):
            pltpu.make_async_copy(y_hbm.at[pl.ds(pa[0, 0, r], 1)], ybuf.at[s, 0, pl.ds(r, 1)], sem.at[s]).start()
            pltpu.make_async_copy(y_hbm.at[pl.ds(pb[0, 0, r], 1)], ybuf.at[s, 1, pl.ds(r, 1)], sem.at[s]).start()

    @pl.when(i == 0)
    def _():
        start_tile(pa0_ref, pb0_ref, 0)

    @pl.when(i + 1 < n)
    def _():
        start_tile(pa_ref, pb_ref, 1 - slot)

    for k in range(2):
        pltpu.make_async_copy(y_hbm.at[pl.ds(0, tm)], ybuf.at[slot, k], sem.at[slot]).wait()
    w = rw_ref[...]
    moe = w[:, 0:1] * ybuf[slot, 0] + w[:, 1:2] * ybuf[slot, 1]
    o_ref[...] = x1_ref[...] + mod_ref[0][5:6, :] * moe


def _final(x1, rw, y, pos3, mod3, *, tile_base, slot_tiles, cond_base, tiles_per_cond):
    t = x1.shape[0]
    tm = TM_ROW
    n = t // tm
    smem_tile = functools.partial(pl.BlockSpec, (1, 1, tm), memory_space=pltpu.SMEM)
    return pl.pallas_call(
        functools.partial(_final_kernel, tm=tm),
        out_shape=jax.ShapeDtypeStruct((t, D_MODEL), F32),
        grid=(n,),
        in_specs=[smem_tile(lambda i: (tile_base, 0, 0)),
                  smem_tile(lambda i: (slot_tiles + tile_base, 0, 0)),
                  smem_tile(lambda i: (tile_base + jnp.minimum(i + 1, n - 1), 0, 0)),
                  smem_tile(lambda i: (slot_tiles + tile_base + jnp.minimum(i + 1, n - 1), 0, 0)),
                  pl.BlockSpec((tm, D_MODEL), lambda i: (i, 0)),
                  pl.BlockSpec((tm, LANES), lambda i: (i, 0)),
                  pl.BlockSpec((1, N_MOD, D_MODEL), lambda i: (cond_base + i // tiles_per_cond, 0, 0)),
                  pl.BlockSpec(memory_space=pl.ANY)],
        out_specs=pl.BlockSpec((tm, D_MODEL), lambda i: (i, 0)),
        scratch_shapes=[pltpu.VMEM((2, 2, tm, D_MODEL), F32), pltpu.SemaphoreType.DMA((2,))],
        compiler_params=_cparams(("arbitrary",)),
        name="final_ctx" if cond_base == 0 else "final_lat",
    )(pos3, pos3, pos3, pos3, x1, rw, mod3, y)


def _rope_table(n_tokens):
    rows = n_tokens // GRID_W
    inv = ROPE_BASE ** (-jnp.arange(ROPE_PAIRS, dtype=F32) / ROPE_PAIRS)
    row_ang = jnp.arange(rows, dtype=F32)[:, None] * inv
    col_ang = jnp.arange(GRID_W, dtype=F32)[:, None] * inv
    cr, sr = (jnp.repeat(f(row_ang), GRID_W, axis=0) for f in (jnp.cos, jnp.sin))
    cc, sc = (jnp.tile(f(col_ang), (rows, 1)) for f in (jnp.cos, jnp.sin))
    return jnp.concatenate([cr, cr, cc, cc, -sr, sr, -sc, sc], axis=-1)


def _swap_halves(a):
    p = ROPE_PAIRS
    return jnp.concatenate([a[..., p:2 * p], a[..., :p], a[..., 3 * p:], a[..., 2 * p:3 * p]], axis=-1)


def _head_gains(g):
    rope = g[QK_NOPE:]
    return jnp.stack([g[:QK_NOPE], jnp.concatenate([rope, _swap_halves(rope)])])


def kernel(x_prompt, x_sample, cache_ckv, cache_krope, c, c_ctx, ada_w, ada_b, norm1_g, w_in, conv_w,
           q_lora_g, w_uq, kv_lora_g, w_ukv, q_head_g, k_head_g, w_out, norm2_g, router_g, router_e,
           w_gate, w_up, w_down):
    depth = ada_w.shape[0]
    assert depth == 1
    bp, sp, _ = x_prompt.shape
    bs, ss, _ = x_sample.shape
    past = cache_ckv.shape[2]
    tp, ts = bp * sp, bs * ss
    assert sp & (sp - 1) == 0 and ss & (ss - 1) == 0
    assert tp % TM_IN == 0 and ss % TM_IN == 0 and TM_IN % sp == 0 and ss % TQ == 0
    assert bs + 1 <= 8 and past % 256 == 0 and TM_ROW == TM_MOE and tp % TM_ROW == 0 and ts % TM_ROW == 0

    w_in_b = _winprep(w_in[0].T)
    uq = w_uq[0].reshape(Q_LORA, N_HEADS, QK_DIM)
    w_uq_b = jnp.concatenate([uq, _swap_halves(uq[..., QK_NOPE:])], axis=-1)
    w_uq_b = w_uq_b.reshape(Q_LORA, N_HEADS * HEAD_W).astype(BF16)
    ukv = w_ukv[0].reshape(KV_LORA, N_HEADS, QK_NOPE + V_DIM)
    w_uk_b = ukv[..., :QK_NOPE].reshape(KV_LORA, N_HEADS * QK_NOPE).astype(BF16)
    w_uvt_b = ukv[..., QK_NOPE:].reshape(KV_LORA, N_HEADS * V_DIM).T.astype(BF16)
    w_out_b = w_out[0].astype(BF16)
    wr = jnp.concatenate([router_g[0], router_e[0],
                          jnp.zeros((D_MODEL, ROUTER_COLS - N_EXPERT_GROUPS - N_EXPERTS), F32)], axis=1)
    wr_hi = lax.bitcast_convert_type(lax.bitcast_convert_type(wr, jnp.uint32) & jnp.uint32(0xFFFF0000), F32)
    wr = jnp.concatenate([wr_hi, wr - wr_hi], axis=1).astype(BF16)
    gq, gk = _head_gains(q_head_g[0]), _head_gains(k_head_g[0])
    n1g, n2g = norm1_g[0].reshape(1, D_MODEL), norm2_g[0].reshape(1, D_MODEL)
    qlg, kvg = q_lora_g[0].reshape(1, Q_LORA), kv_lora_g[0].reshape(1, KV_LORA)
    cs_lat = _rope_table(ss)
    cs_id = jnp.concatenate([jnp.ones((TM_IN, QK_ROPE), F32), jnp.zeros((TM_IN, QK_ROPE), F32)], axis=1)

    cond8 = jnp.concatenate([c_ctx[None, :], c, jnp.zeros((8 - 1 - bs, D_MODEL), F32)], axis=0)
    mod3 = _modulation(cond8, ada_w[0], ada_b[0]).reshape(8, N_MOD, D_MODEL)

    xp2, xs2 = x_prompt.reshape(tp, D_MODEL), x_sample.reshape(ts, D_MODEL)
    big = 1 << 30

    bg_p, cu_p, q_p, k_p, vt_p, ckv_p, kr_p = _inproj(
        xp2, mod3, n1g, w_in_b, qlg, w_uq_b, kvg, w_uk_b, w_uvt_b, gq, gk, cs_id,
        cond_base=0, tiles_per_cond=big, cs_tiles=1, emit_cache=True)
    o_p = _attention(q_p.reshape(bp, sp, -1), [(k_p.reshape(bp, sp, -1), vt_p)],
                     tq=sp, heads=N_HEADS, name="attn_ctx")
    x1_p, h2_p, rid_p, rw_p = _outproj(
        xp2, bg_p, cu_p, o_p.reshape(tp, -1), mod3, conv_w[0], w_out_b, n2g, wr,
        cond_base=0, tiles_per_cond=big, seq_len=sp)

    kr_c = cache_krope[:, 0].reshape(bs * past, QK_ROPE)
    k_c, vt_c = _kvcache(cache_ckv[:, 0].reshape(bs * past, KV_LORA), jnp.concatenate([kr_c, kr_c], axis=1),
                         cs_id, w_uk_b, w_uvt_b, gk)
    bg_s, cu_s, q_s, k_s, vt_s = _inproj(
        xs2, mod3, n1g, w_in_b, qlg, w_uq_b, kvg, w_uk_b, w_uvt_b, gq, gk, cs_lat,
        cond_base=1, tiles_per_cond=ss // TM_IN, cs_tiles=ss // TM_IN, emit_cache=False)
    o_s = _attention(q_s.reshape(bs, ss, -1),
                     [(k_c.reshape(bs, past, -1), vt_c), (k_s.reshape(bs, ss, -1), vt_s)],
                     tq=TQ, heads=4, name="attn_lat")
    x1_s, h2_s, rid_s, rw_s = _outproj(
        xs2, bg_s, cu_s, o_s.reshape(ts, -1), mod3, conv_w[0], w_out_b, n2g, wr,
        cond_base=1, tiles_per_cond=ss // TM_OUT, seq_len=ss)

    n_tok = tp + ts
    slot_tiles = n_tok // TM_ROW
    nt = 2 * n_tok // TM_MOE + N_EXPERTS
    rid = jnp.concatenate([rid_p[:, :2], rid_s[:, :2]], axis=0).T.reshape(2 * n_tok)
    pos, te, n_valid, pad_start, pad_cnt, first, next_expert, wslot = _route_tables(rid, tm=TM_MOE)
    pos3 = pos.reshape(2 * slot_tiles, 1, TM_ROW)
    xs = _dispatch(pad_start, pad_cnt, n_valid, pos3, h2_p, h2_s, nt=nt)
    y = _moe(te, n_valid, first, next_expert, wslot, xs, w_gate[0], w_up[0], w_down[0])

    y_p = _final(x1_p, rw_p, y, pos3, mod3, tile_base=0, slot_tiles=slot_tiles, cond_base=0, tiles_per_cond=big)
    y_s = _final(x1_s, rw_s, y, pos3, mod3, tile_base=tp // TM_ROW, slot_tiles=slot_tiles, cond_base=1,
                 tiles_per_cond=ss // TM_ROW)

    return (y_p.reshape(bp, sp, D_MODEL), y_s.reshape(bs, ss, D_MODEL),
            ckv_p.reshape(bp, 1, sp, KV_LORA), kr_p.reshape(bp, 1, sp, QK_ROPE))
```

```python
import functools

import jax
import jax.numpy as jnp
from jax import lax
from jax.experimental import pallas as pl
from jax.experimental.pallas import tpu as pltpu

F32 = jnp.float32
BF16 = jnp.bfloat16
HIGHEST = lax.Precision.HIGHEST

D_MODEL = 2048
CONV_WIDTH = 1024
N_HEADS = 8
QK_NOPE = 128
QK_ROPE = 64
V_DIM = 128
QK_DIM = QK_NOPE + QK_ROPE
Q_LORA = 512
KV_LORA = 256
GRID_W = 64
ROPE_PAIRS = QK_ROPE // 4
ROPE_BASE = 10000.0
N_EXPERT_GROUPS = 4
EXPERTS_PER_GROUP = 8
N_EXPERTS = N_EXPERT_GROUPS * EXPERTS_PER_GROUP
D_EXPERT = 512
N_MOD = 6
EPS = 1e-6
LOG2_E = 1.4426950408889634

HEAD_W = 2 * QK_NOPE
LANES = 128
SUBLANES = 8
IN_COLS = 3 * CONV_WIDTH + Q_LORA + KV_LORA + 2 * QK_ROPE
ROUTER_COLS = LANES
VMEM_LIMIT = 56 * 1024 * 1024

TM_IN = 512
TM_OUT = 512
OUT_SLABS = 2
TQ = 256
ATTN_CHUNK = 512
ATTN_AHEAD = 1
TM_MOE = 256
TM_ROW = 256
BN_MOD = 1024
PAD_BITS = tuple(1 << b for b in reversed(range(TM_MOE.bit_length() - 1)))


def _cparams(sem):
    return pltpu.CompilerParams(dimension_semantics=sem, vmem_limit_bytes=VMEM_LIMIT)


def _const_spec(shape):
    nd = len(shape)
    return pl.BlockSpec(shape, lambda *_: (0,) * nd, pipeline_mode=pl.Buffered(1))


def _rms(x):
    return x * lax.rsqrt(jnp.mean(x * x, axis=-1, keepdims=True) + EPS)


def _rowsum(x):
    return jnp.sum(x, axis=-1, keepdims=True)


def _mod_kernel(c_ref, w_ref, b_ref, o_ref):
    c = c_ref[...]
    s = c / (1.0 + jnp.exp(-c))
    s_hi = s.astype(BF16)
    s_lo = (s - s_hi.astype(F32)).astype(BF16)
    w = w_ref[...]
    w_hi = w.astype(BF16)
    w_lo = (w - w_hi.astype(F32)).astype(BF16)
    rows = s.shape[0]
    a = jnp.dot(jnp.concatenate([s_hi, s_lo], axis=0), w_hi, preferred_element_type=F32)
    b = jnp.dot(s_hi, w_lo, preferred_element_type=F32)
    o_ref[...] = a[:rows] + (a[rows:] + b) + b_ref[...]


def _modulation(cond8, ada_w, ada_b):
    n = ada_w.shape[1]
    return pl.pallas_call(
        _mod_kernel,
        out_shape=jax.ShapeDtypeStruct((8, n), F32),
        grid=(n // BN_MOD,),
        in_specs=[pl.BlockSpec((8, D_MODEL), lambda j: (0, 0)),
                  pl.BlockSpec((D_MODEL, BN_MOD), lambda j: (0, j)),
                  pl.BlockSpec((1, BN_MOD), lambda j: (0, j))],
        out_specs=pl.BlockSpec((8, BN_MOD), lambda j: (0, j)),
        compiler_params=_cparams(("arbitrary",)),
        name="mod",
    )(cond8, ada_w, ada_b.reshape(1, n))


def _winprep_kernel(w_ref, o_ref):
    n = w_ref.shape[0]
    o_ref[:n, :] = w_ref[...].astype(BF16)
    p = ROPE_PAIRS
    for dst, src in ((0, p), (p, 0), (2 * p, 3 * p), (3 * p, 2 * p)):
        o_ref[n + dst:n + dst + p, :] = w_ref[n - QK_ROPE + src:n - QK_ROPE + src + p, :].astype(BF16)


def _winprep(w_in_t):
    n, k = w_in_t.shape
    tk = 512
    return pl.pallas_call(
        _winprep_kernel,
        out_shape=jax.ShapeDtypeStruct((IN_COLS, k), BF16),
        grid=(k // tk,),
        in_specs=[pl.BlockSpec((n, tk), lambda i: (0, i))],
        out_specs=pl.BlockSpec((IN_COLS, tk), lambda i: (0, i)),
        compiler_params=_cparams(("arbitrary",)),
        name="winprep",
    )(w_in_t)


def _emit_kv(ckv, kraw, cs, w_uk_ref, w_uvt_ref, gk_ref, k_ref, vt_ref):
    cb = ckv.astype(BF16)
    kn_all = jnp.dot(cb, w_uk_ref[...], preferred_element_type=F32)
    vt = lax.dot_general(w_uvt_ref[...], cb, (((1,), (1,)), ((), ())), preferred_element_type=F32)
    vt_ref[...] = vt.astype(BF16)
    ss_rope = 0.5 * _rowsum(kraw * kraw)
    t = kraw * (cs * gk_ref[1:2, :])
    tt = t + pltpu.roll(t, QK_ROPE, axis=1)
    g_nope = gk_ref[0:1, :]
    for h in range(N_HEADS):
        kn = kn_all[:, h * QK_NOPE:(h + 1) * QK_NOPE]
        r = lax.rsqrt((_rowsum(kn * kn) + ss_rope) * (1.0 / QK_DIM) + EPS)
        k_ref[:, h * HEAD_W:h * HEAD_W + QK_NOPE] = (kn * r * g_nope).astype(BF16)
        k_ref[:, h * HEAD_W + QK_NOPE:(h + 1) * HEAD_W] = (tt * r).astype(BF16)


def _kvcache_kernel(ckv_ref, kraw_ref, cs_ref, w_uk_ref, w_uvt_ref, gk_ref, k_ref, vt_ref):
    _emit_kv(ckv_ref[...], kraw_ref[...], cs_ref[...], w_uk_ref, w_uvt_ref, gk_ref, k_ref, vt_ref)


def _kvcache(ckv, kraw, cs_id, w_uk_b, w_uvt_b, gk):
    n = ckv.shape[0]
    tm = 256
    return pl.pallas_call(
        _kvcache_kernel,
        out_shape=(jax.ShapeDtypeStruct((n, N_HEADS * HEAD_W), BF16),
                   jax.ShapeDtypeStruct((N_HEADS * V_DIM, n), BF16)),
        grid=(n // tm,),
        in_specs=[pl.BlockSpec((tm, KV_LORA), lambda i: (i, 0)),
                  pl.BlockSpec((tm, LANES), lambda i: (i, 0)),
                  pl.BlockSpec((tm, LANES), lambda i: (0, 0)),
                  _const_spec((KV_LORA, N_HEADS * QK_NOPE)),
                  _const_spec((N_HEADS * V_DIM, KV_LORA)),
                  _const_spec((2, LANES))],
        out_specs=(pl.BlockSpec((tm, N_HEADS * HEAD_W), lambda i: (i, 0)),
                   pl.BlockSpec((N_HEADS * V_DIM, tm), lambda i: (0, i))),
        compiler_params=_cparams(("arbitrary",)),
        name="kvcache",
    )(ckv, kraw, cs_id, w_uk_b, w_uvt_b, gk)


def _inproj_kernel(x_ref, mod_ref, n1g_ref, w_in_ref, qlg_ref, w_uq_ref, kvg_ref, w_uk_ref, w_uvt_ref,
                   gq_ref, gk_ref, cs_ref, bg_ref, cu_ref, q_ref, k_ref, vt_ref, *cache_refs):
    x = x_ref[...]
    mod = mod_ref[0]
    h = _rms(x) * n1g_ref[...] * (1.0 + mod[1:2, :]) + mod[0:1, :]
    hb = h.astype(BF16)

    def proj(a, b):
        return lax.dot_general(hb, w_in_ref[a:b, :], (((1,), (1,)), ((), ())), preferred_element_type=F32)

    c1, c2, c3 = CONV_WIDTH, 2 * CONV_WIDTH, 3 * CONV_WIDTH
    q_lat = proj(c3, c3 + Q_LORA)
    kvk = proj(c3 + Q_LORA, IN_COLS)
    bg_ref[...] = proj(0, c1).astype(BF16)
    cs = cs_ref[...]

    qn = _rms(q_lat) * qlg_ref[...]
    q = jnp.dot(qn.astype(BF16), w_uq_ref[...], preferred_element_type=F32)
    scale = QK_DIM ** -0.5 * LOG2_E
    g_nope = gq_ref[0:1, :] * scale
    tq = cs * (gq_ref[1:2, :] * scale)
    for hd in range(N_HEADS):
        lo = q[:, hd * HEAD_W:hd * HEAD_W + QK_NOPE]
        up = q[:, hd * HEAD_W + QK_NOPE:(hd + 1) * HEAD_W]
        ss = _rowsum(lo * lo) + 0.5 * _rowsum(up * up)
        r = lax.rsqrt(ss * (1.0 / QK_DIM) + EPS)
        q_ref[:, hd * HEAD_W:hd * HEAD_W + QK_NOPE] = (lo * r * g_nope).astype(BF16)
        q_ref[:, hd * HEAD_W + QK_NOPE:(hd + 1) * HEAD_W] = (up * r * tq).astype(BF16)

    cu_ref[...] = (proj(c1, c2) * proj(c2, c3)).astype(BF16)

    kv_lat = kvk[:, :KV_LORA]
    kraw = kvk[:, KV_LORA:]
    ckv = _rms(kv_lat) * kvg_ref[...]
    if cache_refs:
        ckv_out_ref, kr_out_ref = cache_refs
        ckv_out_ref[...] = ckv
        kr_out_ref[...] = kraw[:, :QK_ROPE]
    _emit_kv(ckv, kraw, cs, w_uk_ref, w_uvt_ref, gk_ref, k_ref, vt_ref)


def _inproj(x2d, mod3, n1g, w_in_b, qlg, w_uq_b, kvg, w_uk_b, w_uvt_b, gq, gk, cs, *,
            cond_base, tiles_per_cond, cs_tiles, emit_cache):
    t = x2d.shape[0]
    tm = TM_IN
    out_shape = [jax.ShapeDtypeStruct((t, CONV_WIDTH), BF16),
                 jax.ShapeDtypeStruct((t, CONV_WIDTH), BF16),
                 jax.ShapeDtypeStruct((t, N_HEADS * HEAD_W), BF16),
                 jax.ShapeDtypeStruct((t, N_HEADS * HEAD_W), BF16),
                 jax.ShapeDtypeStruct((N_HEADS * V_DIM, t), BF16)]
    out_specs = [pl.BlockSpec((tm, CONV_WIDTH), lambda i: (i, 0)),
                 pl.BlockSpec((tm, CONV_WIDTH), lambda i: (i, 0)),
                 pl.BlockSpec((tm, N_HEADS * HEAD_W), lambda i: (i, 0)),
                 pl.BlockSpec((tm, N_HEADS * HEAD_W), lambda i: (i, 0)),
                 pl.BlockSpec((N_HEADS * V_DIM, tm), lambda i: (0, i))]
    if emit_cache:
        out_shape += [jax.ShapeDtypeStruct((t, KV_LORA), F32), jax.ShapeDtypeStruct((t, QK_ROPE), F32)]
        out_specs += [pl.BlockSpec((tm, KV_LORA), lambda i: (i, 0)),
                      pl.BlockSpec((tm, QK_ROPE), lambda i: (i, 0))]
    return pl.pallas_call(
        _inproj_kernel,
        out_shape=tuple(out_shape),
        grid=(t // tm,),
        in_specs=[pl.BlockSpec((tm, D_MODEL), lambda i: (i, 0)),
                  pl.BlockSpec((1, N_MOD, D_MODEL), lambda i: (cond_base + i // tiles_per_cond, 0, 0)),
                  _const_spec((1, D_MODEL)),
                  _const_spec((IN_COLS, D_MODEL)),
                  _const_spec((1, Q_LORA)),
                  _const_spec((Q_LORA, N_HEADS * HEAD_W)),
                  _const_spec((1, KV_LORA)),
                  _const_spec((KV_LORA, N_HEADS * QK_NOPE)),
                  _const_spec((N_HEADS * V_DIM, KV_LORA)),
                  _const_spec((2, LANES)),
                  _const_spec((2, LANES)),
                  pl.BlockSpec((tm, LANES), lambda i: (i % cs_tiles, 0))],
        out_specs=tuple(out_specs),
        compiler_params=_cparams(("arbitrary",)),
        name="inproj_ctx" if emit_cache else "inproj_lat",
    )(x2d, mod3, n1g, w_in_b, qlg, w_uq_b, kvg, w_uk_b, w_uvt_b, gq, gk, cs)


def _attn_kernel(*refs, n_kv, heads):
    q_ref = refs[0]
    o_ref = refs[-1]
    chunks = []
    for j in range(n_kv):
        sk = refs[1 + 2 * j].shape[1]
        step = min(sk, ATTN_CHUNK)
        chunks += [(j, lo, lo + step) for lo in range(0, sk, step)]
    q = [q_ref[0, :, h * HEAD_W:(h + 1) * HEAD_W] for h in range(heads)]

    def score(h, c):
        j, lo, hi = chunks[c]
        k = refs[1 + 2 * j][0, lo:hi, h * HEAD_W:(h + 1) * HEAD_W]
        return lax.dot_general(k, q[h], (((1,), (1,)), ((), ())), preferred_element_type=F32)

    m = [None] * heads
    acc = [None] * heads
    ahead = [[score(h, c) for c in range(min(ATTN_AHEAD, len(chunks)))] for h in range(heads)]
    for c, (j, lo, hi) in enumerate(chunks):
        for h in range(heads):
            s = ahead[h].pop(0)
            if c + ATTN_AHEAD < len(chunks):
                ahead[h].append(score(h, c + ATTN_AHEAD))
            mc = jnp.max(s, axis=0, keepdims=True)
            m_new = mc if m[h] is None else jnp.maximum(m[h], mc)
            p = jnp.exp2(s - m_new).astype(BF16)
            vt = refs[2 + 2 * j][h * V_DIM:(h + 1) * V_DIM, lo:hi]
            vt1 = jnp.concatenate([vt, jnp.ones((2 * SUBLANES, hi - lo), BF16)], axis=0)
            part = jnp.dot(vt1, p, preferred_element_type=F32)
            acc[h] = part if acc[h] is None else acc[h] * jnp.exp2(m[h] - m_new) + part
            m[h] = m_new
    for h in range(heads):
        ot = acc[h][:V_DIM, :] / acc[h][V_DIM:V_DIM + 1, :]
        o_ref[0, :, h * V_DIM:(h + 1) * V_DIM] = ot.T.astype(BF16)


def _attention(q, kvs, *, tq, heads, name):
    b, s, _ = q.shape
    in_specs = [pl.BlockSpec((1, tq, heads * HEAD_W), lambda bi, hi, qi: (bi, qi, hi))]
    args = [q]
    for k, v in kvs:
        sk = k.shape[1]
        in_specs.append(pl.BlockSpec((1, sk, heads * HEAD_W), lambda bi, hi, qi: (bi, 0, hi)))
        in_specs.append(pl.BlockSpec((heads * V_DIM, sk), lambda bi, hi, qi: (hi, bi)))
        args += [k, v]
    return pl.pallas_call(
        functools.partial(_attn_kernel, n_kv=len(kvs), heads=heads),
        out_shape=jax.ShapeDtypeStruct((b, s, N_HEADS * V_DIM), BF16),
        grid=(b, N_HEADS // heads, s // tq),
        in_specs=in_specs,
        out_specs=pl.BlockSpec((1, tq, heads * V_DIM), lambda bi, hi, qi: (bi, qi, hi)),
        compiler_params=_cparams(("arbitrary", "arbitrary", "arbitrary")),
        name=name,
    )(*args)


def _outproj_kernel(x_ref, bg_ref, cu_ref, cup_ref, cun_ref, o_ref, mod_ref, cw_ref, w_out_ref,
                    n2g_ref, wr_ref, x1_ref, h2_ref, rid_ref, rw_ref, *, tm, seq_len):
    i = pl.program_id(0)
    mod = mod_ref[0]
    cu = cu_ref[...].astype(F32)
    prev_row = cup_ref[...].astype(F32)[15:16, :]
    next_row = cun_ref[...].astype(F32)[0:1, :]
    row = lax.broadcasted_iota(jnp.int32, (tm, 1), 0)
    pos = (i * tm + row) & (seq_len - 1)
    up = jnp.where(row == 0, prev_row, pltpu.roll(cu, 1, axis=0))
    up = jnp.where(pos == 0, 0.0, up)
    dn = jnp.where(row == tm - 1, next_row, pltpu.roll(cu, tm - 1, axis=0))
    dn = jnp.where(pos == seq_len - 1, 0.0, dn)
    cw = cw_ref[...]
    y_conv = bg_ref[...].astype(F32) * (up * cw[0:1, :] + cu * cw[1:2, :] + dn * cw[2:3, :])
    y_conv = y_conv.astype(BF16)

    slabs = [(r, r + tm // OUT_SLABS) for r in range(0, tm, tm // OUT_SLABS)]
    mixes = []
    for lo, hi in slabs:
        mix = jnp.dot(y_conv[lo:hi], w_out_ref[:CONV_WIDTH, :], preferred_element_type=F32)
        mixes.append(mix + jnp.dot(o_ref[lo:hi, :], w_out_ref[CONV_WIDTH:, :], preferred_element_type=F32))
    for (lo, hi), mix in zip(slabs, mixes):
        x1 = x_ref[lo:hi, :] + mod[2:3, :] * mix
        x1_ref[lo:hi, :] = x1
        h2 = _rms(x1) * n2g_ref[...] * (1.0 + mod[4:5, :]) + mod[3:4, :]
        h2_hi = h2.astype(BF16)
        h2_ref[lo:hi, :] = h2_hi

        h2_lo = (h2 - h2_hi.astype(F32)).astype(BF16)
        hh_hl = jnp.dot(h2_hi, wr_ref[...], preferred_element_type=F32)
        lh = jnp.dot(h2_lo, wr_ref[:, :ROUTER_COLS], preferred_element_type=F32)
        logits = hh_hl[:, :ROUTER_COLS] + (hh_hl[:, ROUTER_COLS:] + lh)
        lane = lax.broadcasted_iota(jnp.int32, logits.shape, 1)
        neg = -jnp.inf
        big = jnp.int32(1 << 20)
        gl = jnp.where(lane < N_EXPERT_GROUPS, logits, neg)
        gmax = jnp.max(gl, axis=-1, keepdims=True)
        p_top = 1.0 / _rowsum(jnp.exp(gl - gmax))
        g_top = jnp.min(jnp.where(gl == gmax, lane, big), axis=-1, keepdims=True)
        e_lo = N_EXPERT_GROUPS + EXPERTS_PER_GROUP * g_top
        el = jnp.where((lane >= e_lo) & (lane < e_lo + EXPERTS_PER_GROUP), logits, neg)
        v1 = jnp.max(el, axis=-1, keepdims=True)
        i1 = jnp.min(jnp.where(el == v1, lane, big), axis=-1, keepdims=True)
        el2 = jnp.where(lane == i1, neg, el)
        v2 = jnp.max(el2, axis=-1, keepdims=True)
        i2 = jnp.min(jnp.where(el2 == v2, lane, big), axis=-1, keepdims=True)
        e21 = jnp.exp(v2 - v1)
        w1 = p_top / (1.0 + e21)
        w2 = w1 * e21
        rid_ref[lo:hi, :] = jnp.where(lane == 0, i1 - N_EXPERT_GROUPS, i2 - N_EXPERT_GROUPS)
        rw_ref[lo:hi, :] = jnp.where(lane == 0, w1, w2)


def _outproj(x2d, bg, cu, o2d, mod3, conv_w, w_out_b, n2g, wr, *, cond_base, tiles_per_cond, seq_len):
    t = x2d.shape[0]
    tm = TM_OUT
    hb = tm // 16
    nhb = t // 16
    return pl.pallas_call(
        functools.partial(_outproj_kernel, tm=tm, seq_len=seq_len),
        out_shape=(jax.ShapeDtypeStruct((t, D_MODEL), F32),
                   jax.ShapeDtypeStruct((t, D_MODEL), BF16),
                   jax.ShapeDtypeStruct((t, LANES), jnp.int32),
                   jax.ShapeDtypeStruct((t, LANES), F32)),
        grid=(t // tm,),
        in_specs=[pl.BlockSpec((tm, D_MODEL), lambda i: (i, 0)),
                  pl.BlockSpec((tm, CONV_WIDTH), lambda i: (i, 0)),
                  pl.BlockSpec((tm, CONV_WIDTH), lambda i: (i, 0)),
                  pl.BlockSpec((16, CONV_WIDTH), lambda i: (jnp.maximum(i * hb - 1, 0), 0)),
                  pl.BlockSpec((16, CONV_WIDTH), lambda i: (jnp.minimum((i + 1) * hb, nhb - 1), 0)),
                  pl.BlockSpec((tm, N_HEADS * V_DIM), lambda i: (i, 0)),
                  pl.BlockSpec((1, N_MOD, D_MODEL), lambda i: (cond_base + i // tiles_per_cond, 0, 0)),
                  _const_spec((3, CONV_WIDTH)),
                  _const_spec((D_MODEL, D_MODEL)),
                  _const_spec((1, D_MODEL)),
                  _const_spec((D_MODEL, 2 * ROUTER_COLS))],
        out_specs=(pl.BlockSpec((tm, D_MODEL), lambda i: (i, 0)),
                   pl.BlockSpec((tm, D_MODEL), lambda i: (i, 0)),
                   pl.BlockSpec((tm, LANES), lambda i: (i, 0)),
                   pl.BlockSpec((tm, LANES), lambda i: (i, 0))),
        compiler_params=_cparams(("arbitrary",)),
        name="outproj_ctx" if cond_base == 0 else "outproj_lat",
    )(x2d, bg, cu, cu, cu, o2d, mod3, conv_w, w_out_b, n2g, wr)


def _route_tables(rid, *, tm):
    n_pairs = rid.shape[0]
    nt = n_pairs // tm + N_EXPERTS
    experts = jnp.arange(N_EXPERTS, dtype=jnp.int32)
    onehot = (rid[None, :] == experts[:, None]).astype(jnp.int32)
    csum = jnp.cumsum(onehot, axis=1)
    counts = csum[:, -1]
    tiles_e = (counts + tm - 1) // tm
    tile_end = jnp.cumsum(tiles_e)
    tile_start = tile_end - tiles_e
    n_valid = tile_end[-1]
    pos = jnp.sum(onehot * (csum - 1 + tile_start[:, None] * tm), axis=0)
    tile_raw = jnp.arange(nt, dtype=jnp.int32)
    tile_idx = jnp.minimum(tile_raw, n_valid - 1)
    te = jnp.sum((tile_end[None, :] <= tile_idx[:, None]).astype(jnp.int32), axis=1)
    used = tiles_e > 0
    first = ((tile_raw == tile_start[te]) & (tile_raw < n_valid)).astype(jnp.int32)
    nxt = lax.cummin(jnp.where(used, experts, N_EXPERTS), axis=0, reverse=True)
    nxt = jnp.concatenate([nxt[1:], jnp.full((1,), N_EXPERTS, jnp.int32)])
    next_expert = jnp.where(nxt[te] < N_EXPERTS, nxt[te], -1).astype(jnp.int32)
    wslot = ((jnp.cumsum(used.astype(jnp.int32)) - 1)[te] & 1).astype(jnp.int32)
    return (pos, te, n_valid.reshape(1).astype(jnp.int32),
            (tile_start * tm + counts).astype(jnp.int32), (tiles_e * tm - counts).astype(jnp.int32),
            first, next_expert, wslot)


def _dispatch_kernel(pstart_ref, pcnt_ref, nval_ref, pos0_ref, pos1_ref, h2a_ref, h2b_ref, xs_hbm,
                     buf, zbuf, sem, sem_z, *, tm, nt, n_a):
    i = pl.program_id(0)
    n = pl.num_programs(0)

    def row_copy(r, t, pos_ref, s):
        return pltpu.make_async_copy(buf.at[s, pl.ds(r, 1)], xs_hbm.at[pl.ds(pos_ref[0, 0, t], 1)], sem.at[s])

    def wait_tile(s):
        for _ in range(2):
            pltpu.make_async_copy(buf.at[s], xs_hbm.at[pl.ds(0, tm)], sem.at[s]).wait()

    def pad_copies(e, fn):
        cnt = pcnt_ref[e]
        start = pstart_ref[e]
        off = start + cnt
        for b in PAD_BITS:
            if b < SUBLANES:
                break
            off = off - (cnt & b)
            dst = pl.multiple_of(off, SUBLANES)

            @pl.when((cnt & b) != 0)
            def _():
                fn(pltpu.make_async_copy(zbuf.at[pl.ds(0, b)], xs_hbm.at[pl.ds(dst, b)], sem_z))
        for j in range(SUBLANES - 1):
            @pl.when(j < (cnt & (SUBLANES - 1)))
            def _():
                fn(pltpu.make_async_copy(zbuf.at[pl.ds(0, 1)], xs_hbm.at[pl.ds(start + j, 1)], sem_z))

    def tail_copy(j):
        return pltpu.make_async_copy(zbuf, xs_hbm.at[pl.ds(pl.multiple_of(j * tm, tm), tm)], sem_z)

    @pl.when(i == 0)
    def _():
        zbuf[...] = jnp.zeros((tm, D_MODEL), F32)
        lax.fori_loop(0, N_EXPERTS, lambda e, c: (pad_copies(e, lambda d: d.start()), c)[1], 0)
        lax.fori_loop(nval_ref[0], nt, lambda j, c: (tail_copy(j).start(), c)[1], 0)
        lax.fori_loop(0, N_EXPERTS, lambda e, c: (pad_copies(e, lambda d: d.wait()), c)[1], 0)
        lax.fori_loop(nval_ref[0], nt, lambda j, c: (tail_copy(j).wait(), c)[1], 0)

    for s in range(2):
        @pl.when(i >= 1)
        def _():
            wait_tile(s)

        @pl.when(i < n_a)
        def _():
            buf[s] = h2a_ref[s * tm:(s + 1) * tm, :].astype(F32)

        @pl.when(i >= n_a)
        def _():
            buf[s] = h2b_ref[s * tm:(s + 1) * tm, :].astype(F32)

        for r in range(tm):
            row_copy(r, s * tm + r, pos0_ref, s).start()
            row_copy(r, s * tm + r, pos1_ref, s).start()

    @pl.when(i == n - 1)
    def _():
        wait_tile(0)
        wait_tile(1)


def _dispatch(pad_start, pad_cnt, n_valid, pos3, h2_a, h2_b, *, nt):
    tm = TM_ROW
    n_a, n_b = h2_a.shape[0] // (2 * tm), h2_b.shape[0] // (2 * tm)
    smem_tile = functools.partial(pl.BlockSpec, (1, 1, 2 * tm), memory_space=pltpu.SMEM)
    grid_spec = pltpu.PrefetchScalarGridSpec(
        num_scalar_prefetch=3,
        grid=(n_a + n_b,),
        in_specs=[smem_tile(lambda i, *_: (i, 0, 0)),
                  smem_tile(lambda i, *_: (n_a + n_b + i, 0, 0)),
                  pl.BlockSpec((2 * tm, D_MODEL), lambda i, *_: (jnp.minimum(i, n_a - 1), 0)),
                  pl.BlockSpec((2 * tm, D_MODEL), lambda i, *_: (jnp.maximum(i - n_a, 0), 0))],
        out_specs=pl.BlockSpec(memory_space=pl.ANY),
        scratch_shapes=[pltpu.VMEM((2, tm, D_MODEL), F32),
                        pltpu.VMEM((tm, D_MODEL), F32),
                        pltpu.SemaphoreType.DMA((2,)),
                        pltpu.SemaphoreType.DMA])
    return pl.pallas_call(
        functools.partial(_dispatch_kernel, tm=tm, nt=nt, n_a=n_a),
        out_shape=jax.ShapeDtypeStruct((nt * TM_MOE, D_MODEL), F32),
        grid_spec=grid_spec,
        compiler_params=pltpu.CompilerParams(dimension_semantics=("arbitrary",),
                                             vmem_limit_bytes=VMEM_LIMIT, has_side_effects=True),
        name="dispatch",
    )(pad_start, pad_cnt, n_valid, pos3, pos3, h2_a, h2_b)


def _moe_kernel(texp_ref, nval_ref, first_ref, next_ref, wslot_ref, xs_ref, wg_hbm, wu_hbm, wd_hbm, y_ref,
                wg32, wu32, wd32, wgb, wub, wdb, sem_w):
    i = pl.program_id(0)

    def weight_copies(e, s):
        return (pltpu.make_async_copy(wg_hbm.at[e], wg32.at[s], sem_w.at[s]),
                pltpu.make_async_copy(wu_hbm.at[e], wu32.at[s], sem_w.at[s]),
                pltpu.make_async_copy(wd_hbm.at[e], wd32.at[s], sem_w.at[s]))

    @pl.when(i == 0)
    def _():
        for d in weight_copies(texp_ref[0], 0):
            d.start()

    @pl.when(i < nval_ref[0])
    def _():
        @pl.when(first_ref[i] == 1)
        def _():
            s = wslot_ref[i]
            for d in weight_copies(texp_ref[i], s):
                d.wait()
            e_next = next_ref[i]

            @pl.when(e_next >= 0)
            def _():
                for d in weight_copies(e_next, 1 - s):
                    d.start()
            wgb[...] = wg32[s].astype(BF16)
            wub[...] = wu32[s].astype(BF16)
            wdb[...] = wd32[s].astype(BF16)

        x = xs_ref[...].astype(BF16)
        g = jnp.dot(x, wgb[...], preferred_element_type=F32)
        u = jnp.dot(x, wub[...], preferred_element_type=F32)
        a = g / (1.0 + jnp.exp(-g)) * u
        y_ref[...] = jnp.dot(a.astype(BF16), wdb[...], preferred_element_type=F32)

    @pl.when(i >= nval_ref[0])
    def _():
        y_ref[...] = jnp.zeros(y_ref.shape, F32)


def _moe(tile_expert, n_valid, first, next_expert, wslot, xs, w_gate, w_up, w_down):
    nt = tile_expert.shape[0]
    tm = TM_MOE
    grid_spec = pltpu.PrefetchScalarGridSpec(
        num_scalar_prefetch=5,
        grid=(nt,),
        in_specs=[pl.BlockSpec((tm, D_MODEL), lambda i, te, nv, *_: (jnp.minimum(i, nv[0] - 1), 0)),
                  pl.BlockSpec(memory_space=pl.ANY),
                  pl.BlockSpec(memory_space=pl.ANY),
                  pl.BlockSpec(memory_space=pl.ANY)],
        out_specs=pl.BlockSpec((tm, D_MODEL), lambda i, *_: (i, 0)),
        scratch_shapes=[pltpu.VMEM((2, D_MODEL, D_EXPERT), F32),
                        pltpu.VMEM((2, D_MODEL, D_EXPERT), F32),
                        pltpu.VMEM((2, D_EXPERT, D_MODEL), F32),
                        pltpu.VMEM((D_MODEL, D_EXPERT), BF16),
                        pltpu.VMEM((D_MODEL, D_EXPERT), BF16),
                        pltpu.VMEM((D_EXPERT, D_MODEL), BF16),
                        pltpu.SemaphoreType.DMA((2,))])
    return pl.pallas_call(
        _moe_kernel,
        out_shape=jax.ShapeDtypeStruct((nt * tm, D_MODEL), F32),
        grid_spec=grid_spec,
        compiler_params=_cparams(("arbitrary",)),
        name="moe",
    )(tile_expert, n_valid, first, next_expert, wslot, xs, w_gate, w_up, w_down)


def _final_kernel(pa0_ref, pb0_ref, pa_ref, pb_ref, x1_ref, rw_ref, mod_ref, y_hbm, o_ref, ybuf, sem, *, tm):
    i = pl.program_id(0)
    n = pl.num_programs(0)

    def start_tile(pa, pb, s):
        for r in range(tm):
            t = s * tm + r
            pltpu.make_async_copy(y_hbm.at[pl.ds(pa[0, 0, t], 1)], ybuf.at[s, 0, pl.ds(r, 1)], sem.at[s]).start()
            pltpu.make_async_copy(y_hbm.at[pl.ds(pb[0, 0, t], 1)], ybuf.at[s, 1, pl.ds(r, 1)], sem.at[s]).start()

    @pl.when(i == 0)
    def _():
        start_tile(pa0_ref, pb0_ref, 0)
        start_tile(pa0_ref, pb0_ref, 1)

    gate = mod_ref[0][5:6, :]
    for s in range(2):
        rows = slice(s * tm, (s + 1) * tm)
        for k in range(2):
            pltpu.make_async_copy(y_hbm.at[pl.ds(0, tm)], ybuf.at[s, k], sem.at[s]).wait()
        w = rw_ref[rows, :]
        moe = w[:, 0:1] * ybuf[s, 0] + w[:, 1:2] * ybuf[s, 1]
        o_ref[rows, :] = x1_ref[rows, :] + gate * moe

        @pl.when(i + 1 < n)
        def _():
            start_tile(pa_ref, pb_ref, s)


def _final(x1, rw, y, pos3, mod3, *, tile_base, slot_tiles, cond_base, tiles_per_cond):
    t = x1.shape[0]
    tm = TM_ROW
    n = t // (2 * tm)
    smem_tile = functools.partial(pl.BlockSpec, (1, 1, 2 * tm), memory_space=pltpu.SMEM)
    return pl.pallas_call(
        functools.partial(_final_kernel, tm=tm),
        out_shape=jax.ShapeDtypeStruct((t, D_MODEL), F32),
        grid=(n,),
        in_specs=[smem_tile(lambda i: (tile_base, 0, 0)),
                  smem_tile(lambda i: (slot_tiles + tile_base, 0, 0)),
                  smem_tile(lambda i: (tile_base + jnp.minimum(i + 1, n - 1), 0, 0)),
                  smem_tile(lambda i: (slot_tiles + tile_base + jnp.minimum(i + 1, n - 1), 0, 0)),
                  pl.BlockSpec((2 * tm, D_MODEL), lambda i: (i, 0)),
                  pl.BlockSpec((2 * tm, LANES), lambda i: (i, 0)),
                  pl.BlockSpec((1, N_MOD, D_MODEL), lambda i: (cond_base + i // tiles_per_cond, 0, 0)),
                  pl.BlockSpec(memory_space=pl.ANY)],
        out_specs=pl.BlockSpec((2 * tm, D_MODEL), lambda i: (i, 0)),
        scratch_shapes=[pltpu.VMEM((2, 2, tm, D_MODEL), F32), pltpu.SemaphoreType.DMA((2,))],
        compiler_params=_cparams(("arbitrary",)),
        name="final_ctx" if cond_base == 0 else "final_lat",
    )(pos3, pos3, pos3, pos3, x1, rw, mod3, y)


def _rope_table(n_tokens):
    rows = n_tokens // GRID_W
    inv = ROPE_BASE ** (-jnp.arange(ROPE_PAIRS, dtype=F32) / ROPE_PAIRS)
    row_ang = jnp.arange(rows, dtype=F32)[:, None] * inv
    col_ang = jnp.arange(GRID_W, dtype=F32)[:, None] * inv
    cr, sr = (jnp.repeat(f(row_ang), GRID_W, axis=0) for f in (jnp.cos, jnp.sin))
    cc, sc = (jnp.tile(f(col_ang), (rows, 1)) for f in (jnp.cos, jnp.sin))
    return jnp.concatenate([cr, cr, cc, cc, -sr, sr, -sc, sc], axis=-1)


def _swap_halves(a):
    p = ROPE_PAIRS
    return jnp.concatenate([a[..., p:2 * p], a[..., :p], a[..., 3 * p:], a[..., 2 * p:3 * p]], axis=-1)


def _head_gains(g):
    rope = g[QK_NOPE:]
    return jnp.stack([g[:QK_NOPE], jnp.concatenate([rope, _swap_halves(rope)])])


def kernel(x_prompt, x_sample, cache_ckv, cache_krope, c, c_ctx, ada_w, ada_b, norm1_g, w_in, conv_w,
           q_lora_g, w_uq, kv_lora_g, w_ukv, q_head_g, k_head_g, w_out, norm2_g, router_g, router_e,
           w_gate, w_up, w_down):
    depth = ada_w.shape[0]
    assert depth == 1
    bp, sp, _ = x_prompt.shape
    bs, ss, _ = x_sample.shape
    past = cache_ckv.shape[2]
    tp, ts = bp * sp, bs * ss
    assert sp & (sp - 1) == 0 and ss & (ss - 1) == 0
    assert tp % TM_IN == 0 and ss % TM_IN == 0 and TM_IN % sp == 0 and ss % TQ == 0
    assert bs + 1 <= 8 and past % 256 == 0 and TM_ROW == TM_MOE and tp % (2 * TM_ROW) == 0 and ss % (2 * TM_ROW) == 0

    w_in_b = _winprep(w_in[0].T)
    uq = w_uq[0].reshape(Q_LORA, N_HEADS, QK_DIM)
    w_uq_b = jnp.concatenate([uq, _swap_halves(uq[..., QK_NOPE:])], axis=-1)
    w_uq_b = w_uq_b.reshape(Q_LORA, N_HEADS * HEAD_W).astype(BF16)
    ukv = w_ukv[0].reshape(KV_LORA, N_HEADS, QK_NOPE + V_DIM)
    w_uk_b = ukv[..., :QK_NOPE].reshape(KV_LORA, N_HEADS * QK_NOPE).astype(BF16)
    w_uvt_b = ukv[..., QK_NOPE:].reshape(KV_LORA, N_HEADS * V_DIM).T.astype(BF16)
    w_out_b = w_out[0].astype(BF16)
    wr = jnp.concatenate([router_g[0], router_e[0],
                          jnp.zeros((D_MODEL, ROUTER_COLS - N_EXPERT_GROUPS - N_EXPERTS), F32)], axis=1)
    wr_hi = lax.bitcast_convert_type(lax.bitcast_convert_type(wr, jnp.uint32) & jnp.uint32(0xFFFF0000), F32)
    wr = jnp.concatenate([wr_hi, wr - wr_hi], axis=1).astype(BF16)
    gq, gk = _head_gains(q_head_g[0]), _head_gains(k_head_g[0])
    n1g, n2g = norm1_g[0].reshape(1, D_MODEL), norm2_g[0].reshape(1, D_MODEL)
    qlg, kvg = q_lora_g[0].reshape(1, Q_LORA), kv_lora_g[0].reshape(1, KV_LORA)
    cs_lat = _rope_table(ss)
    cs_id = jnp.concatenate([jnp.ones((TM_IN, QK_ROPE), F32), jnp.zeros((TM_IN, QK_ROPE), F32)], axis=1)

    cond8 = jnp.concatenate([c_ctx[None, :], c, jnp.zeros((8 - 1 - bs, D_MODEL), F32)], axis=0)
    mod3 = _modulation(cond8, ada_w[0], ada_b[0]).reshape(8, N_MOD, D_MODEL)

    xp2, xs2 = x_prompt.reshape(tp, D_MODEL), x_sample.reshape(ts, D_MODEL)
    big = 1 << 30

    bg_p, cu_p, q_p, k_p, vt_p, ckv_p, kr_p = _inproj(
        xp2, mod3, n1g, w_in_b, qlg, w_uq_b, kvg, w_uk_b, w_uvt_b, gq, gk, cs_id,
        cond_base=0, tiles_per_cond=big, cs_tiles=1, emit_cache=True)
    o_p = _attention(q_p.reshape(bp, sp, -1), [(k_p.reshape(bp, sp, -1), vt_p)],
                     tq=sp, heads=N_HEADS, name="attn_ctx")
    x1_p, h2_p, rid_p, rw_p = _outproj(
        xp2, bg_p, cu_p, o_p.reshape(tp, -1), mod3, conv_w[0], w_out_b, n2g, wr,
        cond_base=0, tiles_per_cond=big, seq_len=sp)

    kr_c = cache_krope[:, 0].reshape(bs * past, QK_ROPE)
    k_c, vt_c = _kvcache(cache_ckv[:, 0].reshape(bs * past, KV_LORA), jnp.concatenate([kr_c, kr_c], axis=1),
                         cs_id, w_uk_b, w_uvt_b, gk)
    bg_s, cu_s, q_s, k_s, vt_s = _inproj(
        xs2, mod3, n1g, w_in_b, qlg, w_uq_b, kvg, w_uk_b, w_uvt_b, gq, gk, cs_lat,
        cond_base=1, tiles_per_cond=ss // TM_IN, cs_tiles=ss // TM_IN, emit_cache=False)
    o_s = _attention(q_s.reshape(bs, ss, -1),
                     [(k_c.reshape(bs, past, -1), vt_c), (k_s.reshape(bs, ss, -1), vt_s)],
                     tq=TQ, heads=4, name="attn_lat")
    x1_s, h2_s, rid_s, rw_s = _outproj(
        xs2, bg_s, cu_s, o_s.reshape(ts, -1), mod3, conv_w[0], w_out_b, n2g, wr,
        cond_base=1, tiles_per_cond=ss // TM_OUT, seq_len=ss)

    n_tok = tp + ts
    row_block = 2 * TM_ROW
    slot_tiles = n_tok // row_block
    nt = 2 * n_tok // TM_MOE + N_EXPERTS
    rid = jnp.concatenate([rid_p[:, :2], rid_s[:, :2]], axis=0).T.reshape(2 * n_tok)
    pos, te, n_valid, pad_start, pad_cnt, first, next_expert, wslot = _route_tables(rid, tm=TM_MOE)
    pos3 = pos.reshape(2 * slot_tiles, 1, row_block)
    xs = _dispatch(pad_start, pad_cnt, n_valid, pos3, h2_p, h2_s, nt=nt)
    y = _moe(te, n_valid, first, next_expert, wslot, xs, w_gate[0], w_up[0], w_down[0])

    y_p = _final(x1_p, rw_p, y, pos3, mod3, tile_base=0, slot_tiles=slot_tiles, cond_base=0, tiles_per_cond=big)
    y_s = _final(x1_s, rw_s, y, pos3, mod3, tile_base=tp // row_block, slot_tiles=slot_tiles, cond_base=1,
                 tiles_per_cond=ss // row_block)

    return (y_p.reshape(bp, sp, D_MODEL), y_s.reshape(bs, ss, D_MODEL),
            ckv_p.reshape(bp, 1, sp, KV_LORA), kr_p.reshape(bp, 1, sp, QK_ROPE))
```

```python
import functools

import jax
import jax.numpy as jnp
from jax import lax
from jax.experimental import pallas as pl
from jax.experimental.pallas import tpu as pltpu

F32 = jnp.float32
BF16 = jnp.bfloat16
HIGHEST = lax.Precision.HIGHEST

D_MODEL = 2048
CONV_WIDTH = 1024
N_HEADS = 8
QK_NOPE = 128
QK_ROPE = 64
V_DIM = 128
QK_DIM = QK_NOPE + QK_ROPE
Q_LORA = 512
KV_LORA = 256
GRID_W = 64
ROPE_PAIRS = QK_ROPE // 4
ROPE_BASE = 10000.0
N_EXPERT_GROUPS = 4
EXPERTS_PER_GROUP = 8
N_EXPERTS = N_EXPERT_GROUPS * EXPERTS_PER_GROUP
D_EXPERT = 512
N_MOD = 6
EPS = 1e-6
LOG2_E = 1.4426950408889634

HEAD_W = 2 * QK_NOPE
LANES = 128
SUBLANES = 8
IN_COLS = 3 * CONV_WIDTH + Q_LORA + KV_LORA + 2 * QK_ROPE
ROUTER_COLS = LANES
VMEM_LIMIT = 56 * 1024 * 1024

TM_IN = 512
TM_OUT = 512
OUT_SLABS = 2
TQ = 256
ATTN_CHUNK = 512
ATTN_AHEAD = 2
TM_MOE = 256
TM_ROW = 256
BN_MOD = 1024
PAD_BITS = tuple(1 << b for b in reversed(range(TM_MOE.bit_length() - 1)))


def _cparams(sem):
    return pltpu.CompilerParams(dimension_semantics=sem, vmem_limit_bytes=VMEM_LIMIT)


def _const_spec(shape):
    nd = len(shape)
    return pl.BlockSpec(shape, lambda *_: (0,) * nd, pipeline_mode=pl.Buffered(1))


def _rms(x):
    return x * lax.rsqrt(jnp.mean(x * x, axis=-1, keepdims=True) + EPS)


def _rowsum(x):
    return jnp.sum(x, axis=-1, keepdims=True)


def _mod_kernel(c_ref, w_ref, b_ref, o_ref):
    c = c_ref[...]
    s = c / (1.0 + jnp.exp(-c))
    s_hi = s.astype(BF16)
    s_lo = (s - s_hi.astype(F32)).astype(BF16)
    w = w_ref[...]
    w_hi = w.astype(BF16)
    w_lo = (w - w_hi.astype(F32)).astype(BF16)
    rows = s.shape[0]
    a = jnp.dot(jnp.concatenate([s_hi, s_lo], axis=0), w_hi, preferred_element_type=F32)
    b = jnp.dot(s_hi, w_lo, preferred_element_type=F32)
    o_ref[...] = a[:rows] + (a[rows:] + b) + b_ref[...]


def _modulation(cond8, ada_w, ada_b):
    n = ada_w.shape[1]
    return pl.pallas_call(
        _mod_kernel,
        out_shape=jax.ShapeDtypeStruct((8, n), F32),
        grid=(n // BN_MOD,),
        in_specs=[pl.BlockSpec((8, D_MODEL), lambda j: (0, 0)),
                  pl.BlockSpec((D_MODEL, BN_MOD), lambda j: (0, j)),
                  pl.BlockSpec((1, BN_MOD), lambda j: (0, j))],
        out_specs=pl.BlockSpec((8, BN_MOD), lambda j: (0, j)),
        compiler_params=_cparams(("arbitrary",)),
        name="mod",
    )(cond8, ada_w, ada_b.reshape(1, n))


def _winprep_kernel(w_ref, o_ref):
    n = w_ref.shape[0]
    o_ref[:n, :] = w_ref[...].astype(BF16)
    p = ROPE_PAIRS
    for dst, src in ((0, p), (p, 0), (2 * p, 3 * p), (3 * p, 2 * p)):
        o_ref[n + dst:n + dst + p, :] = w_ref[n - QK_ROPE + src:n - QK_ROPE + src + p, :].astype(BF16)


def _winprep(w_in_t):
    n, k = w_in_t.shape
    tk = 512
    return pl.pallas_call(
        _winprep_kernel,
        out_shape=jax.ShapeDtypeStruct((IN_COLS, k), BF16),
        grid=(k // tk,),
        in_specs=[pl.BlockSpec((n, tk), lambda i: (0, i))],
        out_specs=pl.BlockSpec((IN_COLS, tk), lambda i: (0, i)),
        compiler_params=_cparams(("arbitrary",)),
        name="winprep",
    )(w_in_t)


def _emit_kv(ckv, kraw, cs, w_uk_ref, w_uvt_ref, gk_ref, k_ref, vt_ref):
    cb = ckv.astype(BF16)
    kn_all = jnp.dot(cb, w_uk_ref[...], preferred_element_type=F32)
    vt = lax.dot_general(w_uvt_ref[...], cb, (((1,), (1,)), ((), ())), preferred_element_type=F32)
    vt_ref[...] = vt.astype(BF16)
    ss_rope = 0.5 * _rowsum(kraw * kraw)
    t = kraw * (cs * gk_ref[1:2, :])
    tt = t + pltpu.roll(t, QK_ROPE, axis=1)
    g_nope = gk_ref[0:1, :]
    for h in range(N_HEADS):
        kn = kn_all[:, h * QK_NOPE:(h + 1) * QK_NOPE]
        r = lax.rsqrt((_rowsum(kn * kn) + ss_rope) * (1.0 / QK_DIM) + EPS)
        k_ref[:, h * HEAD_W:h * HEAD_W + QK_NOPE] = (kn * r * g_nope).astype(BF16)
        k_ref[:, h * HEAD_W + QK_NOPE:(h + 1) * HEAD_W] = (tt * r).astype(BF16)


def _kvcache_kernel(ckv_ref, kraw_ref, cs_ref, w_uk_ref, w_uvt_ref, gk_ref, k_ref, vt_ref):
    _emit_kv(ckv_ref[...], kraw_ref[...], cs_ref[...], w_uk_ref, w_uvt_ref, gk_ref, k_ref, vt_ref)


def _kvcache(ckv, kraw, cs_id, w_uk_b, w_uvt_b, gk):
    n = ckv.shape[0]
    tm = 256
    return pl.pallas_call(
        _kvcache_kernel,
        out_shape=(jax.ShapeDtypeStruct((n, N_HEADS * HEAD_W), BF16),
                   jax.ShapeDtypeStruct((N_HEADS * V_DIM, n), BF16)),
        grid=(n // tm,),
        in_specs=[pl.BlockSpec((tm, KV_LORA), lambda i: (i, 0)),
                  pl.BlockSpec((tm, LANES), lambda i: (i, 0)),
                  pl.BlockSpec((tm, LANES), lambda i: (0, 0)),
                  _const_spec((KV_LORA, N_HEADS * QK_NOPE)),
                  _const_spec((N_HEADS * V_DIM, KV_LORA)),
                  _const_spec((2, LANES))],
        out_specs=(pl.BlockSpec((tm, N_HEADS * HEAD_W), lambda i: (i, 0)),
                   pl.BlockSpec((N_HEADS * V_DIM, tm), lambda i: (0, i))),
        compiler_params=_cparams(("arbitrary",)),
        name="kvcache",
    )(ckv, kraw, cs_id, w_uk_b, w_uvt_b, gk)


def _inproj_kernel(x_ref, mod_ref, n1g_ref, w_in_ref, qlg_ref, w_uq_ref, kvg_ref, w_uk_ref, w_uvt_ref,
                   gq_ref, gk_ref, cs_ref, bg_ref, cu_ref, q_ref, k_ref, vt_ref, *cache_refs):
    x = x_ref[...]
    mod = mod_ref[0]
    h = _rms(x) * n1g_ref[...] * (1.0 + mod[1:2, :]) + mod[0:1, :]
    hb = h.astype(BF16)

    def proj(a, b):
        return lax.dot_general(hb, w_in_ref[a:b, :], (((1,), (1,)), ((), ())), preferred_element_type=F32)

    c1, c2, c3 = CONV_WIDTH, 2 * CONV_WIDTH, 3 * CONV_WIDTH
    q_lat = proj(c3, c3 + Q_LORA)
    kvk = proj(c3 + Q_LORA, IN_COLS)
    bg_ref[...] = proj(0, c1).astype(BF16)
    cs = cs_ref[...]

    qn = _rms(q_lat) * qlg_ref[...]
    q = jnp.dot(qn.astype(BF16), w_uq_ref[...], preferred_element_type=F32)
    scale = QK_DIM ** -0.5 * LOG2_E
    g_nope = gq_ref[0:1, :] * scale
    tq = cs * (gq_ref[1:2, :] * scale)
    for hd in range(N_HEADS):
        lo = q[:, hd * HEAD_W:hd * HEAD_W + QK_NOPE]
        up = q[:, hd * HEAD_W + QK_NOPE:(hd + 1) * HEAD_W]
        ss = _rowsum(lo * lo) + 0.5 * _rowsum(up * up)
        r = lax.rsqrt(ss * (1.0 / QK_DIM) + EPS)
        q_ref[:, hd * HEAD_W:hd * HEAD_W + QK_NOPE] = (lo * r * g_nope).astype(BF16)
        q_ref[:, hd * HEAD_W + QK_NOPE:(hd + 1) * HEAD_W] = (up * r * tq).astype(BF16)

    cu_ref[...] = (proj(c1, c2) * proj(c2, c3)).astype(BF16)

    kv_lat = kvk[:, :KV_LORA]
    kraw = kvk[:, KV_LORA:]
    ckv = _rms(kv_lat) * kvg_ref[...]
    if cache_refs:
        ckv_out_ref, kr_out_ref = cache_refs
        ckv_out_ref[...] = ckv
        kr_out_ref[...] = kraw[:, :QK_ROPE]
    _emit_kv(ckv, kraw, cs, w_uk_ref, w_uvt_ref, gk_ref, k_ref, vt_ref)


def _inproj(x2d, mod3, n1g, w_in_b, qlg, w_uq_b, kvg, w_uk_b, w_uvt_b, gq, gk, cs, *,
            cond_base, tiles_per_cond, cs_tiles, emit_cache):
    t = x2d.shape[0]
    tm = TM_IN
    out_shape = [jax.ShapeDtypeStruct((t, CONV_WIDTH), BF16),
                 jax.ShapeDtypeStruct((t, CONV_WIDTH), BF16),
                 jax.ShapeDtypeStruct((t, N_HEADS * HEAD_W), BF16),
                 jax.ShapeDtypeStruct((t, N_HEADS * HEAD_W), BF16),
                 jax.ShapeDtypeStruct((N_HEADS * V_DIM, t), BF16)]
    out_specs = [pl.BlockSpec((tm, CONV_WIDTH), lambda i: (i, 0)),
                 pl.BlockSpec((tm, CONV_WIDTH), lambda i: (i, 0)),
                 pl.BlockSpec((tm, N_HEADS * HEAD_W), lambda i: (i, 0)),
                 pl.BlockSpec((tm, N_HEADS * HEAD_W), lambda i: (i, 0)),
                 pl.BlockSpec((N_HEADS * V_DIM, tm), lambda i: (0, i))]
    if emit_cache:
        out_shape += [jax.ShapeDtypeStruct((t, KV_LORA), F32), jax.ShapeDtypeStruct((t, QK_ROPE), F32)]
        out_specs += [pl.BlockSpec((tm, KV_LORA), lambda i: (i, 0)),
                      pl.BlockSpec((tm, QK_ROPE), lambda i: (i, 0))]
    return pl.pallas_call(
        _inproj_kernel,
        out_shape=tuple(out_shape),
        grid=(t // tm,),
        in_specs=[pl.BlockSpec((tm, D_MODEL), lambda i: (i, 0)),
                  pl.BlockSpec((1, N_MOD, D_MODEL), lambda i: (cond_base + i // tiles_per_cond, 0, 0)),
                  _const_spec((1, D_MODEL)),
                  _const_spec((IN_COLS, D_MODEL)),
                  _const_spec((1, Q_LORA)),
                  _const_spec((Q_LORA, N_HEADS * HEAD_W)),
                  _const_spec((1, KV_LORA)),
                  _const_spec((KV_LORA, N_HEADS * QK_NOPE)),
                  _const_spec((N_HEADS * V_DIM, KV_LORA)),
                  _const_spec((2, LANES)),
                  _const_spec((2, LANES)),
                  pl.BlockSpec((tm, LANES), lambda i: (i % cs_tiles, 0))],
        out_specs=tuple(out_specs),
        compiler_params=_cparams(("arbitrary",)),
        name="inproj_ctx" if emit_cache else "inproj_lat",
    )(x2d, mod3, n1g, w_in_b, qlg, w_uq_b, kvg, w_uk_b, w_uvt_b, gq, gk, cs)


def _attn_kernel(*refs, n_kv, heads):
    q_ref = refs[0]
    o_ref = refs[-1]
    chunks = []
    for j in range(n_kv):
        sk = refs[1 + 2 * j].shape[1]
        step = min(sk, ATTN_CHUNK)
        chunks += [(j, lo, lo + step) for lo in range(0, sk, step)]
    q = [q_ref[0, :, h * HEAD_W:(h + 1) * HEAD_W] for h in range(heads)]

    def score(h, c):
        j, lo, hi = chunks[c]
        k = refs[1 + 2 * j][0, lo:hi, h * HEAD_W:(h + 1) * HEAD_W]
        return lax.dot_general(k, q[h], (((1,), (1,)), ((), ())), preferred_element_type=F32)

    m = [None] * heads
    acc = [None] * heads
    ahead = [[score(h, c) for c in range(min(ATTN_AHEAD, len(chunks)))] for h in range(heads)]
    for c, (j, lo, hi) in enumerate(chunks):
        for h in range(heads):
            s = ahead[h].pop(0)
            if c + ATTN_AHEAD < len(chunks):
                ahead[h].append(score(h, c + ATTN_AHEAD))
            mc = jnp.max(s, axis=0, keepdims=True)
            m_new = mc if m[h] is None else jnp.maximum(m[h], mc)
            p = jnp.exp2(s - m_new).astype(BF16)
            vt = refs[2 + 2 * j][h * V_DIM:(h + 1) * V_DIM, lo:hi]
            vt1 = jnp.concatenate([vt, jnp.ones((2 * SUBLANES, hi - lo), BF16)], axis=0)
            part = jnp.dot(vt1, p, preferred_element_type=F32)
            acc[h] = part if acc[h] is None else acc[h] * jnp.exp2(m[h] - m_new) + part
            m[h] = m_new
    for h in range(heads):
        ot = acc[h][:V_DIM, :] / acc[h][V_DIM:V_DIM + 1, :]
        o_ref[0, :, h * V_DIM:(h + 1) * V_DIM] = ot.T.astype(BF16)


def _attention(q, kvs, *, tq, heads, name):
    b, s, _ = q.shape
    in_specs = [pl.BlockSpec((1, tq, heads * HEAD_W), lambda bi, hi, qi: (bi, qi, hi))]
    args = [q]
    for k, v in kvs:
        sk = k.shape[1]
        in_specs.append(pl.BlockSpec((1, sk, heads * HEAD_W), lambda bi, hi, qi: (bi, 0, hi)))
        in_specs.append(pl.BlockSpec((heads * V_DIM, sk), lambda bi, hi, qi: (hi, bi)))
        args += [k, v]
    return pl.pallas_call(
        functools.partial(_attn_kernel, n_kv=len(kvs), heads=heads),
        out_shape=jax.ShapeDtypeStruct((b, s, N_HEADS * V_DIM), BF16),
        grid=(b, N_HEADS // heads, s // tq),
        in_specs=in_specs,
        out_specs=pl.BlockSpec((1, tq, heads * V_DIM), lambda bi, hi, qi: (bi, qi, hi)),
        compiler_params=_cparams(("arbitrary", "arbitrary", "arbitrary")),
        name=name,
    )(*args)


def _outproj_kernel(x_ref, bg_ref, cu_ref, cup_ref, cun_ref, o_ref, mod_ref, cw_ref, w_out_ref,
                    n2g_ref, wr_ref, x1_ref, h2_ref, rid_ref, rw_ref, *, tm, seq_len):
    i = pl.program_id(0)
    mod = mod_ref[0]
    cu = cu_ref[...].astype(F32)
    prev_row = cup_ref[...].astype(F32)[15:16, :]
    next_row = cun_ref[...].astype(F32)[0:1, :]
    row = lax.broadcasted_iota(jnp.int32, (tm, 1), 0)
    pos = (i * tm + row) & (seq_len - 1)
    up = jnp.where(row == 0, prev_row, pltpu.roll(cu, 1, axis=0))
    up = jnp.where(pos == 0, 0.0, up)
    dn = jnp.where(row == tm - 1, next_row, pltpu.roll(cu, tm - 1, axis=0))
    dn = jnp.where(pos == seq_len - 1, 0.0, dn)
    cw = cw_ref[...]
    y_conv = bg_ref[...].astype(F32) * (up * cw[0:1, :] + cu * cw[1:2, :] + dn * cw[2:3, :])
    y_conv = y_conv.astype(BF16)

    slabs = [(r, r + tm // OUT_SLABS) for r in range(0, tm, tm // OUT_SLABS)]
    mixes = []
    for lo, hi in slabs:
        mix = jnp.dot(y_conv[lo:hi], w_out_ref[:CONV_WIDTH, :], preferred_element_type=F32)
        mixes.append(mix + jnp.dot(o_ref[lo:hi, :], w_out_ref[CONV_WIDTH:, :], preferred_element_type=F32))
    for (lo, hi), mix in zip(slabs, mixes):
        x1 = x_ref[lo:hi, :] + mod[2:3, :] * mix
        x1_ref[lo:hi, :] = x1
        h2 = _rms(x1) * n2g_ref[...] * (1.0 + mod[4:5, :]) + mod[3:4, :]
        h2_hi = h2.astype(BF16)
        h2_ref[lo:hi, :] = h2_hi

        h2_lo = (h2 - h2_hi.astype(F32)).astype(BF16)
        hh_hl = jnp.dot(h2_hi, wr_ref[...], preferred_element_type=F32)
        lh = jnp.dot(h2_lo, wr_ref[:, :ROUTER_COLS], preferred_element_type=F32)
        logits = hh_hl[:, :ROUTER_COLS] + (hh_hl[:, ROUTER_COLS:] + lh)
        lane = lax.broadcasted_iota(jnp.int32, logits.shape, 1)
        neg = -jnp.inf
        big = jnp.int32(1 << 20)
        gl = jnp.where(lane < N_EXPERT_GROUPS, logits, neg)
        gmax = jnp.max(gl, axis=-1, keepdims=True)
        p_top = 1.0 / _rowsum(jnp.exp(gl - gmax))
        g_top = jnp.min(jnp.where(gl == gmax, lane, big), axis=-1, keepdims=True)
        e_lo = N_EXPERT_GROUPS + EXPERTS_PER_GROUP * g_top
        el = jnp.where((lane >= e_lo) & (lane < e_lo + EXPERTS_PER_GROUP), logits, neg)
        v1 = jnp.max(el, axis=-1, keepdims=True)
        i1 = jnp.min(jnp.where(el == v1, lane, big), axis=-1, keepdims=True)
        el2 = jnp.where(lane == i1, neg, el)
        v2 = jnp.max(el2, axis=-1, keepdims=True)
        i2 = jnp.min(jnp.where(el2 == v2, lane, big), axis=-1, keepdims=True)
        e21 = jnp.exp(v2 - v1)
        w1 = p_top / (1.0 + e21)
        w2 = w1 * e21
        rid_ref[lo:hi, :] = jnp.where(lane == 0, i1 - N_EXPERT_GROUPS, i2 - N_EXPERT_GROUPS)
        rw_ref[lo:hi, :] = jnp.where(lane == 0, w1, w2)


def _outproj(x2d, bg, cu, o2d, mod3, conv_w, w_out_b, n2g, wr, *, cond_base, tiles_per_cond, seq_len):
    t = x2d.shape[0]
    tm = TM_OUT
    hb = tm // 16
    nhb = t // 16
    return pl.pallas_call(
        functools.partial(_outproj_kernel, tm=tm, seq_len=seq_len),
        out_shape=(jax.ShapeDtypeStruct((t, D_MODEL), F32),
                   jax.ShapeDtypeStruct((t, D_MODEL), BF16),
                   jax.ShapeDtypeStruct((t, LANES), jnp.int32),
                   jax.ShapeDtypeStruct((t, LANES), F32)),
        grid=(t // tm,),
        in_specs=[pl.BlockSpec((tm, D_MODEL), lambda i: (i, 0)),
                  pl.BlockSpec((tm, CONV_WIDTH), lambda i: (i, 0)),
                  pl.BlockSpec((tm, CONV_WIDTH), lambda i: (i, 0)),
                  pl.BlockSpec((16, CONV_WIDTH), lambda i: (jnp.maximum(i * hb - 1, 0), 0)),
                  pl.BlockSpec((16, CONV_WIDTH), lambda i: (jnp.minimum((i + 1) * hb, nhb - 1), 0)),
                  pl.BlockSpec((tm, N_HEADS * V_DIM), lambda i: (i, 0)),
                  pl.BlockSpec((1, N_MOD, D_MODEL), lambda i: (cond_base + i // tiles_per_cond, 0, 0)),
                  _const_spec((3, CONV_WIDTH)),
                  _const_spec((D_MODEL, D_MODEL)),
                  _const_spec((1, D_MODEL)),
                  _const_spec((D_MODEL, 2 * ROUTER_COLS))],
        out_specs=(pl.BlockSpec((tm, D_MODEL), lambda i: (i, 0)),
                   pl.BlockSpec((tm, D_MODEL), lambda i: (i, 0)),
                   pl.BlockSpec((tm, LANES), lambda i: (i, 0)),
                   pl.BlockSpec((tm, LANES), lambda i: (i, 0))),
        compiler_params=_cparams(("arbitrary",)),
        name="outproj_ctx" if cond_base == 0 else "outproj_lat",
    )(x2d, bg, cu, cu, cu, o2d, mod3, conv_w, w_out_b, n2g, wr)


def _route_tables(rid, *, tm):
    n_pairs = rid.shape[0]
    nt = n_pairs // tm + N_EXPERTS
    experts = jnp.arange(N_EXPERTS, dtype=jnp.int32)
    onehot = (rid[None, :] == experts[:, None]).astype(jnp.int32)
    csum = jnp.cumsum(onehot, axis=1)
    counts = csum[:, -1]
    tiles_e = (counts + tm - 1) // tm
    tile_end = jnp.cumsum(tiles_e)
    tile_start = tile_end - tiles_e
    n_valid = tile_end[-1]
    pos = jnp.sum(onehot * (csum - 1 + tile_start[:, None] * tm), axis=0)
    tile_raw = jnp.arange(nt, dtype=jnp.int32)
    tile_idx = jnp.minimum(tile_raw, n_valid - 1)
    te = jnp.sum((tile_end[None, :] <= tile_idx[:, None]).astype(jnp.int32), axis=1)
    used = tiles_e > 0
    first = ((tile_raw == tile_start[te]) & (tile_raw < n_valid)).astype(jnp.int32)
    nxt = lax.cummin(jnp.where(used, experts, N_EXPERTS), axis=0, reverse=True)
    nxt = jnp.concatenate([nxt[1:], jnp.full((1,), N_EXPERTS, jnp.int32)])
    next_expert = jnp.where(nxt[te] < N_EXPERTS, nxt[te], -1).astype(jnp.int32)
    wslot = ((jnp.cumsum(used.astype(jnp.int32)) - 1)[te] & 1).astype(jnp.int32)
    return (pos, te, n_valid.reshape(1).astype(jnp.int32),
            (tile_start * tm + counts).astype(jnp.int32), (tiles_e * tm - counts).astype(jnp.int32),
            first, next_expert, wslot)


def _dispatch_kernel(pstart_ref, pcnt_ref, nval_ref, pos0_ref, pos1_ref, h2a_ref, h2b_ref, xs_hbm,
                     buf, zbuf, sem, sem_z, *, tm, nt, n_a):
    i = pl.program_id(0)
    n = pl.num_programs(0)
    slot = i & 1

    def row_copy(r, pos_ref, s):
        return pltpu.make_async_copy(buf.at[s, pl.ds(r, 1)], xs_hbm.at[pl.ds(pos_ref[0, 0, r], 1)], sem.at[s])

    def wait_tile(s):
        for _ in range(2):
            pltpu.make_async_copy(buf.at[s], xs_hbm.at[pl.ds(0, tm)], sem.at[s]).wait()

    def pad_copies(e, fn):
        cnt = pcnt_ref[e]
        start = pstart_ref[e]
        off = start + cnt
        for b in PAD_BITS:
            if b < SUBLANES:
                break
            off = off - (cnt & b)
            dst = pl.multiple_of(off, SUBLANES)

            @pl.when((cnt & b) != 0)
            def _():
                fn(pltpu.make_async_copy(zbuf.at[pl.ds(0, b)], xs_hbm.at[pl.ds(dst, b)], sem_z))
        for j in range(SUBLANES - 1):
            @pl.when(j < (cnt & (SUBLANES - 1)))
            def _():
                fn(pltpu.make_async_copy(zbuf.at[pl.ds(0, 1)], xs_hbm.at[pl.ds(start + j, 1)], sem_z))

    def tail_copy(j):
        return pltpu.make_async_copy(zbuf, xs_hbm.at[pl.ds(pl.multiple_of(j * tm, tm), tm)], sem_z)

    @pl.when(i == 0)
    def _():
        zbuf[...] = jnp.zeros((tm, D_MODEL), F32)
        lax.fori_loop(0, N_EXPERTS, lambda e, c: (pad_copies(e, lambda d: d.start()), c)[1], 0)
        lax.fori_loop(nval_ref[0], nt, lambda j, c: (tail_copy(j).start(), c)[1], 0)
        lax.fori_loop(0, N_EXPERTS, lambda e, c: (pad_copies(e, lambda d: d.wait()), c)[1], 0)
        lax.fori_loop(nval_ref[0], nt, lambda j, c: (tail_copy(j).wait(), c)[1], 0)

    @pl.when(i >= 2)
    def _():
        wait_tile(slot)

    @pl.when(i < n_a)
    def _():
        buf[slot] = h2a_ref[...].astype(F32)

    @pl.when(i >= n_a)
    def _():
        buf[slot] = h2b_ref[...].astype(F32)

    for r in range(tm):
        row_copy(r, pos0_ref, slot).start()
        row_copy(r, pos1_ref, slot).start(priority=1)

    @pl.when(i == n - 1)
    def _():
        wait_tile(slot)
        wait_tile(1 - slot)


def _dispatch(pad_start, pad_cnt, n_valid, pos3, h2_a, h2_b, *, nt):
    tm = TM_ROW
    n_a, n_b = h2_a.shape[0] // tm, h2_b.shape[0] // tm
    smem_tile = functools.partial(pl.BlockSpec, (1, 1, tm), memory_space=pltpu.SMEM)
    grid_spec = pltpu.PrefetchScalarGridSpec(
        num_scalar_prefetch=3,
        grid=(n_a + n_b,),
        in_specs=[smem_tile(lambda i, *_: (i, 0, 0)),
                  smem_tile(lambda i, *_: (n_a + n_b + i, 0, 0)),
                  pl.BlockSpec((tm, D_MODEL), lambda i, *_: (jnp.minimum(i, n_a - 1), 0)),
                  pl.BlockSpec((tm, D_MODEL), lambda i, *_: (jnp.maximum(i - n_a, 0), 0))],
        out_specs=pl.BlockSpec(memory_space=pl.ANY),
        scratch_shapes=[pltpu.VMEM((2, tm, D_MODEL), F32),
                        pltpu.VMEM((tm, D_MODEL), F32),
                        pltpu.SemaphoreType.DMA((2,)),
                        pltpu.SemaphoreType.DMA])
    return pl.pallas_call(
        functools.partial(_dispatch_kernel, tm=tm, nt=nt, n_a=n_a),
        out_shape=jax.ShapeDtypeStruct((nt * TM_MOE, D_MODEL), F32),
        grid_spec=grid_spec,
        compiler_params=pltpu.CompilerParams(dimension_semantics=("arbitrary",),
                                             vmem_limit_bytes=VMEM_LIMIT, has_side_effects=True),
        name="dispatch",
    )(pad_start, pad_cnt, n_valid, pos3, pos3, h2_a, h2_b)


def _moe_kernel(texp_ref, nval_ref, first_ref, next_ref, wslot_ref, xs_ref, wg_hbm, wu_hbm, wd_hbm, y_ref,
                wg32, wu32, wd32, wgb, wub, wdb, sem_w):
    i = pl.program_id(0)

    def weight_copies(e, s):
        return (pltpu.make_async_copy(wg_hbm.at[e], wg32.at[s], sem_w.at[s]),
                pltpu.make_async_copy(wu_hbm.at[e], wu32.at[s], sem_w.at[s]),
                pltpu.make_async_copy(wd_hbm.at[e], wd32.at[s], sem_w.at[s]))

    @pl.when(i == 0)
    def _():
        for d in weight_copies(texp_ref[0], 0):
            d.start()

    @pl.when(i < nval_ref[0])
    def _():
        @pl.when(first_ref[i] == 1)
        def _():
            s = wslot_ref[i]
            for d in weight_copies(texp_ref[i], s):
                d.wait()
            e_next = next_ref[i]

            @pl.when(e_next >= 0)
            def _():
                for d in weight_copies(e_next, 1 - s):
                    d.start()
            wgb[...] = wg32[s].astype(BF16)
            wub[...] = wu32[s].astype(BF16)
            wdb[...] = wd32[s].astype(BF16)

        x = xs_ref[...].astype(BF16)
        g = jnp.dot(x, wgb[...], preferred_element_type=F32)
        u = jnp.dot(x, wub[...], preferred_element_type=F32)
        a = g / (1.0 + jnp.exp(-g)) * u
        y_ref[...] = jnp.dot(a.astype(BF16), wdb[...], preferred_element_type=F32)

    @pl.when(i >= nval_ref[0])
    def _():
        y_ref[...] = jnp.zeros(y_ref.shape, F32)


def _moe(tile_expert, n_valid, first, next_expert, wslot, xs, w_gate, w_up, w_down):
    nt = tile_expert.shape[0]
    tm = TM_MOE
    grid_spec = pltpu.PrefetchScalarGridSpec(
        num_scalar_prefetch=5,
        grid=(nt,),
        in_specs=[pl.BlockSpec((tm, D_MODEL), lambda i, te, nv, *_: (jnp.minimum(i, nv[0] - 1), 0)),
                  pl.BlockSpec(memory_space=pl.ANY),
                  pl.BlockSpec(memory_space=pl.ANY),
                  pl.BlockSpec(memory_space=pl.ANY)],
        out_specs=pl.BlockSpec((tm, D_MODEL), lambda i, *_: (i, 0)),
        scratch_shapes=[pltpu.VMEM((2, D_MODEL, D_EXPERT), F32),
                        pltpu.VMEM((2, D_MODEL, D_EXPERT), F32),
                        pltpu.VMEM((2, D_EXPERT, D_MODEL), F32),
                        pltpu.VMEM((D_MODEL, D_EXPERT), BF16),
                        pltpu.VMEM((D_MODEL, D_EXPERT), BF16),
                        pltpu.VMEM((D_EXPERT, D_MODEL), BF16),
                        pltpu.SemaphoreType.DMA((2,))])
    return pl.pallas_call(
        _moe_kernel,
        out_shape=jax.ShapeDtypeStruct((nt * tm, D_MODEL), F32),
        grid_spec=grid_spec,
        compiler_params=_cparams(("arbitrary",)),
        name="moe",
    )(tile_expert, n_valid, first, next_expert, wslot, xs, w_gate, w_up, w_down)


def _final_kernel(pa0_ref, pb0_ref, pa_ref, pb_ref, x1_ref, rw_ref, mod_ref, y_hbm, o_ref, ybuf, sem, *, tm):
    i = pl.program_id(0)
    n = pl.num_programs(0)
    slot = i & 1

    def start_tile(pa, pb, s):
        for r in range(tm):
            pltpu.make_async_copy(y_hbm.at[pl.ds(pa[0, 0, r], 1)], ybuf.at[s, 0, pl.ds(r, 1)], sem.at[s]).start()
            pltpu.make_async_copy(y_hbm.at[pl.ds(pb[0, 0, r], 1)], ybuf.at[s, 1, pl.ds(r, 1)],
                                  sem.at[s]).start(priority=1)

    @pl.when(i == 0)
    def _():
        start_tile(pa0_ref, pb0_ref, 0)

    @pl.when(i + 1 < n)
    def _():
        start_tile(pa_ref, pb_ref, 1 - slot)

    for k in range(2):
        pltpu.make_async_copy(y_hbm.at[pl.ds(0, tm)], ybuf.at[slot, k], sem.at[slot]).wait()
    w = rw_ref[...]
    moe = w[:, 0:1] * ybuf[slot, 0] + w[:, 1:2] * ybuf[slot, 1]
    o_ref[...] = x1_ref[...] + mod_ref[0][5:6, :] * moe


def _final(x1, rw, y, pos3, mod3, *, tile_base, slot_tiles, cond_base, tiles_per_cond):
    t = x1.shape[0]
    tm = TM_ROW
    n = t // tm
    smem_tile = functools.partial(pl.BlockSpec, (1, 1, tm), memory_space=pltpu.SMEM)
    return pl.pallas_call(
        functools.partial(_final_kernel, tm=tm),
        out_shape=jax.ShapeDtypeStruct((t, D_MODEL), F32),
        grid=(n,),
        in_specs=[smem_tile(lambda i: (tile_base, 0, 0)),
                  smem_tile(lambda i: (slot_tiles + tile_base, 0, 0)),
                  smem_tile(lambda i: (tile_base + jnp.minimum(i + 1, n - 1), 0, 0)),
                  smem_tile(lambda i: (slot_tiles + tile_base + jnp.minimum(i + 1, n - 1), 0, 0)),
                  pl.BlockSpec((tm, D_MODEL), lambda i: (i, 0)),
                  pl.BlockSpec((tm, LANES), lambda i: (i, 0)),
                  pl.BlockSpec((1, N_MOD, D_MODEL), lambda i: (cond_base + i // tiles_per_cond, 0, 0)),
                  pl.BlockSpec(memory_space=pl.ANY)],
        out_specs=pl.BlockSpec((tm, D_MODEL), lambda i: (i, 0)),
        scratch_shapes=[pltpu.VMEM((2, 2, tm, D_MODEL), F32), pltpu.SemaphoreType.DMA((2,))],
        compiler_params=_cparams(("arbitrary",)),
        name="final_ctx" if cond_base == 0 else "final_lat",
    )(pos3, pos3, pos3, pos3, x1, rw, mod3, y)


def _rope_table(n_tokens):
    rows = n_tokens // GRID_W
    inv = ROPE_BASE ** (-jnp.arange(ROPE_PAIRS, dtype=F32) / ROPE_PAIRS)
    row_ang = jnp.arange(rows, dtype=F32)[:, None] * inv
    col_ang = jnp.arange(GRID_W, dtype=F32)[:, None] * inv
    cr, sr, cc, sc = jnp.cos(row_ang), jnp.sin(row_ang), jnp.cos(col_ang), jnp.sin(col_ang)
    zr, zc = jnp.zeros_like(cr), jnp.zeros_like(cc)
    row_t = jnp.concatenate([cr, cr, zr, zr, -sr, sr, zr, zr], axis=-1)
    col_t = jnp.concatenate([zc, zc, cc, cc, zc, zc, -sc, sc], axis=-1)
    return (row_t[:, None, :] + col_t[None, :, :]).reshape(n_tokens, 2 * QK_ROPE)


def _swap_halves(a):
    p = ROPE_PAIRS
    return jnp.concatenate([a[..., p:2 * p], a[..., :p], a[..., 3 * p:], a[..., 2 * p:3 * p]], axis=-1)


def _head_gains(g):
    rope = g[QK_NOPE:]
    return jnp.stack([g[:QK_NOPE], jnp.concatenate([rope, _swap_halves(rope)])])


def kernel(x_prompt, x_sample, cache_ckv, cache_krope, c, c_ctx, ada_w, ada_b, norm1_g, w_in, conv_w,
           q_lora_g, w_uq, kv_lora_g, w_ukv, q_head_g, k_head_g, w_out, norm2_g, router_g, router_e,
           w_gate, w_up, w_down):
    depth = ada_w.shape[0]
    assert depth == 1
    bp, sp, _ = x_prompt.shape
    bs, ss, _ = x_sample.shape
    past = cache_ckv.shape[2]
    tp, ts = bp * sp, bs * ss
    assert sp & (sp - 1) == 0 and ss & (ss - 1) == 0
    assert tp % TM_IN == 0 and ss % TM_IN == 0 and TM_IN % sp == 0 and ss % TQ == 0
    assert bs + 1 <= 8 and past % 256 == 0 and TM_ROW == TM_MOE and tp % TM_ROW == 0 and ts % TM_ROW == 0

    w_in_b = _winprep(w_in[0].T)
    uq = w_uq[0].reshape(Q_LORA, N_HEADS, QK_DIM)
    w_uq_b = jnp.concatenate([uq, _swap_halves(uq[..., QK_NOPE:])], axis=-1)
    w_uq_b = w_uq_b.reshape(Q_LORA, N_HEADS * HEAD_W).astype(BF16)
    ukv = w_ukv[0].reshape(KV_LORA, N_HEADS, QK_NOPE + V_DIM)
    w_uk_b = ukv[..., :QK_NOPE].reshape(KV_LORA, N_HEADS * QK_NOPE).astype(BF16)
    w_uvt_b = ukv[..., QK_NOPE:].reshape(KV_LORA, N_HEADS * V_DIM).T.astype(BF16)
    w_out_b = w_out[0].astype(BF16)
    wr = jnp.concatenate([router_g[0], router_e[0],
                          jnp.zeros((D_MODEL, ROUTER_COLS - N_EXPERT_GROUPS - N_EXPERTS), F32)], axis=1)
    wr_hi = lax.bitcast_convert_type(lax.bitcast_convert_type(wr, jnp.uint32) & jnp.uint32(0xFFFF0000), F32)
    wr = jnp.concatenate([wr_hi, wr - wr_hi], axis=1).astype(BF16)
    gq, gk = _head_gains(q_head_g[0]), _head_gains(k_head_g[0])
    n1g, n2g = norm1_g[0].reshape(1, D_MODEL), norm2_g[0].reshape(1, D_MODEL)
    qlg, kvg = q_lora_g[0].reshape(1, Q_LORA), kv_lora_g[0].reshape(1, KV_LORA)
    cs_lat = _rope_table(ss)
    cs_id = jnp.concatenate([jnp.ones((TM_IN, QK_ROPE), F32), jnp.zeros((TM_IN, QK_ROPE), F32)], axis=1)

    cond8 = jnp.concatenate([c_ctx[None, :], c, jnp.zeros((8 - 1 - bs, D_MODEL), F32)], axis=0)
    mod3 = _modulation(cond8, ada_w[0], ada_b[0]).reshape(8, N_MOD, D_MODEL)

    xp2, xs2 = x_prompt.reshape(tp, D_MODEL), x_sample.reshape(ts, D_MODEL)
    big = 1 << 30

    bg_p, cu_p, q_p, k_p, vt_p, ckv_p, kr_p = _inproj(
        xp2, mod3, n1g, w_in_b, qlg, w_uq_b, kvg, w_uk_b, w_uvt_b, gq, gk, cs_id,
        cond_base=0, tiles_per_cond=big, cs_tiles=1, emit_cache=True)
    o_p = _attention(q_p.reshape(bp, sp, -1), [(k_p.reshape(bp, sp, -1), vt_p)],
                     tq=sp, heads=N_HEADS, name="attn_ctx")
    x1_p, h2_p, rid_p, rw_p = _outproj(
        xp2, bg_p, cu_p, o_p.reshape(tp, -1), mod3, conv_w[0], w_out_b, n2g, wr,
        cond_base=0, tiles_per_cond=big, seq_len=sp)

    kr_c = cache_krope[:, 0].reshape(bs * past, QK_ROPE)
    k_c, vt_c = _kvcache(cache_ckv[:, 0].reshape(bs * past, KV_LORA), jnp.concatenate([kr_c, kr_c], axis=1),
                         cs_id, w_uk_b, w_uvt_b, gk)
    bg_s, cu_s, q_s, k_s, vt_s = _inproj(
        xs2, mod3, n1g, w_in_b, qlg, w_uq_b, kvg, w_uk_b, w_uvt_b, gq, gk, cs_lat,
        cond_base=1, tiles_per_cond=ss // TM_IN, cs_tiles=ss // TM_IN, emit_cache=False)
    o_s = _attention(q_s.reshape(bs, ss, -1),
                     [(k_c.reshape(bs, past, -1), vt_c), (k_s.reshape(bs, ss, -1), vt_s)],
                     tq=TQ, heads=4, name="attn_lat")
    x1_s, h2_s, rid_s, rw_s = _outproj(
        xs2, bg_s, cu_s, o_s.reshape(ts, -1), mod3, conv_w[0], w_out_b, n2g, wr,
        cond_base=1, tiles_per_cond=ss // TM_OUT, seq_len=ss)

    n_tok = tp + ts
    slot_tiles = n_tok // TM_ROW
    nt = 2 * n_tok // TM_MOE + N_EXPERTS
    rid = jnp.concatenate([rid_p[:, :2], rid_s[:, :2]], axis=0).T.reshape(2 * n_tok)
    pos, te, n_valid, pad_start, pad_cnt, first, next_expert, wslot = _route_tables(rid, tm=TM_MOE)
    pos3 = pos.reshape(2 * slot_tiles, 1, TM_ROW)
    xs = _dispatch(pad_start, pad_cnt, n_valid, pos3, h2_p, h2_s, nt=nt)
    y = _moe(te, n_valid, first, next_expert, wslot, xs, w_gate[0], w_up[0], w_down[0])

    y_p = _final(x1_p, rw_p, y, pos3, mod3, tile_base=0, slot_tiles=slot_tiles, cond_base=0, tiles_per_cond=big)
    y_s = _final(x1_s, rw_s, y, pos3, mod3, tile_base=tp // TM_ROW, slot_tiles=slot_tiles, cond_base=1,
                 tiles_per_cond=ss // TM_ROW)

    return (y_p.reshape(bp, sp, D_MODEL), y_s.reshape(bs, ss, D_MODEL),
            ckv_p.reshape(bp, 1, sp, KV_LORA), kr_p.reshape(bp, 1, sp, QK_ROPE))
```

```python
import functools

import jax
import jax.numpy as jnp
from jax import lax
from jax.experimental import pallas as pl
from jax.experimental.pallas import tpu as pltpu

F32 = jnp.float32
BF16 = jnp.bfloat16
HIGHEST = lax.Precision.HIGHEST

D_MODEL = 2048
CONV_WIDTH = 1024
N_HEADS = 8
QK_NOPE = 128
QK_ROPE = 64
V_DIM = 128
QK_DIM = QK_NOPE + QK_ROPE
Q_LORA = 512
KV_LORA = 256
GRID_W = 64
ROPE_PAIRS = QK_ROPE // 4
ROPE_BASE = 10000.0
N_EXPERT_GROUPS = 4
EXPERTS_PER_GROUP = 8
N_EXPERTS = N_EXPERT_GROUPS * EXPERTS_PER_GROUP
D_EXPERT = 512
N_MOD = 6
EPS = 1e-6
LOG2_E = 1.4426950408889634

HEAD_W = 2 * QK_NOPE
LANES = 128
SUBLANES = 8
IN_COLS = 3 * CONV_WIDTH + Q_LORA + KV_LORA + 2 * QK_ROPE
ROUTER_COLS = LANES
VMEM_LIMIT = 56 * 1024 * 1024

TM_IN = 512
TM_OUT = 512
OUT_SLABS = 2
TQ = 256
ATTN_CHUNK = 512
ATTN_AHEAD = 2
TM_MOE = 256
XS_DEPTH = 4
TM_ROW = 256
BN_MOD = 1024
PAD_BITS = tuple(1 << b for b in reversed(range(TM_MOE.bit_length() - 1)))


def _cparams(sem):
    return pltpu.CompilerParams(dimension_semantics=sem, vmem_limit_bytes=VMEM_LIMIT)


def _const_spec(shape):
    nd = len(shape)
    return pl.BlockSpec(shape, lambda *_: (0,) * nd, pipeline_mode=pl.Buffered(1))


def _rms(x):
    return x * lax.rsqrt(jnp.mean(x * x, axis=-1, keepdims=True) + EPS)


def _rowsum(x):
    return jnp.sum(x, axis=-1, keepdims=True)


def _mod_kernel(c_ref, w_ref, b_ref, o_ref):
    c = c_ref[...]
    s = c / (1.0 + jnp.exp(-c))
    s_hi = s.astype(BF16)
    s_lo = (s - s_hi.astype(F32)).astype(BF16)
    w = w_ref[...]
    w_hi = w.astype(BF16)
    w_lo = (w - w_hi.astype(F32)).astype(BF16)
    rows = s.shape[0]
    a = jnp.dot(jnp.concatenate([s_hi, s_lo], axis=0), w_hi, preferred_element_type=F32)
    b = jnp.dot(s_hi, w_lo, preferred_element_type=F32)
    o_ref[...] = a[:rows] + (a[rows:] + b) + b_ref[...]


def _modulation(cond8, ada_w, ada_b):
    n = ada_w.shape[1]
    return pl.pallas_call(
        _mod_kernel,
        out_shape=jax.ShapeDtypeStruct((8, n), F32),
        grid=(n // BN_MOD,),
        in_specs=[pl.BlockSpec((8, D_MODEL), lambda j: (0, 0)),
                  pl.BlockSpec((D_MODEL, BN_MOD), lambda j: (0, j)),
                  pl.BlockSpec((1, BN_MOD), lambda j: (0, j))],
        out_specs=pl.BlockSpec((8, BN_MOD), lambda j: (0, j)),
        compiler_params=_cparams(("arbitrary",)),
        name="mod",
    )(cond8, ada_w, ada_b.reshape(1, n))


def _winprep_kernel(w_ref, o_ref):
    n = w_ref.shape[0]
    o_ref[:n, :] = w_ref[...].astype(BF16)
    p = ROPE_PAIRS
    for dst, src in ((0, p), (p, 0), (2 * p, 3 * p), (3 * p, 2 * p)):
        o_ref[n + dst:n + dst + p, :] = w_ref[n - QK_ROPE + src:n - QK_ROPE + src + p, :].astype(BF16)


def _winprep(w_in_t):
    n, k = w_in_t.shape
    tk = 512
    return pl.pallas_call(
        _winprep_kernel,
        out_shape=jax.ShapeDtypeStruct((IN_COLS, k), BF16),
        grid=(k // tk,),
        in_specs=[pl.BlockSpec((n, tk), lambda i: (0, i))],
        out_specs=pl.BlockSpec((IN_COLS, tk), lambda i: (0, i)),
        compiler_params=_cparams(("arbitrary",)),
        name="winprep",
    )(w_in_t)


def _emit_kv(ckv, kraw, cs, w_uk_ref, w_uvt_ref, gk_ref, k_ref, vt_ref):
    cb = ckv.astype(BF16)
    kn_all = jnp.dot(cb, w_uk_ref[...], preferred_element_type=F32)
    vt = lax.dot_general(w_uvt_ref[...], cb, (((1,), (1,)), ((), ())), preferred_element_type=F32)
    vt_ref[...] = vt.astype(BF16)
    ss_rope = 0.5 * _rowsum(kraw * kraw)
    t = kraw * (cs * gk_ref[1:2, :])
    tt = t + pltpu.roll(t, QK_ROPE, axis=1)
    g_nope = gk_ref[0:1, :]
    for h in range(N_HEADS):
        kn = kn_all[:, h * QK_NOPE:(h + 1) * QK_NOPE]
        r = lax.rsqrt((_rowsum(kn * kn) + ss_rope) * (1.0 / QK_DIM) + EPS)
        k_ref[:, h * HEAD_W:h * HEAD_W + QK_NOPE] = (kn * r * g_nope).astype(BF16)
        k_ref[:, h * HEAD_W + QK_NOPE:(h + 1) * HEAD_W] = (tt * r).astype(BF16)


def _kvcache_kernel(ckv_ref, kraw_ref, cs_ref, w_uk_ref, w_uvt_ref, gk_ref, k_ref, vt_ref):
    _emit_kv(ckv_ref[...], kraw_ref[...], cs_ref[...], w_uk_ref, w_uvt_ref, gk_ref, k_ref, vt_ref)


def _kvcache(ckv, kraw, cs_id, w_uk_b, w_uvt_b, gk):
    n = ckv.shape[0]
    tm = 256
    return pl.pallas_call(
        _kvcache_kernel,
        out_shape=(jax.ShapeDtypeStruct((n, N_HEADS * HEAD_W), BF16),
                   jax.ShapeDtypeStruct((N_HEADS * V_DIM, n), BF16)),
        grid=(n // tm,),
        in_specs=[pl.BlockSpec((tm, KV_LORA), lambda i: (i, 0)),
                  pl.BlockSpec((tm, LANES), lambda i: (i, 0)),
                  pl.BlockSpec((tm, LANES), lambda i: (0, 0)),
                  _const_spec((KV_LORA, N_HEADS * QK_NOPE)),
                  _const_spec((N_HEADS * V_DIM, KV_LORA)),
                  _const_spec((2, LANES))],
        out_specs=(pl.BlockSpec((tm, N_HEADS * HEAD_W), lambda i: (i, 0)),
                   pl.BlockSpec((N_HEADS * V_DIM, tm), lambda i: (0, i))),
        compiler_params=_cparams(("arbitrary",)),
        name="kvcache",
    )(ckv, kraw, cs_id, w_uk_b, w_uvt_b, gk)


def _inproj_kernel(x_ref, mod_ref, n1g_ref, w_in_ref, qlg_ref, w_uq_ref, kvg_ref, w_uk_ref, w_uvt_ref,
                   gq_ref, gk_ref, cs_ref, bg_ref, cu_ref, q_ref, k_ref, vt_ref, *cache_refs):
    x = x_ref[...]
    mod = mod_ref[0]
    h = _rms(x) * n1g_ref[...] * (1.0 + mod[1:2, :]) + mod[0:1, :]
    hb = h.astype(BF16)

    def proj(a, b):
        return lax.dot_general(hb, w_in_ref[a:b, :], (((1,), (1,)), ((), ())), preferred_element_type=F32)

    c1, c2, c3 = CONV_WIDTH, 2 * CONV_WIDTH, 3 * CONV_WIDTH
    q_lat = proj(c3, c3 + Q_LORA)
    kvk = proj(c3 + Q_LORA, IN_COLS)
    bg_ref[...] = proj(0, c1).astype(BF16)
    cs = cs_ref[...]

    qn = _rms(q_lat) * qlg_ref[...]
    q = jnp.dot(qn.astype(BF16), w_uq_ref[...], preferred_element_type=F32)
    scale = QK_DIM ** -0.5 * LOG2_E
    g_nope = gq_ref[0:1, :] * scale
    tq = cs * (gq_ref[1:2, :] * scale)
    for hd in range(N_HEADS):
        lo = q[:, hd * HEAD_W:hd * HEAD_W + QK_NOPE]
        up = q[:, hd * HEAD_W + QK_NOPE:(hd + 1) * HEAD_W]
        ss = _rowsum(lo * lo) + 0.5 * _rowsum(up * up)
        r = lax.rsqrt(ss * (1.0 / QK_DIM) + EPS)
        q_ref[:, hd * HEAD_W:hd * HEAD_W + QK_NOPE] = (lo * r * g_nope).astype(BF16)
        q_ref[:, hd * HEAD_W + QK_NOPE:(hd + 1) * HEAD_W] = (up * r * tq).astype(BF16)

    cu_ref[...] = (proj(c1, c2) * proj(c2, c3)).astype(BF16)

    kv_lat = kvk[:, :KV_LORA]
    kraw = kvk[:, KV_LORA:]
    ckv = _rms(kv_lat) * kvg_ref[...]
    if cache_refs:
        ckv_out_ref, kr_out_ref = cache_refs
        ckv_out_ref[...] = ckv
        kr_out_ref[...] = kraw[:, :QK_ROPE]
    _emit_kv(ckv, kraw, cs, w_uk_ref, w_uvt_ref, gk_ref, k_ref, vt_ref)


def _inproj(x2d, mod3, n1g, w_in_b, qlg, w_uq_b, kvg, w_uk_b, w_uvt_b, gq, gk, cs, *,
            cond_base, tiles_per_cond, cs_tiles, emit_cache):
    t = x2d.shape[0]
    tm = TM_IN
    out_shape = [jax.ShapeDtypeStruct((t, CONV_WIDTH), BF16),
                 jax.ShapeDtypeStruct((t, CONV_WIDTH), BF16),
                 jax.ShapeDtypeStruct((t, N_HEADS * HEAD_W), BF16),
                 jax.ShapeDtypeStruct((t, N_HEADS * HEAD_W), BF16),
                 jax.ShapeDtypeStruct((N_HEADS * V_DIM, t), BF16)]
    out_specs = [pl.BlockSpec((tm, CONV_WIDTH), lambda i: (i, 0)),
                 pl.BlockSpec((tm, CONV_WIDTH), lambda i: (i, 0)),
                 pl.BlockSpec((tm, N_HEADS * HEAD_W), lambda i: (i, 0)),
                 pl.BlockSpec((tm, N_HEADS * HEAD_W), lambda i: (i, 0)),
                 pl.BlockSpec((N_HEADS * V_DIM, tm), lambda i: (0, i))]
    if emit_cache:
        out_shape += [jax.ShapeDtypeStruct((t, KV_LORA), F32), jax.ShapeDtypeStruct((t, QK_ROPE), F32)]
        out_specs += [pl.BlockSpec((tm, KV_LORA), lambda i: (i, 0)),
                      pl.BlockSpec((tm, QK_ROPE), lambda i: (i, 0))]
    return pl.pallas_call(
        _inproj_kernel,
        out_shape=tuple(out_shape),
        grid=(t // tm,),
        in_specs=[pl.BlockSpec((tm, D_MODEL), lambda i: (i, 0)),
                  pl.BlockSpec((1, N_MOD, D_MODEL), lambda i: (cond_base + i // tiles_per_cond, 0, 0)),
                  _const_spec((1, D_MODEL)),
                  _const_spec((IN_COLS, D_MODEL)),
                  _const_spec((1, Q_LORA)),
                  _const_spec((Q_LORA, N_HEADS * HEAD_W)),
                  _const_spec((1, KV_LORA)),
                  _const_spec((KV_LORA, N_HEADS * QK_NOPE)),
                  _const_spec((N_HEADS * V_DIM, KV_LORA)),
                  _const_spec((2, LANES)),
                  _const_spec((2, LANES)),
                  pl.BlockSpec((tm, LANES), lambda i: (i % cs_tiles, 0))],
        out_specs=tuple(out_specs),
        compiler_params=_cparams(("arbitrary",)),
        name="inproj_ctx" if emit_cache else "inproj_lat",
    )(x2d, mod3, n1g, w_in_b, qlg, w_uq_b, kvg, w_uk_b, w_uvt_b, gq, gk, cs)


def _attn_kernel(*refs, n_kv, heads):
    q_ref = refs[0]
    o_ref = refs[-1]
    chunks = []
    for j in range(n_kv):
        sk = refs[1 + 2 * j].shape[1]
        step = min(sk, ATTN_CHUNK)
        chunks += [(j, lo, lo + step) for lo in range(0, sk, step)]
    q = [q_ref[0, :, h * HEAD_W:(h + 1) * HEAD_W] for h in range(heads)]

    def score(h, c):
        j, lo, hi = chunks[c]
        k = refs[1 + 2 * j][0, lo:hi, h * HEAD_W:(h + 1) * HEAD_W]
        return lax.dot_general(k, q[h], (((1,), (1,)), ((), ())), preferred_element_type=F32)

    m = [None] * heads
    acc = [None] * heads
    ahead = [[score(h, c) for c in range(min(ATTN_AHEAD, len(chunks)))] for h in range(heads)]
    for c, (j, lo, hi) in enumerate(chunks):
        for h in range(heads):
            s = ahead[h].pop(0)
            if c + ATTN_AHEAD < len(chunks):
                ahead[h].append(score(h, c + ATTN_AHEAD))
            mc = jnp.max(s, axis=0, keepdims=True)
            m_new = mc if m[h] is None else jnp.maximum(m[h], mc)
            p = jnp.exp2(s - m_new).astype(BF16)
            vt = refs[2 + 2 * j][h * V_DIM:(h + 1) * V_DIM, lo:hi]
            vt1 = jnp.concatenate([vt, jnp.ones((2 * SUBLANES, hi - lo), BF16)], axis=0)
            part = jnp.dot(vt1, p, preferred_element_type=F32)
            acc[h] = part if acc[h] is None else acc[h] * jnp.exp2(m[h] - m_new) + part
            m[h] = m_new
    for h in range(heads):
        ot = acc[h][:V_DIM, :] / acc[h][V_DIM:V_DIM + 1, :]
        o_ref[0, :, h * V_DIM:(h + 1) * V_DIM] = ot.T.astype(BF16)


def _attention(q, kvs, *, tq, heads, name):
    b, s, _ = q.shape
    in_specs = [pl.BlockSpec((1, tq, heads * HEAD_W), lambda bi, hi, qi: (bi, qi, hi))]
    args = [q]
    for k, v in kvs:
        sk = k.shape[1]
        in_specs.append(pl.BlockSpec((1, sk, heads * HEAD_W), lambda bi, hi, qi: (bi, 0, hi)))
        in_specs.append(pl.BlockSpec((heads * V_DIM, sk), lambda bi, hi, qi: (hi, bi)))
        args += [k, v]
    return pl.pallas_call(
        functools.partial(_attn_kernel, n_kv=len(kvs), heads=heads),
        out_shape=jax.ShapeDtypeStruct((b, s, N_HEADS * V_DIM), BF16),
        grid=(b, N_HEADS // heads, s // tq),
        in_specs=in_specs,
        out_specs=pl.BlockSpec((1, tq, heads * V_DIM), lambda bi, hi, qi: (bi, qi, hi)),
        compiler_params=_cparams(("arbitrary", "arbitrary", "arbitrary")),
        name=name,
    )(*args)


def _outproj_kernel(x_ref, bg_ref, cu_ref, cup_ref, cun_ref, o_ref, mod_ref, cw_ref, w_out_ref,
                    n2g_ref, wr_ref, x1_ref, h2_ref, rid_ref, rw_ref, *, tm, seq_len):
    i = pl.program_id(0)
    mod = mod_ref[0]
    cu = cu_ref[...].astype(F32)
    prev_row = cup_ref[...].astype(F32)[15:16, :]
    next_row = cun_ref[...].astype(F32)[0:1, :]
    row = lax.broadcasted_iota(jnp.int32, (tm, 1), 0)
    pos = (i * tm + row) & (seq_len - 1)
    up = jnp.where(row == 0, prev_row, pltpu.roll(cu, 1, axis=0))
    up = jnp.where(pos == 0, 0.0, up)
    dn = jnp.where(row == tm - 1, next_row, pltpu.roll(cu, tm - 1, axis=0))
    dn = jnp.where(pos == seq_len - 1, 0.0, dn)
    cw = cw_ref[...]
    y_conv = bg_ref[...].astype(F32) * (up * cw[0:1, :] + cu * cw[1:2, :] + dn * cw[2:3, :])
    y_conv = y_conv.astype(BF16)

    slabs = [(r, r + tm // OUT_SLABS) for r in range(0, tm, tm // OUT_SLABS)]
    mixes = []
    for lo, hi in slabs:
        mix = jnp.dot(y_conv[lo:hi], w_out_ref[:CONV_WIDTH, :], preferred_element_type=F32)
        mixes.append(mix + jnp.dot(o_ref[lo:hi, :], w_out_ref[CONV_WIDTH:, :], preferred_element_type=F32))
    for (lo, hi), mix in zip(slabs, mixes):
        x1 = x_ref[lo:hi, :] + mod[2:3, :] * mix
        x1_ref[lo:hi, :] = x1
        h2 = _rms(x1) * n2g_ref[...] * (1.0 + mod[4:5, :]) + mod[3:4, :]
        h2_hi = h2.astype(BF16)
        h2_ref[lo:hi, :] = h2_hi

        h2_lo = (h2 - h2_hi.astype(F32)).astype(BF16)
        hh_hl = jnp.dot(h2_hi, wr_ref[...], preferred_element_type=F32)
        lh = jnp.dot(h2_lo, wr_ref[:, :ROUTER_COLS], preferred_element_type=F32)
        logits = hh_hl[:, :ROUTER_COLS] + (hh_hl[:, ROUTER_COLS:] + lh)
        lane = lax.broadcasted_iota(jnp.int32, logits.shape, 1)
        neg = -jnp.inf
        big = jnp.int32(1 << 20)
        gl = jnp.where(lane < N_EXPERT_GROUPS, logits, neg)
        gmax = jnp.max(gl, axis=-1, keepdims=True)
        p_top = 1.0 / _rowsum(jnp.exp(gl - gmax))
        g_top = jnp.min(jnp.where(gl == gmax, lane, big), axis=-1, keepdims=True)
        e_lo = N_EXPERT_GROUPS + EXPERTS_PER_GROUP * g_top
        el = jnp.where((lane >= e_lo) & (lane < e_lo + EXPERTS_PER_GROUP), logits, neg)
        v1 = jnp.max(el, axis=-1, keepdims=True)
        i1 = jnp.min(jnp.where(el == v1, lane, big), axis=-1, keepdims=True)
        el2 = jnp.where(lane == i1, neg, el)
        v2 = jnp.max(el2, axis=-1, keepdims=True)
        i2 = jnp.min(jnp.where(el2 == v2, lane, big), axis=-1, keepdims=True)
        e21 = jnp.exp(v2 - v1)
        w1 = p_top / (1.0 + e21)
        w2 = w1 * e21
        rid_ref[lo:hi, :] = jnp.where(lane == 0, i1 - N_EXPERT_GROUPS, i2 - N_EXPERT_GROUPS)
        rw_ref[lo:hi, :] = jnp.where(lane == 0, w1, w2)


def _outproj(x2d, bg, cu, o2d, mod3, conv_w, w_out_b, n2g, wr, *, cond_base, tiles_per_cond, seq_len):
    t = x2d.shape[0]
    tm = TM_OUT
    hb = tm // 16
    nhb = t // 16
    return pl.pallas_call(
        functools.partial(_outproj_kernel, tm=tm, seq_len=seq_len),
        out_shape=(jax.ShapeDtypeStruct((t, D_MODEL), F32),
                   jax.ShapeDtypeStruct((t, D_MODEL), BF16),
                   jax.ShapeDtypeStruct((t, LANES), jnp.int32),
                   jax.ShapeDtypeStruct((t, LANES), F32)),
        grid=(t // tm,),
        in_specs=[pl.BlockSpec((tm, D_MODEL), lambda i: (i, 0)),
                  pl.BlockSpec((tm, CONV_WIDTH), lambda i: (i, 0)),
                  pl.BlockSpec((tm, CONV_WIDTH), lambda i: (i, 0)),
                  pl.BlockSpec((16, CONV_WIDTH), lambda i: (jnp.maximum(i * hb - 1, 0), 0)),
                  pl.BlockSpec((16, CONV_WIDTH), lambda i: (jnp.minimum((i + 1) * hb, nhb - 1), 0)),
                  pl.BlockSpec((tm, N_HEADS * V_DIM), lambda i: (i, 0)),
                  pl.BlockSpec((1, N_MOD, D_MODEL), lambda i: (cond_base + i // tiles_per_cond, 0, 0)),
                  _const_spec((3, CONV_WIDTH)),
                  _const_spec((D_MODEL, D_MODEL)),
                  _const_spec((1, D_MODEL)),
                  _const_spec((D_MODEL, 2 * ROUTER_COLS))],
        out_specs=(pl.BlockSpec((tm, D_MODEL), lambda i: (i, 0)),
                   pl.BlockSpec((tm, D_MODEL), lambda i: (i, 0)),
                   pl.BlockSpec((tm, LANES), lambda i: (i, 0)),
                   pl.BlockSpec((tm, LANES), lambda i: (i, 0))),
        compiler_params=_cparams(("arbitrary",)),
        name="outproj_ctx" if cond_base == 0 else "outproj_lat",
    )(x2d, bg, cu, cu, cu, o2d, mod3, conv_w, w_out_b, n2g, wr)


def _route_tables(rid, *, tm):
    n_pairs = rid.shape[0]
    nt = n_pairs // tm + N_EXPERTS
    experts = jnp.arange(N_EXPERTS, dtype=jnp.int32)
    onehot = (rid[None, :] == experts[:, None]).astype(jnp.int32)
    csum = jnp.cumsum(onehot, axis=1)
    counts = csum[:, -1]
    tiles_e = (counts + tm - 1) // tm
    tile_end = jnp.cumsum(tiles_e)
    tile_start = tile_end - tiles_e
    n_valid = tile_end[-1]
    pos = jnp.sum(onehot * (csum - 1 + tile_start[:, None] * tm), axis=0)
    tile_raw = jnp.arange(nt, dtype=jnp.int32)
    tile_idx = jnp.minimum(tile_raw, n_valid - 1)
    te = jnp.sum((tile_end[None, :] <= tile_idx[:, None]).astype(jnp.int32), axis=1)
    used = tiles_e > 0
    first = ((tile_raw == tile_start[te]) & (tile_raw < n_valid)).astype(jnp.int32)
    nxt = lax.cummin(jnp.where(used, experts, N_EXPERTS), axis=0, reverse=True)
    nxt = jnp.concatenate([nxt[1:], jnp.full((1,), N_EXPERTS, jnp.int32)])
    next_expert = jnp.where(nxt[te] < N_EXPERTS, nxt[te], -1).astype(jnp.int32)
    wslot = ((jnp.cumsum(used.astype(jnp.int32)) - 1)[te] & 1).astype(jnp.int32)
    return (pos, te, n_valid.reshape(1).astype(jnp.int32),
            (tile_start * tm + counts).astype(jnp.int32), (tiles_e * tm - counts).astype(jnp.int32),
            first, next_expert, wslot)


def _dispatch_kernel(pstart_ref, pcnt_ref, nval_ref, pos0_ref, pos1_ref, h2a_ref, h2b_ref, xs_hbm,
                     buf, zbuf, sem, sem_z, *, tm, nt, n_a):
    i = pl.program_id(0)
    n = pl.num_programs(0)
    slot = i & 1

    def row_copy(r, pos_ref, s):
        return pltpu.make_async_copy(buf.at[s, pl.ds(r, 1)], xs_hbm.at[pl.ds(pos_ref[0, 0, r], 1)], sem.at[s])

    def wait_tile(s):
        for _ in range(2):
            pltpu.make_async_copy(buf.at[s], xs_hbm.at[pl.ds(0, tm)], sem.at[s]).wait()

    def pad_copies(e, fn):
        cnt = pcnt_ref[e]
        start = pstart_ref[e]
        off = start + cnt
        for b in PAD_BITS:
            if b < SUBLANES:
                break
            off = off - (cnt & b)
            dst = pl.multiple_of(off, SUBLANES)

            @pl.when((cnt & b) != 0)
            def _():
                fn(pltpu.make_async_copy(zbuf.at[pl.ds(0, b)], xs_hbm.at[pl.ds(dst, b)], sem_z))
        for j in range(SUBLANES - 1):
            @pl.when(j < (cnt & (SUBLANES - 1)))
            def _():
                fn(pltpu.make_async_copy(zbuf.at[pl.ds(0, 1)], xs_hbm.at[pl.ds(start + j, 1)], sem_z))

    def tail_copy(j):
        return pltpu.make_async_copy(zbuf, xs_hbm.at[pl.ds(pl.multiple_of(j * tm, tm), tm)], sem_z)

    @pl.when(i == 0)
    def _():
        zbuf[...] = jnp.zeros((tm, D_MODEL), F32)
        lax.fori_loop(0, N_EXPERTS, lambda e, c: (pad_copies(e, lambda d: d.start()), c)[1], 0)
        lax.fori_loop(nval_ref[0], nt, lambda j, c: (tail_copy(j).start(), c)[1], 0)
        lax.fori_loop(0, N_EXPERTS, lambda e, c: (pad_copies(e, lambda d: d.wait()), c)[1], 0)
        lax.fori_loop(nval_ref[0], nt, lambda j, c: (tail_copy(j).wait(), c)[1], 0)

    @pl.when(i >= 2)
    def _():
        wait_tile(slot)

    @pl.when(i < n_a)
    def _():
        buf[slot] = h2a_ref[...].astype(F32)

    @pl.when(i >= n_a)
    def _():
        buf[slot] = h2b_ref[...].astype(F32)

    for r in range(tm):
        row_copy(r, pos0_ref, slot).start()
        row_copy(r, pos1_ref, slot).start(priority=1)

    @pl.when(i == n - 1)
    def _():
        wait_tile(slot)
        wait_tile(1 - slot)


def _dispatch(pad_start, pad_cnt, n_valid, pos3, h2_a, h2_b, *, nt):
    tm = TM_ROW
    n_a, n_b = h2_a.shape[0] // tm, h2_b.shape[0] // tm
    smem_tile = functools.partial(pl.BlockSpec, (1, 1, tm), memory_space=pltpu.SMEM)
    grid_spec = pltpu.PrefetchScalarGridSpec(
        num_scalar_prefetch=3,
        grid=(n_a + n_b,),
        in_specs=[smem_tile(lambda i, *_: (i, 0, 0)),
                  smem_tile(lambda i, *_: (n_a + n_b + i, 0, 0)),
                  pl.BlockSpec((tm, D_MODEL), lambda i, *_: (jnp.minimum(i, n_a - 1), 0)),
                  pl.BlockSpec((tm, D_MODEL), lambda i, *_: (jnp.maximum(i - n_a, 0), 0))],
        out_specs=pl.BlockSpec(memory_space=pl.ANY),
        scratch_shapes=[pltpu.VMEM((2, tm, D_MODEL), F32),
                        pltpu.VMEM((tm, D_MODEL), F32),
                        pltpu.SemaphoreType.DMA((2,)),
                        pltpu.SemaphoreType.DMA])
    return pl.pallas_call(
        functools.partial(_dispatch_kernel, tm=tm, nt=nt, n_a=n_a),
        out_shape=jax.ShapeDtypeStruct((nt * TM_MOE, D_MODEL), F32),
        grid_spec=grid_spec,
        compiler_params=pltpu.CompilerParams(dimension_semantics=("arbitrary",),
                                             vmem_limit_bytes=VMEM_LIMIT, has_side_effects=True),
        name="dispatch",
    )(pad_start, pad_cnt, n_valid, pos3, pos3, h2_a, h2_b)


def _moe_kernel(texp_ref, nval_ref, first_ref, next_ref, wslot_ref, xs_hbm, wg_hbm, wu_hbm, wd_hbm, y_ref,
                xbuf, wg32, wu32, wd32, wgb, wub, wdb, sem_x, sem_w, *, tm):
    i = pl.program_id(0)
    n_valid = nval_ref[0]

    def tile_copy(j):
        s = lax.rem(j, XS_DEPTH)
        return pltpu.make_async_copy(xs_hbm.at[pl.ds(pl.multiple_of(j * tm, tm), tm)], xbuf.at[s], sem_x.at[s])

    @pl.when(i == 0)
    def _():
        for j in range(XS_DEPTH - 1):
            @pl.when(j < n_valid)
            def _():
                tile_copy(j).start()

    @pl.when(i + XS_DEPTH - 1 < n_valid)
    def _():
        tile_copy(i + XS_DEPTH - 1).start()

    def weight_copies(e, s):
        return (pltpu.make_async_copy(wg_hbm.at[e], wg32.at[s], sem_w.at[s]),
                pltpu.make_async_copy(wu_hbm.at[e], wu32.at[s], sem_w.at[s]),
                pltpu.make_async_copy(wd_hbm.at[e], wd32.at[s], sem_w.at[s]))

    @pl.when(i == 0)
    def _():
        for d in weight_copies(texp_ref[0], 0):
            d.start()

    @pl.when(i < nval_ref[0])
    def _():
        @pl.when(first_ref[i] == 1)
        def _():
            s = wslot_ref[i]
            for d in weight_copies(texp_ref[i], s):
                d.wait()
            e_next = next_ref[i]

            @pl.when(e_next >= 0)
            def _():
                for d in weight_copies(e_next, 1 - s):
                    d.start()
            wgb[...] = wg32[s].astype(BF16)
            wub[...] = wu32[s].astype(BF16)
            wdb[...] = wd32[s].astype(BF16)

        tile_copy(i).wait()
        x = xbuf[lax.rem(i, XS_DEPTH)].astype(BF16)
        g = jnp.dot(x, wgb[...], preferred_element_type=F32)
        u = jnp.dot(x, wub[...], preferred_element_type=F32)
        a = g / (1.0 + jnp.exp(-g)) * u
        y_ref[...] = jnp.dot(a.astype(BF16), wdb[...], preferred_element_type=F32)

    @pl.when(i >= nval_ref[0])
    def _():
        y_ref[...] = jnp.zeros(y_ref.shape, F32)


def _moe(tile_expert, n_valid, first, next_expert, wslot, xs, w_gate, w_up, w_down):
    nt = tile_expert.shape[0]
    tm = TM_MOE
    grid_spec = pltpu.PrefetchScalarGridSpec(
        num_scalar_prefetch=5,
        grid=(nt,),
        in_specs=[pl.BlockSpec(memory_space=pl.ANY),
                  pl.BlockSpec(memory_space=pl.ANY),
                  pl.BlockSpec(memory_space=pl.ANY),
                  pl.BlockSpec(memory_space=pl.ANY)],
        out_specs=pl.BlockSpec((tm, D_MODEL), lambda i, *_: (i, 0)),
        scratch_shapes=[pltpu.VMEM((XS_DEPTH, tm, D_MODEL), F32),
                        pltpu.VMEM((2, D_MODEL, D_EXPERT), F32),
                        pltpu.VMEM((2, D_MODEL, D_EXPERT), F32),
                        pltpu.VMEM((2, D_EXPERT, D_MODEL), F32),
                        pltpu.VMEM((D_MODEL, D_EXPERT), BF16),
                        pltpu.VMEM((D_MODEL, D_EXPERT), BF16),
                        pltpu.VMEM((D_EXPERT, D_MODEL), BF16),
                        pltpu.SemaphoreType.DMA((XS_DEPTH,)),
                        pltpu.SemaphoreType.DMA((2,))])
    return pl.pallas_call(
        functools.partial(_moe_kernel, tm=tm),
        out_shape=jax.ShapeDtypeStruct((nt * tm, D_MODEL), F32),
        grid_spec=grid_spec,
        compiler_params=_cparams(("arbitrary",)),
        name="moe",
    )(tile_expert, n_valid, first, next_expert, wslot, xs, w_gate, w_up, w_down)


def _final_kernel(pa0_ref, pb0_ref, pa_ref, pb_ref, x1_ref, rw_ref, mod_ref, y_hbm, o_ref, ybuf, sem, *, tm):
    i = pl.program_id(0)
    n = pl.num_programs(0)
    slot = i & 1

    def start_tile(pa, pb, s):
        for r in range(tm):
            pltpu.make_async_copy(y_hbm.at[pl.ds(pa[0, 0, r], 1)], ybuf.at[s, 0, pl.ds(r, 1)], sem.at[s]).start()
            pltpu.make_async_copy(y_hbm.at[pl.ds(pb[0, 0, r], 1)], ybuf.at[s, 1, pl.ds(r, 1)],
                                  sem.at[s]).start(priority=1)

    @pl.when(i == 0)
    def _():
        start_tile(pa0_ref, pb0_ref, 0)

    @pl.when(i + 1 < n)
    def _():
        start_tile(pa_ref, pb_ref, 1 - slot)

    for k in range(2):
        pltpu.make_async_copy(y_hbm.at[pl.ds(0, tm)], ybuf.at[slot, k], sem.at[slot]).wait()
    w = rw_ref[...]
    moe = w[:, 0:1] * ybuf[slot, 0] + w[:, 1:2] * ybuf[slot, 1]
    o_ref[...] = x1_ref[...] + mod_ref[0][5:6, :] * moe


def _final(x1, rw, y, pos3, mod3, *, tile_base, slot_tiles, cond_base, tiles_per_cond):
    t = x1.shape[0]
    tm = TM_ROW
    n = t // tm
    smem_tile = functools.partial(pl.BlockSpec, (1, 1, tm), memory_space=pltpu.SMEM)
    return pl.pallas_call(
        functools.partial(_final_kernel, tm=tm),
        out_shape=jax.ShapeDtypeStruct((t, D_MODEL), F32),
        grid=(n,),
        in_specs=[smem_tile(lambda i: (tile_base, 0, 0)),
                  smem_tile(lambda i: (slot_tiles + tile_base, 0, 0)),
                  smem_tile(lambda i: (tile_base + jnp.minimum(i + 1, n - 1), 0, 0)),
                  smem_tile(lambda i: (slot_tiles + tile_base + jnp.minimum(i + 1, n - 1), 0, 0)),
                  pl.BlockSpec((tm, D_MODEL), lambda i: (i, 0)),
                  pl.BlockSpec((tm, LANES), lambda i: (i, 0)),
                  pl.BlockSpec((1, N_MOD, D_MODEL), lambda i: (cond_base + i // tiles_per_cond, 0, 0)),
                  pl.BlockSpec(memory_space=pl.ANY)],
        out_specs=pl.BlockSpec((tm, D_MODEL), lambda i: (i, 0)),
        scratch_shapes=[pltpu.VMEM((2, 2, tm, D_MODEL), F32), pltpu.SemaphoreType.DMA((2,))],
        compiler_params=_cparams(("arbitrary",)),
        name="final_ctx" if cond_base == 0 else "final_lat",
    )(pos3, pos3, pos3, pos3, x1, rw, mod3, y)


def _rope_table(n_tokens):
    rows = n_tokens // GRID_W
    inv = ROPE_BASE ** (-jnp.arange(ROPE_PAIRS, dtype=F32) / ROPE_PAIRS)
    row_ang = jnp.arange(rows, dtype=F32)[:, None] * inv
    col_ang = jnp.arange(GRID_W, dtype=F32)[:, None] * inv
    cr, sr, cc, sc = jnp.cos(row_ang), jnp.sin(row_ang), jnp.cos(col_ang), jnp.sin(col_ang)
    zr, zc = jnp.zeros_like(cr), jnp.zeros_like(cc)
    row_t = jnp.concatenate([cr, cr, zr, zr, -sr, sr, zr, zr], axis=-1)
    col_t = jnp.concatenate([zc, zc, cc, cc, zc, zc, -sc, sc], axis=-1)
    return (row_t[:, None, :] + col_t[None, :, :]).reshape(n_tokens, 2 * QK_ROPE)


def _swap_halves(a):
    p = ROPE_PAIRS
    return jnp.concatenate([a[..., p:2 * p], a[..., :p], a[..., 3 * p:], a[..., 2 * p:3 * p]], axis=-1)


def _head_gains(g):
    rope = g[QK_NOPE:]
    return jnp.stack([g[:QK_NOPE], jnp.concatenate([rope, _swap_halves(rope)])])


def kernel(x_prompt, x_sample, cache_ckv, cache_krope, c, c_ctx, ada_w, ada_b, norm1_g, w_in, conv_w,
           q_lora_g, w_uq, kv_lora_g, w_ukv, q_head_g, k_head_g, w_out, norm2_g, router_g, router_e,
           w_gate, w_up, w_down):
    depth = ada_w.shape[0]
    assert depth == 1
    bp, sp, _ = x_prompt.shape
    bs, ss, _ = x_sample.shape
    past = cache_ckv.shape[2]
    tp, ts = bp * sp, bs * ss
    assert sp & (sp - 1) == 0 and ss & (ss - 1) == 0
    assert tp % TM_IN == 0 and ss % TM_IN == 0 and TM_IN % sp == 0 and ss % TQ == 0
    assert bs + 1 <= 8 and past % 256 == 0 and TM_ROW == TM_MOE and tp % TM_ROW == 0 and ts % TM_ROW == 0

    w_in_b = _winprep(w_in[0].T)
    uq = w_uq[0].reshape(Q_LORA, N_HEADS, QK_DIM)
    w_uq_b = jnp.concatenate([uq, _swap_halves(uq[..., QK_NOPE:])], axis=-1)
    w_uq_b = w_uq_b.reshape(Q_LORA, N_HEADS * HEAD_W).astype(BF16)
    ukv = w_ukv[0].reshape(KV_LORA, N_HEADS, QK_NOPE + V_DIM)
    w_uk_b = ukv[..., :QK_NOPE].reshape(KV_LORA, N_HEADS * QK_NOPE).astype(BF16)
    w_uvt_b = ukv[..., QK_NOPE:].reshape(KV_LORA, N_HEADS * V_DIM).T.astype(BF16)
    w_out_b = w_out[0].astype(BF16)
    wr = jnp.concatenate([router_g[0], router_e[0],
                          jnp.zeros((D_MODEL, ROUTER_COLS - N_EXPERT_GROUPS - N_EXPERTS), F32)], axis=1)
    wr_hi = lax.bitcast_convert_type(lax.bitcast_convert_type(wr, jnp.uint32) & jnp.uint32(0xFFFF0000), F32)
    wr = jnp.concatenate([wr_hi, wr - wr_hi], axis=1).astype(BF16)
    gq, gk = _head_gains(q_head_g[0]), _head_gains(k_head_g[0])
    n1g, n2g = norm1_g[0].reshape(1, D_MODEL), norm2_g[0].reshape(1, D_MODEL)
    qlg, kvg = q_lora_g[0].reshape(1, Q_LORA), kv_lora_g[0].reshape(1, KV_LORA)
    cs_lat = _rope_table(ss)
    cs_id = jnp.concatenate([jnp.ones((TM_IN, QK_ROPE), F32), jnp.zeros((TM_IN, QK_ROPE), F32)], axis=1)

    cond8 = jnp.concatenate([c_ctx[None, :], c, jnp.zeros((8 - 1 - bs, D_MODEL), F32)], axis=0)
    mod3 = _modulation(cond8, ada_w[0], ada_b[0]).reshape(8, N_MOD, D_MODEL)

    xp2, xs2 = x_prompt.reshape(tp, D_MODEL), x_sample.reshape(ts, D_MODEL)
    big = 1 << 30

    bg_p, cu_p, q_p, k_p, vt_p, ckv_p, kr_p = _inproj(
        xp2, mod3, n1g, w_in_b, qlg, w_uq_b, kvg, w_uk_b, w_uvt_b, gq, gk, cs_id,
        cond_base=0, tiles_per_cond=big, cs_tiles=1, emit_cache=True)
    o_p = _attention(q_p.reshape(bp, sp, -1), [(k_p.reshape(bp, sp, -1), vt_p)],
                     tq=sp, heads=N_HEADS, name="attn_ctx")
    x1_p, h2_p, rid_p, rw_p = _outproj(
        xp2, bg_p, cu_p, o_p.reshape(tp, -1), mod3, conv_w[0], w_out_b, n2g, wr,
        cond_base=0, tiles_per_cond=big, seq_len=sp)

    kr_c = cache_krope[:, 0].reshape(bs * past, QK_ROPE)
    k_c, vt_c = _kvcache(cache_ckv[:, 0].reshape(bs * past, KV_LORA), jnp.concatenate([kr_c, kr_c], axis=1),
                         cs_id, w_uk_b, w_uvt_b, gk)
    bg_s, cu_s, q_s, k_s, vt_s = _inproj(
        xs2, mod3, n1g, w_in_b, qlg, w_uq_b, kvg, w_uk_b, w_uvt_b, gq, gk, cs_lat,
        cond_base=1, tiles_per_cond=ss // TM_IN, cs_tiles=ss // TM_IN, emit_cache=False)
    o_s = _attention(q_s.reshape(bs, ss, -1),
                     [(k_c.reshape(bs, past, -1), vt_c), (k_s.reshape(bs, ss, -1), vt_s)],
                     tq=TQ, heads=4, name="attn_lat")
    x1_s, h2_s, rid_s, rw_s = _outproj(
        xs2, bg_s, cu_s, o_s.reshape(ts, -1), mod3, conv_w[0], w_out_b, n2g, wr,
        cond_base=1, tiles_per_cond=ss // TM_OUT, seq_len=ss)

    n_tok = tp + ts
    slot_tiles = n_tok // TM_ROW
    nt = 2 * n_tok // TM_MOE + N_EXPERTS
    rid = jnp.concatenate([rid_p[:, :2], rid_s[:, :2]], axis=0).T.reshape(2 * n_tok)
    pos, te, n_valid, pad_start, pad_cnt, first, next_expert, wslot = _route_tables(rid, tm=TM_MOE)
    pos3 = pos.reshape(2 * slot_tiles, 1, TM_ROW)
    xs = _dispatch(pad_start, pad_cnt, n_valid, pos3, h2_p, h2_s, nt=nt)
    y = _moe(te, n_valid, first, next_expert, wslot, xs, w_gate[0], w_up[0], w_down[0])

    y_p = _final(x1_p, rw_p, y, pos3, mod3, tile_base=0, slot_tiles=slot_tiles, cond_base=0, tiles_per_cond=big)
    y_s = _final(x1_s, rw_s, y, pos3, mod3, tile_base=tp // TM_ROW, slot_tiles=slot_tiles, cond_base=1,
                 tiles_per_cond=ss // TM_ROW)

    return (y_p.reshape(bp, sp, D_MODEL), y_s.reshape(bs, ss, D_MODEL),
            ckv_p.reshape(bp, 1, sp, KV_LORA), kr_p.reshape(bp, 1, sp, QK_ROPE))
```

```python
import functools

import jax
import jax.numpy as jnp
from jax import lax
from jax.experimental import pallas as pl
from jax.experimental.pallas import tpu as pltpu

F32 = jnp.float32
BF16 = jnp.bfloat16
HIGHEST = lax.Precision.HIGHEST

D_MODEL = 2048
CONV_WIDTH = 1024
N_HEADS = 8
QK_NOPE = 128
QK_ROPE = 64
V_DIM = 128
QK_DIM = QK_NOPE + QK_ROPE
Q_LORA = 512
KV_LORA = 256
GRID_W = 64
ROPE_PAIRS = QK_ROPE // 4
ROPE_BASE = 10000.0
N_EXPERT_GROUPS = 4
EXPERTS_PER_GROUP = 8
N_EXPERTS = N_EXPERT_GROUPS * EXPERTS_PER_GROUP
D_EXPERT = 512
N_MOD = 6
EPS = 1e-6
LOG2_E = 1.4426950408889634

HEAD_W = 2 * QK_NOPE
LANES = 128
SUBLANES = 8
IN_COLS = 3 * CONV_WIDTH + Q_LORA + KV_LORA + 2 * QK_ROPE
ROUTER_COLS = LANES
VMEM_LIMIT = 56 * 1024 * 1024

TM_IN = 512
TM_OUT = 512
OUT_SLABS = 2
TQ = 256
ATTN_CHUNK = 512
ATTN_AHEAD = 2
EXP_ROWS = 2 * SUBLANES
TM_MOE = 256
XS_DEPTH = 4
TM_ROW = 256
BN_MOD = 1024
PAD_BITS = tuple(1 << b for b in reversed(range(TM_MOE.bit_length() - 1)))


def _cparams(sem):
    return pltpu.CompilerParams(dimension_semantics=sem, vmem_limit_bytes=VMEM_LIMIT)


def _const_spec(shape):
    nd = len(shape)
    return pl.BlockSpec(shape, lambda *_: (0,) * nd, pipeline_mode=pl.Buffered(1))


def _rms(x):
    return x * lax.rsqrt(jnp.mean(x * x, axis=-1, keepdims=True) + EPS)


def _rowsum(x):
    return jnp.sum(x, axis=-1, keepdims=True)


def _mod_kernel(c_ref, w_ref, b_ref, o_ref):
    c = c_ref[...]
    s = c / (1.0 + jnp.exp(-c))
    s_hi = s.astype(BF16)
    s_lo = (s - s_hi.astype(F32)).astype(BF16)
    w = w_ref[...]
    w_hi = w.astype(BF16)
    w_lo = (w - w_hi.astype(F32)).astype(BF16)
    rows = s.shape[0]
    a = jnp.dot(jnp.concatenate([s_hi, s_lo], axis=0), w_hi, preferred_element_type=F32)
    b = jnp.dot(s_hi, w_lo, preferred_element_type=F32)
    o_ref[...] = a[:rows] + (a[rows:] + b) + b_ref[...]


def _modulation(cond8, ada_w, ada_b):
    n = ada_w.shape[1]
    return pl.pallas_call(
        _mod_kernel,
        out_shape=jax.ShapeDtypeStruct((8, n), F32),
        grid=(n // BN_MOD,),
        in_specs=[pl.BlockSpec((8, D_MODEL), lambda j: (0, 0)),
                  pl.BlockSpec((D_MODEL, BN_MOD), lambda j: (0, j)),
                  pl.BlockSpec((1, BN_MOD), lambda j: (0, j))],
        out_specs=pl.BlockSpec((8, BN_MOD), lambda j: (0, j)),
        compiler_params=_cparams(("arbitrary",)),
        name="mod",
    )(cond8, ada_w, ada_b.reshape(1, n))


def _winprep_kernel(w_ref, o_ref):
    n = w_ref.shape[0]
    o_ref[:n, :] = w_ref[...].astype(BF16)
    p = ROPE_PAIRS
    for dst, src in ((0, p), (p, 0), (2 * p, 3 * p), (3 * p, 2 * p)):
        o_ref[n + dst:n + dst + p, :] = w_ref[n - QK_ROPE + src:n - QK_ROPE + src + p, :].astype(BF16)


def _winprep(w_in_t):
    n, k = w_in_t.shape
    tk = 512
    return pl.pallas_call(
        _winprep_kernel,
        out_shape=jax.ShapeDtypeStruct((IN_COLS, k), BF16),
        grid=(k // tk,),
        in_specs=[pl.BlockSpec((n, tk), lambda i: (0, i))],
        out_specs=pl.BlockSpec((IN_COLS, tk), lambda i: (0, i)),
        compiler_params=_cparams(("arbitrary",)),
        name="winprep",
    )(w_in_t)


def _emit_kv(ckv, kraw, cs, w_uk_ref, w_uvt_ref, gk_ref, k_ref, vt_ref):
    cb = ckv.astype(BF16)
    kn_all = jnp.dot(cb, w_uk_ref[...], preferred_element_type=F32)
    vt = lax.dot_general(w_uvt_ref[...], cb, (((1,), (1,)), ((), ())), preferred_element_type=F32)
    vt_ref[...] = vt.astype(BF16)
    ss_rope = 0.5 * _rowsum(kraw * kraw)
    t = kraw * (cs * gk_ref[1:2, :])
    tt = t + pltpu.roll(t, QK_ROPE, axis=1)
    g_nope = gk_ref[0:1, :]
    for h in range(N_HEADS):
        kn = kn_all[:, h * QK_NOPE:(h + 1) * QK_NOPE]
        r = lax.rsqrt((_rowsum(kn * kn) + ss_rope) * (1.0 / QK_DIM) + EPS)
        k_ref[:, h * HEAD_W:h * HEAD_W + QK_NOPE] = (kn * r * g_nope).astype(BF16)
        k_ref[:, h * HEAD_W + QK_NOPE:(h + 1) * HEAD_W] = (tt * r).astype(BF16)


def _kvcache_kernel(ckv_ref, kraw_ref, cs_ref, w_uk_ref, w_uvt_ref, gk_ref, k_ref, vt_ref):
    _emit_kv(ckv_ref[...], kraw_ref[...], cs_ref[...], w_uk_ref, w_uvt_ref, gk_ref, k_ref, vt_ref)


def _kvcache(ckv, kraw, cs_id, w_uk_b, w_uvt_b, gk):
    n = ckv.shape[0]
    tm = 256
    return pl.pallas_call(
        _kvcache_kernel,
        out_shape=(jax.ShapeDtypeStruct((n, N_HEADS * HEAD_W), BF16),
                   jax.ShapeDtypeStruct((N_HEADS * V_DIM, n), BF16)),
        grid=(n // tm,),
        in_specs=[pl.BlockSpec((tm, KV_LORA), lambda i: (i, 0)),
                  pl.BlockSpec((tm, LANES), lambda i: (i, 0)),
                  pl.BlockSpec((tm, LANES), lambda i: (0, 0)),
                  _const_spec((KV_LORA, N_HEADS * QK_NOPE)),
                  _const_spec((N_HEADS * V_DIM, KV_LORA)),
                  _const_spec((2, LANES))],
        out_specs=(pl.BlockSpec((tm, N_HEADS * HEAD_W), lambda i: (i, 0)),
                   pl.BlockSpec((N_HEADS * V_DIM, tm), lambda i: (0, i))),
        compiler_params=_cparams(("arbitrary",)),
        name="kvcache",
    )(ckv, kraw, cs_id, w_uk_b, w_uvt_b, gk)


def _inproj_kernel(x_ref, mod_ref, n1g_ref, w_in_ref, qlg_ref, w_uq_ref, kvg_ref, w_uk_ref, w_uvt_ref,
                   gq_ref, gk_ref, cs_ref, bg_ref, cu_ref, q_ref, k_ref, vt_ref, *cache_refs):
    x = x_ref[...]
    mod = mod_ref[0]
    h = _rms(x) * n1g_ref[...] * (1.0 + mod[1:2, :]) + mod[0:1, :]
    hb = h.astype(BF16)

    def proj(a, b):
        return lax.dot_general(hb, w_in_ref[a:b, :], (((1,), (1,)), ((), ())), preferred_element_type=F32)

    c1, c2, c3 = CONV_WIDTH, 2 * CONV_WIDTH, 3 * CONV_WIDTH
    q_lat = proj(c3, c3 + Q_LORA)
    kvk = proj(c3 + Q_LORA, IN_COLS)
    bg_ref[...] = proj(0, c1).astype(BF16)
    cs = cs_ref[...]

    qn = _rms(q_lat) * qlg_ref[...]
    q = jnp.dot(qn.astype(BF16), w_uq_ref[...], preferred_element_type=F32)
    scale = QK_DIM ** -0.5 * LOG2_E
    g_nope = gq_ref[0:1, :] * scale
    tq = cs * (gq_ref[1:2, :] * scale)
    for hd in range(N_HEADS):
        lo = q[:, hd * HEAD_W:hd * HEAD_W + QK_NOPE]
        up = q[:, hd * HEAD_W + QK_NOPE:(hd + 1) * HEAD_W]
        ss = _rowsum(lo * lo) + 0.5 * _rowsum(up * up)
        r = lax.rsqrt(ss * (1.0 / QK_DIM) + EPS)
        q_ref[:, hd * HEAD_W:hd * HEAD_W + QK_NOPE] = (lo * r * g_nope).astype(BF16)
        q_ref[:, hd * HEAD_W + QK_NOPE:(hd + 1) * HEAD_W] = (up * r * tq).astype(BF16)

    cu_ref[...] = (proj(c1, c2) * proj(c2, c3)).astype(BF16)

    kv_lat = kvk[:, :KV_LORA]
    kraw = kvk[:, KV_LORA:]
    ckv = _rms(kv_lat) * kvg_ref[...]
    if cache_refs:
        ckv_out_ref, kr_out_ref = cache_refs
        ckv_out_ref[...] = ckv
        kr_out_ref[...] = kraw[:, :QK_ROPE]
    _emit_kv(ckv, kraw, cs, w_uk_ref, w_uvt_ref, gk_ref, k_ref, vt_ref)


def _inproj(x2d, mod3, n1g, w_in_b, qlg, w_uq_b, kvg, w_uk_b, w_uvt_b, gq, gk, cs, *,
            cond_base, tiles_per_cond, cs_tiles, emit_cache):
    t = x2d.shape[0]
    tm = TM_IN
    out_shape = [jax.ShapeDtypeStruct((t, CONV_WIDTH), BF16),
                 jax.ShapeDtypeStruct((t, CONV_WIDTH), BF16),
                 jax.ShapeDtypeStruct((t, N_HEADS * HEAD_W), BF16),
                 jax.ShapeDtypeStruct((t, N_HEADS * HEAD_W), BF16),
                 jax.ShapeDtypeStruct((N_HEADS * V_DIM, t), BF16)]
    out_specs = [pl.BlockSpec((tm, CONV_WIDTH), lambda i: (i, 0)),
                 pl.BlockSpec((tm, CONV_WIDTH), lambda i: (i, 0)),
                 pl.BlockSpec((tm, N_HEADS * HEAD_W), lambda i: (i, 0)),
                 pl.BlockSpec((tm, N_HEADS * HEAD_W), lambda i: (i, 0)),
                 pl.BlockSpec((N_HEADS * V_DIM, tm), lambda i: (0, i))]
    if emit_cache:
        out_shape += [jax.ShapeDtypeStruct((t, KV_LORA), F32), jax.ShapeDtypeStruct((t, QK_ROPE), F32)]
        out_specs += [pl.BlockSpec((tm, KV_LORA), lambda i: (i, 0)),
                      pl.BlockSpec((tm, QK_ROPE), lambda i: (i, 0))]
    return pl.pallas_call(
        _inproj_kernel,
        out_shape=tuple(out_shape),
        grid=(t // tm,),
        in_specs=[pl.BlockSpec((tm, D_MODEL), lambda i: (i, 0)),
                  pl.BlockSpec((1, N_MOD, D_MODEL), lambda i: (cond_base + i // tiles_per_cond, 0, 0)),
                  _const_spec((1, D_MODEL)),
                  _const_spec((IN_COLS, D_MODEL)),
                  _const_spec((1, Q_LORA)),
                  _const_spec((Q_LORA, N_HEADS * HEAD_W)),
                  _const_spec((1, KV_LORA)),
                  _const_spec((KV_LORA, N_HEADS * QK_NOPE)),
                  _const_spec((N_HEADS * V_DIM, KV_LORA)),
                  _const_spec((2, LANES)),
                  _const_spec((2, LANES)),
                  pl.BlockSpec((tm, LANES), lambda i: (i % cs_tiles, 0))],
        out_specs=tuple(out_specs),
        compiler_params=_cparams(("arbitrary",)),
        name="inproj_ctx" if emit_cache else "inproj_lat",
    )(x2d, mod3, n1g, w_in_b, qlg, w_uq_b, kvg, w_uk_b, w_uvt_b, gq, gk, cs)


def _attn_kernel(*refs, n_kv, heads):
    q_ref = refs[0]
    o_ref = refs[-1]
    chunks = []
    for j in range(n_kv):
        sk = refs[1 + 2 * j].shape[1]
        step = min(sk, ATTN_CHUNK)
        chunks += [(j, lo, lo + step) for lo in range(0, sk, step)]
    q = [q_ref[0, :, h * HEAD_W:(h + 1) * HEAD_W] for h in range(heads)]

    def score(h, c):
        j, lo, hi = chunks[c]
        k = refs[1 + 2 * j][0, lo:hi, h * HEAD_W:(h + 1) * HEAD_W]
        return lax.dot_general(k, q[h], (((1,), (1,)), ((), ())), preferred_element_type=F32)

    m = [None] * heads
    acc = [None] * heads
    ahead = [[score(h, c) for c in range(min(ATTN_AHEAD, len(chunks)))] for h in range(heads)]
    for c, (j, lo, hi) in enumerate(chunks):
        for h in range(heads):
            s = ahead[h].pop(0)
            if c + ATTN_AHEAD < len(chunks):
                ahead[h].append(score(h, c + ATTN_AHEAD))
            mc = jnp.max(s, axis=0, keepdims=True)
            m_new = mc if m[h] is None else jnp.maximum(m[h], mc)
            p = jnp.concatenate([jnp.exp2(s[r:r + EXP_ROWS] - m_new).astype(BF16)
                                 for r in range(0, hi - lo, EXP_ROWS)], axis=0)
            vt = refs[2 + 2 * j][h * V_DIM:(h + 1) * V_DIM, lo:hi]
            vt1 = jnp.concatenate([vt, jnp.ones((2 * SUBLANES, hi - lo), BF16)], axis=0)
            part = jnp.dot(vt1, p, preferred_element_type=F32)
            acc[h] = part if acc[h] is None else acc[h] * jnp.exp2(m[h] - m_new) + part
            m[h] = m_new
    for h in range(heads):
        ot = acc[h][:V_DIM, :] / acc[h][V_DIM:V_DIM + 1, :]
        o_ref[0, :, h * V_DIM:(h + 1) * V_DIM] = ot.T.astype(BF16)


def _attention(q, kvs, *, tq, heads, name):
    b, s, _ = q.shape
    in_specs = [pl.BlockSpec((1, tq, heads * HEAD_W), lambda bi, hi, qi: (bi, qi, hi))]
    args = [q]
    for k, v in kvs:
        sk = k.shape[1]
        in_specs.append(pl.BlockSpec((1, sk, heads * HEAD_W), lambda bi, hi, qi: (bi, 0, hi)))
        in_specs.append(pl.BlockSpec((heads * V_DIM, sk), lambda bi, hi, qi: (hi, bi)))
        args += [k, v]
    return pl.pallas_call(
        functools.partial(_attn_kernel, n_kv=len(kvs), heads=heads),
        out_shape=jax.ShapeDtypeStruct((b, s, N_HEADS * V_DIM), BF16),
        grid=(b, N_HEADS // heads, s // tq),
        in_specs=in_specs,
        out_specs=pl.BlockSpec((1, tq, heads * V_DIM), lambda bi, hi, qi: (bi, qi, hi)),
        compiler_params=_cparams(("arbitrary", "arbitrary", "arbitrary")),
        name=name,
    )(*args)


def _outproj_kernel(x_ref, bg_ref, cu_ref, cup_ref, cun_ref, o_ref, mod_ref, cw_ref, w_out_ref,
                    n2g_ref, wr_ref, x1_ref, h2_ref, rid_ref, rw_ref, *, tm, seq_len):
    i = pl.program_id(0)
    mod = mod_ref[0]
    cu = cu_ref[...].astype(F32)
    prev_row = cup_ref[...].astype(F32)[15:16, :]
    next_row = cun_ref[...].astype(F32)[0:1, :]
    row = lax.broadcasted_iota(jnp.int32, (tm, 1), 0)
    pos = (i * tm + row) & (seq_len - 1)
    up = jnp.where(row == 0, prev_row, pltpu.roll(cu, 1, axis=0))
    up = jnp.where(pos == 0, 0.0, up)
    dn = jnp.where(row == tm - 1, next_row, pltpu.roll(cu, tm - 1, axis=0))
    dn = jnp.where(pos == seq_len - 1, 0.0, dn)
    cw = cw_ref[...]
    y_conv = bg_ref[...].astype(F32) * (up * cw[0:1, :] + cu * cw[1:2, :] + dn * cw[2:3, :])
    y_conv = y_conv.astype(BF16)

    slabs = [(r, r + tm // OUT_SLABS) for r in range(0, tm, tm // OUT_SLABS)]
    mixes = []
    for lo, hi in slabs:
        mix = jnp.dot(y_conv[lo:hi], w_out_ref[:CONV_WIDTH, :], preferred_element_type=F32)
        mixes.append(mix + jnp.dot(o_ref[lo:hi, :], w_out_ref[CONV_WIDTH:, :], preferred_element_type=F32))
    for (lo, hi), mix in zip(slabs, mixes):
        x1 = x_ref[lo:hi, :] + mod[2:3, :] * mix
        x1_ref[lo:hi, :] = x1
        h2 = _rms(x1) * n2g_ref[...] * (1.0 + mod[4:5, :]) + mod[3:4, :]
        h2_hi = h2.astype(BF16)
        h2_ref[lo:hi, :] = h2_hi

        h2_lo = (h2 - h2_hi.astype(F32)).astype(BF16)
        hh_hl = jnp.dot(h2_hi, wr_ref[...], preferred_element_type=F32)
        lh = jnp.dot(h2_lo, wr_ref[:, :ROUTER_COLS], preferred_element_type=F32)
        logits = hh_hl[:, :ROUTER_COLS] + (hh_hl[:, ROUTER_COLS:] + lh)
        lane = lax.broadcasted_iota(jnp.int32, logits.shape, 1)
        neg = -jnp.inf
        big = jnp.int32(1 << 20)
        gl = jnp.where(lane < N_EXPERT_GROUPS, logits, neg)
        gmax = jnp.max(gl, axis=-1, keepdims=True)
        p_top = 1.0 / _rowsum(jnp.exp(gl - gmax))
        g_top = jnp.min(jnp.where(gl == gmax, lane, big), axis=-1, keepdims=True)
        e_lo = N_EXPERT_GROUPS + EXPERTS_PER_GROUP * g_top
        el = jnp.where((lane >= e_lo) & (lane < e_lo + EXPERTS_PER_GROUP), logits, neg)
        v1 = jnp.max(el, axis=-1, keepdims=True)
        i1 = jnp.min(jnp.where(el == v1, lane, big), axis=-1, keepdims=True)
        el2 = jnp.where(lane == i1, neg, el)
        v2 = jnp.max(el2, axis=-1, keepdims=True)
        i2 = jnp.min(jnp.where(el2 == v2, lane, big), axis=-1, keepdims=True)
        e21 = jnp.exp(v2 - v1)
        w1 = p_top / (1.0 + e21)
        w2 = w1 * e21
        rid_ref[lo:hi, :] = jnp.where(lane == 0, i1 - N_EXPERT_GROUPS, i2 - N_EXPERT_GROUPS)
        rw_ref[lo:hi, :] = jnp.where(lane == 0, w1, w2)


def _outproj(x2d, bg, cu, o2d, mod3, conv_w, w_out_b, n2g, wr, *, cond_base, tiles_per_cond, seq_len):
    t = x2d.shape[0]
    tm = TM_OUT
    hb = tm // 16
    nhb = t // 16
    return pl.pallas_call(
        functools.partial(_outproj_kernel, tm=tm, seq_len=seq_len),
        out_shape=(jax.ShapeDtypeStruct((t, D_MODEL), F32),
                   jax.ShapeDtypeStruct((t, D_MODEL), BF16),
                   jax.ShapeDtypeStruct((t, LANES), jnp.int32),
                   jax.ShapeDtypeStruct((t, LANES), F32)),
        grid=(t // tm,),
        in_specs=[pl.BlockSpec((tm, D_MODEL), lambda i: (i, 0)),
                  pl.BlockSpec((tm, CONV_WIDTH), lambda i: (i, 0)),
                  pl.BlockSpec((tm, CONV_WIDTH), lambda i: (i, 0)),
                  pl.BlockSpec((16, CONV_WIDTH), lambda i: (jnp.maximum(i * hb - 1, 0), 0)),
                  pl.BlockSpec((16, CONV_WIDTH), lambda i: (jnp.minimum((i + 1) * hb, nhb - 1), 0)),
                  pl.BlockSpec((tm, N_HEADS * V_DIM), lambda i: (i, 0)),
                  pl.BlockSpec((1, N_MOD, D_MODEL), lambda i: (cond_base + i // tiles_per_cond, 0, 0)),
                  _const_spec((3, CONV_WIDTH)),
                  _const_spec((D_MODEL, D_MODEL)),
                  _const_spec((1, D_MODEL)),
                  _const_spec((D_MODEL, 2 * ROUTER_COLS))],
        out_specs=(pl.BlockSpec((tm, D_MODEL), lambda i: (i, 0)),
                   pl.BlockSpec((tm, D_MODEL), lambda i: (i, 0)),
                   pl.BlockSpec((tm, LANES), lambda i: (i, 0)),
                   pl.BlockSpec((tm, LANES), lambda i: (i, 0))),
        compiler_params=_cparams(("arbitrary",)),
        name="outproj_ctx" if cond_base == 0 else "outproj_lat",
    )(x2d, bg, cu, cu, cu, o2d, mod3, conv_w, w_out_b, n2g, wr)


def _route_tables(rid, *, tm):
    n_pairs = rid.shape[0]
    nt = n_pairs // tm + N_EXPERTS
    experts = jnp.arange(N_EXPERTS, dtype=jnp.int32)
    onehot = (rid[None, :] == experts[:, None]).astype(jnp.int32)
    csum = jnp.cumsum(onehot, axis=1)
    counts = csum[:, -1]
    tiles_e = (counts + tm - 1) // tm
    tile_end = jnp.cumsum(tiles_e)
    tile_start = tile_end - tiles_e
    n_valid = tile_end[-1]
    pos = jnp.sum(onehot * (csum - 1 + tile_start[:, None] * tm), axis=0)
    tile_raw = jnp.arange(nt, dtype=jnp.int32)
    tile_idx = jnp.minimum(tile_raw, n_valid - 1)
    te = jnp.sum((tile_end[None, :] <= tile_idx[:, None]).astype(jnp.int32), axis=1)
    used = tiles_e > 0
    first = ((tile_raw == tile_start[te]) & (tile_raw < n_valid)).astype(jnp.int32)
    nxt = lax.cummin(jnp.where(used, experts, N_EXPERTS), axis=0, reverse=True)
    nxt = jnp.concatenate([nxt[1:], jnp.full((1,), N_EXPERTS, jnp.int32)])
    next_expert = jnp.where(nxt[te] < N_EXPERTS, nxt[te], -1).astype(jnp.int32)
    wslot = ((jnp.cumsum(used.astype(jnp.int32)) - 1)[te] & 1).astype(jnp.int32)
    return (pos, te, n_valid.reshape(1).astype(jnp.int32),
            (tile_start * tm + counts).astype(jnp.int32), (tiles_e * tm - counts).astype(jnp.int32),
            first, next_expert, wslot)


def _dispatch_kernel(pstart_ref, pcnt_ref, nval_ref, pos0_ref, pos1_ref, h2a_ref, h2b_ref, xs_hbm,
                     buf, zbuf, sem, sem_z, *, tm, nt, n_a):
    i = pl.program_id(0)
    n = pl.num_programs(0)
    slot = i & 1

    def row_copy(r, pos_ref, s):
        return pltpu.make_async_copy(buf.at[s, pl.ds(r, 1)], xs_hbm.at[pl.ds(pos_ref[0, 0, r], 1)], sem.at[s])

    def wait_tile(s):
        for _ in range(2):
            pltpu.make_async_copy(buf.at[s], xs_hbm.at[pl.ds(0, tm)], sem.at[s]).wait()

    def pad_copies(e, fn):
        cnt = pcnt_ref[e]
        start = pstart_ref[e]
        off = start + cnt
        for b in PAD_BITS:
            if b < SUBLANES:
                break
            off = off - (cnt & b)
            dst = pl.multiple_of(off, SUBLANES)

            @pl.when((cnt & b) != 0)
            def _():
                fn(pltpu.make_async_copy(zbuf.at[pl.ds(0, b)], xs_hbm.at[pl.ds(dst, b)], sem_z))
        for j in range(SUBLANES - 1):
            @pl.when(j < (cnt & (SUBLANES - 1)))
            def _():
                fn(pltpu.make_async_copy(zbuf.at[pl.ds(0, 1)], xs_hbm.at[pl.ds(start + j, 1)], sem_z))

    def tail_copy(j):
        return pltpu.make_async_copy(zbuf, xs_hbm.at[pl.ds(pl.multiple_of(j * tm, tm), tm)], sem_z)

    @pl.when(i == 0)
    def _():
        zbuf[...] = jnp.zeros((tm, D_MODEL), F32)
        lax.fori_loop(0, N_EXPERTS, lambda e, c: (pad_copies(e, lambda d: d.start()), c)[1], 0)
        lax.fori_loop(nval_ref[0], nt, lambda j, c: (tail_copy(j).start(), c)[1], 0)
        lax.fori_loop(0, N_EXPERTS, lambda e, c: (pad_copies(e, lambda d: d.wait()), c)[1], 0)
        lax.fori_loop(nval_ref[0], nt, lambda j, c: (tail_copy(j).wait(), c)[1], 0)

    @pl.when(i >= 2)
    def _():
        wait_tile(slot)

    @pl.when(i < n_a)
    def _():
        buf[slot] = h2a_ref[...].astype(F32)

    @pl.when(i >= n_a)
    def _():
        buf[slot] = h2b_ref[...].astype(F32)

    for r in range(tm):
        row_copy(r, pos0_ref, slot).start()
        row_copy(r, pos1_ref, slot).start(priority=1)

    @pl.when(i == n - 1)
    def _():
        wait_tile(slot)
        wait_tile(1 - slot)


def _dispatch(pad_start, pad_cnt, n_valid, pos3, h2_a, h2_b, *, nt):
    tm = TM_ROW
    n_a, n_b = h2_a.shape[0] // tm, h2_b.shape[0] // tm
    smem_tile = functools.partial(pl.BlockSpec, (1, 1, tm), memory_space=pltpu.SMEM)
    grid_spec = pltpu.PrefetchScalarGridSpec(
        num_scalar_prefetch=3,
        grid=(n_a + n_b,),
        in_specs=[smem_tile(lambda i, *_: (i, 0, 0)),
                  smem_tile(lambda i, *_: (n_a + n_b + i, 0, 0)),
                  pl.BlockSpec((tm, D_MODEL), lambda i, *_: (jnp.minimum(i, n_a - 1), 0)),
                  pl.BlockSpec((tm, D_MODEL), lambda i, *_: (jnp.maximum(i - n_a, 0), 0))],
        out_specs=pl.BlockSpec(memory_space=pl.ANY),
        scratch_shapes=[pltpu.VMEM((2, tm, D_MODEL), F32),
                        pltpu.VMEM((tm, D_MODEL), F32),
                        pltpu.SemaphoreType.DMA((2,)),
                        pltpu.SemaphoreType.DMA])
    return pl.pallas_call(
        functools.partial(_dispatch_kernel, tm=tm, nt=nt, n_a=n_a),
        out_shape=jax.ShapeDtypeStruct((nt * TM_MOE, D_MODEL), F32),
        grid_spec=grid_spec,
        compiler_params=pltpu.CompilerParams(dimension_semantics=("arbitrary",),
                                             vmem_limit_bytes=VMEM_LIMIT, has_side_effects=True),
        name="dispatch",
    )(pad_start, pad_cnt, n_valid, pos3, pos3, h2_a, h2_b)


def _moe_kernel(texp_ref, nval_ref, first_ref, next_ref, wslot_ref, xs_hbm, wg_hbm, wu_hbm, wd_hbm, y_ref,
                xbuf, wg32, wu32, wd32, wgb, wub, wdb, sem_x, sem_w, *, tm):
    i = pl.program_id(0)
    n_valid = nval_ref[0]

    def tile_copy(j):
        s = lax.rem(j, XS_DEPTH)
        return pltpu.make_async_copy(xs_hbm.at[pl.ds(pl.multiple_of(j * tm, tm), tm)], xbuf.at[s], sem_x.at[s])

    @pl.when(i == 0)
    def _():
        for j in range(XS_DEPTH - 1):
            @pl.when(j < n_valid)
            def _():
                tile_copy(j).start()

    @pl.when(i + XS_DEPTH - 1 < n_valid)
    def _():
        tile_copy(i + XS_DEPTH - 1).start()

    def weight_copies(e, s):
        return (pltpu.make_async_copy(wg_hbm.at[e], wg32.at[s], sem_w.at[s]),
                pltpu.make_async_copy(wu_hbm.at[e], wu32.at[s], sem_w.at[s]),
                pltpu.make_async_copy(wd_hbm.at[e], wd32.at[s], sem_w.at[s]))

    @pl.when(i == 0)
    def _():
        for d in weight_copies(texp_ref[0], 0):
            d.start()

    @pl.when(i < nval_ref[0])
    def _():
        @pl.when(first_ref[i] == 1)
        def _():
            s = wslot_ref[i]
            for d in weight_copies(texp_ref[i], s):
                d.wait()
            e_next = next_ref[i]

            @pl.when(e_next >= 0)
            def _():
                for d in weight_copies(e_next, 1 - s):
                    d.start()
            wgb[...] = wg32[s].astype(BF16)
            wub[...] = wu32[s].astype(BF16)
            wdb[...] = wd32[s].astype(BF16)

        tile_copy(i).wait()
        x = xbuf[lax.rem(i, XS_DEPTH)].astype(BF16)
        g = jnp.dot(x, wgb[...], preferred_element_type=F32)
        u = jnp.dot(x, wub[...], preferred_element_type=F32)
        a = g / (1.0 + jnp.exp(-g)) * u
        y_ref[...] = jnp.dot(a.astype(BF16), wdb[...], preferred_element_type=F32)

    @pl.when(i >= nval_ref[0])
    def _():
        y_ref[...] = jnp.zeros(y_ref.shape, F32)


def _moe(tile_expert, n_valid, first, next_expert, wslot, xs, w_gate, w_up, w_down):
    nt = tile_expert.shape[0]
    tm = TM_MOE
    grid_spec = pltpu.PrefetchScalarGridSpec(
        num_scalar_prefetch=5,
        grid=(nt,),
        in_specs=[pl.BlockSpec(memory_space=pl.ANY),
                  pl.BlockSpec(memory_space=pl.ANY),
                  pl.BlockSpec(memory_space=pl.ANY),
                  pl.BlockSpec(memory_space=pl.ANY)],
        out_specs=pl.BlockSpec((tm, D_MODEL), lambda i, *_: (i, 0)),
        scratch_shapes=[pltpu.VMEM((XS_DEPTH, tm, D_MODEL), F32),
                        pltpu.VMEM((2, D_MODEL, D_EXPERT), F32),
                        pltpu.VMEM((2, D_MODEL, D_EXPERT), F32),
                        pltpu.VMEM((2, D_EXPERT, D_MODEL), F32),
                        pltpu.VMEM((D_MODEL, D_EXPERT), BF16),
                        pltpu.VMEM((D_MODEL, D_EXPERT), BF16),
                        pltpu.VMEM((D_EXPERT, D_MODEL), BF16),
                        pltpu.SemaphoreType.DMA((XS_DEPTH,)),
                        pltpu.SemaphoreType.DMA((2,))])
    return pl.pallas_call(
        functools.partial(_moe_kernel, tm=tm),
        out_shape=jax.ShapeDtypeStruct((nt * tm, D_MODEL), F32),
        grid_spec=grid_spec,
        compiler_params=_cparams(("arbitrary",)),
        name="moe",
    )(tile_expert, n_valid, first, next_expert, wslot, xs, w_gate, w_up, w_down)


def _final_kernel(pa0_ref, pb0_ref, pa_ref, pb_ref, x1_ref, rw_ref, mod_ref, y_hbm, o_ref, ybuf, sem, *, tm):
    i = pl.program_id(0)
    n = pl.num_programs(0)
    slot = i & 1

    def start_tile(pa, pb, s):
        for r in range(tm):
            pltpu.make_async_copy(y_hbm.at[pl.ds(pa[0, 0, r], 1)], ybuf.at[s, 0, pl.ds(r, 1)], sem.at[s]).start()
            pltpu.make_async_copy(y_hbm.at[pl.ds(pb[0, 0, r], 1)], ybuf.at[s, 1, pl.ds(r, 1)],
                                  sem.at[s]).start(priority=1)

    @pl.when(i == 0)
    def _():
        start_tile(pa0_ref, pb0_ref, 0)

    @pl.when(i + 1 < n)
    def _():
        start_tile(pa_ref, pb_ref, 1 - slot)

    for k in range(2):
        pltpu.make_async_copy(y_hbm.at[pl.ds(0, tm)], ybuf.at[slot, k], sem.at[slot]).wait()
    w = rw_ref[...]
    moe = w[:, 0:1] * ybuf[slot, 0] + w[:, 1:2] * ybuf[slot, 1]
    o_ref[...] = x1_ref[...] + mod_ref[0][5:6, :] * moe


def _final(x1, rw, y, pos3, mod3, *, tile_base, slot_tiles, cond_base, tiles_per_cond):
    t = x1.shape[0]
    tm = TM_ROW
    n = t // tm
    smem_tile = functools.partial(pl.BlockSpec, (1, 1, tm), memory_space=pltpu.SMEM)
    return pl.pallas_call(
        functools.partial(_final_kernel, tm=tm),
        out_shape=jax.ShapeDtypeStruct((t, D_MODEL), F32),
        grid=(n,),
        in_specs=[smem_tile(lambda i: (tile_base, 0, 0)),
                  smem_tile(lambda i: (slot_tiles + tile_base, 0, 0)),
                  smem_tile(lambda i: (tile_base + jnp.minimum(i + 1, n - 1), 0, 0)),
                  smem_tile(lambda i: (slot_tiles + tile_base + jnp.minimum(i + 1, n - 1), 0, 0)),
                  pl.BlockSpec((tm, D_MODEL), lambda i: (i, 0)),
                  pl.BlockSpec((tm, LANES), lambda i: (i, 0)),
                  pl.BlockSpec((1, N_MOD, D_MODEL), lambda i: (cond_base + i // tiles_per_cond, 0, 0)),
                  pl.BlockSpec(memory_space=pl.ANY)],
        out_specs=pl.BlockSpec((tm, D_MODEL), lambda i: (i, 0)),
        scratch_shapes=[pltpu.VMEM((2, 2, tm, D_MODEL), F32), pltpu.SemaphoreType.DMA((2,))],
        compiler_params=_cparams(("arbitrary",)),
        name="final_ctx" if cond_base == 0 else "final_lat",
    )(pos3, pos3, pos3, pos3, x1, rw, mod3, y)


def _rope_table(n_tokens):
    rows = n_tokens // GRID_W
    inv = ROPE_BASE ** (-jnp.arange(ROPE_PAIRS, dtype=F32) / ROPE_PAIRS)
    row_ang = jnp.arange(rows, dtype=F32)[:, None] * inv
    col_ang = jnp.arange(GRID_W, dtype=F32)[:, None] * inv
    cr, sr, cc, sc = jnp.cos(row_ang), jnp.sin(row_ang), jnp.cos(col_ang), jnp.sin(col_ang)
    zr, zc = jnp.zeros_like(cr), jnp.zeros_like(cc)
    row_t = jnp.concatenate([cr, cr, zr, zr, -sr, sr, zr, zr], axis=-1)
    col_t = jnp.concatenate([zc, zc, cc, cc, zc, zc, -sc, sc], axis=-1)
    return (row_t[:, None, :] + col_t[None, :, :]).reshape(n_tokens, 2 * QK_ROPE)


def _swap_halves(a):
    p = ROPE_PAIRS
    return jnp.concatenate([a[..., p:2 * p], a[..., :p], a[..., 3 * p:], a[..., 2 * p:3 * p]], axis=-1)


def _head_gains(g):
    rope = g[QK_NOPE:]
    return jnp.stack([g[:QK_NOPE], jnp.concatenate([rope, _swap_halves(rope)])])


def kernel(x_prompt, x_sample, cache_ckv, cache_krope, c, c_ctx, ada_w, ada_b, norm1_g, w_in, conv_w,
           q_lora_g, w_uq, kv_lora_g, w_ukv, q_head_g, k_head_g, w_out, norm2_g, router_g, router_e,
           w_gate, w_up, w_down):
    depth = ada_w.shape[0]
    assert depth == 1
    bp, sp, _ = x_prompt.shape
    bs, ss, _ = x_sample.shape
    past = cache_ckv.shape[2]
    tp, ts = bp * sp, bs * ss
    assert sp & (sp - 1) == 0 and ss & (ss - 1) == 0
    assert tp % TM_IN == 0 and ss % TM_IN == 0 and TM_IN % sp == 0 and ss % TQ == 0
    assert bs + 1 <= 8 and past % 256 == 0 and TM_ROW == TM_MOE and tp % TM_ROW == 0 and ts % TM_ROW == 0

    w_in_b = _winprep(w_in[0].T)
    uq = w_uq[0].reshape(Q_LORA, N_HEADS, QK_DIM)
    w_uq_b = jnp.concatenate([uq, _swap_halves(uq[..., QK_NOPE:])], axis=-1)
    w_uq_b = w_uq_b.reshape(Q_LORA, N_HEADS * HEAD_W).astype(BF16)
    ukv = w_ukv[0].reshape(KV_LORA, N_HEADS, QK_NOPE + V_DIM)
    w_uk_b = ukv[..., :QK_NOPE].reshape(KV_LORA, N_HEADS * QK_NOPE).astype(BF16)
    w_uvt_b = ukv[..., QK_NOPE:].reshape(KV_LORA, N_HEADS * V_DIM).T.astype(BF16)
    w_out_b = w_out[0].astype(BF16)
    wr = jnp.concatenate([router_g[0], router_e[0],
                          jnp.zeros((D_MODEL, ROUTER_COLS - N_EXPERT_GROUPS - N_EXPERTS), F32)], axis=1)
    wr_hi = lax.bitcast_convert_type(lax.bitcast_convert_type(wr, jnp.uint32) & jnp.uint32(0xFFFF0000), F32)
    wr = jnp.concatenate([wr_hi, wr - wr_hi], axis=1).astype(BF16)
    gq, gk = _head_gains(q_head_g[0]), _head_gains(k_head_g[0])
    n1g, n2g = norm1_g[0].reshape(1, D_MODEL), norm2_g[0].reshape(1, D_MODEL)
    qlg, kvg = q_lora_g[0].reshape(1, Q_LORA), kv_lora_g[0].reshape(1, KV_LORA)
    cs_lat = _rope_table(ss)
    cs_id = jnp.concatenate([jnp.ones((TM_IN, QK_ROPE), F32), jnp.zeros((TM_IN, QK_ROPE), F32)], axis=1)

    cond8 = jnp.concatenate([c_ctx[None, :], c, jnp.zeros((8 - 1 - bs, D_MODEL), F32)], axis=0)
    mod3 = _modulation(cond8, ada_w[0], ada_b[0]).reshape(8, N_MOD, D_MODEL)

    xp2, xs2 = x_prompt.reshape(tp, D_MODEL), x_sample.reshape(ts, D_MODEL)
    big = 1 << 30

    bg_p, cu_p, q_p, k_p, vt_p, ckv_p, kr_p = _inproj(
        xp2, mod3, n1g, w_in_b, qlg, w_uq_b, kvg, w_uk_b, w_uvt_b, gq, gk, cs_id,
        cond_base=0, tiles_per_cond=big, cs_tiles=1, emit_cache=True)
    o_p = _attention(q_p.reshape(bp, sp, -1), [(k_p.reshape(bp, sp, -1), vt_p)],
                     tq=sp, heads=N_HEADS, name="attn_ctx")
    x1_p, h2_p, rid_p, rw_p = _outproj(
        xp2, bg_p, cu_p, o_p.reshape(tp, -1), mod3, conv_w[0], w_out_b, n2g, wr,
        cond_base=0, tiles_per_cond=big, seq_len=sp)

    kr_c = cache_krope[:, 0].reshape(bs * past, QK_ROPE)
    k_c, vt_c = _kvcache(cache_ckv[:, 0].reshape(bs * past, KV_LORA), jnp.concatenate([kr_c, kr_c], axis=1),
                         cs_id, w_uk_b, w_uvt_b, gk)
    bg_s, cu_s, q_s, k_s, vt_s = _inproj(
        xs2, mod3, n1g, w_in_b, qlg, w_uq_b, kvg, w_uk_b, w_uvt_b, gq, gk, cs_lat,
        cond_base=1, tiles_per_cond=ss // TM_IN, cs_tiles=ss // TM_IN, emit_cache=False)
    o_s = _attention(q_s.reshape(bs, ss, -1),
                     [(k_c.reshape(bs, past, -1), vt_c), (k_s.reshape(bs, ss, -1), vt_s)],
                     tq=TQ, heads=4, name="attn_lat")
    x1_s, h2_s, rid_s, rw_s = _outproj(
        xs2, bg_s, cu_s, o_s.reshape(ts, -1), mod3, conv_w[0], w_out_b, n2g, wr,
        cond_base=1, tiles_per_cond=ss // TM_OUT, seq_len=ss)

    n_tok = tp + ts
    slot_tiles = n_tok // TM_ROW
    nt = 2 * n_tok // TM_MOE + N_EXPERTS
    rid = jnp.concatenate([rid_p[:, :2], rid_s[:, :2]], axis=0).T.reshape(2 * n_tok)
    pos, te, n_valid, pad_start, pad_cnt, first, next_expert, wslot = _route_tables(rid, tm=TM_MOE)
    pos3 = pos.reshape(2 * slot_tiles, 1, TM_ROW)
    xs = _dispatch(pad_start, pad_cnt, n_valid, pos3, h2_p, h2_s, nt=nt)
    y = _moe(te, n_valid, first, next_expert, wslot, xs, w_gate[0], w_up[0], w_down[0])

    y_p = _final(x1_p, rw_p, y, pos3, mod3, tile_base=0, slot_tiles=slot_tiles, cond_base=0, tiles_per_cond=big)
    y_s = _final(x1_s, rw_s, y, pos3, mod3, tile_base=tp // TM_ROW, slot_tiles=slot_tiles, cond_base=1,
                 tiles_per_cond=ss // TM_ROW)

    return (y_p.reshape(bp, sp, D_MODEL), y_s.reshape(bs, ss, D_MODEL),
            ckv_p.reshape(bp, 1, sp, KV_LORA), kr_p.reshape(bp, 1, sp, QK_ROPE))
```

```python
import functools

import jax
import jax.numpy as jnp
from jax import lax
from jax.experimental import pallas as pl
from jax.experimental.pallas import tpu as pltpu

F32 = jnp.float32
BF16 = jnp.bfloat16
HIGHEST = lax.Precision.HIGHEST

D_MODEL = 2048
CONV_WIDTH = 1024
N_HEADS = 8
QK_NOPE = 128
QK_ROPE = 64
V_DIM = 128
QK_DIM = QK_NOPE + QK_ROPE
Q_LORA = 512
KV_LORA = 256
GRID_W = 64
ROPE_PAIRS = QK_ROPE // 4
ROPE_BASE = 10000.0
N_EXPERT_GROUPS = 4
EXPERTS_PER_GROUP = 8
N_EXPERTS = N_EXPERT_GROUPS * EXPERTS_PER_GROUP
D_EXPERT = 512
N_MOD = 6
EPS = 1e-6
LOG2_E = 1.4426950408889634

HEAD_W = 2 * QK_NOPE
LANES = 128
SUBLANES = 8
IN_COLS = 3 * CONV_WIDTH + Q_LORA + KV_LORA + 2 * QK_ROPE
ROUTER_COLS = LANES
VMEM_LIMIT = 56 * 1024 * 1024

TM_IN = 512
TM_OUT = 512
OUT_SLABS = 2
TQ = 256
ATTN_CHUNK = 1024
ATTN_AHEAD = 2
EXP_ROWS = 2 * SUBLANES
TM_MOE = 256
XS_DEPTH = 4
TM_ROW = 256
BN_MOD = 1024
PAD_BITS = tuple(1 << b for b in reversed(range(TM_MOE.bit_length() - 1)))


def _cparams(sem):
    return pltpu.CompilerParams(dimension_semantics=sem, vmem_limit_bytes=VMEM_LIMIT)


def _const_spec(shape):
    nd = len(shape)
    return pl.BlockSpec(shape, lambda *_: (0,) * nd, pipeline_mode=pl.Buffered(1))


def _rms(x):
    return x * lax.rsqrt(jnp.mean(x * x, axis=-1, keepdims=True) + EPS)


def _rowsum(x):
    return jnp.sum(x, axis=-1, keepdims=True)


def _mod_kernel(c_ref, w_ref, b_ref, o_ref):
    c = c_ref[...]
    s = c / (1.0 + jnp.exp(-c))
    s_hi = s.astype(BF16)
    s_lo = (s - s_hi.astype(F32)).astype(BF16)
    w = w_ref[...]
    w_hi = w.astype(BF16)
    w_lo = (w - w_hi.astype(F32)).astype(BF16)
    rows = s.shape[0]
    a = jnp.dot(jnp.concatenate([s_hi, s_lo], axis=0), w_hi, preferred_element_type=F32)
    b = jnp.dot(s_hi, w_lo, preferred_element_type=F32)
    o_ref[...] = a[:rows] + (a[rows:] + b) + b_ref[...]


def _modulation(cond8, ada_w, ada_b):
    n = ada_w.shape[1]
    return pl.pallas_call(
        _mod_kernel,
        out_shape=jax.ShapeDtypeStruct((8, n), F32),
        grid=(n // BN_MOD,),
        in_specs=[pl.BlockSpec((8, D_MODEL), lambda j: (0, 0)),
                  pl.BlockSpec((D_MODEL, BN_MOD), lambda j: (0, j)),
                  pl.BlockSpec((1, BN_MOD), lambda j: (0, j))],
        out_specs=pl.BlockSpec((8, BN_MOD), lambda j: (0, j)),
        compiler_params=_cparams(("arbitrary",)),
        name="mod",
    )(cond8, ada_w, ada_b.reshape(1, n))


def _winprep_kernel(w_ref, o_ref):
    n = w_ref.shape[0]
    o_ref[:n, :] = w_ref[...].astype(BF16)
    p = ROPE_PAIRS
    for dst, src in ((0, p), (p, 0), (2 * p, 3 * p), (3 * p, 2 * p)):
        o_ref[n + dst:n + dst + p, :] = w_ref[n - QK_ROPE + src:n - QK_ROPE + src + p, :].astype(BF16)


def _winprep(w_in_t):
    n, k = w_in_t.shape
    tk = 512
    return pl.pallas_call(
        _winprep_kernel,
        out_shape=jax.ShapeDtypeStruct((IN_COLS, k), BF16),
        grid=(k // tk,),
        in_specs=[pl.BlockSpec((n, tk), lambda i: (0, i))],
        out_specs=pl.BlockSpec((IN_COLS, tk), lambda i: (0, i)),
        compiler_params=_cparams(("arbitrary",)),
        name="winprep",
    )(w_in_t)


def _emit_kv(ckv, kraw, cs, w_uk_ref, w_uvt_ref, gk_ref, k_ref, vt_ref):
    cb = ckv.astype(BF16)
    kn_all = jnp.dot(cb, w_uk_ref[...], preferred_element_type=F32)
    vt = lax.dot_general(w_uvt_ref[...], cb, (((1,), (1,)), ((), ())), preferred_element_type=F32)
    vt_ref[...] = vt.astype(BF16)
    ss_rope = 0.5 * _rowsum(kraw * kraw)
    t = kraw * (cs * gk_ref[1:2, :])
    tt = t + pltpu.roll(t, QK_ROPE, axis=1)
    g_nope = gk_ref[0:1, :]
    for h in range(N_HEADS):
        kn = kn_all[:, h * QK_NOPE:(h + 1) * QK_NOPE]
        r = lax.rsqrt((_rowsum(kn * kn) + ss_rope) * (1.0 / QK_DIM) + EPS)
        k_ref[:, h * HEAD_W:h * HEAD_W + QK_NOPE] = (kn * r * g_nope).astype(BF16)
        k_ref[:, h * HEAD_W + QK_NOPE:(h + 1) * HEAD_W] = (tt * r).astype(BF16)


def _kvcache_kernel(ckv_ref, kraw_ref, cs_ref, w_uk_ref, w_uvt_ref, gk_ref, k_ref, vt_ref):
    _emit_kv(ckv_ref[...], kraw_ref[...], cs_ref[...], w_uk_ref, w_uvt_ref, gk_ref, k_ref, vt_ref)


def _kvcache(ckv, kraw, cs_id, w_uk_b, w_uvt_b, gk):
    n = ckv.shape[0]
    tm = 256
    return pl.pallas_call(
        _kvcache_kernel,
        out_shape=(jax.ShapeDtypeStruct((n, N_HEADS * HEAD_W), BF16),
                   jax.ShapeDtypeStruct((N_HEADS * V_DIM, n), BF16)),
        grid=(n // tm,),
        in_specs=[pl.BlockSpec((tm, KV_LORA), lambda i: (i, 0)),
                  pl.BlockSpec((tm, LANES), lambda i: (i, 0)),
                  pl.BlockSpec((tm, LANES), lambda i: (0, 0)),
                  _const_spec((KV_LORA, N_HEADS * QK_NOPE)),
                  _const_spec((N_HEADS * V_DIM, KV_LORA)),
                  _const_spec((2, LANES))],
        out_specs=(pl.BlockSpec((tm, N_HEADS * HEAD_W), lambda i: (i, 0)),
                   pl.BlockSpec((N_HEADS * V_DIM, tm), lambda i: (0, i))),
        compiler_params=_cparams(("arbitrary",)),
        name="kvcache",
    )(ckv, kraw, cs_id, w_uk_b, w_uvt_b, gk)


def _inproj_kernel(x_ref, mod_ref, n1g_ref, w_in_ref, qlg_ref, w_uq_ref, kvg_ref, w_uk_ref, w_uvt_ref,
                   gq_ref, gk_ref, cs_ref, bg_ref, cu_ref, q_ref, k_ref, vt_ref, *cache_refs):
    x = x_ref[...]
    mod = mod_ref[0]
    h = _rms(x) * n1g_ref[...] * (1.0 + mod[1:2, :]) + mod[0:1, :]
    hb = h.astype(BF16)

    def proj(a, b):
        return lax.dot_general(hb, w_in_ref[a:b, :], (((1,), (1,)), ((), ())), preferred_element_type=F32)

    c1, c2, c3 = CONV_WIDTH, 2 * CONV_WIDTH, 3 * CONV_WIDTH
    q_lat = proj(c3, c3 + Q_LORA)
    kvk = proj(c3 + Q_LORA, IN_COLS)
    bg_ref[...] = proj(0, c1).astype(BF16)
    cs = cs_ref[...]

    qn = _rms(q_lat) * qlg_ref[...]
    q = jnp.dot(qn.astype(BF16), w_uq_ref[...], preferred_element_type=F32)
    scale = QK_DIM ** -0.5 * LOG2_E
    g_nope = gq_ref[0:1, :] * scale
    tq = cs * (gq_ref[1:2, :] * scale)
    for hd in range(N_HEADS):
        lo = q[:, hd * HEAD_W:hd * HEAD_W + QK_NOPE]
        up = q[:, hd * HEAD_W + QK_NOPE:(hd + 1) * HEAD_W]
        ss = _rowsum(lo * lo) + 0.5 * _rowsum(up * up)
        r = lax.rsqrt(ss * (1.0 / QK_DIM) + EPS)
        q_ref[:, hd * HEAD_W:hd * HEAD_W + QK_NOPE] = (lo * r * g_nope).astype(BF16)
        q_ref[:, hd * HEAD_W + QK_NOPE:(hd + 1) * HEAD_W] = (up * r * tq).astype(BF16)

    cu_ref[...] = (proj(c1, c2) * proj(c2, c3)).astype(BF16)

    kv_lat = kvk[:, :KV_LORA]
    kraw = kvk[:, KV_LORA:]
    ckv = _rms(kv_lat) * kvg_ref[...]
    if cache_refs:
        ckv_out_ref, kr_out_ref = cache_refs
        ckv_out_ref[...] = ckv
        kr_out_ref[...] = kraw[:, :QK_ROPE]
    _emit_kv(ckv, kraw, cs, w_uk_ref, w_uvt_ref, gk_ref, k_ref, vt_ref)


def _inproj(x2d, mod3, n1g, w_in_b, qlg, w_uq_b, kvg, w_uk_b, w_uvt_b, gq, gk, cs, *,
            cond_base, tiles_per_cond, cs_tiles, emit_cache):
    t = x2d.shape[0]
    tm = TM_IN
    out_shape = [jax.ShapeDtypeStruct((t, CONV_WIDTH), BF16),
                 jax.ShapeDtypeStruct((t, CONV_WIDTH), BF16),
                 jax.ShapeDtypeStruct((t, N_HEADS * HEAD_W), BF16),
                 jax.ShapeDtypeStruct((t, N_HEADS * HEAD_W), BF16),
                 jax.ShapeDtypeStruct((N_HEADS * V_DIM, t), BF16)]
    out_specs = [pl.BlockSpec((tm, CONV_WIDTH), lambda i: (i, 0)),
                 pl.BlockSpec((tm, CONV_WIDTH), lambda i: (i, 0)),
                 pl.BlockSpec((tm, N_HEADS * HEAD_W), lambda i: (i, 0)),
                 pl.BlockSpec((tm, N_HEADS * HEAD_W), lambda i: (i, 0)),
                 pl.BlockSpec((N_HEADS * V_DIM, tm), lambda i: (0, i))]
    if emit_cache:
        out_shape += [jax.ShapeDtypeStruct((t, KV_LORA), F32), jax.ShapeDtypeStruct((t, QK_ROPE), F32)]
        out_specs += [pl.BlockSpec((tm, KV_LORA), lambda i: (i, 0)),
                      pl.BlockSpec((tm, QK_ROPE), lambda i: (i, 0))]
    return pl.pallas_call(
        _inproj_kernel,
        out_shape=tuple(out_shape),
        grid=(t // tm,),
        in_specs=[pl.BlockSpec((tm, D_MODEL), lambda i: (i, 0)),
                  pl.BlockSpec((1, N_MOD, D_MODEL), lambda i: (cond_base + i // tiles_per_cond, 0, 0)),
                  _const_spec((1, D_MODEL)),
                  _const_spec((IN_COLS, D_MODEL)),
                  _const_spec((1, Q_LORA)),
                  _const_spec((Q_LORA, N_HEADS * HEAD_W)),
                  _const_spec((1, KV_LORA)),
                  _const_spec((KV_LORA, N_HEADS * QK_NOPE)),
                  _const_spec((N_HEADS * V_DIM, KV_LORA)),
                  _const_spec((2, LANES)),
                  _const_spec((2, LANES)),
                  pl.BlockSpec((tm, LANES), lambda i: (i % cs_tiles, 0))],
        out_specs=tuple(out_specs),
        compiler_params=_cparams(("arbitrary",)),
        name="inproj_ctx" if emit_cache else "inproj_lat",
    )(x2d, mod3, n1g, w_in_b, qlg, w_uq_b, kvg, w_uk_b, w_uvt_b, gq, gk, cs)


def _attn_kernel(*refs, n_kv, heads):
    q_ref = refs[0]
    o_ref = refs[-1]
    chunks = []
    for j in range(n_kv):
        sk = refs[1 + 2 * j].shape[1]
        step = min(sk, ATTN_CHUNK)
        chunks += [(j, lo, lo + step) for lo in range(0, sk, step)]
    q = [q_ref[0, :, h * HEAD_W:(h + 1) * HEAD_W] for h in range(heads)]

    def score(h, c):
        j, lo, hi = chunks[c]
        k = refs[1 + 2 * j][0, lo:hi, h * HEAD_W:(h + 1) * HEAD_W]
        return lax.dot_general(k, q[h], (((1,), (1,)), ((), ())), preferred_element_type=F32)

    m = [None] * heads
    acc = [None] * heads
    ahead = [[score(h, c) for c in range(min(ATTN_AHEAD, len(chunks)))] for h in range(heads)]
    for c, (j, lo, hi) in enumerate(chunks):
        for h in range(heads):
            s = ahead[h].pop(0)
            if c + ATTN_AHEAD < len(chunks):
                ahead[h].append(score(h, c + ATTN_AHEAD))
            mc = jnp.max(s, axis=0, keepdims=True)
            m_new = mc if m[h] is None else jnp.maximum(m[h], mc)
            p = jnp.concatenate([jnp.exp2(s[r:r + EXP_ROWS] - m_new).astype(BF16)
                                 for r in range(0, hi - lo, EXP_ROWS)], axis=0)
            vt = refs[2 + 2 * j][h * V_DIM:(h + 1) * V_DIM, lo:hi]
            vt1 = jnp.concatenate([vt, jnp.ones((2 * SUBLANES, hi - lo), BF16)], axis=0)
            part = jnp.dot(vt1, p, preferred_element_type=F32)
            acc[h] = part if acc[h] is None else acc[h] * jnp.exp2(m[h] - m_new) + part
            m[h] = m_new
    for h in range(heads):
        ot = acc[h][:V_DIM, :] / acc[h][V_DIM:V_DIM + 1, :]
        o_ref[0, :, h * V_DIM:(h + 1) * V_DIM] = ot.T.astype(BF16)


def _attention(q, kvs, *, tq, heads, name):
    b, s, _ = q.shape
    in_specs = [pl.BlockSpec((1, tq, heads * HEAD_W), lambda bi, hi, qi: (bi, qi, hi))]
    args = [q]
    for k, v in kvs:
        sk = k.shape[1]
        in_specs.append(pl.BlockSpec((1, sk, heads * HEAD_W), lambda bi, hi, qi: (bi, 0, hi)))
        in_specs.append(pl.BlockSpec((heads * V_DIM, sk), lambda bi, hi, qi: (hi, bi)))
        args += [k, v]
    return pl.pallas_call(
        functools.partial(_attn_kernel, n_kv=len(kvs), heads=heads),
        out_shape=jax.ShapeDtypeStruct((b, s, N_HEADS * V_DIM), BF16),
        grid=(b, N_HEADS // heads, s // tq),
        in_specs=in_specs,
        out_specs=pl.BlockSpec((1, tq, heads * V_DIM), lambda bi, hi, qi: (bi, qi, hi)),
        compiler_params=_cparams(("arbitrary", "arbitrary", "arbitrary")),
        name=name,
    )(*args)


def _outproj_kernel(x_ref, bg_ref, cu_ref, cup_ref, cun_ref, o_ref, mod_ref, cw_ref, w_out_ref,
                    n2g_ref, wr_ref, x1_ref, h2_ref, rid_ref, rw_ref, *, tm, seq_len):
    i = pl.program_id(0)
    mod = mod_ref[0]
    cu = cu_ref[...].astype(F32)
    prev_row = cup_ref[...].astype(F32)[15:16, :]
    next_row = cun_ref[...].astype(F32)[0:1, :]
    row = lax.broadcasted_iota(jnp.int32, (tm, 1), 0)
    pos = (i * tm + row) & (seq_len - 1)
    up = jnp.where(row == 0, prev_row, pltpu.roll(cu, 1, axis=0))
    up = jnp.where(pos == 0, 0.0, up)
    dn = jnp.where(row == tm - 1, next_row, pltpu.roll(cu, tm - 1, axis=0))
    dn = jnp.where(pos == seq_len - 1, 0.0, dn)
    cw = cw_ref[...]
    y_conv = bg_ref[...].astype(F32) * (up * cw[0:1, :] + cu * cw[1:2, :] + dn * cw[2:3, :])
    y_conv = y_conv.astype(BF16)

    slabs = [(r, r + tm // OUT_SLABS) for r in range(0, tm, tm // OUT_SLABS)]
    mixes = []
    for lo, hi in slabs:
        mix = jnp.dot(y_conv[lo:hi], w_out_ref[:CONV_WIDTH, :], preferred_element_type=F32)
        mixes.append(mix + jnp.dot(o_ref[lo:hi, :], w_out_ref[CONV_WIDTH:, :], preferred_element_type=F32))
    for (lo, hi), mix in zip(slabs, mixes):
        x1 = x_ref[lo:hi, :] + mod[2:3, :] * mix
        x1_ref[lo:hi, :] = x1
        h2 = _rms(x1) * n2g_ref[...] * (1.0 + mod[4:5, :]) + mod[3:4, :]
        h2_hi = h2.astype(BF16)
        h2_ref[lo:hi, :] = h2_hi

        h2_lo = (h2 - h2_hi.astype(F32)).astype(BF16)
        hh_hl = jnp.dot(h2_hi, wr_ref[...], preferred_element_type=F32)
        lh = jnp.dot(h2_lo, wr_ref[:, :ROUTER_COLS], preferred_element_type=F32)
        logits = hh_hl[:, :ROUTER_COLS] + (hh_hl[:, ROUTER_COLS:] + lh)
        lane = lax.broadcasted_iota(jnp.int32, logits.shape, 1)
        neg = -jnp.inf
        big = jnp.int32(1 << 20)
        gl = jnp.where(lane < N_EXPERT_GROUPS, logits, neg)
        gmax = jnp.max(gl, axis=-1, keepdims=True)
        p_top = 1.0 / _rowsum(jnp.exp(gl - gmax))
        g_top = jnp.min(jnp.where(gl == gmax, lane, big), axis=-1, keepdims=True)
        e_lo = N_EXPERT_GROUPS + EXPERTS_PER_GROUP * g_top
        el = jnp.where((lane >= e_lo) & (lane < e_lo + EXPERTS_PER_GROUP), logits, neg)
        v1 = jnp.max(el, axis=-1, keepdims=True)
        i1 = jnp.min(jnp.where(el == v1, lane, big), axis=-1, keepdims=True)
        el2 = jnp.where(lane == i1, neg, el)
        v2 = jnp.max(el2, axis=-1, keepdims=True)
        i2 = jnp.min(jnp.where(el2 == v2, lane, big), axis=-1, keepdims=True)
        e21 = jnp.exp(v2 - v1)
        w1 = p_top / (1.0 + e21)
        w2 = w1 * e21
        rid_ref[lo:hi, :] = jnp.where(lane == 0, i1 - N_EXPERT_GROUPS, i2 - N_EXPERT_GROUPS)
        rw_ref[lo:hi, :] = jnp.where(lane == 0, w1, w2)


def _outproj(x2d, bg, cu, o2d, mod3, conv_w, w_out_b, n2g, wr, *, cond_base, tiles_per_cond, seq_len):
    t = x2d.shape[0]
    tm = TM_OUT
    hb = tm // 16
    nhb = t // 16
    return pl.pallas_call(
        functools.partial(_outproj_kernel, tm=tm, seq_len=seq_len),
        out_shape=(jax.ShapeDtypeStruct((t, D_MODEL), F32),
                   jax.ShapeDtypeStruct((t, D_MODEL), BF16),
                   jax.ShapeDtypeStruct((t, LANES), jnp.int32),
                   jax.ShapeDtypeStruct((t, LANES), F32)),
        grid=(t // tm,),
        in_specs=[pl.BlockSpec((tm, D_MODEL), lambda i: (i, 0)),
                  pl.BlockSpec((tm, CONV_WIDTH), lambda i: (i, 0)),
                  pl.BlockSpec((tm, CONV_WIDTH), lambda i: (i, 0)),
                  pl.BlockSpec((16, CONV_WIDTH), lambda i: (jnp.maximum(i * hb - 1, 0), 0)),
                  pl.BlockSpec((16, CONV_WIDTH), lambda i: (jnp.minimum((i + 1) * hb, nhb - 1), 0)),
                  pl.BlockSpec((tm, N_HEADS * V_DIM), lambda i: (i, 0)),
                  pl.BlockSpec((1, N_MOD, D_MODEL), lambda i: (cond_base + i // tiles_per_cond, 0, 0)),
                  _const_spec((3, CONV_WIDTH)),
                  _const_spec((D_MODEL, D_MODEL)),
                  _const_spec((1, D_MODEL)),
                  _const_spec((D_MODEL, 2 * ROUTER_COLS))],
        out_specs=(pl.BlockSpec((tm, D_MODEL), lambda i: (i, 0)),
                   pl.BlockSpec((tm, D_MODEL), lambda i: (i, 0)),
                   pl.BlockSpec((tm, LANES), lambda i: (i, 0)),
                   pl.BlockSpec((tm, LANES), lambda i: (i, 0))),
        compiler_params=_cparams(("arbitrary",)),
        name="outproj_ctx" if cond_base == 0 else "outproj_lat",
    )(x2d, bg, cu, cu, cu, o2d, mod3, conv_w, w_out_b, n2g, wr)


def _route_tables(rid, *, tm):
    n_pairs = rid.shape[0]
    nt = n_pairs // tm + N_EXPERTS
    experts = jnp.arange(N_EXPERTS, dtype=jnp.int32)
    onehot = (rid[None, :] == experts[:, None]).astype(jnp.int32)
    csum = jnp.cumsum(onehot, axis=1)
    counts = csum[:, -1]
    tiles_e = (counts + tm - 1) // tm
    tile_end = jnp.cumsum(tiles_e)
    tile_start = tile_end - tiles_e
    n_valid = tile_end[-1]
    pos = jnp.sum(onehot * (csum - 1 + tile_start[:, None] * tm), axis=0)
    tile_raw = jnp.arange(nt, dtype=jnp.int32)
    tile_idx = jnp.minimum(tile_raw, n_valid - 1)
    te = jnp.sum((tile_end[None, :] <= tile_idx[:, None]).astype(jnp.int32), axis=1)
    used = tiles_e > 0
    first = ((tile_raw == tile_start[te]) & (tile_raw < n_valid)).astype(jnp.int32)
    nxt = lax.cummin(jnp.where(used, experts, N_EXPERTS), axis=0, reverse=True)
    nxt = jnp.concatenate([nxt[1:], jnp.full((1,), N_EXPERTS, jnp.int32)])
    next_expert = jnp.where(nxt[te] < N_EXPERTS, nxt[te], -1).astype(jnp.int32)
    wslot = ((jnp.cumsum(used.astype(jnp.int32)) - 1)[te] & 1).astype(jnp.int32)
    return (pos, te, n_valid.reshape(1).astype(jnp.int32),
            (tile_start * tm + counts).astype(jnp.int32), (tiles_e * tm - counts).astype(jnp.int32),
            first, next_expert, wslot)


def _dispatch_kernel(pstart_ref, pcnt_ref, nval_ref, pos0_ref, pos1_ref, h2a_ref, h2b_ref, xs_hbm,
                     buf, zbuf, sem, sem_z, *, tm, nt, n_a):
    i = pl.program_id(0)
    n = pl.num_programs(0)
    slot = i & 1

    def row_copy(r, pos_ref, s):
        return pltpu.make_async_copy(buf.at[s, pl.ds(r, 1)], xs_hbm.at[pl.ds(pos_ref[0, 0, r], 1)], sem.at[s])

    def wait_tile(s):
        for _ in range(2):
            pltpu.make_async_copy(buf.at[s], xs_hbm.at[pl.ds(0, tm)], sem.at[s]).wait()

    def pad_copies(e, fn):
        cnt = pcnt_ref[e]
        start = pstart_ref[e]
        off = start + cnt
        for b in PAD_BITS:
            if b < SUBLANES:
                break
            off = off - (cnt & b)
            dst = pl.multiple_of(off, SUBLANES)

            @pl.when((cnt & b) != 0)
            def _():
                fn(pltpu.make_async_copy(zbuf.at[pl.ds(0, b)], xs_hbm.at[pl.ds(dst, b)], sem_z))
        for j in range(SUBLANES - 1):
            @pl.when(j < (cnt & (SUBLANES - 1)))
            def _():
                fn(pltpu.make_async_copy(zbuf.at[pl.ds(0, 1)], xs_hbm.at[pl.ds(start + j, 1)], sem_z))

    def tail_copy(j):
        return pltpu.make_async_copy(zbuf, xs_hbm.at[pl.ds(pl.multiple_of(j * tm, tm), tm)], sem_z)

    @pl.when(i == 0)
    def _():
        zbuf[...] = jnp.zeros((tm, D_MODEL), F32)
        lax.fori_loop(0, N_EXPERTS, lambda e, c: (pad_copies(e, lambda d: d.start()), c)[1], 0)
        lax.fori_loop(nval_ref[0], nt, lambda j, c: (tail_copy(j).start(), c)[1], 0)
        lax.fori_loop(0, N_EXPERTS, lambda e, c: (pad_copies(e, lambda d: d.wait()), c)[1], 0)
        lax.fori_loop(nval_ref[0], nt, lambda j, c: (tail_copy(j).wait(), c)[1], 0)

    @pl.when(i >= 2)
    def _():
        wait_tile(slot)

    @pl.when(i < n_a)
    def _():
        buf[slot] = h2a_ref[...].astype(F32)

    @pl.when(i >= n_a)
    def _():
        buf[slot] = h2b_ref[...].astype(F32)

    for r in range(tm):
        row_copy(r, pos0_ref, slot).start()
        row_copy(r, pos1_ref, slot).start(priority=1)

    @pl.when(i == n - 1)
    def _():
        wait_tile(slot)
        wait_tile(1 - slot)


def _dispatch(pad_start, pad_cnt, n_valid, pos3, h2_a, h2_b, *, nt):
    tm = TM_ROW
    n_a, n_b = h2_a.shape[0] // tm, h2_b.shape[0] // tm
    smem_tile = functools.partial(pl.BlockSpec, (1, 1, tm), memory_space=pltpu.SMEM)
    grid_spec = pltpu.PrefetchScalarGridSpec(
        num_scalar_prefetch=3,
        grid=(n_a + n_b,),
        in_specs=[smem_tile(lambda i, *_: (i, 0, 0)),
                  smem_tile(lambda i, *_: (n_a + n_b + i, 0, 0)),
                  pl.BlockSpec((tm, D_MODEL), lambda i, *_: (jnp.minimum(i, n_a - 1), 0)),
                  pl.BlockSpec((tm, D_MODEL), lambda i, *_: (jnp.maximum(i - n_a, 0), 0))],
        out_specs=pl.BlockSpec(memory_space=pl.ANY),
        scratch_shapes=[pltpu.VMEM((2, tm, D_MODEL), F32),
                        pltpu.VMEM((tm, D_MODEL), F32),
                        pltpu.SemaphoreType.DMA((2,)),
                        pltpu.SemaphoreType.DMA])
    return pl.pallas_call(
        functools.partial(_dispatch_kernel, tm=tm, nt=nt, n_a=n_a),
        out_shape=jax.ShapeDtypeStruct((nt * TM_MOE, D_MODEL), F32),
        grid_spec=grid_spec,
        compiler_params=pltpu.CompilerParams(dimension_semantics=("arbitrary",),
                                             vmem_limit_bytes=VMEM_LIMIT, has_side_effects=True),
        name="dispatch",
    )(pad_start, pad_cnt, n_valid, pos3, pos3, h2_a, h2_b)


def _moe_kernel(texp_ref, nval_ref, first_ref, next_ref, wslot_ref, xs_hbm, wg_hbm, wu_hbm, wd_hbm, y_ref,
                xbuf, wg32, wu32, wd32, wgb, wub, wdb, sem_x, sem_w, *, tm):
    i = pl.program_id(0)
    n_valid = nval_ref[0]

    def tile_copy(j):
        s = lax.rem(j, XS_DEPTH)
        return pltpu.make_async_copy(xs_hbm.at[pl.ds(pl.multiple_of(j * tm, tm), tm)], xbuf.at[s], sem_x.at[s])

    @pl.when(i == 0)
    def _():
        for j in range(XS_DEPTH - 1):
            @pl.when(j < n_valid)
            def _():
                tile_copy(j).start()

    @pl.when(i + XS_DEPTH - 1 < n_valid)
    def _():
        tile_copy(i + XS_DEPTH - 1).start()

    def weight_copies(e, s):
        return (pltpu.make_async_copy(wg_hbm.at[e], wg32.at[s], sem_w.at[s]),
                pltpu.make_async_copy(wu_hbm.at[e], wu32.at[s], sem_w.at[s]),
                pltpu.make_async_copy(wd_hbm.at[e], wd32.at[s], sem_w.at[s]))

    @pl.when(i == 0)
    def _():
        for d in weight_copies(texp_ref[0], 0):
            d.start()

    @pl.when(i < nval_ref[0])
    def _():
        @pl.when(first_ref[i] == 1)
        def _():
            s = wslot_ref[i]
            for d in weight_copies(texp_ref[i], s):
                d.wait()
            e_next = next_ref[i]

            @pl.when(e_next >= 0)
            def _():
                for d in weight_copies(e_next, 1 - s):
                    d.start()
            wgb[...] = wg32[s].astype(BF16)
            wub[...] = wu32[s].astype(BF16)
            wdb[...] = wd32[s].astype(BF16)

        tile_copy(i).wait()
        x = xbuf[lax.rem(i, XS_DEPTH)].astype(BF16)
        g = jnp.dot(x, wgb[...], preferred_element_type=F32)
        u = jnp.dot(x, wub[...], preferred_element_type=F32)
        a = g / (1.0 + jnp.exp(-g)) * u
        y_ref[...] = jnp.dot(a.astype(BF16), wdb[...], preferred_element_type=F32)

    @pl.when(i >= nval_ref[0])
    def _():
        y_ref[...] = jnp.zeros(y_ref.shape, F32)


def _moe(tile_expert, n_valid, first, next_expert, wslot, xs, w_gate, w_up, w_down):
    nt = tile_expert.shape[0]
    tm = TM_MOE
    grid_spec = pltpu.PrefetchScalarGridSpec(
        num_scalar_prefetch=5,
        grid=(nt,),
        in_specs=[pl.BlockSpec(memory_space=pl.ANY),
                  pl.BlockSpec(memory_space=pl.ANY),
                  pl.BlockSpec(memory_space=pl.ANY),
                  pl.BlockSpec(memory_space=pl.ANY)],
        out_specs=pl.BlockSpec((tm, D_MODEL), lambda i, *_: (i, 0)),
        scratch_shapes=[pltpu.VMEM((XS_DEPTH, tm, D_MODEL), F32),
                        pltpu.VMEM((2, D_MODEL, D_EXPERT), F32),
                        pltpu.VMEM((2, D_MODEL, D_EXPERT), F32),
                        pltpu.VMEM((2, D_EXPERT, D_MODEL), F32),
                        pltpu.VMEM((D_MODEL, D_EXPERT), BF16),
                        pltpu.VMEM((D_MODEL, D_EXPERT), BF16),
                        pltpu.VMEM((D_EXPERT, D_MODEL), BF16),
                        pltpu.SemaphoreType.DMA((XS_DEPTH,)),
                        pltpu.SemaphoreType.DMA((2,))])
    return pl.pallas_call(
        functools.partial(_moe_kernel, tm=tm),
        out_shape=jax.ShapeDtypeStruct((nt * tm, D_MODEL), F32),
        grid_spec=grid_spec,
        compiler_params=_cparams(("arbitrary",)),
        name="moe",
    )(tile_expert, n_valid, first, next_expert, wslot, xs, w_gate, w_up, w_down)


def _final_kernel(pa0_ref, pb0_ref, pa_ref, pb_ref, x1_ref, rw_ref, mod_ref, y_hbm, o_ref, ybuf, sem, *, tm):
    i = pl.program_id(0)
    n = pl.num_programs(0)
    slot = i & 1

    def start_tile(pa, pb, s):
        for r in range(tm):
            pltpu.make_async_copy(y_hbm.at[pl.ds(pa[0, 0, r], 1)], ybuf.at[s, 0, pl.ds(r, 1)], sem.at[s]).start()
            pltpu.make_async_copy(y_hbm.at[pl.ds(pb[0, 0, r], 1)], ybuf.at[s, 1, pl.ds(r, 1)],
                                  sem.at[s]).start(priority=1)

    @pl.when(i == 0)
    def _():
        start_tile(pa0_ref, pb0_ref, 0)

    @pl.when(i + 1 < n)
    def _():
        start_tile(pa_ref, pb_ref, 1 - slot)

    for k in range(2):
        pltpu.make_async_copy(y_hbm.at[pl.ds(0, tm)], ybuf.at[slot, k], sem.at[slot]).wait()
    w = rw_ref[...]
    moe = w[:, 0:1] * ybuf[slot, 0] + w[:, 1:2] * ybuf[slot, 1]
    o_ref[...] = x1_ref[...] + mod_ref[0][5:6, :] * moe


def _final(x1, rw, y, pos3, mod3, *, tile_base, slot_tiles, cond_base, tiles_per_cond):
    t = x1.shape[0]
    tm = TM_ROW
    n = t // tm
    smem_tile = functools.partial(pl.BlockSpec, (1, 1, tm), memory_space=pltpu.SMEM)
    return pl.pallas_call(
        functools.partial(_final_kernel, tm=tm),
        out_shape=jax.ShapeDtypeStruct((t, D_MODEL), F32),
        grid=(n,),
        in_specs=[smem_tile(lambda i: (tile_base, 0, 0)),
                  smem_tile(lambda i: (slot_tiles + tile_base, 0, 0)),
                  smem_tile(lambda i: (tile_base + jnp.minimum(i + 1, n - 1), 0, 0)),
                  smem_tile(lambda i: (slot_tiles + tile_base + jnp.minimum(i + 1, n - 1), 0, 0)),
                  pl.BlockSpec((tm, D_MODEL), lambda i: (i, 0)),
                  pl.BlockSpec((tm, LANES), lambda i: (i, 0)),
                  pl.BlockSpec((1, N_MOD, D_MODEL), lambda i: (cond_base + i // tiles_per_cond, 0, 0)),
                  pl.BlockSpec(memory_space=pl.ANY)],
        out_specs=pl.BlockSpec((tm, D_MODEL), lambda i: (i, 0)),
        scratch_shapes=[pltpu.VMEM((2, 2, tm, D_MODEL), F32), pltpu.SemaphoreType.DMA((2,))],
        compiler_params=_cparams(("arbitrary",)),
        name="final_ctx" if cond_base == 0 else "final_lat",
    )(pos3, pos3, pos3, pos3, x1, rw, mod3, y)


def _rope_table(n_tokens):
    rows = n_tokens // GRID_W
    inv = ROPE_BASE ** (-jnp.arange(ROPE_PAIRS, dtype=F32) / ROPE_PAIRS)
    row_ang = jnp.arange(rows, dtype=F32)[:, None] * inv
    col_ang = jnp.arange(GRID_W, dtype=F32)[:, None] * inv
    cr, sr, cc, sc = jnp.cos(row_ang), jnp.sin(row_ang), jnp.cos(col_ang), jnp.sin(col_ang)
    zr, zc = jnp.zeros_like(cr), jnp.zeros_like(cc)
    row_t = jnp.concatenate([cr, cr, zr, zr, -sr, sr, zr, zr], axis=-1)
    col_t = jnp.concatenate([zc, zc, cc, cc, zc, zc, -sc, sc], axis=-1)
    return (row_t[:, None, :] + col_t[None, :, :]).reshape(n_tokens, 2 * QK_ROPE)


def _swap_halves(a):
    p = ROPE_PAIRS
    return jnp.concatenate([a[..., p:2 * p], a[..., :p], a[..., 3 * p:], a[..., 2 * p:3 * p]], axis=-1)


def _head_gains(g):
    rope = g[QK_NOPE:]
    return jnp.stack([g[:QK_NOPE], jnp.concatenate([rope, _swap_halves(rope)])])


def kernel(x_prompt, x_sample, cache_ckv, cache_krope, c, c_ctx, ada_w, ada_b, norm1_g, w_in, conv_w,
           q_lora_g, w_uq, kv_lora_g, w_ukv, q_head_g, k_head_g, w_out, norm2_g, router_g, router_e,
           w_gate, w_up, w_down):
    depth = ada_w.shape[0]
    assert depth == 1
    bp, sp, _ = x_prompt.shape
    bs, ss, _ = x_sample.shape
    past = cache_ckv.shape[2]
    tp, ts = bp * sp, bs * ss
    assert sp & (sp - 1) == 0 and ss & (ss - 1) == 0
    assert tp % TM_IN == 0 and ss % TM_IN == 0 and TM_IN % sp == 0 and ss % TQ == 0
    assert bs + 1 <= 8 and past % 256 == 0 and TM_ROW == TM_MOE and tp % TM_ROW == 0 and ts % TM_ROW == 0

    w_in_b = _winprep(w_in[0].T)
    uq = w_uq[0].reshape(Q_LORA, N_HEADS, QK_DIM)
    w_uq_b = jnp.concatenate([uq, _swap_halves(uq[..., QK_NOPE:])], axis=-1)
    w_uq_b = w_uq_b.reshape(Q_LORA, N_HEADS * HEAD_W).astype(BF16)
    ukv = w_ukv[0].reshape(KV_LORA, N_HEADS, QK_NOPE + V_DIM)
    w_uk_b = ukv[..., :QK_NOPE].reshape(KV_LORA, N_HEADS * QK_NOPE).astype(BF16)
    w_uvt_b = ukv[..., QK_NOPE:].reshape(KV_LORA, N_HEADS * V_DIM).T.astype(BF16)
    w_out_b = w_out[0].astype(BF16)
    wr = jnp.concatenate([router_g[0], router_e[0],
                          jnp.zeros((D_MODEL, ROUTER_COLS - N_EXPERT_GROUPS - N_EXPERTS), F32)], axis=1)
    wr_hi = lax.bitcast_convert_type(lax.bitcast_convert_type(wr, jnp.uint32) & jnp.uint32(0xFFFF0000), F32)
    wr = jnp.concatenate([wr_hi, wr - wr_hi], axis=1).astype(BF16)
    gq, gk = _head_gains(q_head_g[0]), _head_gains(k_head_g[0])
    n1g, n2g = norm1_g[0].reshape(1, D_MODEL), norm2_g[0].reshape(1, D_MODEL)
    qlg, kvg = q_lora_g[0].reshape(1, Q_LORA), kv_lora_g[0].reshape(1, KV_LORA)
    cs_lat = _rope_table(ss)
    cs_id = jnp.concatenate([jnp.ones((TM_IN, QK_ROPE), F32), jnp.zeros((TM_IN, QK_ROPE), F32)], axis=1)

    cond8 = jnp.concatenate([c_ctx[None, :], c, jnp.zeros((8 - 1 - bs, D_MODEL), F32)], axis=0)
    mod3 = _modulation(cond8, ada_w[0], ada_b[0]).reshape(8, N_MOD, D_MODEL)

    xp2, xs2 = x_prompt.reshape(tp, D_MODEL), x_sample.reshape(ts, D_MODEL)
    big = 1 << 30

    bg_p, cu_p, q_p, k_p, vt_p, ckv_p, kr_p = _inproj(
        xp2, mod3, n1g, w_in_b, qlg, w_uq_b, kvg, w_uk_b, w_uvt_b, gq, gk, cs_id,
        cond_base=0, tiles_per_cond=big, cs_tiles=1, emit_cache=True)
    o_p = _attention(q_p.reshape(bp, sp, -1), [(k_p.reshape(bp, sp, -1), vt_p)],
                     tq=sp, heads=N_HEADS, name="attn_ctx")
    x1_p, h2_p, rid_p, rw_p = _outproj(
        xp2, bg_p, cu_p, o_p.reshape(tp, -1), mod3, conv_w[0], w_out_b, n2g, wr,
        cond_base=0, tiles_per_cond=big, seq_len=sp)

    kr_c = cache_krope[:, 0].reshape(bs * past, QK_ROPE)
    k_c, vt_c = _kvcache(cache_ckv[:, 0].reshape(bs * past, KV_LORA), jnp.concatenate([kr_c, kr_c], axis=1),
                         cs_id, w_uk_b, w_uvt_b, gk)
    bg_s, cu_s, q_s, k_s, vt_s = _inproj(
        xs2, mod3, n1g, w_in_b, qlg, w_uq_b, kvg, w_uk_b, w_uvt_b, gq, gk, cs_lat,
        cond_base=1, tiles_per_cond=ss // TM_IN, cs_tiles=ss // TM_IN, emit_cache=False)
    o_s = _attention(q_s.reshape(bs, ss, -1),
                     [(k_c.reshape(bs, past, -1), vt_c), (k_s.reshape(bs, ss, -1), vt_s)],
                     tq=TQ, heads=4, name="attn_lat")
    x1_s, h2_s, rid_s, rw_s = _outproj(
        xs2, bg_s, cu_s, o_s.reshape(ts, -1), mod3, conv_w[0], w_out_b, n2g, wr,
        cond_base=1, tiles_per_cond=ss // TM_OUT, seq_len=ss)

    n_tok = tp + ts
    slot_tiles = n_tok // TM_ROW
    nt = 2 * n_tok // TM_MOE + N_EXPERTS
    rid = jnp.concatenate([rid_p[:, :2], rid_s[:, :2]], axis=0).T.reshape(2 * n_tok)
    pos, te, n_valid, pad_start, pad_cnt, first, next_expert, wslot = _route_tables(rid, tm=TM_MOE)
    pos3 = pos.reshape(2 * slot_tiles, 1, TM_ROW)
    xs = _dispatch(pad_start, pad_cnt, n_valid, pos3, h2_p, h2_s, nt=nt)
    y = _moe(te, n_valid, first, next_expert, wslot, xs, w_gate[0], w_up[0], w_down[0])

    y_p = _final(x1_p, rw_p, y, pos3, mod3, tile_base=0, slot_tiles=slot_tiles, cond_base=0, tiles_per_cond=big)
    y_s = _final(x1_s, rw_s, y, pos3, mod3, tile_base=tp // TM_ROW, slot_tiles=slot_tiles, cond_base=1,
                 tiles_per_cond=ss // TM_ROW)

    return (y_p.reshape(bp, sp, D_MODEL), y_s.reshape(bs, ss, D_MODEL),
            ckv_p.reshape(bp, 1, sp, KV_LORA), kr_p.reshape(bp, 1, sp, QK_ROPE))
```

```python
import functools

import jax
import jax.numpy as jnp
from jax import lax
from jax.experimental import pallas as pl
from jax.experimental.pallas import tpu as pltpu

F32 = jnp.float32
BF16 = jnp.bfloat16

D_MODEL = 2048
CONV_WIDTH = 1024
N_HEADS = 8
QK_NOPE = 128
QK_ROPE = 64
V_DIM = 128
QK_DIM = QK_NOPE + QK_ROPE
Q_LORA = 512
KV_LORA = 256
GRID_W = 64
ROPE_PAIRS = QK_ROPE // 4
ROPE_BASE = 10000.0
N_EXPERT_GROUPS = 4
EXPERTS_PER_GROUP = 8
N_EXPERTS = N_EXPERT_GROUPS * EXPERTS_PER_GROUP
D_EXPERT = 512
N_MOD = 6
EPS = 1e-6
LOG2_E = 1.4426950408889634

HEAD_W = 2 * QK_NOPE
LANES = 128
SUBLANES = 8
BF16_ROWS = 2 * SUBLANES
IN_COLS = 3 * CONV_WIDTH + Q_LORA + KV_LORA + 2 * QK_ROPE
ROUTER_COLS = LANES
VMEM_LIMIT = 56 * 1024 * 1024

TM_IN = 512
TM_OUT = 512
OUT_SLABS = 2
TQ = 256
ATTN_CHUNK = 512
ATTN_AHEAD = 2
EXP_ROWS = BF16_ROWS
TM_MOE = 256
XS_DEPTH = 4
TM_ROW = 256
BN_MOD = 1024
TK_PREP = 512
TM_CACHE = 256
COND_ROWS = SUBLANES
PAD_BITS = tuple(1 << b for b in reversed(range(TM_MOE.bit_length() - 1)))


def _cparams(sem):
    return pltpu.CompilerParams(dimension_semantics=sem, vmem_limit_bytes=VMEM_LIMIT)


def _const_spec(shape):
    nd = len(shape)
    return pl.BlockSpec(shape, lambda *_: (0,) * nd, pipeline_mode=pl.Buffered(1))


def _rms(x):
    return x * lax.rsqrt(jnp.mean(x * x, axis=-1, keepdims=True) + EPS)


def _rowsum(x):
    return jnp.sum(x, axis=-1, keepdims=True)


def _mod_kernel(c_ref, w_ref, b_ref, o_ref):
    c = c_ref[...]
    s = c / (1.0 + jnp.exp(-c))
    s_hi = s.astype(BF16)
    s_lo = (s - s_hi.astype(F32)).astype(BF16)
    w = w_ref[...]
    w_hi = w.astype(BF16)
    w_lo = (w - w_hi.astype(F32)).astype(BF16)
    rows = s.shape[0]
    a = jnp.dot(jnp.concatenate([s_hi, s_lo], axis=0), w_hi, preferred_element_type=F32)
    b = jnp.dot(s_hi, w_lo, preferred_element_type=F32)
    o_ref[...] = a[:rows] + (a[rows:] + b) + b_ref[...]


def _modulation(cond8, ada_w, ada_b):
    n = ada_w.shape[1]
    return pl.pallas_call(
        _mod_kernel,
        out_shape=jax.ShapeDtypeStruct((COND_ROWS, n), F32),
        grid=(n // BN_MOD,),
        in_specs=[pl.BlockSpec((COND_ROWS, D_MODEL), lambda j: (0, 0)),
                  pl.BlockSpec((D_MODEL, BN_MOD), lambda j: (0, j)),
                  pl.BlockSpec((1, BN_MOD), lambda j: (0, j))],
        out_specs=pl.BlockSpec((COND_ROWS, BN_MOD), lambda j: (0, j)),
        compiler_params=_cparams(("arbitrary",)),
        name="mod",
    )(cond8, ada_w, ada_b.reshape(1, n))


def _winprep_kernel(w_ref, o_ref):
    n = w_ref.shape[0]
    o_ref[:n, :] = w_ref[...].astype(BF16)
    p = ROPE_PAIRS
    for dst, src in ((0, p), (p, 0), (2 * p, 3 * p), (3 * p, 2 * p)):
        o_ref[n + dst:n + dst + p, :] = w_ref[n - QK_ROPE + src:n - QK_ROPE + src + p, :].astype(BF16)


def _winprep(w_in_t):
    n, k = w_in_t.shape
    tk = TK_PREP
    return pl.pallas_call(
        _winprep_kernel,
        out_shape=jax.ShapeDtypeStruct((IN_COLS, k), BF16),
        grid=(k // tk,),
        in_specs=[pl.BlockSpec((n, tk), lambda i: (0, i))],
        out_specs=pl.BlockSpec((IN_COLS, tk), lambda i: (0, i)),
        compiler_params=_cparams(("arbitrary",)),
        name="winprep",
    )(w_in_t)


def _emit_kv(ckv, kraw, cs, w_uk_ref, w_uvt_ref, gk_ref, k_ref, vt_ref):
    cb = ckv.astype(BF16)
    kn_all = jnp.dot(cb, w_uk_ref[...], preferred_element_type=F32)
    vt = lax.dot_general(w_uvt_ref[...], cb, (((1,), (1,)), ((), ())), preferred_element_type=F32)
    vt_ref[...] = vt.astype(BF16)
    ss_rope = 0.5 * _rowsum(kraw * kraw)
    t = kraw * (cs * gk_ref[1:2, :])
    tt = t + pltpu.roll(t, QK_ROPE, axis=1)
    g_nope = gk_ref[0:1, :]
    for h in range(N_HEADS):
        kn = kn_all[:, h * QK_NOPE:(h + 1) * QK_NOPE]
        r = lax.rsqrt((_rowsum(kn * kn) + ss_rope) * (1.0 / QK_DIM) + EPS)
        k_ref[:, h * HEAD_W:h * HEAD_W + QK_NOPE] = (kn * r * g_nope).astype(BF16)
        k_ref[:, h * HEAD_W + QK_NOPE:(h + 1) * HEAD_W] = (tt * r).astype(BF16)


def _kvcache_kernel(ckv_ref, kraw_ref, cs_ref, w_uk_ref, w_uvt_ref, gk_ref, k_ref, vt_ref):
    _emit_kv(ckv_ref[...], kraw_ref[...], cs_ref[...], w_uk_ref, w_uvt_ref, gk_ref, k_ref, vt_ref)


def _kvcache(ckv, kraw, cs_id, w_uk_b, w_uvt_b, gk):
    n = ckv.shape[0]
    tm = TM_CACHE
    return pl.pallas_call(
        _kvcache_kernel,
        out_shape=(jax.ShapeDtypeStruct((n, N_HEADS * HEAD_W), BF16),
                   jax.ShapeDtypeStruct((N_HEADS * V_DIM, n), BF16)),
        grid=(n // tm,),
        in_specs=[pl.BlockSpec((tm, KV_LORA), lambda i: (i, 0)),
                  pl.BlockSpec((tm, LANES), lambda i: (i, 0)),
                  pl.BlockSpec((tm, LANES), lambda i: (0, 0)),
                  _const_spec((KV_LORA, N_HEADS * QK_NOPE)),
                  _const_spec((N_HEADS * V_DIM, KV_LORA)),
                  _const_spec((2, LANES))],
        out_specs=(pl.BlockSpec((tm, N_HEADS * HEAD_W), lambda i: (i, 0)),
                   pl.BlockSpec((N_HEADS * V_DIM, tm), lambda i: (0, i))),
        compiler_params=_cparams(("arbitrary",)),
        name="kvcache",
    )(ckv, kraw, cs_id, w_uk_b, w_uvt_b, gk)


def _inproj_kernel(x_ref, mod_ref, n1g_ref, w_in_ref, qlg_ref, w_uq_ref, kvg_ref, w_uk_ref, w_uvt_ref,
                   gq_ref, gk_ref, cs_ref, bg_ref, cu_ref, q_ref, k_ref, vt_ref, *cache_refs):
    x = x_ref[...]
    mod = mod_ref[0]
    h = _rms(x) * n1g_ref[...] * (1.0 + mod[1:2, :]) + mod[0:1, :]
    hb = h.astype(BF16)

    def proj(a, b):
        return lax.dot_general(hb, w_in_ref[a:b, :], (((1,), (1,)), ((), ())), preferred_element_type=F32)

    c1, c2, c3 = CONV_WIDTH, 2 * CONV_WIDTH, 3 * CONV_WIDTH
    q_lat = proj(c3, c3 + Q_LORA)
    kvk = proj(c3 + Q_LORA, IN_COLS)
    bg_ref[...] = proj(0, c1).astype(BF16)
    cs = cs_ref[...]

    qn = _rms(q_lat) * qlg_ref[...]
    q = jnp.dot(qn.astype(BF16), w_uq_ref[...], preferred_element_type=F32)
    scale = QK_DIM ** -0.5 * LOG2_E
    g_nope = gq_ref[0:1, :] * scale
    tq = cs * (gq_ref[1:2, :] * scale)
    for hd in range(N_HEADS):
        lo = q[:, hd * HEAD_W:hd * HEAD_W + QK_NOPE]
        up = q[:, hd * HEAD_W + QK_NOPE:(hd + 1) * HEAD_W]
        ss = _rowsum(lo * lo) + 0.5 * _rowsum(up * up)
        r = lax.rsqrt(ss * (1.0 / QK_DIM) + EPS)
        q_ref[:, hd * HEAD_W:hd * HEAD_W + QK_NOPE] = (lo * r * g_nope).astype(BF16)
        q_ref[:, hd * HEAD_W + QK_NOPE:(hd + 1) * HEAD_W] = (up * r * tq).astype(BF16)

    cu_ref[...] = (proj(c1, c2) * proj(c2, c3)).astype(BF16)

    kv_lat = kvk[:, :KV_LORA]
    kraw = kvk[:, KV_LORA:]
    ckv = _rms(kv_lat) * kvg_ref[...]
    if cache_refs:
        ckv_out_ref, kr_out_ref = cache_refs
        ckv_out_ref[...] = ckv
        kr_out_ref[...] = kraw[:, :QK_ROPE]
    _emit_kv(ckv, kraw, cs, w_uk_ref, w_uvt_ref, gk_ref, k_ref, vt_ref)


def _inproj(x2d, mod3, n1g, w_in_b, qlg, w_uq_b, kvg, w_uk_b, w_uvt_b, gq, gk, cs, *,
            cond_base, tiles_per_cond, cs_tiles, emit_cache):
    t = x2d.shape[0]
    tm = TM_IN
    out_shape = [jax.ShapeDtypeStruct((t, CONV_WIDTH), BF16),
                 jax.ShapeDtypeStruct((t, CONV_WIDTH), BF16),
                 jax.ShapeDtypeStruct((t, N_HEADS * HEAD_W), BF16),
                 jax.ShapeDtypeStruct((t, N_HEADS * HEAD_W), BF16),
                 jax.ShapeDtypeStruct((N_HEADS * V_DIM, t), BF16)]
    out_specs = [pl.BlockSpec((tm, CONV_WIDTH), lambda i: (i, 0)),
                 pl.BlockSpec((tm, CONV_WIDTH), lambda i: (i, 0)),
                 pl.BlockSpec((tm, N_HEADS * HEAD_W), lambda i: (i, 0)),
                 pl.BlockSpec((tm, N_HEADS * HEAD_W), lambda i: (i, 0)),
                 pl.BlockSpec((N_HEADS * V_DIM, tm), lambda i: (0, i))]
    if emit_cache:
        out_shape += [jax.ShapeDtypeStruct((t, KV_LORA), F32), jax.ShapeDtypeStruct((t, QK_ROPE), F32)]
        out_specs += [pl.BlockSpec((tm, KV_LORA), lambda i: (i, 0)),
                      pl.BlockSpec((tm, QK_ROPE), lambda i: (i, 0))]
    return pl.pallas_call(
        _inproj_kernel,
        out_shape=tuple(out_shape),
        grid=(t // tm,),
        in_specs=[pl.BlockSpec((tm, D_MODEL), lambda i: (i, 0)),
                  pl.BlockSpec((1, N_MOD, D_MODEL), lambda i: (cond_base + i // tiles_per_cond, 0, 0)),
                  _const_spec((1, D_MODEL)),
                  _const_spec((IN_COLS, D_MODEL)),
                  _const_spec((1, Q_LORA)),
                  _const_spec((Q_LORA, N_HEADS * HEAD_W)),
                  _const_spec((1, KV_LORA)),
                  _const_spec((KV_LORA, N_HEADS * QK_NOPE)),
                  _const_spec((N_HEADS * V_DIM, KV_LORA)),
                  _const_spec((2, LANES)),
                  _const_spec((2, LANES)),
                  pl.BlockSpec((tm, LANES), lambda i: (i % cs_tiles, 0))],
        out_specs=tuple(out_specs),
        compiler_params=_cparams(("arbitrary",)),
        name="inproj_ctx" if emit_cache else "inproj_lat",
    )(x2d, mod3, n1g, w_in_b, qlg, w_uq_b, kvg, w_uk_b, w_uvt_b, gq, gk, cs)


def _attn_kernel(*refs, n_kv, heads):
    q_ref = refs[0]
    o_ref = refs[-1]
    chunks = []
    for j in range(n_kv):
        sk = refs[1 + 2 * j].shape[1]
        step = min(sk, ATTN_CHUNK)
        chunks += [(j, lo, lo + step) for lo in range(0, sk, step)]
    q = [q_ref[0, :, h * HEAD_W:(h + 1) * HEAD_W] for h in range(heads)]

    def score(h, c):
        j, lo, hi = chunks[c]
        k = refs[1 + 2 * j][0, lo:hi, h * HEAD_W:(h + 1) * HEAD_W]
        return lax.dot_general(k, q[h], (((1,), (1,)), ((), ())), preferred_element_type=F32)

    m = [None] * heads
    acc = [None] * heads
    ahead = [[score(h, c) for c in range(min(ATTN_AHEAD, len(chunks)))] for h in range(heads)]
    for c, (j, lo, hi) in enumerate(chunks):
        for h in range(heads):
            s = ahead[h].pop(0)
            if c + ATTN_AHEAD < len(chunks):
                ahead[h].append(score(h, c + ATTN_AHEAD))
            mc = jnp.max(s, axis=0, keepdims=True)
            m_new = mc if m[h] is None else jnp.maximum(m[h], mc)
            p = jnp.concatenate([jnp.exp2(s[r:r + EXP_ROWS] - m_new).astype(BF16)
                                 for r in range(0, hi - lo, EXP_ROWS)], axis=0)
            vt = refs[2 + 2 * j][h * V_DIM:(h + 1) * V_DIM, lo:hi]
            vt1 = jnp.concatenate([vt, jnp.ones((BF16_ROWS, hi - lo), BF16)], axis=0)
            part = jnp.dot(vt1, p, preferred_element_type=F32)
            acc[h] = part if acc[h] is None else acc[h] * jnp.exp2(m[h] - m_new) + part
            m[h] = m_new
    for h in range(heads):
        ot = acc[h][:V_DIM, :] / acc[h][V_DIM:V_DIM + 1, :]
        o_ref[0, :, h * V_DIM:(h + 1) * V_DIM] = ot.T.astype(BF16)


def _attention(q, kvs, *, tq, heads, name):
    b, s, _ = q.shape
    in_specs = [pl.BlockSpec((1, tq, heads * HEAD_W), lambda bi, hi, qi: (bi, qi, hi))]
    args = [q]
    kv_bytes = sum(k.shape[1] * heads * (HEAD_W + V_DIM) * 2 for k, _ in kvs)
    mode = pl.Buffered(1 if 2 * kv_bytes > VMEM_LIMIT // 3 else 2)
    for k, v in kvs:
        sk = k.shape[1]
        in_specs.append(pl.BlockSpec((1, sk, heads * HEAD_W), lambda bi, hi, qi: (bi, 0, hi), pipeline_mode=mode))
        in_specs.append(pl.BlockSpec((heads * V_DIM, sk), lambda bi, hi, qi: (hi, bi), pipeline_mode=mode))
        args += [k, v]
    return pl.pallas_call(
        functools.partial(_attn_kernel, n_kv=len(kvs), heads=heads),
        out_shape=jax.ShapeDtypeStruct((b, s, N_HEADS * V_DIM), BF16),
        grid=(b, N_HEADS // heads, s // tq),
        in_specs=in_specs,
        out_specs=pl.BlockSpec((1, tq, heads * V_DIM), lambda bi, hi, qi: (bi, qi, hi)),
        compiler_params=_cparams(("arbitrary", "arbitrary", "arbitrary")),
        name=name,
    )(*args)


def _outproj_kernel(x_ref, bg_ref, cu_ref, cup_ref, cun_ref, o_ref, mod_ref, cw_ref, w_out_ref,
                    n2g_ref, wr_ref, x1_ref, h2_ref, rid_ref, rw_ref, *, tm, seq_len):
    i = pl.program_id(0)
    mod = mod_ref[0]
    cu = cu_ref[...].astype(F32)
    prev_row = cup_ref[...].astype(F32)[BF16_ROWS - 1:BF16_ROWS, :]
    next_row = cun_ref[...].astype(F32)[0:1, :]
    row = lax.broadcasted_iota(jnp.int32, (tm, 1), 0)
    pos = (i * tm + row) & (seq_len - 1)
    up = jnp.where(row == 0, prev_row, pltpu.roll(cu, 1, axis=0))
    up = jnp.where(pos == 0, 0.0, up)
    dn = jnp.where(row == tm - 1, next_row, pltpu.roll(cu, tm - 1, axis=0))
    dn = jnp.where(pos == seq_len - 1, 0.0, dn)
    cw = cw_ref[...]
    y_conv = bg_ref[...].astype(F32) * (up * cw[0:1, :] + cu * cw[1:2, :] + dn * cw[2:3, :])
    y_conv = y_conv.astype(BF16)

    slabs = [(r, r + tm // OUT_SLABS) for r in range(0, tm, tm // OUT_SLABS)]
    mixes = []
    for lo, hi in slabs:
        mix = jnp.dot(y_conv[lo:hi], w_out_ref[:CONV_WIDTH, :], preferred_element_type=F32)
        mixes.append(mix + jnp.dot(o_ref[lo:hi, :], w_out_ref[CONV_WIDTH:, :], preferred_element_type=F32))
    for (lo, hi), mix in zip(slabs, mixes):
        x1 = x_ref[lo:hi, :] + mod[2:3, :] * mix
        x1_ref[lo:hi, :] = x1
        h2 = _rms(x1) * n2g_ref[...] * (1.0 + mod[4:5, :]) + mod[3:4, :]
        h2_hi = h2.astype(BF16)
        h2_ref[lo:hi, :] = h2_hi

        h2_lo = (h2 - h2_hi.astype(F32)).astype(BF16)
        hh_hl = jnp.dot(h2_hi, wr_ref[...], preferred_element_type=F32)
        lh = jnp.dot(h2_lo, wr_ref[:, :ROUTER_COLS], preferred_element_type=F32)
        logits = hh_hl[:, :ROUTER_COLS] + (hh_hl[:, ROUTER_COLS:] + lh)
        lane = lax.broadcasted_iota(jnp.int32, logits.shape, 1)
        neg = -jnp.inf
        big = jnp.int32(1 << 20)
        gl = jnp.where(lane < N_EXPERT_GROUPS, logits, neg)
        gmax = jnp.max(gl, axis=-1, keepdims=True)
        p_top = 1.0 / _rowsum(jnp.exp(gl - gmax))
        g_top = jnp.min(jnp.where(gl == gmax, lane, big), axis=-1, keepdims=True)
        e_lo = N_EXPERT_GROUPS + EXPERTS_PER_GROUP * g_top
        el = jnp.where((lane >= e_lo) & (lane < e_lo + EXPERTS_PER_GROUP), logits, neg)
        v1 = jnp.max(el, axis=-1, keepdims=True)
        i1 = jnp.min(jnp.where(el == v1, lane, big), axis=-1, keepdims=True)
        el2 = jnp.where(lane == i1, neg, el)
        v2 = jnp.max(el2, axis=-1, keepdims=True)
        i2 = jnp.min(jnp.where(el2 == v2, lane, big), axis=-1, keepdims=True)
        e21 = jnp.exp(v2 - v1)
        w1 = p_top / (1.0 + e21)
        w2 = w1 * e21
        rid_ref[lo:hi, :] = jnp.where(lane == 0, i1 - N_EXPERT_GROUPS, i2 - N_EXPERT_GROUPS)
        rw_ref[lo:hi, :] = jnp.where(lane == 0, w1, w2)


def _outproj(x2d, bg, cu, o2d, mod3, conv_w, w_out_b, n2g, wr, *, cond_base, tiles_per_cond, seq_len):
    t = x2d.shape[0]
    tm = TM_OUT
    hb = tm // BF16_ROWS
    nhb = t // BF16_ROWS
    return pl.pallas_call(
        functools.partial(_outproj_kernel, tm=tm, seq_len=seq_len),
        out_shape=(jax.ShapeDtypeStruct((t, D_MODEL), F32),
                   jax.ShapeDtypeStruct((t, D_MODEL), BF16),
                   jax.ShapeDtypeStruct((t, LANES), jnp.int32),
                   jax.ShapeDtypeStruct((t, LANES), F32)),
        grid=(t // tm,),
        in_specs=[pl.BlockSpec((tm, D_MODEL), lambda i: (i, 0)),
                  pl.BlockSpec((tm, CONV_WIDTH), lambda i: (i, 0)),
                  pl.BlockSpec((tm, CONV_WIDTH), lambda i: (i, 0)),
                  pl.BlockSpec((BF16_ROWS, CONV_WIDTH), lambda i: (jnp.maximum(i * hb - 1, 0), 0)),
                  pl.BlockSpec((BF16_ROWS, CONV_WIDTH), lambda i: (jnp.minimum((i + 1) * hb, nhb - 1), 0)),
                  pl.BlockSpec((tm, N_HEADS * V_DIM), lambda i: (i, 0)),
                  pl.BlockSpec((1, N_MOD, D_MODEL), lambda i: (cond_base + i // tiles_per_cond, 0, 0)),
                  _const_spec((3, CONV_WIDTH)),
                  _const_spec((D_MODEL, D_MODEL)),
                  _const_spec((1, D_MODEL)),
                  _const_spec((D_MODEL, 2 * ROUTER_COLS))],
        out_specs=(pl.BlockSpec((tm, D_MODEL), lambda i: (i, 0)),
                   pl.BlockSpec((tm, D_MODEL), lambda i: (i, 0)),
                   pl.BlockSpec((tm, LANES), lambda i: (i, 0)),
                   pl.BlockSpec((tm, LANES), lambda i: (i, 0))),
        compiler_params=_cparams(("arbitrary",)),
        name="outproj_ctx" if cond_base == 0 else "outproj_lat",
    )(x2d, bg, cu, cu, cu, o2d, mod3, conv_w, w_out_b, n2g, wr)


def _route_tables(rid, *, tm):
    n_pairs = rid.shape[0]
    nt = n_pairs // tm + N_EXPERTS
    experts = jnp.arange(N_EXPERTS, dtype=jnp.int32)
    onehot = (rid[None, :] == experts[:, None]).astype(jnp.int32)
    csum = jnp.cumsum(onehot, axis=1)
    counts = csum[:, -1]
    tiles_e = (counts + tm - 1) // tm
    tile_end = jnp.cumsum(tiles_e)
    tile_start = tile_end - tiles_e
    n_valid = tile_end[-1]
    pos = jnp.sum(onehot * (csum - 1 + tile_start[:, None] * tm), axis=0)
    tile_raw = jnp.arange(nt, dtype=jnp.int32)
    tile_idx = jnp.minimum(tile_raw, n_valid - 1)
    te = jnp.sum((tile_end[None, :] <= tile_idx[:, None]).astype(jnp.int32), axis=1)
    used = tiles_e > 0
    first = ((tile_raw == tile_start[te]) & (tile_raw < n_valid)).astype(jnp.int32)
    nxt = lax.cummin(jnp.where(used, experts, N_EXPERTS), axis=0, reverse=True)
    nxt = jnp.concatenate([nxt[1:], jnp.full((1,), N_EXPERTS, jnp.int32)])
    next_expert = jnp.where(nxt[te] < N_EXPERTS, nxt[te], -1).astype(jnp.int32)
    wslot = ((jnp.cumsum(used.astype(jnp.int32)) - 1)[te] & 1).astype(jnp.int32)
    return (pos, te, n_valid.reshape(1).astype(jnp.int32),
            (tile_start * tm + counts).astype(jnp.int32), (tiles_e * tm - counts).astype(jnp.int32),
            first, next_expert, wslot)


def _dispatch_kernel(pstart_ref, pcnt_ref, nval_ref, pos0_ref, pos1_ref, h2a_ref, h2b_ref, xs_hbm,
                     buf, zbuf, sem, sem_z, *, tm, nt, n_a):
    i = pl.program_id(0)
    n = pl.num_programs(0)
    slot = i & 1

    def row_copy(r, pos_ref, s):
        return pltpu.make_async_copy(buf.at[s, pl.ds(r, 1)], xs_hbm.at[pl.ds(pos_ref[0, 0, r], 1)], sem.at[s])

    def wait_tile(s):
        for _ in range(2):
            pltpu.make_async_copy(buf.at[s], xs_hbm.at[pl.ds(0, tm)], sem.at[s]).wait()

    def pad_copies(e, fn):
        cnt = pcnt_ref[e]
        start = pstart_ref[e]
        off = start + cnt
        for b in PAD_BITS:
            if b < SUBLANES:
                break
            off = off - (cnt & b)
            dst = pl.multiple_of(off, SUBLANES)

            @pl.when((cnt & b) != 0)
            def _():
                fn(pltpu.make_async_copy(zbuf.at[pl.ds(0, b)], xs_hbm.at[pl.ds(dst, b)], sem_z))
        for j in range(SUBLANES - 1):
            @pl.when(j < (cnt & (SUBLANES - 1)))
            def _():
                fn(pltpu.make_async_copy(zbuf.at[pl.ds(0, 1)], xs_hbm.at[pl.ds(start + j, 1)], sem_z))

    def tail_copy(j):
        return pltpu.make_async_copy(zbuf, xs_hbm.at[pl.ds(pl.multiple_of(j * tm, tm), tm)], sem_z)

    @pl.when(i == 0)
    def _():
        zbuf[...] = jnp.zeros((tm, D_MODEL), F32)
        lax.fori_loop(0, N_EXPERTS, lambda e, c: (pad_copies(e, lambda d: d.start()), c)[1], 0)
        lax.fori_loop(nval_ref[0], nt, lambda j, c: (tail_copy(j).start(), c)[1], 0)
        lax.fori_loop(0, N_EXPERTS, lambda e, c: (pad_copies(e, lambda d: d.wait()), c)[1], 0)
        lax.fori_loop(nval_ref[0], nt, lambda j, c: (tail_copy(j).wait(), c)[1], 0)

    @pl.when(i >= 2)
    def _():
        wait_tile(slot)

    @pl.when(i < n_a)
    def _():
        buf[slot] = h2a_ref[...].astype(F32)

    @pl.when(i >= n_a)
    def _():
        buf[slot] = h2b_ref[...].astype(F32)

    for r in range(tm):
        row_copy(r, pos0_ref, slot).start()
        row_copy(r, pos1_ref, slot).start(priority=1)

    @pl.when(i == n - 1)
    def _():
        wait_tile(slot)
        wait_tile(1 - slot)


def _dispatch(pad_start, pad_cnt, n_valid, pos3, h2_a, h2_b, *, nt):
    tm = TM_ROW
    n_a, n_b = h2_a.shape[0] // tm, h2_b.shape[0] // tm
    smem_tile = functools.partial(pl.BlockSpec, (1, 1, tm), memory_space=pltpu.SMEM)
    grid_spec = pltpu.PrefetchScalarGridSpec(
        num_scalar_prefetch=3,
        grid=(n_a + n_b,),
        in_specs=[smem_tile(lambda i, *_: (i, 0, 0)),
                  smem_tile(lambda i, *_: (n_a + n_b + i, 0, 0)),
                  pl.BlockSpec((tm, D_MODEL), lambda i, *_: (jnp.minimum(i, n_a - 1), 0)),
                  pl.BlockSpec((tm, D_MODEL), lambda i, *_: (jnp.maximum(i - n_a, 0), 0))],
        out_specs=pl.BlockSpec(memory_space=pl.ANY),
        scratch_shapes=[pltpu.VMEM((2, tm, D_MODEL), F32),
                        pltpu.VMEM((tm, D_MODEL), F32),
                        pltpu.SemaphoreType.DMA((2,)),
                        pltpu.SemaphoreType.DMA])
    return pl.pallas_call(
        functools.partial(_dispatch_kernel, tm=tm, nt=nt, n_a=n_a),
        out_shape=jax.ShapeDtypeStruct((nt * TM_MOE, D_MODEL), F32),
        grid_spec=grid_spec,
        compiler_params=pltpu.CompilerParams(dimension_semantics=("arbitrary",),
                                             vmem_limit_bytes=VMEM_LIMIT, has_side_effects=True),
        name="dispatch",
    )(pad_start, pad_cnt, n_valid, pos3, pos3, h2_a, h2_b)


def _moe_kernel(texp_ref, nval_ref, first_ref, next_ref, wslot_ref, xs_hbm, wg_hbm, wu_hbm, wd_hbm, y_ref,
                xbuf, wg32, wu32, wd32, wgb, wub, wdb, sem_x, sem_w, *, tm):
    i = pl.program_id(0)
    n_valid = nval_ref[0]

    def tile_copy(j):
        s = lax.rem(j, XS_DEPTH)
        return pltpu.make_async_copy(xs_hbm.at[pl.ds(pl.multiple_of(j * tm, tm), tm)], xbuf.at[s], sem_x.at[s])

    @pl.when(i == 0)
    def _():
        for j in range(XS_DEPTH - 1):
            @pl.when(j < n_valid)
            def _():
                tile_copy(j).start()

    @pl.when(i + XS_DEPTH - 1 < n_valid)
    def _():
        tile_copy(i + XS_DEPTH - 1).start()

    def weight_copies(e, s):
        return (pltpu.make_async_copy(wg_hbm.at[e], wg32.at[s], sem_w.at[s]),
                pltpu.make_async_copy(wu_hbm.at[e], wu32.at[s], sem_w.at[s]),
                pltpu.make_async_copy(wd_hbm.at[e], wd32.at[s], sem_w.at[s]))

    @pl.when(i == 0)
    def _():
        for d in weight_copies(texp_ref[0], 0):
            d.start()

    @pl.when(i < nval_ref[0])
    def _():
        @pl.when(first_ref[i] == 1)
        def _():
            s = wslot_ref[i]
            for d in weight_copies(texp_ref[i], s):
                d.wait()
            e_next = next_ref[i]

            @pl.when(e_next >= 0)
            def _():
                for d in weight_copies(e_next, 1 - s):
                    d.start()
            wgb[...] = wg32[s].astype(BF16)
            wub[...] = wu32[s].astype(BF16)
            wdb[...] = wd32[s].astype(BF16)

        tile_copy(i).wait()
        x = xbuf[lax.rem(i, XS_DEPTH)].astype(BF16)
        g = jnp.dot(x, wgb[...], preferred_element_type=F32)
        u = jnp.dot(x, wub[...], preferred_element_type=F32)
        a = g / (1.0 + jnp.exp(-g)) * u
        y_ref[...] = jnp.dot(a.astype(BF16), wdb[...], preferred_element_type=F32)

    @pl.when(i >= nval_ref[0])
    def _():
        y_ref[...] = jnp.zeros(y_ref.shape, F32)


def _moe(tile_expert, n_valid, first, next_expert, wslot, xs, w_gate, w_up, w_down):
    nt = tile_expert.shape[0]
    tm = TM_MOE
    grid_spec = pltpu.PrefetchScalarGridSpec(
        num_scalar_prefetch=5,
        grid=(nt,),
        in_specs=[pl.BlockSpec(memory_space=pl.ANY),
                  pl.BlockSpec(memory_space=pl.ANY),
                  pl.BlockSpec(memory_space=pl.ANY),
                  pl.BlockSpec(memory_space=pl.ANY)],
        out_specs=pl.BlockSpec((tm, D_MODEL), lambda i, *_: (i, 0)),
        scratch_shapes=[pltpu.VMEM((XS_DEPTH, tm, D_MODEL), F32),
                        pltpu.VMEM((2, D_MODEL, D_EXPERT), F32),
                        pltpu.VMEM((2, D_MODEL, D_EXPERT), F32),
                        pltpu.VMEM((2, D_EXPERT, D_MODEL), F32),
                        pltpu.VMEM((D_MODEL, D_EXPERT), BF16),
                        pltpu.VMEM((D_MODEL, D_EXPERT), BF16),
                        pltpu.VMEM((D_EXPERT, D_MODEL), BF16),
                        pltpu.SemaphoreType.DMA((XS_DEPTH,)),
                        pltpu.SemaphoreType.DMA((2,))])
    return pl.pallas_call(
        functools.partial(_moe_kernel, tm=tm),
        out_shape=jax.ShapeDtypeStruct((nt * tm, D_MODEL), F32),
        grid_spec=grid_spec,
        compiler_params=_cparams(("arbitrary",)),
        name="moe",
    )(tile_expert, n_valid, first, next_expert, wslot, xs, w_gate, w_up, w_down)


def _final_kernel(pa0_ref, pb0_ref, pa_ref, pb_ref, x1_ref, rw_ref, mod_ref, y_hbm, o_ref, ybuf, sem, *, tm):
    i = pl.program_id(0)
    n = pl.num_programs(0)
    slot = i & 1

    def start_tile(pa, pb, s):
        for r in range(tm):
            pltpu.make_async_copy(y_hbm.at[pl.ds(pa[0, 0, r], 1)], ybuf.at[s, 0, pl.ds(r, 1)], sem.at[s]).start()
            pltpu.make_async_copy(y_hbm.at[pl.ds(pb[0, 0, r], 1)], ybuf.at[s, 1, pl.ds(r, 1)],
                                  sem.at[s]).start(priority=1)

    @pl.when(i == 0)
    def _():
        start_tile(pa0_ref, pb0_ref, 0)

    @pl.when(i + 1 < n)
    def _():
        start_tile(pa_ref, pb_ref, 1 - slot)

    for k in range(2):
        pltpu.make_async_copy(y_hbm.at[pl.ds(0, tm)], ybuf.at[slot, k], sem.at[slot]).wait()
    w = rw_ref[...]
    moe = w[:, 0:1] * ybuf[slot, 0] + w[:, 1:2] * ybuf[slot, 1]
    o_ref[...] = x1_ref[...] + mod_ref[0][5:6, :] * moe


def _final(x1, rw, y, pos3, mod3, *, tile_base, slot_tiles, cond_base, tiles_per_cond):
    t = x1.shape[0]
    tm = TM_ROW
    n = t // tm
    smem_tile = functools.partial(pl.BlockSpec, (1, 1, tm), memory_space=pltpu.SMEM)
    return pl.pallas_call(
        functools.partial(_final_kernel, tm=tm),
        out_shape=jax.ShapeDtypeStruct((t, D_MODEL), F32),
        grid=(n,),
        in_specs=[smem_tile(lambda i: (tile_base, 0, 0)),
                  smem_tile(lambda i: (slot_tiles + tile_base, 0, 0)),
                  smem_tile(lambda i: (tile_base + jnp.minimum(i + 1, n - 1), 0, 0)),
                  smem_tile(lambda i: (slot_tiles + tile_base + jnp.minimum(i + 1, n - 1), 0, 0)),
                  pl.BlockSpec((tm, D_MODEL), lambda i: (i, 0)),
                  pl.BlockSpec((tm, LANES), lambda i: (i, 0)),
                  pl.BlockSpec((1, N_MOD, D_MODEL), lambda i: (cond_base + i // tiles_per_cond, 0, 0)),
                  pl.BlockSpec(memory_space=pl.ANY)],
        out_specs=pl.BlockSpec((tm, D_MODEL), lambda i: (i, 0)),
        scratch_shapes=[pltpu.VMEM((2, 2, tm, D_MODEL), F32), pltpu.SemaphoreType.DMA((2,))],
        compiler_params=_cparams(("arbitrary",)),
        name="final_ctx" if cond_base == 0 else "final_lat",
    )(pos3, pos3, pos3, pos3, x1, rw, mod3, y)


def _rope_table(n_tokens):
    rows = n_tokens // GRID_W
    inv = ROPE_BASE ** (-jnp.arange(ROPE_PAIRS, dtype=F32) / ROPE_PAIRS)
    row_ang = jnp.arange(rows, dtype=F32)[:, None] * inv
    col_ang = jnp.arange(GRID_W, dtype=F32)[:, None] * inv
    cr, sr, cc, sc = jnp.cos(row_ang), jnp.sin(row_ang), jnp.cos(col_ang), jnp.sin(col_ang)
    zr, zc = jnp.zeros_like(cr), jnp.zeros_like(cc)
    row_t = jnp.concatenate([cr, cr, zr, zr, -sr, sr, zr, zr], axis=-1)
    col_t = jnp.concatenate([zc, zc, cc, cc, zc, zc, -sc, sc], axis=-1)
    return (row_t[:, None, :] + col_t[None, :, :]).reshape(n_tokens, 2 * QK_ROPE)


def _swap_halves(a):
    p = ROPE_PAIRS
    return jnp.concatenate([a[..., p:2 * p], a[..., :p], a[..., 3 * p:], a[..., 2 * p:3 * p]], axis=-1)


def _head_gains(g):
    rope = g[QK_NOPE:]
    return jnp.stack([g[:QK_NOPE], jnp.concatenate([rope, _swap_halves(rope)])])


def kernel(x_prompt, x_sample, cache_ckv, cache_krope, c, c_ctx, ada_w, ada_b, norm1_g, w_in, conv_w,
           q_lora_g, w_uq, kv_lora_g, w_ukv, q_head_g, k_head_g, w_out, norm2_g, router_g, router_e,
           w_gate, w_up, w_down):
    depth = ada_w.shape[0]
    assert depth == 1
    bp, sp, _ = x_prompt.shape
    bs, ss, _ = x_sample.shape
    past = cache_ckv.shape[2]
    tp, ts = bp * sp, bs * ss
    assert sp & (sp - 1) == 0 and ss & (ss - 1) == 0
    assert tp % TM_IN == 0 and ss % TM_IN == 0 and TM_IN % sp == 0 and ss % TQ == 0
    assert bs + 1 <= COND_ROWS and past % TM_CACHE == 0 and TM_ROW == TM_MOE and tp % TM_ROW == 0 and ts % TM_ROW == 0

    w_in_b = _winprep(w_in[0].T)
    uq = w_uq[0].reshape(Q_LORA, N_HEADS, QK_DIM)
    w_uq_b = jnp.concatenate([uq, _swap_halves(uq[..., QK_NOPE:])], axis=-1)
    w_uq_b = w_uq_b.reshape(Q_LORA, N_HEADS * HEAD_W).astype(BF16)
    ukv = w_ukv[0].reshape(KV_LORA, N_HEADS, QK_NOPE + V_DIM)
    w_uk_b = ukv[..., :QK_NOPE].reshape(KV_LORA, N_HEADS * QK_NOPE).astype(BF16)
    w_uvt_b = ukv[..., QK_NOPE:].reshape(KV_LORA, N_HEADS * V_DIM).T.astype(BF16)
    w_out_b = w_out[0].astype(BF16)
    wr = jnp.concatenate([router_g[0], router_e[0],
                          jnp.zeros((D_MODEL, ROUTER_COLS - N_EXPERT_GROUPS - N_EXPERTS), F32)], axis=1)
    wr_hi = lax.bitcast_convert_type(lax.bitcast_convert_type(wr, jnp.uint32) & jnp.uint32(0xFFFF0000), F32)
    wr = jnp.concatenate([wr_hi, wr - wr_hi], axis=1).astype(BF16)
    gq, gk = _head_gains(q_head_g[0]), _head_gains(k_head_g[0])
    n1g, n2g = norm1_g[0].reshape(1, D_MODEL), norm2_g[0].reshape(1, D_MODEL)
    qlg, kvg = q_lora_g[0].reshape(1, Q_LORA), kv_lora_g[0].reshape(1, KV_LORA)
    cs_lat = _rope_table(ss)
    cs_id = jnp.concatenate([jnp.ones((TM_IN, QK_ROPE), F32), jnp.zeros((TM_IN, QK_ROPE), F32)], axis=1)

    cond8 = jnp.concatenate([c_ctx[None, :], c, jnp.zeros((COND_ROWS - 1 - bs, D_MODEL), F32)], axis=0)
    mod3 = _modulation(cond8, ada_w[0], ada_b[0]).reshape(COND_ROWS, N_MOD, D_MODEL)

    xp2, xs2 = x_prompt.reshape(tp, D_MODEL), x_sample.reshape(ts, D_MODEL)
    big = 1 << 30

    bg_p, cu_p, q_p, k_p, vt_p, ckv_p, kr_p = _inproj(
        xp2, mod3, n1g, w_in_b, qlg, w_uq_b, kvg, w_uk_b, w_uvt_b, gq, gk, cs_id,
        cond_base=0, tiles_per_cond=big, cs_tiles=1, emit_cache=True)
    o_p = _attention(q_p.reshape(bp, sp, -1), [(k_p.reshape(bp, sp, -1), vt_p)],
                     tq=sp, heads=N_HEADS, name="attn_ctx")
    x1_p, h2_p, rid_p, rw_p = _outproj(
        xp2, bg_p, cu_p, o_p.reshape(tp, -1), mod3, conv_w[0], w_out_b, n2g, wr,
        cond_base=0, tiles_per_cond=big, seq_len=sp)

    kr_c = cache_krope[:, 0].reshape(bs * past, QK_ROPE)
    k_c, vt_c = _kvcache(cache_ckv[:, 0].reshape(bs * past, KV_LORA), jnp.concatenate([kr_c, kr_c], axis=1),
                         cs_id, w_uk_b, w_uvt_b, gk)
    bg_s, cu_s, q_s, k_s, vt_s = _inproj(
        xs2, mod3, n1g, w_in_b, qlg, w_uq_b, kvg, w_uk_b, w_uvt_b, gq, gk, cs_lat,
        cond_base=1, tiles_per_cond=ss // TM_IN, cs_tiles=ss // TM_IN, emit_cache=False)
    o_s = _attention(q_s.reshape(bs, ss, -1),
                     [(k_c.reshape(bs, past, -1), vt_c), (k_s.reshape(bs, ss, -1), vt_s)],
                     tq=TQ, heads=N_HEADS, name="attn_lat")
    x1_s, h2_s, rid_s, rw_s = _outproj(
        xs2, bg_s, cu_s, o_s.reshape(ts, -1), mod3, conv_w[0], w_out_b, n2g, wr,
        cond_base=1, tiles_per_cond=ss // TM_OUT, seq_len=ss)

    n_tok = tp + ts
    slot_tiles = n_tok // TM_ROW
    nt = 2 * n_tok // TM_MOE + N_EXPERTS
    rid = jnp.concatenate([rid_p[:, :2], rid_s[:, :2]], axis=0).T.reshape(2 * n_tok)
    pos, te, n_valid, pad_start, pad_cnt, first, next_expert, wslot = _route_tables(rid, tm=TM_MOE)
    pos3 = pos.reshape(2 * slot_tiles, 1, TM_ROW)
    xs = _dispatch(pad_start, pad_cnt, n_valid, pos3, h2_p, h2_s, nt=nt)
    y = _moe(te, n_valid, first, next_expert, wslot, xs, w_gate[0], w_up[0], w_down[0])

    y_p = _final(x1_p, rw_p, y, pos3, mod3, tile_base=0, slot_tiles=slot_tiles, cond_base=0, tiles_per_cond=big)
    y_s = _final(x1_s, rw_s, y, pos3, mod3, tile_base=tp // TM_ROW, slot_tiles=slot_tiles, cond_base=1,
                 tiles_per_cond=ss // TM_ROW)

    return (y_p.reshape(bp, sp, D_MODEL), y_s.reshape(bs, ss, D_MODEL),
            ckv_p.reshape(bp, 1, sp, KV_LORA), kr_p.reshape(bp, 1, sp, QK_ROPE))
```

```python
import functools

import jax
import jax.numpy as jnp
from jax import lax
from jax.experimental import pallas as pl
from jax.experimental.pallas import tpu as pltpu

F32 = jnp.float32
BF16 = jnp.bfloat16

D_MODEL = 2048
CONV_WIDTH = 1024
N_HEADS = 8
QK_NOPE = 128
QK_ROPE = 64
V_DIM = 128
QK_DIM = QK_NOPE + QK_ROPE
Q_LORA = 512
KV_LORA = 256
GRID_W = 64
ROPE_PAIRS = QK_ROPE // 4
ROPE_BASE = 10000.0
N_EXPERT_GROUPS = 4
EXPERTS_PER_GROUP = 8
N_EXPERTS = N_EXPERT_GROUPS * EXPERTS_PER_GROUP
D_EXPERT = 512
N_MOD = 6
EPS = 1e-6
LOG2_E = 1.4426950408889634

HEAD_W = 2 * QK_NOPE
LANES = 128
SUBLANES = 8
BF16_ROWS = 2 * SUBLANES
IN_COLS = 3 * CONV_WIDTH + Q_LORA + KV_LORA + 2 * QK_ROPE
ROUTER_COLS = LANES
VMEM_LIMIT = 56 * 1024 * 1024

TM_IN = 512
TM_OUT = 512
OUT_SLABS = 2
TQ = 256
ATTN_CHUNK = 512
ATTN_AHEAD = 2
EXP_ROWS = BF16_ROWS
TM_MOE = 256
XS_DEPTH = 6
TM_ROW = 256
BN_MOD = 1024
TK_PREP = 512
TM_CACHE = 256
COND_ROWS = SUBLANES
PAD_BITS = tuple(1 << b for b in reversed(range(TM_MOE.bit_length() - 1)))


def _cparams(sem):
    return pltpu.CompilerParams(dimension_semantics=sem, vmem_limit_bytes=VMEM_LIMIT)


def _const_spec(shape):
    nd = len(shape)
    return pl.BlockSpec(shape, lambda *_: (0,) * nd, pipeline_mode=pl.Buffered(1))


def _rms(x):
    return x * lax.rsqrt(jnp.mean(x * x, axis=-1, keepdims=True) + EPS)


def _rowsum(x):
    return jnp.sum(x, axis=-1, keepdims=True)


def _mod_kernel(c_ref, w_ref, b_ref, o_ref):
    c = c_ref[...]
    s = c / (1.0 + jnp.exp(-c))
    s_hi = s.astype(BF16)
    s_lo = (s - s_hi.astype(F32)).astype(BF16)
    w = w_ref[...]
    w_hi = w.astype(BF16)
    w_lo = (w - w_hi.astype(F32)).astype(BF16)
    rows = s.shape[0]
    a = jnp.dot(jnp.concatenate([s_hi, s_lo], axis=0), w_hi, preferred_element_type=F32)
    b = jnp.dot(s_hi, w_lo, preferred_element_type=F32)
    o_ref[...] = a[:rows] + (a[rows:] + b) + b_ref[...]


def _modulation(cond8, ada_w, ada_b):
    n = ada_w.shape[1]
    return pl.pallas_call(
        _mod_kernel,
        out_shape=jax.ShapeDtypeStruct((COND_ROWS, n), F32),
        grid=(n // BN_MOD,),
        in_specs=[pl.BlockSpec((COND_ROWS, D_MODEL), lambda j: (0, 0)),
                  pl.BlockSpec((D_MODEL, BN_MOD), lambda j: (0, j)),
                  pl.BlockSpec((1, BN_MOD), lambda j: (0, j))],
        out_specs=pl.BlockSpec((COND_ROWS, BN_MOD), lambda j: (0, j)),
        compiler_params=_cparams(("arbitrary",)),
        name="mod",
    )(cond8, ada_w, ada_b.reshape(1, n))


def _winprep_kernel(w_ref, o_ref):
    n = w_ref.shape[0]
    o_ref[:n, :] = w_ref[...].astype(BF16)
    p = ROPE_PAIRS
    for dst, src in ((0, p), (p, 0), (2 * p, 3 * p), (3 * p, 2 * p)):
        o_ref[n + dst:n + dst + p, :] = w_ref[n - QK_ROPE + src:n - QK_ROPE + src + p, :].astype(BF16)


def _winprep(w_in_t):
    n, k = w_in_t.shape
    tk = TK_PREP
    return pl.pallas_call(
        _winprep_kernel,
        out_shape=jax.ShapeDtypeStruct((IN_COLS, k), BF16),
        grid=(k // tk,),
        in_specs=[pl.BlockSpec((n, tk), lambda i: (0, i))],
        out_specs=pl.BlockSpec((IN_COLS, tk), lambda i: (0, i)),
        compiler_params=_cparams(("arbitrary",)),
        name="winprep",
    )(w_in_t)


def _emit_kv(ckv, kraw, cs, w_uk_ref, w_uvt_ref, gk_ref, k_ref, vt_ref):
    cb = ckv.astype(BF16)
    kn_all = jnp.dot(cb, w_uk_ref[...], preferred_element_type=F32)
    vt = lax.dot_general(w_uvt_ref[...], cb, (((1,), (1,)), ((), ())), preferred_element_type=F32)
    vt_ref[...] = vt.astype(BF16)
    ss_rope = 0.5 * _rowsum(kraw * kraw)
    t = kraw * (cs * gk_ref[1:2, :])
    tt = t + pltpu.roll(t, QK_ROPE, axis=1)
    g_nope = gk_ref[0:1, :]
    for h in range(N_HEADS):
        kn = kn_all[:, h * QK_NOPE:(h + 1) * QK_NOPE]
        r = lax.rsqrt((_rowsum(kn * kn) + ss_rope) * (1.0 / QK_DIM) + EPS)
        k_ref[:, h * HEAD_W:h * HEAD_W + QK_NOPE] = (kn * r * g_nope).astype(BF16)
        k_ref[:, h * HEAD_W + QK_NOPE:(h + 1) * HEAD_W] = (tt * r).astype(BF16)


def _kvcache_kernel(ckv_ref, kraw_ref, cs_ref, w_uk_ref, w_uvt_ref, gk_ref, k_ref, vt_ref):
    _emit_kv(ckv_ref[...], kraw_ref[...], cs_ref[...], w_uk_ref, w_uvt_ref, gk_ref, k_ref, vt_ref)


def _kvcache(ckv, kraw, cs_id, w_uk_b, w_uvt_b, gk):
    n = ckv.shape[0]
    tm = TM_CACHE
    return pl.pallas_call(
        _kvcache_kernel,
        out_shape=(jax.ShapeDtypeStruct((n, N_HEADS * HEAD_W), BF16),
                   jax.ShapeDtypeStruct((N_HEADS * V_DIM, n), BF16)),
        grid=(n // tm,),
        in_specs=[pl.BlockSpec((tm, KV_LORA), lambda i: (i, 0)),
                  pl.BlockSpec((tm, LANES), lambda i: (i, 0)),
                  pl.BlockSpec((tm, LANES), lambda i: (0, 0)),
                  _const_spec((KV_LORA, N_HEADS * QK_NOPE)),
                  _const_spec((N_HEADS * V_DIM, KV_LORA)),
                  _const_spec((2, LANES))],
        out_specs=(pl.BlockSpec((tm, N_HEADS * HEAD_W), lambda i: (i, 0)),
                   pl.BlockSpec((N_HEADS * V_DIM, tm), lambda i: (0, i))),
        compiler_params=_cparams(("arbitrary",)),
        name="kvcache",
    )(ckv, kraw, cs_id, w_uk_b, w_uvt_b, gk)


def _inproj_kernel(x_ref, mod_ref, n1g_ref, w_in_ref, qlg_ref, w_uq_ref, kvg_ref, w_uk_ref, w_uvt_ref,
                   gq_ref, gk_ref, cs_ref, bg_ref, cu_ref, q_ref, k_ref, vt_ref, *cache_refs):
    x = x_ref[...]
    mod = mod_ref[0]
    h = _rms(x) * n1g_ref[...] * (1.0 + mod[1:2, :]) + mod[0:1, :]
    hb = h.astype(BF16)

    def proj(a, b):
        return lax.dot_general(hb, w_in_ref[a:b, :], (((1,), (1,)), ((), ())), preferred_element_type=F32)

    c1, c2, c3 = CONV_WIDTH, 2 * CONV_WIDTH, 3 * CONV_WIDTH
    q_lat = proj(c3, c3 + Q_LORA)
    kvk = proj(c3 + Q_LORA, IN_COLS)
    bg_ref[...] = proj(0, c1).astype(BF16)
    cs = cs_ref[...]

    qn = _rms(q_lat) * qlg_ref[...]
    q = jnp.dot(qn.astype(BF16), w_uq_ref[...], preferred_element_type=F32)
    scale = QK_DIM ** -0.5 * LOG2_E
    g_nope = gq_ref[0:1, :] * scale
    tq = cs * (gq_ref[1:2, :] * scale)
    for hd in range(N_HEADS):
        lo = q[:, hd * HEAD_W:hd * HEAD_W + QK_NOPE]
        up = q[:, hd * HEAD_W + QK_NOPE:(hd + 1) * HEAD_W]
        ss = _rowsum(lo * lo) + 0.5 * _rowsum(up * up)
        r = lax.rsqrt(ss * (1.0 / QK_DIM) + EPS)
        q_ref[:, hd * HEAD_W:hd * HEAD_W + QK_NOPE] = (lo * r * g_nope).astype(BF16)
        q_ref[:, hd * HEAD_W + QK_NOPE:(hd + 1) * HEAD_W] = (up * r * tq).astype(BF16)

    cu_ref[...] = (proj(c1, c2) * proj(c2, c3)).astype(BF16)

    kv_lat = kvk[:, :KV_LORA]
    kraw = kvk[:, KV_LORA:]
    ckv = _rms(kv_lat) * kvg_ref[...]
    if cache_refs:
        ckv_out_ref, kr_out_ref = cache_refs
        ckv_out_ref[...] = ckv
        kr_out_ref[...] = kraw[:, :QK_ROPE]
    _emit_kv(ckv, kraw, cs, w_uk_ref, w_uvt_ref, gk_ref, k_ref, vt_ref)


def _inproj(x2d, mod3, n1g, w_in_b, qlg, w_uq_b, kvg, w_uk_b, w_uvt_b, gq, gk, cs, *,
            cond_base, tiles_per_cond, cs_tiles, emit_cache):
    t = x2d.shape[0]
    tm = TM_IN
    out_shape = [jax.ShapeDtypeStruct((t, CONV_WIDTH), BF16),
                 jax.ShapeDtypeStruct((t, CONV_WIDTH), BF16),
                 jax.ShapeDtypeStruct((t, N_HEADS * HEAD_W), BF16),
                 jax.ShapeDtypeStruct((t, N_HEADS * HEAD_W), BF16),
                 jax.ShapeDtypeStruct((N_HEADS * V_DIM, t), BF16)]
    out_specs = [pl.BlockSpec((tm, CONV_WIDTH), lambda i: (i, 0)),
                 pl.BlockSpec((tm, CONV_WIDTH), lambda i: (i, 0)),
                 pl.BlockSpec((tm, N_HEADS * HEAD_W), lambda i: (i, 0)),
                 pl.BlockSpec((tm, N_HEADS * HEAD_W), lambda i: (i, 0)),
                 pl.BlockSpec((N_HEADS * V_DIM, tm), lambda i: (0, i))]
    if emit_cache:
        out_shape += [jax.ShapeDtypeStruct((t, KV_LORA), F32), jax.ShapeDtypeStruct((t, QK_ROPE), F32)]
        out_specs += [pl.BlockSpec((tm, KV_LORA), lambda i: (i, 0)),
                      pl.BlockSpec((tm, QK_ROPE), lambda i: (i, 0))]
    return pl.pallas_call(
        _inproj_kernel,
        out_shape=tuple(out_shape),
        grid=(t // tm,),
        in_specs=[pl.BlockSpec((tm, D_MODEL), lambda i: (i, 0)),
                  pl.BlockSpec((1, N_MOD, D_MODEL), lambda i: (cond_base + i // tiles_per_cond, 0, 0)),
                  _const_spec((1, D_MODEL)),
                  _const_spec((IN_COLS, D_MODEL)),
                  _const_spec((1, Q_LORA)),
                  _const_spec((Q_LORA, N_HEADS * HEAD_W)),
                  _const_spec((1, KV_LORA)),
                  _const_spec((KV_LORA, N_HEADS * QK_NOPE)),
                  _const_spec((N_HEADS * V_DIM, KV_LORA)),
                  _const_spec((2, LANES)),
                  _const_spec((2, LANES)),
                  pl.BlockSpec((tm, LANES), lambda i: (i % cs_tiles, 0))],
        out_specs=tuple(out_specs),
        compiler_params=_cparams(("arbitrary",)),
        name="inproj_ctx" if emit_cache else "inproj_lat",
    )(x2d, mod3, n1g, w_in_b, qlg, w_uq_b, kvg, w_uk_b, w_uvt_b, gq, gk, cs)


def _attn_kernel(*refs, n_kv, heads):
    q_ref = refs[0]
    o_ref = refs[-1]
    chunks = []
    for j in range(n_kv):
        sk = refs[1 + 2 * j].shape[1]
        step = min(sk, ATTN_CHUNK)
        chunks += [(j, lo, lo + step) for lo in range(0, sk, step)]
    q = [q_ref[0, :, h * HEAD_W:(h + 1) * HEAD_W] for h in range(heads)]

    def score(h, c):
        j, lo, hi = chunks[c]
        k = refs[1 + 2 * j][0, lo:hi, h * HEAD_W:(h + 1) * HEAD_W]
        return lax.dot_general(k, q[h], (((1,), (1,)), ((), ())), preferred_element_type=F32)

    m = [None] * heads
    acc = [None] * heads
    ahead = [[score(h, c) for c in range(min(ATTN_AHEAD, len(chunks)))] for h in range(heads)]
    for c, (j, lo, hi) in enumerate(chunks):
        for h in range(heads):
            s = ahead[h].pop(0)
            if c + ATTN_AHEAD < len(chunks):
                ahead[h].append(score(h, c + ATTN_AHEAD))
            mc = jnp.max(s, axis=0, keepdims=True)
            m_new = mc if m[h] is None else jnp.maximum(m[h], mc)
            p = jnp.concatenate([jnp.exp2(s[r:r + EXP_ROWS] - m_new).astype(BF16)
                                 for r in range(0, hi - lo, EXP_ROWS)], axis=0)
            vt = refs[2 + 2 * j][h * V_DIM:(h + 1) * V_DIM, lo:hi]
            vt1 = jnp.concatenate([vt, jnp.ones((BF16_ROWS, hi - lo), BF16)], axis=0)
            part = jnp.dot(vt1, p, preferred_element_type=F32)
            acc[h] = part if acc[h] is None else acc[h] * jnp.exp2(m[h] - m_new) + part
            m[h] = m_new
    for h in range(heads):
        ot = acc[h][:V_DIM, :] / acc[h][V_DIM:V_DIM + 1, :]
        o_ref[0, :, h * V_DIM:(h + 1) * V_DIM] = ot.T.astype(BF16)


def _attention(q, kvs, *, tq, heads, name):
    b, s, _ = q.shape
    in_specs = [pl.BlockSpec((1, tq, heads * HEAD_W), lambda bi, hi, qi: (bi, qi, hi))]
    args = [q]
    kv_bytes = sum(k.shape[1] * heads * (HEAD_W + V_DIM) * 2 for k, _ in kvs)
    mode = pl.Buffered(1 if 2 * kv_bytes > VMEM_LIMIT // 3 else 2)
    for k, v in kvs:
        sk = k.shape[1]
        in_specs.append(pl.BlockSpec((1, sk, heads * HEAD_W), lambda bi, hi, qi: (bi, 0, hi), pipeline_mode=mode))
        in_specs.append(pl.BlockSpec((heads * V_DIM, sk), lambda bi, hi, qi: (hi, bi), pipeline_mode=mode))
        args += [k, v]
    return pl.pallas_call(
        functools.partial(_attn_kernel, n_kv=len(kvs), heads=heads),
        out_shape=jax.ShapeDtypeStruct((b, s, N_HEADS * V_DIM), BF16),
        grid=(b, N_HEADS // heads, s // tq),
        in_specs=in_specs,
        out_specs=pl.BlockSpec((1, tq, heads * V_DIM), lambda bi, hi, qi: (bi, qi, hi)),
        compiler_params=_cparams(("arbitrary", "arbitrary", "arbitrary")),
        name=name,
    )(*args)


def _outproj_kernel(x_ref, bg_ref, cu_ref, cup_ref, cun_ref, o_ref, mod_ref, cw_ref, w_out_ref,
                    n2g_ref, wr_ref, x1_ref, h2_ref, rid_ref, rw_ref, *, tm, seq_len):
    i = pl.program_id(0)
    mod = mod_ref[0]
    cu = cu_ref[...].astype(F32)
    prev_row = cup_ref[...].astype(F32)[BF16_ROWS - 1:BF16_ROWS, :]
    next_row = cun_ref[...].astype(F32)[0:1, :]
    row = lax.broadcasted_iota(jnp.int32, (tm, 1), 0)
    pos = (i * tm + row) & (seq_len - 1)
    up = jnp.where(row == 0, prev_row, pltpu.roll(cu, 1, axis=0))
    up = jnp.where(pos == 0, 0.0, up)
    dn = jnp.where(row == tm - 1, next_row, pltpu.roll(cu, tm - 1, axis=0))
    dn = jnp.where(pos == seq_len - 1, 0.0, dn)
    cw = cw_ref[...]
    y_conv = bg_ref[...].astype(F32) * (up * cw[0:1, :] + cu * cw[1:2, :] + dn * cw[2:3, :])
    y_conv = y_conv.astype(BF16)

    slabs = [(r, r + tm // OUT_SLABS) for r in range(0, tm, tm // OUT_SLABS)]
    mixes = []
    for lo, hi in slabs:
        mix = jnp.dot(y_conv[lo:hi], w_out_ref[:CONV_WIDTH, :], preferred_element_type=F32)
        mixes.append(mix + jnp.dot(o_ref[lo:hi, :], w_out_ref[CONV_WIDTH:, :], preferred_element_type=F32))
    for (lo, hi), mix in zip(slabs, mixes):
        x1 = x_ref[lo:hi, :] + mod[2:3, :] * mix
        x1_ref[lo:hi, :] = x1
        h2 = _rms(x1) * n2g_ref[...] * (1.0 + mod[4:5, :]) + mod[3:4, :]
        h2_hi = h2.astype(BF16)
        h2_ref[lo:hi, :] = h2_hi

        h2_lo = (h2 - h2_hi.astype(F32)).astype(BF16)
        hh_hl = jnp.dot(h2_hi, wr_ref[...], preferred_element_type=F32)
        lh = jnp.dot(h2_lo, wr_ref[:, :ROUTER_COLS], preferred_element_type=F32)
        logits = hh_hl[:, :ROUTER_COLS] + (hh_hl[:, ROUTER_COLS:] + lh)
        lane = lax.broadcasted_iota(jnp.int32, logits.shape, 1)
        neg = -jnp.inf
        big = jnp.int32(1 << 20)
        gl = jnp.where(lane < N_EXPERT_GROUPS, logits, neg)
        gmax = jnp.max(gl, axis=-1, keepdims=True)
        p_top = 1.0 / _rowsum(jnp.exp(gl - gmax))
        g_top = jnp.min(jnp.where(gl == gmax, lane, big), axis=-1, keepdims=True)
        e_lo = N_EXPERT_GROUPS + EXPERTS_PER_GROUP * g_top
        el = jnp.where((lane >= e_lo) & (lane < e_lo + EXPERTS_PER_GROUP), logits, neg)
        v1 = jnp.max(el, axis=-1, keepdims=True)
        i1 = jnp.min(jnp.where(el == v1, lane, big), axis=-1, keepdims=True)
        el2 = jnp.where(lane == i1, neg, el)
        v2 = jnp.max(el2, axis=-1, keepdims=True)
        i2 = jnp.min(jnp.where(el2 == v2, lane, big), axis=-1, keepdims=True)
        e21 = jnp.exp(v2 - v1)
        w1 = p_top / (1.0 + e21)
        w2 = w1 * e21
        rid_ref[lo:hi, :] = jnp.where(lane == 0, i1 - N_EXPERT_GROUPS, i2 - N_EXPERT_GROUPS)
        rw_ref[lo:hi, :] = jnp.where(lane == 0, w1, w2)


def _outproj(x2d, bg, cu, o2d, mod3, conv_w, w_out_b, n2g, wr, *, cond_base, tiles_per_cond, seq_len):
    t = x2d.shape[0]
    tm = TM_OUT
    hb = tm // BF16_ROWS
    nhb = t // BF16_ROWS
    return pl.pallas_call(
        functools.partial(_outproj_kernel, tm=tm, seq_len=seq_len),
        out_shape=(jax.ShapeDtypeStruct((t, D_MODEL), F32),
                   jax.ShapeDtypeStruct((t, D_MODEL), BF16),
                   jax.ShapeDtypeStruct((t, LANES), jnp.int32),
                   jax.ShapeDtypeStruct((t, LANES), F32)),
        grid=(t // tm,),
        in_specs=[pl.BlockSpec((tm, D_MODEL), lambda i: (i, 0)),
                  pl.BlockSpec((tm, CONV_WIDTH), lambda i: (i, 0)),
                  pl.BlockSpec((tm, CONV_WIDTH), lambda i: (i, 0)),
                  pl.BlockSpec((BF16_ROWS, CONV_WIDTH), lambda i: (jnp.maximum(i * hb - 1, 0), 0)),
                  pl.BlockSpec((BF16_ROWS, CONV_WIDTH), lambda i: (jnp.minimum((i + 1) * hb, nhb - 1), 0)),
                  pl.BlockSpec((tm, N_HEADS * V_DIM), lambda i: (i, 0)),
                  pl.BlockSpec((1, N_MOD, D_MODEL), lambda i: (cond_base + i // tiles_per_cond, 0, 0)),
                  _const_spec((3, CONV_WIDTH)),
                  _const_spec((D_MODEL, D_MODEL)),
                  _const_spec((1, D_MODEL)),
                  _const_spec((D_MODEL, 2 * ROUTER_COLS))],
        out_specs=(pl.BlockSpec((tm, D_MODEL), lambda i: (i, 0)),
                   pl.BlockSpec((tm, D_MODEL), lambda i: (i, 0)),
                   pl.BlockSpec((tm, LANES), lambda i: (i, 0)),
                   pl.BlockSpec((tm, LANES), lambda i: (i, 0))),
        compiler_params=_cparams(("arbitrary",)),
        name="outproj_ctx" if cond_base == 0 else "outproj_lat",
    )(x2d, bg, cu, cu, cu, o2d, mod3, conv_w, w_out_b, n2g, wr)


def _route_tables(rid, *, tm):
    n_pairs = rid.shape[0]
    nt = n_pairs // tm + N_EXPERTS
    experts = jnp.arange(N_EXPERTS, dtype=jnp.int32)
    onehot = (rid[None, :] == experts[:, None]).astype(jnp.int32)
    csum = jnp.cumsum(onehot, axis=1)
    counts = csum[:, -1]
    tiles_e = (counts + tm - 1) // tm
    tile_end = jnp.cumsum(tiles_e)
    tile_start = tile_end - tiles_e
    n_valid = tile_end[-1]
    pos = jnp.sum(onehot * (csum - 1 + tile_start[:, None] * tm), axis=0)
    tile_raw = jnp.arange(nt, dtype=jnp.int32)
    tile_idx = jnp.minimum(tile_raw, n_valid - 1)
    te = jnp.sum((tile_end[None, :] <= tile_idx[:, None]).astype(jnp.int32), axis=1)
    used = tiles_e > 0
    first = ((tile_raw == tile_start[te]) & (tile_raw < n_valid)).astype(jnp.int32)
    nxt = lax.cummin(jnp.where(used, experts, N_EXPERTS), axis=0, reverse=True)
    nxt = jnp.concatenate([nxt[1:], jnp.full((1,), N_EXPERTS, jnp.int32)])
    next_expert = jnp.where(nxt[te] < N_EXPERTS, nxt[te], -1).astype(jnp.int32)
    wslot = ((jnp.cumsum(used.astype(jnp.int32)) - 1)[te] & 1).astype(jnp.int32)
    return (pos, te, n_valid.reshape(1).astype(jnp.int32),
            (tile_start * tm + counts).astype(jnp.int32), (tiles_e * tm - counts).astype(jnp.int32),
            first, next_expert, wslot)


def _dispatch_kernel(pstart_ref, pcnt_ref, nval_ref, pos0_ref, pos1_ref, h2a_ref, h2b_ref, xs_hbm,
                     buf, zbuf, sem, sem_z, *, tm, nt, n_a):
    i = pl.program_id(0)
    n = pl.num_programs(0)
    slot = i & 1

    def row_copy(r, pos_ref, s):
        return pltpu.make_async_copy(buf.at[s, pl.ds(r, 1)], xs_hbm.at[pl.ds(pos_ref[0, 0, r], 1)], sem.at[s])

    def wait_tile(s):
        for _ in range(2):
            pltpu.make_async_copy(buf.at[s], xs_hbm.at[pl.ds(0, tm)], sem.at[s]).wait()

    def pad_copies(e, fn):
        cnt = pcnt_ref[e]
        start = pstart_ref[e]
        off = start + cnt
        for b in PAD_BITS:
            if b < SUBLANES:
                break
            off = off - (cnt & b)
            dst = pl.multiple_of(off, SUBLANES)

            @pl.when((cnt & b) != 0)
            def _():
                fn(pltpu.make_async_copy(zbuf.at[pl.ds(0, b)], xs_hbm.at[pl.ds(dst, b)], sem_z))
        for j in range(SUBLANES - 1):
            @pl.when(j < (cnt & (SUBLANES - 1)))
            def _():
                fn(pltpu.make_async_copy(zbuf.at[pl.ds(0, 1)], xs_hbm.at[pl.ds(start + j, 1)], sem_z))

    def tail_copy(j):
        return pltpu.make_async_copy(zbuf, xs_hbm.at[pl.ds(pl.multiple_of(j * tm, tm), tm)], sem_z)

    @pl.when(i == 0)
    def _():
        zbuf[...] = jnp.zeros((tm, D_MODEL), F32)
        lax.fori_loop(0, N_EXPERTS, lambda e, c: (pad_copies(e, lambda d: d.start()), c)[1], 0)
        lax.fori_loop(nval_ref[0], nt, lambda j, c: (tail_copy(j).start(), c)[1], 0)
        lax.fori_loop(0, N_EXPERTS, lambda e, c: (pad_copies(e, lambda d: d.wait()), c)[1], 0)
        lax.fori_loop(nval_ref[0], nt, lambda j, c: (tail_copy(j).wait(), c)[1], 0)

    @pl.when(i >= 2)
    def _():
        wait_tile(slot)

    @pl.when(i < n_a)
    def _():
        buf[slot] = h2a_ref[...].astype(F32)

    @pl.when(i >= n_a)
    def _():
        buf[slot] = h2b_ref[...].astype(F32)

    for r in range(tm):
        row_copy(r, pos0_ref, slot).start()
        row_copy(r, pos1_ref, slot).start(priority=1)

    @pl.when(i == n - 1)
    def _():
        wait_tile(slot)
        wait_tile(1 - slot)


def _dispatch(pad_start, pad_cnt, n_valid, pos3, h2_a, h2_b, *, nt):
    tm = TM_ROW
    n_a, n_b = h2_a.shape[0] // tm, h2_b.shape[0] // tm
    smem_tile = functools.partial(pl.BlockSpec, (1, 1, tm), memory_space=pltpu.SMEM)
    grid_spec = pltpu.PrefetchScalarGridSpec(
        num_scalar_prefetch=3,
        grid=(n_a + n_b,),
        in_specs=[smem_tile(lambda i, *_: (i, 0, 0)),
                  smem_tile(lambda i, *_: (n_a + n_b + i, 0, 0)),
                  pl.BlockSpec((tm, D_MODEL), lambda i, *_: (jnp.minimum(i, n_a - 1), 0)),
                  pl.BlockSpec((tm, D_MODEL), lambda i, *_: (jnp.maximum(i - n_a, 0), 0))],
        out_specs=pl.BlockSpec(memory_space=pl.ANY),
        scratch_shapes=[pltpu.VMEM((2, tm, D_MODEL), F32),
                        pltpu.VMEM((tm, D_MODEL), F32),
                        pltpu.SemaphoreType.DMA((2,)),
                        pltpu.SemaphoreType.DMA])
    return pl.pallas_call(
        functools.partial(_dispatch_kernel, tm=tm, nt=nt, n_a=n_a),
        out_shape=jax.ShapeDtypeStruct((nt * TM_MOE, D_MODEL), F32),
        grid_spec=grid_spec,
        compiler_params=pltpu.CompilerParams(dimension_semantics=("arbitrary",),
                                             vmem_limit_bytes=VMEM_LIMIT, has_side_effects=True),
        name="dispatch",
    )(pad_start, pad_cnt, n_valid, pos3, pos3, h2_a, h2_b)


def _moe_kernel(texp_ref, nval_ref, first_ref, next_ref, wslot_ref, xs_hbm, wg_hbm, wu_hbm, wd_hbm, y_ref,
                xbuf, wg32, wu32, wd32, wgb, wub, wdb, sem_x, sem_w, *, tm):
    i = pl.program_id(0)
    n_valid = nval_ref[0]

    def tile_copy(j):
        s = lax.rem(j, XS_DEPTH)
        return pltpu.make_async_copy(xs_hbm.at[pl.ds(pl.multiple_of(j * tm, tm), tm)], xbuf.at[s], sem_x.at[s])

    @pl.when(i == 0)
    def _():
        for j in range(XS_DEPTH - 1):
            @pl.when(j < n_valid)
            def _():
                tile_copy(j).start()

    @pl.when(i + XS_DEPTH - 1 < n_valid)
    def _():
        tile_copy(i + XS_DEPTH - 1).start()

    def weight_copies(e, s):
        return (pltpu.make_async_copy(wg_hbm.at[e], wg32.at[s], sem_w.at[s]),
                pltpu.make_async_copy(wu_hbm.at[e], wu32.at[s], sem_w.at[s]),
                pltpu.make_async_copy(wd_hbm.at[e], wd32.at[s], sem_w.at[s]))

    @pl.when(i == 0)
    def _():
        for d in weight_copies(texp_ref[0], 0):
            d.start()

    @pl.when(i < nval_ref[0])
    def _():
        @pl.when(first_ref[i] == 1)
        def _():
            s = wslot_ref[i]
            for d in weight_copies(texp_ref[i], s):
                d.wait()
            e_next = next_ref[i]

            @pl.when(e_next >= 0)
            def _():
                for d in weight_copies(e_next, 1 - s):
                    d.start()
            wgb[...] = wg32[s].astype(BF16)
            wub[...] = wu32[s].astype(BF16)
            wdb[...] = wd32[s].astype(BF16)

        tile_copy(i).wait()
        x = xbuf[lax.rem(i, XS_DEPTH)].astype(BF16)
        g = jnp.dot(x, wgb[...], preferred_element_type=F32)
        u = jnp.dot(x, wub[...], preferred_element_type=F32)
        a = g / (1.0 + jnp.exp(-g)) * u
        y_ref[...] = jnp.dot(a.astype(BF16), wdb[...], preferred_element_type=F32)

    @pl.when(i >= nval_ref[0])
    def _():
        y_ref[...] = jnp.zeros(y_ref.shape, F32)


def _moe(tile_expert, n_valid, first, next_expert, wslot, xs, w_gate, w_up, w_down):
    nt = tile_expert.shape[0]
    tm = TM_MOE
    grid_spec = pltpu.PrefetchScalarGridSpec(
        num_scalar_prefetch=5,
        grid=(nt,),
        in_specs=[pl.BlockSpec(memory_space=pl.ANY),
                  pl.BlockSpec(memory_space=pl.ANY),
                  pl.BlockSpec(memory_space=pl.ANY),
                  pl.BlockSpec(memory_space=pl.ANY)],
        out_specs=pl.BlockSpec((tm, D_MODEL), lambda i, *_: (i, 0)),
        scratch_shapes=[pltpu.VMEM((XS_DEPTH, tm, D_MODEL), F32),
                        pltpu.VMEM((2, D_MODEL, D_EXPERT), F32),
                        pltpu.VMEM((2, D_MODEL, D_EXPERT), F32),
                        pltpu.VMEM((2, D_EXPERT, D_MODEL), F32),
                        pltpu.VMEM((D_MODEL, D_EXPERT), BF16),
                        pltpu.VMEM((D_MODEL, D_EXPERT), BF16),
                        pltpu.VMEM((D_EXPERT, D_MODEL), BF16),
                        pltpu.SemaphoreType.DMA((XS_DEPTH,)),
                        pltpu.SemaphoreType.DMA((2,))])
    return pl.pallas_call(
        functools.partial(_moe_kernel, tm=tm),
        out_shape=jax.ShapeDtypeStruct((nt * tm, D_MODEL), F32),
        grid_spec=grid_spec,
        compiler_params=_cparams(("arbitrary",)),
        name="moe",
    )(tile_expert, n_valid, first, next_expert, wslot, xs, w_gate, w_up, w_down)


def _final_kernel(pa0_ref, pb0_ref, pa_ref, pb_ref, x1_ref, rw_ref, mod_ref, y_hbm, o_ref, ybuf, sem, *, tm):
    i = pl.program_id(0)
    n = pl.num_programs(0)
    slot = i & 1

    def start_tile(pa, pb, s):
        for r in range(tm):
            pltpu.make_async_copy(y_hbm.at[pl.ds(pa[0, 0, r], 1)], ybuf.at[s, 0, pl.ds(r, 1)], sem.at[s]).start()
            pltpu.make_async_copy(y_hbm.at[pl.ds(pb[0, 0, r], 1)], ybuf.at[s, 1, pl.ds(r, 1)],
                                  sem.at[s]).start(priority=1)

    @pl.when(i == 0)
    def _():
        start_tile(pa0_ref, pb0_ref, 0)

    @pl.when(i + 1 < n)
    def _():
        start_tile(pa_ref, pb_ref, 1 - slot)

    for k in range(2):
        pltpu.make_async_copy(y_hbm.at[pl.ds(0, tm)], ybuf.at[slot, k], sem.at[slot]).wait()
    w = rw_ref[...]
    moe = w[:, 0:1] * ybuf[slot, 0] + w[:, 1:2] * ybuf[slot, 1]
    o_ref[...] = x1_ref[...] + mod_ref[0][5:6, :] * moe


def _final(x1, rw, y, pos3, mod3, *, tile_base, slot_tiles, cond_base, tiles_per_cond):
    t = x1.shape[0]
    tm = TM_ROW
    n = t // tm
    smem_tile = functools.partial(pl.BlockSpec, (1, 1, tm), memory_space=pltpu.SMEM)
    return pl.pallas_call(
        functools.partial(_final_kernel, tm=tm),
        out_shape=jax.ShapeDtypeStruct((t, D_MODEL), F32),
        grid=(n,),
        in_specs=[smem_tile(lambda i: (tile_base, 0, 0)),
                  smem_tile(lambda i: (slot_tiles + tile_base, 0, 0)),
                  smem_tile(lambda i: (tile_base + jnp.minimum(i + 1, n - 1), 0, 0)),
                  smem_tile(lambda i: (slot_tiles + tile_base + jnp.minimum(i + 1, n - 1), 0, 0)),
                  pl.BlockSpec((tm, D_MODEL), lambda i: (i, 0)),
                  pl.BlockSpec((tm, LANES), lambda i: (i, 0)),
                  pl.BlockSpec((1, N_MOD, D_MODEL), lambda i: (cond_base + i // tiles_per_cond, 0, 0)),
                  pl.BlockSpec(memory_space=pl.ANY)],
        out_specs=pl.BlockSpec((tm, D_MODEL), lambda i: (i, 0)),
        scratch_shapes=[pltpu.VMEM((2, 2, tm, D_MODEL), F32), pltpu.SemaphoreType.DMA((2,))],
        compiler_params=_cparams(("arbitrary",)),
        name="final_ctx" if cond_base == 0 else "final_lat",
    )(pos3, pos3, pos3, pos3, x1, rw, mod3, y)


def _rope_table(n_tokens):
    rows = n_tokens // GRID_W
    inv = ROPE_BASE ** (-jnp.arange(ROPE_PAIRS, dtype=F32) / ROPE_PAIRS)
    row_ang = jnp.arange(rows, dtype=F32)[:, None] * inv
    col_ang = jnp.arange(GRID_W, dtype=F32)[:, None] * inv
    cr, sr, cc, sc = jnp.cos(row_ang), jnp.sin(row_ang), jnp.cos(col_ang), jnp.sin(col_ang)
    zr, zc = jnp.zeros_like(cr), jnp.zeros_like(cc)
    row_t = jnp.concatenate([cr, cr, zr, zr, -sr, sr, zr, zr], axis=-1)
    col_t = jnp.concatenate([zc, zc, cc, cc, zc, zc, -sc, sc], axis=-1)
    return (row_t[:, None, :] + col_t[None, :, :]).reshape(n_tokens, 2 * QK_ROPE)


def _swap_halves(a):
    p = ROPE_PAIRS
    return jnp.concatenate([a[..., p:2 * p], a[..., :p], a[..., 3 * p:], a[..., 2 * p:3 * p]], axis=-1)


def _head_gains(g):
    rope = g[QK_NOPE:]
    return jnp.stack([g[:QK_NOPE], jnp.concatenate([rope, _swap_halves(rope)])])


def kernel(x_prompt, x_sample, cache_ckv, cache_krope, c, c_ctx, ada_w, ada_b, norm1_g, w_in, conv_w,
           q_lora_g, w_uq, kv_lora_g, w_ukv, q_head_g, k_head_g, w_out, norm2_g, router_g, router_e,
           w_gate, w_up, w_down):
    depth = ada_w.shape[0]
    assert depth == 1
    bp, sp, _ = x_prompt.shape
    bs, ss, _ = x_sample.shape
    past = cache_ckv.shape[2]
    tp, ts = bp * sp, bs * ss
    assert sp & (sp - 1) == 0 and ss & (ss - 1) == 0
    assert tp % TM_IN == 0 and ss % TM_IN == 0 and TM_IN % sp == 0 and ss % TQ == 0
    assert bs + 1 <= COND_ROWS and past % TM_CACHE == 0 and TM_ROW == TM_MOE and tp % TM_ROW == 0 and ts % TM_ROW == 0

    w_in_b = _winprep(w_in[0].T)
    uq = w_uq[0].reshape(Q_LORA, N_HEADS, QK_DIM)
    w_uq_b = jnp.concatenate([uq, _swap_halves(uq[..., QK_NOPE:])], axis=-1)
    w_uq_b = w_uq_b.reshape(Q_LORA, N_HEADS * HEAD_W).astype(BF16)
    ukv = w_ukv[0].reshape(KV_LORA, N_HEADS, QK_NOPE + V_DIM)
    w_uk_b = ukv[..., :QK_NOPE].reshape(KV_LORA, N_HEADS * QK_NOPE).astype(BF16)
    w_uvt_b = ukv[..., QK_NOPE:].reshape(KV_LORA, N_HEADS * V_DIM).T.astype(BF16)
    w_out_b = w_out[0].astype(BF16)
    wr = jnp.concatenate([router_g[0], router_e[0],
                          jnp.zeros((D_MODEL, ROUTER_COLS - N_EXPERT_GROUPS - N_EXPERTS), F32)], axis=1)
    wr_hi = lax.bitcast_convert_type(lax.bitcast_convert_type(wr, jnp.uint32) & jnp.uint32(0xFFFF0000), F32)
    wr = jnp.concatenate([wr_hi, wr - wr_hi], axis=1).astype(BF16)
    gq, gk = _head_gains(q_head_g[0]), _head_gains(k_head_g[0])
    n1g, n2g = norm1_g[0].reshape(1, D_MODEL), norm2_g[0].reshape(1, D_MODEL)
    qlg, kvg = q_lora_g[0].reshape(1, Q_LORA), kv_lora_g[0].reshape(1, KV_LORA)
    cs_lat = _rope_table(ss)
    cs_id = jnp.concatenate([jnp.ones((TM_IN, QK_ROPE), F32), jnp.zeros((TM_IN, QK_ROPE), F32)], axis=1)

    cond8 = jnp.concatenate([c_ctx[None, :], c, jnp.zeros((COND_ROWS - 1 - bs, D_MODEL), F32)], axis=0)
    mod3 = _modulation(cond8, ada_w[0], ada_b[0]).reshape(COND_ROWS, N_MOD, D_MODEL)

    xp2, xs2 = x_prompt.reshape(tp, D_MODEL), x_sample.reshape(ts, D_MODEL)
    big = 1 << 30

    bg_p, cu_p, q_p, k_p, vt_p, ckv_p, kr_p = _inproj(
        xp2, mod3, n1g, w_in_b, qlg, w_uq_b, kvg, w_uk_b, w_uvt_b, gq, gk, cs_id,
        cond_base=0, tiles_per_cond=big, cs_tiles=1, emit_cache=True)
    o_p = _attention(q_p.reshape(bp, sp, -1), [(k_p.reshape(bp, sp, -1), vt_p)],
                     tq=sp, heads=N_HEADS, name="attn_ctx")
    x1_p, h2_p, rid_p, rw_p = _outproj(
        xp2, bg_p, cu_p, o_p.reshape(tp, -1), mod3, conv_w[0], w_out_b, n2g, wr,
        cond_base=0, tiles_per_cond=big, seq_len=sp)

    kr_c = cache_krope[:, 0].reshape(bs * past, QK_ROPE)
    k_c, vt_c = _kvcache(cache_ckv[:, 0].reshape(bs * past, KV_LORA), jnp.concatenate([kr_c, kr_c], axis=1),
                         cs_id, w_uk_b, w_uvt_b, gk)
    bg_s, cu_s, q_s, k_s, vt_s = _inproj(
        xs2, mod3, n1g, w_in_b, qlg, w_uq_b, kvg, w_uk_b, w_uvt_b, gq, gk, cs_lat,
        cond_base=1, tiles_per_cond=ss // TM_IN, cs_tiles=ss // TM_IN, emit_cache=False)
    o_s = _attention(q_s.reshape(bs, ss, -1),
                     [(k_c.reshape(bs, past, -1), vt_c), (k_s.reshape(bs, ss, -1), vt_s)],
                     tq=TQ, heads=N_HEADS, name="attn_lat")
    x1_s, h2_s, rid_s, rw_s = _outproj(
        xs2, bg_s, cu_s, o_s.reshape(ts, -1), mod3, conv_w[0], w_out_b, n2g, wr,
        cond_base=1, tiles_per_cond=ss // TM_OUT, seq_len=ss)

    n_tok = tp + ts
    slot_tiles = n_tok // TM_ROW
    nt = 2 * n_tok // TM_MOE + N_EXPERTS
    rid = jnp.concatenate([rid_p[:, :2], rid_s[:, :2]], axis=0).T.reshape(2 * n_tok)
    pos, te, n_valid, pad_start, pad_cnt, first, next_expert, wslot = _route_tables(rid, tm=TM_MOE)
    pos3 = pos.reshape(2 * slot_tiles, 1, TM_ROW)
    xs = _dispatch(pad_start, pad_cnt, n_valid, pos3, h2_p, h2_s, nt=nt)
    y = _moe(te, n_valid, first, next_expert, wslot, xs, w_gate[0], w_up[0], w_down[0])

    y_p = _final(x1_p, rw_p, y, pos3, mod3, tile_base=0, slot_tiles=slot_tiles, cond_base=0, tiles_per_cond=big)
    y_s = _final(x1_s, rw_s, y, pos3, mod3, tile_base=tp // TM_ROW, slot_tiles=slot_tiles, cond_base=1,
                 tiles_per_cond=ss // TM_ROW)

    return (y_p.reshape(bp, sp, D_MODEL), y_s.reshape(bs, ss, D_MODEL),
            ckv_p.reshape(bp, 1, sp, KV_LORA), kr_p.reshape(bp, 1, sp, QK_ROPE))
```
